```python
import jax
import jax.numpy as jnp
from jax import lax
import numpy as np

D_MODEL = 1024
BATCH = 1
SEQ = 16384
DEPTH = 2

D_MIX = D_MODEL
GROUP_W = D_MIX // 4
HEAD_DIM = 64
N_HEADS_GROUP = GROUP_W // HEAD_DIM

RWKV_DECAY_RANK = 64
RWKV_A_RANK = 64
RWKV_GATE_RANK = 128
RWKV_VRES_RANK = 32
RWKV_GN_EPS = 64e-5

GDN_CONV = 4
GDN_CHUNK = 64

GLA_KEY_DIM = GROUP_W // 2
GLA_HEAD_K = GLA_KEY_DIM // N_HEADS_GROUP
GLA_GATE_RANK = 16
GLA_GATE_NORMALIZER = 16.0
GLA_CHUNK = 64

SGU_CHUNK = 128
SGU_GROUPS = 4
SGU_GROUP_W = GROUP_W // SGU_GROUPS

D_FF = 4 * D_MODEL
EPS = 1e-6

RWKV_COLS = 3 * GROUP_W + RWKV_DECAY_RANK + RWKV_A_RANK + RWKV_GATE_RANK
GDN_COLS = 4 * GROUP_W + 2 * N_HEADS_GROUP
GLA_COLS = 2 * GLA_KEY_DIM + 2 * GROUP_W + GLA_GATE_RANK
SGU_COLS = 2 * GROUP_W
N_IN = RWKV_COLS + GDN_COLS + GLA_COLS + SGU_COLS

kernel_name = 'hymba_style_rwkv7_gdn_gla_sgu_hybrid'


def _split(x, sizes):
    return jnp.split(x, np.cumsum(sizes)[:-1].tolist(), axis=-1)


def rms_norm(x, g, eps=EPS):
    xf = x.astype(jnp.float32)
    y = xf * lax.rsqrt(jnp.mean(xf * xf, axis=-1, keepdims=True) + eps)
    return (y * g.astype(jnp.float32)).astype(x.dtype)


def layer_norm(x, g, b, eps=1e-5):
    xf = x.astype(jnp.float32)
    mu = jnp.mean(xf, axis=-1, keepdims=True)
    var = jnp.mean(jnp.square(xf - mu), axis=-1, keepdims=True)
    return ((xf - mu) * lax.rsqrt(var + eps) * g + b).astype(x.dtype)


def l2_normalize(t, eps=1e-6):
    tf = t.astype(jnp.float32)
    return tf * lax.rsqrt(jnp.sum(tf * tf, axis=-1, keepdims=True) + eps)


def token_shift(x, mu):
    x_prev = jnp.pad(x, ((0, 0), (1, 0), (0, 0)))[:, :-1]
    return x + (x_prev - x) * mu


def causal_depthwise_conv(x, w):
    K, C = w.shape
    return lax.conv_general_dilated(
        x, w[:, None, :].astype(x.dtype), window_strides=(1,), padding=[(K - 1, 0)],
        dimension_numbers=('NWC', 'WIO', 'NWC'), feature_group_count=C)


def rwkv7_recurrence(r, decay, k, v, a_vec, b_vec):
    B, T, H, D = r.shape
    tm = lambda t: jnp.moveaxis(t.astype(jnp.float32), 1, 0)

    def step(S, inp):
        r_t, w_t, k_t, v_t, a_t, b_t = inp
        sa = jnp.einsum('bhvk,bhk->bhv', S, a_t)
        S = S * w_t[:, :, None, :] + sa[..., None] * b_t[:, :, None, :] + v_t[..., None] * k_t[:, :, None, :]
        return S, jnp.einsum('bhvk,bhk->bhv', S, r_t)

    S0 = jnp.zeros((B, H, D, D), jnp.float32)
    _, y = lax.scan(step, S0, (tm(r), tm(decay), tm(k), tm(v), tm(a_vec), tm(b_vec)))
    return jnp.moveaxis(y, 0, 1)


def rwkv7_time_mix(cols, mu, w0, w_up, a0, a_up, g_up, k_k, k_a, r_k, lnx_w, lnx_b, v_first, vres):
    B, T, _ = cols.shape
    H, Dh = N_HEADS_GROUP, HEAD_DIM
    f32 = jnp.float32
    xs = token_shift(cols, mu)
    r, k, v, w_d, a_d, g_d = _split(xs, (GROUP_W, GROUP_W, GROUP_W, RWKV_DECAY_RANK, RWKV_A_RANK, RWKV_GATE_RANK))
    w_log = -jax.nn.softplus(-(w0 + jnp.tanh(w_d) @ w_up).astype(f32)) - 0.5
    decay = jnp.exp(-jnp.exp(w_log))
    a = jax.nn.sigmoid(a0 + a_d @ a_up)
    g = jax.nn.sigmoid(g_d) @ g_up
    if vres is None:
        v_first = v
    else:
        v0, vres_cols, vres_up = vres
        v = v + (v_first - v) * jax.nn.sigmoid(v0 + vres_cols @ vres_up)
    heads = lambda t: t.reshape(B, T, H, Dh)
    kk = l2_normalize(heads(k * k_k), eps=1e-24)
    k = k * (1 + (a - 1) * k_a)
    r_h, k_h, v_h, a_h = heads(r), heads(k), heads(v), heads(a)
    y = rwkv7_recurrence(r_h, heads(decay), k_h, v_h, -kk, kk * a_h)
    mean = jnp.mean(y, axis=-1, keepdims=True)
    var = jnp.mean(jnp.square(y - mean), axis=-1, keepdims=True)
    y = (y - mean) * lax.rsqrt(var + RWKV_GN_EPS) * lnx_w.reshape(H, Dh) + lnx_b.reshape(H, Dh)
    y = y + jnp.sum(r_h * k_h * r_k, axis=-1, keepdims=True) * v_h
    return (y.reshape(B, T, GROUP_W) * g).astype(cols.dtype), v_first


def chunk_gated_delta_rule(q, k, v, g, beta):
    B, T, H, dk = q.shape
    dv = v.shape[-1]
    C = GDN_CHUNK
    N = T // C
    f32 = jnp.float32

    def chunks(t):
        return t.astype(f32).reshape(B, N, C, H, -1).transpose(1, 0, 3, 2, 4)

    q = chunks(q) * dk ** -0.5
    k, v = chunks(k), chunks(v)
    g = chunks(g[..., None])[..., 0]
    beta = chunks(beta[..., None])[..., 0]
    gc = jnp.cumsum(g, axis=-1)
    tril = jnp.tril(jnp.ones((C, C), bool))
    decay = jnp.exp(jnp.where(tril, gc[..., :, None] - gc[..., None, :], -jnp.inf))
    k_beta = k * beta[..., None]
    lower = jnp.tril(jnp.einsum('nbhid,nbhjd->nbhij', k_beta, k) * decay, -1)
    rhs = jnp.concatenate([v * beta[..., None], k_beta * jnp.exp(gc)[..., None]], axis=-1)
    sol = lax.linalg.triangular_solve(lower + jnp.eye(C, dtype=f32), rhs,
                                      left_side=True, lower=True, unit_diagonal=True)
    u, w = sol[..., :dv], sol[..., dv:]
    qk = jnp.einsum('nbhid,nbhjd->nbhij', q, k) * decay
    q_dec = q * jnp.exp(gc)[..., None]
    k_dec = k * jnp.exp(gc[..., -1:] - gc)[..., None]
    g_last = jnp.exp(gc[..., -1])

    def step(S, inp):
        u_n, w_n, qk_n, qd_n, kd_n, gl_n = inp
        v_new = u_n - w_n @ S
        o = qd_n @ S + qk_n @ v_new
        S = S * gl_n[..., None, None] + jnp.einsum('bhck,bhcv->bhkv', kd_n, v_new)
        return S, o

    S0 = jnp.zeros((B, H, dk, dv), f32)
    _, o = lax.scan(step, S0, (u, w, qk, q_dec, k_dec, g_last))
    return o.transpose(1, 0, 3, 2, 4).reshape(B, T, H, dv)


def gated_deltanet_mix(cols, conv_w, a_log, dt_bias, norm_g):
    B, T, _ = cols.shape
    H, Dh = N_HEADS_GROUP, HEAD_DIM
    qkv, z, b_raw, a_raw = _split(cols, (3 * GROUP_W, GROUP_W, H, H))
    qkv = jax.nn.silu(causal_depthwise_conv(qkv, conv_w))
    q, k, v = _split(qkv, (GROUP_W, GROUP_W, GROUP_W))
    heads = lambda t: t.reshape(B, T, H, Dh)
    q, k = l2_normalize(heads(q)), l2_normalize(heads(k))
    beta = jax.nn.sigmoid(b_raw.astype(jnp.float32))
    g = -jnp.exp(a_log) * jax.nn.softplus(a_raw.astype(jnp.float32) + dt_bias)
    o = chunk_gated_delta_rule(q, k, heads(v), g, beta)
    o = rms_norm(o, norm_g) * jax.nn.silu(heads(z))
    return o.reshape(B, T, GROUP_W).astype(cols.dtype)


def chunk_gla(q, k, v, log_a):
    B, T, H, dk = q.shape
    dv = v.shape[-1]
    C = GLA_CHUNK
    N = T // C
    f32 = jnp.float32

    def chunks(t):
        return t.astype(f32).reshape(B, N, C, H, -1).transpose(1, 0, 3, 2, 4)

    q = chunks(q) * dk ** -0.5
    k, v = chunks(k), chunks(v)
    b = jnp.cumsum(chunks(log_a), axis=3)
    q_inter = q * jnp.exp(b)
    k_state = k * jnp.exp(b[..., -1:, :] - b)
    a_last = jnp.exp(b[..., -1, :])
    tril = jnp.tril(jnp.ones((C, C), bool))[:, :, None]

    def step(S, inp):
        q_n, k_n, v_n, b_n, qi_n, ks_n, al_n = inp
        rel = jnp.exp(jnp.where(tril, b_n[..., :, None, :] - b_n[..., None, :, :], -jnp.inf))
        scores = jnp.einsum('bhid,bhjd,bhijd->bhij', q_n, k_n, rel)
        o = scores @ v_n + qi_n @ S
        S = S * al_n[..., None] + jnp.einsum('bhck,bhcv->bhkv', ks_n, v_n)
        return S, o

    S0 = jnp.zeros((B, H, dk, dv), f32)
    _, o = lax.scan(step, S0, (q, k, v, b, q_inter, k_state, a_last))
    return o.transpose(1, 0, 3, 2, 4).reshape(B, T, H, dv)


def gla_mix(cols, gk_up, gk_bias, norm_g):
    B, T, _ = cols.shape
    H = N_HEADS_GROUP
    q, k, v, gate, gk_d = _split(cols, (GLA_KEY_DIM, GLA_KEY_DIM, GROUP_W, GROUP_W, GLA_GATE_RANK))
    log_a = jax.nn.log_sigmoid((gk_d @ gk_up + gk_bias).astype(jnp.float32)) / GLA_GATE_NORMALIZER
    hk = lambda t: t.reshape(B, T, H, GLA_HEAD_K)
    hv = lambda t: t.reshape(B, T, H, HEAD_DIM)
    o = chunk_gla(hk(q), hk(k), hv(v), hk(log_a))
    o = rms_norm(o, norm_g) * jax.nn.silu(hv(gate))
    return o.reshape(B, T, GROUP_W).astype(cols.dtype)


def sgu_mix(cols, ln_g, ln_b, w_s, b_s):
    B, T, _ = cols.shape
    N = T // SGU_CHUNK
    u, v = _split(jax.nn.gelu(cols), (GROUP_W, GROUP_W))
    v = layer_norm(v, ln_g, ln_b).reshape(B, N, SGU_CHUNK, SGU_GROUPS, SGU_GROUP_W)
    w_causal = jnp.where(jnp.tril(jnp.ones((SGU_CHUNK, SGU_CHUNK), bool)), w_s, 0.0)
    mixed = jnp.einsum('gts,bnsgc->bntgc', w_causal, v) + b_s.T[None, None, :, :, None]
    return (u * mixed.reshape(B, T, GROUP_W)).astype(cols.dtype)


def setup_inputs(seed: int = 0) -> dict:
    key = jax.random.key(seed)
    ks = iter(jax.random.split(key, 48))
    f32 = jnp.float32
    nrm = lambda shape, scale: jax.random.normal(next(ks), shape, f32) * scale
    uni = lambda shape, lo, hi: jax.random.uniform(next(ks), shape, f32, lo, hi)
    L, H = DEPTH, N_HEADS_GROUP
    dt = jnp.exp(uni((L, H), float(np.log(1e-3)), float(np.log(1e-1))))
    return {
        'x': nrm((BATCH, SEQ, D_MODEL), 1.0),
        'w_in': nrm((L, D_MODEL, N_IN), D_MODEL ** -0.5),
        'w_out': nrm((L, D_MIX, D_MODEL), D_MIX ** -0.5),
        'norm_mix_g': 1.0 + nrm((L, D_MODEL), 0.02),
        'norm_ffn_g': 1.0 + nrm((L, D_MODEL), 0.02),
        'norm_final_g': 1.0 + nrm((D_MODEL,), 0.02),
        'rwkv_mu': uni((L, RWKV_COLS), 0.0, 1.0),
        'rwkv_w0': uni((L, GROUP_W), -6.0, -1.0),
        'rwkv_w_up': nrm((L, RWKV_DECAY_RANK, GROUP_W), 0.1 * RWKV_DECAY_RANK ** -0.5),
        'rwkv_a0': nrm((L, GROUP_W), 0.1),
        'rwkv_a_up': nrm((L, RWKV_A_RANK, GROUP_W), RWKV_A_RANK ** -0.5),
        'rwkv_g_up': nrm((L, RWKV_GATE_RANK, GROUP_W), RWKV_GATE_RANK ** -0.5),
        'rwkv_k_k': 0.85 + nrm((L, GROUP_W), 0.02),
        'rwkv_k_a': 1.0 + nrm((L, GROUP_W), 0.02),
        'rwkv_r_k': nrm((L, H, HEAD_DIM), 0.1),
        'rwkv_lnx_w': 1.0 + nrm((L, GROUP_W), 0.02),
        'rwkv_lnx_b': nrm((L, GROUP_W), 0.02),
        'rwkv_v0': 1.0 + nrm((L - 1, GROUP_W), 0.1),
        'rwkv_vres_down': nrm((L - 1, D_MODEL, RWKV_VRES_RANK), D_MODEL ** -0.5),
        'rwkv_vres_up': nrm((L - 1, RWKV_VRES_RANK, GROUP_W), RWKV_VRES_RANK ** -0.5),
        'gdn_conv_w': nrm((L, GDN_CONV, 3 * GROUP_W), GDN_CONV ** -0.5),
        'gdn_a_log': jnp.log(uni((L, H), 1.0, 16.0)),
        'gdn_dt_bias': dt + jnp.log(-jnp.expm1(-dt)),
        'gdn_norm_g': 1.0 + nrm((L, HEAD_DIM), 0.02),
        'gla_gk_up': nrm((L, GLA_GATE_RANK, GLA_KEY_DIM), GLA_GATE_RANK ** -0.5),
        'gla_gk_bias': nrm((L, GLA_KEY_DIM), 0.1),
        'gla_norm_g': 1.0 + nrm((L, HEAD_DIM), 0.02),
        'sgu_ln_g': 1.0 + nrm((L, GROUP_W), 0.02),
        'sgu_ln_b': nrm((L, GROUP_W), 0.02),
        'sgu_w_s': nrm((L, SGU_GROUPS, SGU_CHUNK, SGU_CHUNK), SGU_CHUNK ** -0.5),
        'sgu_b_s': 1.0 + nrm((L, SGU_GROUPS, SGU_CHUNK), 0.1),
        'ffn_w_up': nrm((L, D_MODEL, D_FF), D_MODEL ** -0.5),
        'ffn_w_down': nrm((L, D_FF, D_MODEL), D_FF ** -0.5),
    }


def reference(x, w_in, w_out, norm_mix_g, norm_ffn_g, norm_final_g,
              rwkv_mu, rwkv_w0, rwkv_w_up, rwkv_a0, rwkv_a_up, rwkv_g_up, rwkv_k_k, rwkv_k_a, rwkv_r_k,
              rwkv_lnx_w, rwkv_lnx_b, rwkv_v0, rwkv_vres_down, rwkv_vres_up,
              gdn_conv_w, gdn_a_log, gdn_dt_bias, gdn_norm_g,
              gla_gk_up, gla_gk_bias, gla_norm_g,
              sgu_ln_g, sgu_ln_b, sgu_w_s, sgu_b_s,
              ffn_w_up, ffn_w_down):
    v_first = None
    for l in range(DEPTH):
        h = rms_norm(x, norm_mix_g[l])
        if l == 0:
            w_comb = w_in[l]
        else:
            w_comb = jnp.concatenate([w_in[l], rwkv_vres_down[l - 1]], axis=1)
        cols = h @ w_comb
        c_a, c_b, c_c, c_d = _split(cols[..., :N_IN], (RWKV_COLS, GDN_COLS, GLA_COLS, SGU_COLS))
        vres = None if l == 0 else (rwkv_v0[l - 1], cols[..., N_IN:], rwkv_vres_up[l - 1])
        y_a, v_first = rwkv7_time_mix(c_a, rwkv_mu[l], rwkv_w0[l], rwkv_w_up[l], rwkv_a0[l], rwkv_a_up[l],
                                      rwkv_g_up[l], rwkv_k_k[l], rwkv_k_a[l], rwkv_r_k[l],
                                      rwkv_lnx_w[l], rwkv_lnx_b[l], v_first, vres)
        y_b = gated_deltanet_mix(c_b, gdn_conv_w[l], gdn_a_log[l], gdn_dt_bias[l], gdn_norm_g[l])
        y_c = gla_mix(c_c, gla_gk_up[l], gla_gk_bias[l], gla_norm_g[l])
        y_d = sgu_mix(c_d, sgu_ln_g[l], sgu_ln_b[l], sgu_w_s[l], sgu_b_s[l])
        x = x + jnp.concatenate([y_a, y_b, y_c, y_d], axis=-1) @ w_out[l]
        h = rms_norm(x, norm_ffn_g[l])
        x = x + jnp.square(jax.nn.relu(h @ ffn_w_up[l])) @ ffn_w_down[l]
    return rms_norm(x, norm_final_g)
```

```python
import functools

import jax
import jax.numpy as jnp
from jax import lax
from jax.experimental import pallas as pl
from jax.experimental.pallas import tpu as pltpu

f32 = jnp.float32
bf16 = jnp.bfloat16
HI = lax.Precision.HIGHEST

SEQ = 16384
D_MODEL = 1024
GROUP_W = 256
HEAD_DIM = 64
N_HEADS = 4
GLA_KEY = 128
GLA_HEAD_K = 32
D_FF = 4096
EPS = 1e-6
RWKV_GN_EPS = 64e-5
NEG = -1e30

NA, NB, NC, ND = 1152, 1152, 896, 512
N_PAD = NA + NB + NC + ND

VMEM_LIMIT = 56 * 1024 * 1024

NN = (((1,), (0,)), ((), ()))
NT = (((1,), (1,)), ((), ()))
TN = (((0,), (0,)), ((), ()))


def _dot(a, b, dims=NN, hi=False):
    if hi:
        return lax.dot_general(a.astype(f32), b.astype(f32), dims, precision=HI,
                               preferred_element_type=f32)
    return lax.dot_general(a.astype(bf16), b.astype(bf16), dims, preferred_element_type=f32)


def _iota(shape, axis):
    return lax.broadcasted_iota(jnp.int32, shape, axis)


def _segsum(x, ones_bd):
    hi = x.astype(bf16)
    lo = (x - hi.astype(f32)).astype(bf16)
    return (jnp.dot(hi, ones_bd, preferred_element_type=f32)
            + jnp.dot(lo, ones_bd, preferred_element_type=f32))


def _group_ones(n, width):
    return (_iota((n, n), 0) // width == _iota((n, n), 1) // width).astype(bf16)


def _sigmoid(x):
    return 1.0 / (1.0 + jnp.exp(-x))


def _softplus(x):
    return jnp.maximum(x, 0.0) + jnp.log1p(jnp.exp(-jnp.abs(x)))


def _shift_rows(x, carry, s):
    xs = pltpu.roll(x, s, 0)
    fix = pltpu.roll(carry, s, 0)
    first = jnp.where(_iota(carry.shape, 0) < s, fix, xs[0:8])
    return jnp.concatenate([first, xs[8:]], axis=0)


def _tri_masks(c):
    ri, ci = _iota((c, c), 0), _iota((c, c), 1)
    eye = (ri == ci).astype(f32)
    m16 = (ri // 16 == ci // 16).astype(f32)
    mo1 = ((ri // 32 == ci // 32) & (ri // 16 == ci // 16 + 1)).astype(f32)
    mo2 = ((ri // 32 == 1) & (ci // 32 == 0)).astype(f32)
    return eye, m16, mo1, mo2


def _tri_inv(lm, masks, hi):
    eye, m16, mo1, mo2 = masks
    n = -(lm * m16)
    t = eye + n
    p = n
    for _ in range(3):
        p = _dot(p, p, hi=hi)
        t = t + _dot(t, p, hi=hi)
    for mo in (mo1, mo2):
        t = t - _dot(t, _dot(lm * mo, t, hi=hi), hi=hi)
    return t


def _inproj_kernel(x_ref, g_ref, w_ref, oa_ref, ob_ref, oc_ref, od_ref):
    x = x_ref[...]
    ms = jnp.mean(x * x, axis=-1, keepdims=True)
    h = (x * lax.rsqrt(ms + EPS) * g_ref[...]).astype(bf16)
    off = 0
    for o_ref, n in ((oa_ref, NA), (ob_ref, NB), (oc_ref, NC), (od_ref, ND)):
        o_ref[...] = jnp.dot(h, w_ref[:, off:off + n], preferred_element_type=f32)
        off += n


def _in_proj(x, g, w):
    tm = 512
    return pl.pallas_call(
        _inproj_kernel,
        grid=(SEQ // tm,),
        in_specs=[pl.BlockSpec((tm, D_MODEL), lambda i: (i, 0)),
                  pl.BlockSpec((1, D_MODEL), lambda i: (0, 0)),
                  pl.BlockSpec((D_MODEL, N_PAD), lambda i: (0, 0))],
        out_specs=[pl.BlockSpec((tm, n), lambda i: (i, 0)) for n in (NA, NB, NC, ND)],
        out_shape=[jax.ShapeDtypeStruct((SEQ, n), f32) for n in (NA, NB, NC, ND)],
        compiler_params=pltpu.CompilerParams(dimension_semantics=("arbitrary",),
                                             vmem_limit_bytes=VMEM_LIMIT),
        name="in_proj",
    )(x, g, w)


RW_TB = 256
RW_C = 64


def _rwkv_kernel(*refs, has_vres):
    if has_vres:
        (c_ref, vf_ref, mu_ref, w0_ref, wup_ref, a0_ref, aup_ref, gup_ref, kk_ref, ka_ref, rk_ref,
         lw_ref, lb_ref, v0_ref, vup_ref, y_ref,
         carry_ref, s_ref, r_s, k_s, v_s, lw_s, al_s, be_s, y_s) = refs
    else:
        (c_ref, mu_ref, w0_ref, wup_ref, a0_ref, aup_ref, gup_ref, kk_ref, ka_ref, rk_ref,
         lw_ref, lb_ref, y_ref, vf_out_ref,
         carry_ref, s_ref, r_s, k_s, v_s, lw_s, al_s, be_s, y_s) = refs

    @pl.when(pl.program_id(0) == 0)
    def _():
        carry_ref[...] = jnp.zeros_like(carry_ref)
        s_ref[...] = jnp.zeros_like(s_ref)

    ones_bd = _group_ones(GROUP_W, HEAD_DIM)
    x = c_ref[:, 0:1024]
    x_prev = _shift_rows(x, carry_ref[...], 1)
    carry_ref[...] = x[RW_TB - 8:RW_TB]
    xs = x + (x_prev - x) * mu_ref[...]
    r = xs[:, 0:256]
    k = xs[:, 256:512]
    v = xs[:, 512:768]
    lora = xs[:, 768:896]
    w_pre = w0_ref[...] + _dot(jnp.tanh(lora), wup_ref[...])
    lw = -jnp.exp(-_softplus(-w_pre) - 0.5)
    a = _sigmoid(a0_ref[...] + _dot(lora, aup_ref[...]))
    g = _dot(_sigmoid(xs[:, 896:1024]), gup_ref[...])
    if has_vres:
        mix = _sigmoid(v0_ref[...] + _dot(c_ref[:, 1024:1152], vup_ref[...]))
        v = v + (vf_ref[...] - v) * mix
    else:
        vf_out_ref[...] = v
    kk = k * kk_ref[...]
    kk = kk * lax.rsqrt(_segsum(kk * kk, ones_bd) + 1e-24)
    k = k * (1.0 + (a - 1.0) * ka_ref[...])
    r_s[...] = r
    k_s[...] = k
    v_s[...] = v
    lw_s[...] = lw
    al_s[...] = -kk
    be_s[...] = kk * a

    c = RW_C
    ri, ci = _iota((c, c), 0), _iota((c, c), 1)
    tril_incl = (ri >= ci).astype(f32)
    low_strict = ri > ci
    low_incl = ri >= ci
    masks = _tri_masks(c)

    def chunk(n, _):
        rows = pl.ds(pl.multiple_of(n * c, c), c)
        lwc = lw_s[rows, :]
        lc = _dot(tril_incl, lwc, hi=True)
        lex = lc - lwc
        llast = lc[c - 1:c, :]
        e_in = jnp.exp(lc)
        e_out = jnp.exp(-lc)
        e_rest = jnp.exp(llast - lc)
        rc, kc, vc, alc, bec = r_s[rows, :], k_s[rows, :], v_s[rows, :], al_s[rows, :], be_s[rows, :]
        rt = rc * e_in
        at = alc * jnp.exp(lex)
        bt = bec * e_out
        kt = kc * e_out
        bw = bec * e_rest
        kw = kc * e_rest
        dlast = jnp.exp(llast)
        ys = []
        for h in range(N_HEADS):
            sl = slice(h * HEAD_DIM, (h + 1) * HEAD_DIM)
            a_ab = jnp.where(low_strict, _dot(at[:, sl], bt[:, sl], NT), 0.0)
            a_ak = jnp.where(low_strict, _dot(at[:, sl], kt[:, sl], NT), 0.0)
            a_rb = jnp.where(low_incl, _dot(rt[:, sl], bt[:, sl], NT), 0.0)
            a_rk = jnp.where(low_incl, _dot(rt[:, sl], kt[:, sl], NT), 0.0)
            tm = _tri_inv(-a_ab, masks, hi=False)
            s0 = s_ref[h]
            vh = vc[:, sl]
            u = _dot(tm, _dot(at[:, sl], s0, NT) + _dot(a_ak, vh))
            ys.append(_dot(rt[:, sl], s0, NT) + _dot(a_rb, u) + _dot(a_rk, vh))
            s_ref[h] = s0 * dlast[:, sl] + _dot(u, bw[:, sl], TN) + _dot(vh, kw[:, sl], TN)
        y_s[rows, :] = jnp.concatenate(ys, axis=1)
        return 0

    lax.fori_loop(0, RW_TB // c, chunk, 0)

    y = y_s[...]
    inv_d = 1.0 / HEAD_DIM
    mean = _segsum(y, ones_bd) * inv_d
    yc = y - mean
    var = _segsum(yc * yc, ones_bd) * inv_d
    y = yc * lax.rsqrt(var + RWKV_GN_EPS) * lw_ref[...] + lb_ref[...]
    y = y + _segsum(r * k * rk_ref[...], ones_bd) * v
    y_ref[...] = (y * g).astype(bf16)


def _rwkv(c_a, v_first, p):
    has_vres = v_first is not None
    row = lambda n: pl.BlockSpec((1, n), lambda i: (0, 0))
    full = lambda a, b: pl.BlockSpec((a, b), lambda i: (0, 0))
    blk = lambda n: pl.BlockSpec((RW_TB, n), lambda i: (i, 0))
    in_specs = [blk(NA)]
    args = [c_a]
    if has_vres:
        in_specs.append(blk(GROUP_W))
        args.append(v_first)
    in_specs += [row(1024), row(256), full(128, 256), row(256), full(128, 256), full(128, 256),
                 row(256), row(256), row(256), row(256), row(256)]
    args += [p["mu"], p["w0"], p["w_up"], p["a0"], p["a_up"], p["g_up"], p["k_k"], p["k_a"], p["r_k"],
             p["lnx_w"], p["lnx_b"]]
    if has_vres:
        in_specs += [row(256), full(128, 256)]
        args += [p["v0"], p["vres_up"]]
        out_specs = blk(GROUP_W)
        out_shape = jax.ShapeDtypeStruct((SEQ, GROUP_W), bf16)
    else:
        out_specs = [blk(GROUP_W), blk(GROUP_W)]
        out_shape = [jax.ShapeDtypeStruct((SEQ, GROUP_W), bf16),
                     jax.ShapeDtypeStruct((SEQ, GROUP_W), f32)]
    scratch = [pltpu.VMEM((8, 1024), f32), pltpu.VMEM((N_HEADS, HEAD_DIM, HEAD_DIM), f32)]
    scratch += [pltpu.VMEM((RW_TB, GROUP_W), f32) for _ in range(7)]
    out = pl.pallas_call(
        functools.partial(_rwkv_kernel, has_vres=has_vres),
        grid=(SEQ // RW_TB,),
        in_specs=in_specs,
        out_specs=out_specs,
        out_shape=out_shape,
        scratch_shapes=scratch,
        compiler_params=pltpu.CompilerParams(dimension_semantics=("arbitrary",),
                                             vmem_limit_bytes=VMEM_LIMIT),
        name="rwkv7",
    )(*args)
    if has_vres:
        return out, v_first
    return out[0], out[1]


GD_TB = 256
GD_C = 64


def _gdn_kernel(c_ref, cw_ref, alog_ref, dtb_ref, ng_ref, y_ref,
                carry_ref, s_ref, q_s, k_s, v_s, be_s, g_s, o_s):
    @pl.when(pl.program_id(0) == 0)
    def _():
        carry_ref[...] = jnp.zeros_like(carry_ref)
        s_ref[...] = jnp.zeros_like(s_ref)

    ones_bd = _group_ones(GROUP_W, HEAD_DIM)
    raw = c_ref[:, 0:768]
    carry = carry_ref[...]
    conv = raw * cw_ref[3:4, :]
    for s in range(1, 4):
        conv = conv + _shift_rows(raw, carry, s) * cw_ref[3 - s:4 - s, :]
    carry_ref[...] = raw[GD_TB - 8:GD_TB]
    qkv = conv * _sigmoid(conv)
    q = qkv[:, 0:256]
    k = qkv[:, 256:512]
    q_s[...] = q * lax.rsqrt(_segsum(q * q, ones_bd) + 1e-6) * (HEAD_DIM ** -0.5)
    k_s[...] = k * lax.rsqrt(_segsum(k * k, ones_bd) + 1e-6)
    v_s[...] = qkv[:, 512:768]
    small = c_ref[:, 1024:1152]
    er, ec = _iota((128, GROUP_W), 0), _iota((128, GROUP_W), 1)
    b_exp = _dot(small, (er == ec // HEAD_DIM).astype(f32), hi=True)
    a_exp = _dot(small, (er == ec // HEAD_DIM + N_HEADS).astype(f32), hi=True)
    be_s[...] = _sigmoid(b_exp)
    g_s[...] = -jnp.exp(alog_ref[...]) * _softplus(a_exp + dtb_ref[...])

    c = GD_C
    ri, ci = _iota((c, c), 0), _iota((c, c), 1)
    tril_incl = (ri >= ci).astype(f32)
    low_strict = ri > ci
    low_incl = ri >= ci
    onehot0 = (ci == 0).astype(f32)
    masks = _tri_masks(c)

    def chunk(n, _):
        rows = pl.ds(pl.multiple_of(n * c, c), c)
        gc = _dot(tril_incl, g_s[rows, :], hi=True)
        qc, kc, vc, bc = q_s[rows, :], k_s[rows, :], v_s[rows, :], be_s[rows, :]
        egc = jnp.exp(gc)
        glast = gc[c - 1:c, :]
        erest = jnp.exp(glast - gc)
        eglast = jnp.exp(glast)
        os_ = []
        for h in range(N_HEADS):
            sl = slice(h * HEAD_DIM, (h + 1) * HEAD_DIM)
            gch, kh, qh, vh, bh = gc[:, sl], kc[:, sl], qc[:, sl], vc[:, sl], bc[:, sl]
            gcol = _dot(onehot0, gch, NT, hi=True)
            dm = jnp.exp(jnp.where(low_incl, gch - gcol, NEG))
            kb = kh * bh
            lm = jnp.where(low_strict, _dot(kb, kh, NT) * dm, 0.0)
            tm = _tri_inv(lm, masks, hi=False)
            rhs = jnp.concatenate([vh * bh, kb * egc[:, sl]], axis=1)
            sol = _dot(tm, rhs)
            u, w = sol[:, 0:HEAD_DIM], sol[:, HEAD_DIM:]
            qk = _dot(qh, kh, NT) * dm
            s0 = s_ref[h]
            v_new = u - _dot(w, s0)
            os_.append(_dot(qh * egc[:, sl], s0) + _dot(qk, v_new))
            s_ref[h] = s0 * eglast[:, sl] + _dot(kh * erest[:, sl], v_new, TN)
        o_s[rows, :] = jnp.concatenate(os_, axis=1)
        return 0

    lax.fori_loop(0, GD_TB // c, chunk, 0)

    o = o_s[...]
    ms = _segsum(o * o, ones_bd) * (1.0 / HEAD_DIM)
    z = c_ref[:, 768:1024]
    y_ref[...] = (o * lax.rsqrt(ms + EPS) * ng_ref[...] * (z * _sigmoid(z))).astype(bf16)


def _gdn(c_b, p):
    row = lambda n: pl.BlockSpec((1, n), lambda i: (0, 0))
    scratch = [pltpu.VMEM((8, 768), f32), pltpu.VMEM((N_HEADS, HEAD_DIM, HEAD_DIM), f32)]
    scratch += [pltpu.VMEM((GD_TB, GROUP_W), f32) for _ in range(6)]
    return pl.pallas_call(
        _gdn_kernel,
        grid=(SEQ // GD_TB,),
        in_specs=[pl.BlockSpec((GD_TB, NB), lambda i: (i, 0)),
                  pl.BlockSpec((4, 768), lambda i: (0, 0)), row(256), row(256), row(256)],
        out_specs=pl.BlockSpec((GD_TB, GROUP_W), lambda i: (i, 0)),
        out_shape=jax.ShapeDtypeStruct((SEQ, GROUP_W), bf16),
        scratch_shapes=scratch,
        compiler_params=pltpu.CompilerParams(dimension_semantics=("arbitrary",),
                                             vmem_limit_bytes=VMEM_LIMIT),
        name="gdn",
    )(c_b, p["conv_w"], p["a_log"], p["dt_bias"], p["norm_g"])


GL_TB = 256
GL_C = 16


def _gla_kernel(c_ref, gup_ref, gb_ref, ng_ref, y_ref, st_ref, q_s, k_s, la_s, o_s):
    @pl.when(pl.program_id(0) == 0)
    def _():
        st_ref[...] = jnp.zeros_like(st_ref)

    q_s[...] = c_ref[:, 0:128] * (GLA_HEAD_K ** -0.5)
    k_s[...] = c_ref[:, 128:256]
    pre = _dot(c_ref[:, 768:896], gup_ref[...]) + gb_ref[...]
    la_s[...] = -_softplus(-pre) * (1.0 / 16.0)

    c = GL_C
    tril_incl = (_iota((c, c), 0) >= _iota((c, c), 1)).astype(f32)
    ri = _iota((c, GLA_KEY), 0)
    ind_e = (_iota((GLA_KEY, GROUP_W), 0) // GLA_HEAD_K == _iota((GLA_KEY, GROUP_W), 1) // HEAD_DIM).astype(bf16)
    bd_mask = (_iota((GROUP_W, GLA_KEY), 0) // HEAD_DIM == _iota((GROUP_W, GLA_KEY), 1) // GLA_HEAD_K).astype(f32)

    def chunk(n, _):
        rows = pl.ds(pl.multiple_of(n * c, c), c)
        qc, kc, la = q_s[rows, :], k_s[rows, :], la_s[rows, :]
        vc = c_ref[rows, 256:512]
        b = _dot(tril_incl, la, hi=True)
        blast = b[c - 1:c, :]
        terms = []
        for j in range(c):
            e = jnp.exp(jnp.where(ri >= j, b - b[j:j + 1, :], NEG))
            terms.append(qc * (kc[j:j + 1, :] * e))
        sx = jnp.dot(jnp.concatenate(terms, axis=0).astype(bf16), ind_e, preferred_element_type=f32)
        st = st_ref[...]
        o = _dot(qc * jnp.exp(b), st, NT)
        for j in range(c):
            o = o + sx[j * c:(j + 1) * c, :] * vc[j:j + 1, :]
        o_s[rows, :] = o
        st_ref[...] = st * jnp.exp(blast) + _dot(vc, kc * jnp.exp(blast - b), TN) * bd_mask
        return 0

    lax.fori_loop(0, GL_TB // c, chunk, 0)

    o = o_s[...]
    ms = _segsum(o * o, _group_ones(GROUP_W, HEAD_DIM)) * (1.0 / HEAD_DIM)
    gate = c_ref[:, 512:768]
    y_ref[...] = (o * lax.rsqrt(ms + EPS) * ng_ref[...] * (gate * _sigmoid(gate))).astype(bf16)


def _gla(c_c, p):
    row = lambda n: pl.BlockSpec((1, n), lambda i: (0, 0))
    return pl.pallas_call(
        _gla_kernel,
        grid=(SEQ // GL_TB,),
        in_specs=[pl.BlockSpec((GL_TB, NC), lambda i: (i, 0)),
                  pl.BlockSpec((128, 128), lambda i: (0, 0)), row(128), row(256)],
        out_specs=pl.BlockSpec((GL_TB, GROUP_W), lambda i: (i, 0)),
        out_shape=jax.ShapeDtypeStruct((SEQ, GROUP_W), bf16),
        scratch_shapes=[pltpu.VMEM((GROUP_W, GLA_KEY), f32),
                        pltpu.VMEM((GL_TB, GLA_KEY), f32), pltpu.VMEM((GL_TB, GLA_KEY), f32),
                        pltpu.VMEM((GL_TB, GLA_KEY), f32), pltpu.VMEM((GL_TB, GROUP_W), f32)],
        compiler_params=pltpu.CompilerParams(dimension_semantics=("arbitrary",),
                                             vmem_limit_bytes=VMEM_LIMIT),
        name="gla",
    )(c_c, p["gk_up"], p["gk_bias"], p["norm_g"])


SG_TB = 512
SG_C = 128


def _sgu_kernel(c_ref, lg_ref, lb_ref, w_ref, bias_ref, y_ref):
    x = c_ref[...]
    gx = 0.5 * x * (1.0 + jnp.tanh(0.7978845608028654 * (x + 0.044715 * x * x * x)))
    u = gx[:, 0:256]
    v = gx[:, 256:512]
    mu = jnp.mean(v, axis=-1, keepdims=True)
    vc = v - mu
    var = jnp.mean(vc * vc, axis=-1, keepdims=True)
    v = vc * lax.rsqrt(var + 1e-5) * lg_ref[...] + lb_ref[...]
    wr, wc = _iota((SG_C, 4 * SG_C), 0), _iota((SG_C, 4 * SG_C), 1)
    w = jnp.where(wc % SG_C <= wr, w_ref[...], 0.0).astype(bf16)
    lane_g = _iota((SG_C, GROUP_W), 1) // HEAD_DIM
    outs = []
    for n in range(SG_TB // SG_C):
        vn = v[n * SG_C:(n + 1) * SG_C, :]
        vst = jnp.concatenate([jnp.where(lane_g == g, vn, 0.0) for g in range(4)], axis=0)
        outs.append(jnp.dot(w, vst.astype(bf16), preferred_element_type=f32) + bias_ref[...])
    y_ref[...] = (u * jnp.concatenate(outs, axis=0)).astype(bf16)


def _sgu(c_d, p):
    row = lambda n: pl.BlockSpec((1, n), lambda i: (0, 0))
    return pl.pallas_call(
        _sgu_kernel,
        grid=(SEQ // SG_TB,),
        in_specs=[pl.BlockSpec((SG_TB, ND), lambda i: (i, 0)), row(256), row(256),
                  pl.BlockSpec((SG_C, 4 * SG_C), lambda i: (0, 0)),
                  pl.BlockSpec((SG_C, GROUP_W), lambda i: (0, 0))],
        out_specs=pl.BlockSpec((SG_TB, GROUP_W), lambda i: (i, 0)),
        out_shape=jax.ShapeDtypeStruct((SEQ, GROUP_W), bf16),
        compiler_params=pltpu.CompilerParams(dimension_semantics=("arbitrary",),
                                             vmem_limit_bytes=VMEM_LIMIT),
        name="sgu",
    )(c_d, p["ln_g"], p["ln_b"], p["w_cat"], p["bias_tile"])


FF_TM = 512
FF_TF = 1024


def _outffn_kernel(*refs, final):
    if final:
        (x_ref, ya_ref, yb_ref, yc_ref, yd_ref, wo_ref, g_ref, wu_ref, wd_ref, gf_ref,
         o_ref, acc_ref, h_ref) = refs
    else:
        (x_ref, ya_ref, yb_ref, yc_ref, yd_ref, wo_ref, g_ref, wu_ref, wd_ref,
         o_ref, acc_ref, h_ref) = refs
    kf = pl.program_id(1)

    @pl.when(kf == 0)
    def _():
        x1 = x_ref[...]
        for m, y_ref in enumerate((ya_ref, yb_ref, yc_ref, yd_ref)):
            x1 = x1 + jnp.dot(y_ref[...], wo_ref[m * GROUP_W:(m + 1) * GROUP_W, :],
                              preferred_element_type=f32)
        acc_ref[...] = x1
        ms = jnp.mean(x1 * x1, axis=-1, keepdims=True)
        h_ref[...] = (x1 * lax.rsqrt(ms + EPS) * g_ref[...]).astype(bf16)

    hid = jnp.dot(h_ref[...], wu_ref[...], preferred_element_type=f32)
    hid = jnp.maximum(hid, 0.0)
    acc_ref[...] += jnp.dot((hid * hid).astype(bf16), wd_ref[...], preferred_element_type=f32)

    @pl.when(kf == pl.num_programs(1) - 1)
    def _():
        x2 = acc_ref[...]
        if final:
            ms = jnp.mean(x2 * x2, axis=-1, keepdims=True)
            x2 = x2 * lax.rsqrt(ms + EPS) * gf_ref[...]
        o_ref[...] = x2


def _out_ffn(x, ys, w_out, g, w_up, w_down, g_final):
    final = g_final is not None
    row = pl.BlockSpec((1, D_MODEL), lambda i, k: (0, 0))
    yspec = pl.BlockSpec((FF_TM, GROUP_W), lambda i, k: (i, 0))
    in_specs = [pl.BlockSpec((FF_TM, D_MODEL), lambda i, k: (i, 0)), yspec, yspec, yspec, yspec,
                pl.BlockSpec((D_MODEL, D_MODEL), lambda i, k: (0, 0)), row,
                pl.BlockSpec((D_MODEL, FF_TF), lambda i, k: (0, k)),
                pl.BlockSpec((FF_TF, D_MODEL), lambda i, k: (k, 0))]
    args = [x, *ys, w_out, g, w_up, w_down]
    if final:
        in_specs.append(row)
        args.append(g_final)
    return pl.pallas_call(
        functools.partial(_outffn_kernel, final=final),
        grid=(SEQ // FF_TM, D_FF // FF_TF),
        in_specs=in_specs,
        out_specs=pl.BlockSpec((FF_TM, D_MODEL), lambda i, k: (i, 0)),
        out_shape=jax.ShapeDtypeStruct((SEQ, D_MODEL), f32),
        scratch_shapes=[pltpu.VMEM((FF_TM, D_MODEL), f32), pltpu.VMEM((FF_TM, D_MODEL), bf16)],
        compiler_params=pltpu.CompilerParams(dimension_semantics=("arbitrary", "arbitrary"),
                                             vmem_limit_bytes=VMEM_LIMIT),
        name="out_ffn",
    )(*args)


def _pad_cols(w, n):
    return jnp.pad(w, ((0, 0), (0, n - w.shape[1])))


def _pad_rows(w, top, total):
    return jnp.pad(w, ((top, total - top - w.shape[0]), (0, 0)))


def kernel(x, w_in, w_out, norm_mix_g, norm_ffn_g, norm_final_g, rwkv_mu, rwkv_w0, rwkv_w_up, rwkv_a0, rwkv_a_up, rwkv_g_up, rwkv_k_k, rwkv_k_a, rwkv_r_k, rwkv_lnx_w, rwkv_lnx_b, rwkv_v0, rwkv_vres_down, rwkv_vres_up, gdn_conv_w, gdn_a_log, gdn_dt_bias, gdn_norm_g, gla_gk_up, gla_gk_bias, gla_norm_g, sgu_ln_g, sgu_ln_b, sgu_w_s, sgu_b_s, ffn_w_up, ffn_w_down):
    depth = w_in.shape[0]
    xx = x[0]
    v_first = None
    r2 = lambda a: a.reshape(1, -1)
    per_head = lambda a: jnp.repeat(a, HEAD_DIM).reshape(1, -1)
    for l in range(depth):
        wl = w_in[l]
        w_a = wl[:, 0:1024]
        if l > 0:
            w_a = jnp.concatenate([w_a, rwkv_vres_down[l - 1]], axis=1)
        w_comb = jnp.concatenate([_pad_cols(w_a, NA), _pad_cols(wl[:, 1024:2056], NB),
                                  _pad_cols(wl[:, 2056:2840], NC), wl[:, 2840:3352]], axis=1).astype(bf16)
        c_a, c_b, c_c, c_d = _in_proj(xx, r2(norm_mix_g[l]), w_comb)

        pa = dict(mu=r2(rwkv_mu[l]), w0=r2(rwkv_w0[l]), w_up=_pad_rows(rwkv_w_up[l], 0, 128),
                  a0=r2(rwkv_a0[l]), a_up=_pad_rows(rwkv_a_up[l], 64, 128), g_up=rwkv_g_up[l],
                  k_k=r2(rwkv_k_k[l]), k_a=r2(rwkv_k_a[l]), r_k=r2(rwkv_r_k[l]),
                  lnx_w=r2(rwkv_lnx_w[l]), lnx_b=r2(rwkv_lnx_b[l]))
        if l > 0:
            pa.update(v0=r2(rwkv_v0[l - 1]), vres_up=_pad_rows(rwkv_vres_up[l - 1], 0, 128))
        y_a, v_first = _rwkv(c_a, v_first, pa)

        y_b = _gdn(c_b, dict(conv_w=gdn_conv_w[l], a_log=per_head(gdn_a_log[l]),
                             dt_bias=per_head(gdn_dt_bias[l]), norm_g=r2(jnp.tile(gdn_norm_g[l], N_HEADS))))
        y_c = _gla(c_c, dict(gk_up=_pad_rows(gla_gk_up[l], 0, 128), gk_bias=r2(gla_gk_bias[l]),
                             norm_g=r2(jnp.tile(gla_norm_g[l], N_HEADS))))
        y_d = _sgu(c_d, dict(ln_g=r2(sgu_ln_g[l]), ln_b=r2(sgu_ln_b[l]),
                             w_cat=sgu_w_s[l].transpose(1, 0, 2).reshape(SG_C, 4 * SG_C),
                             bias_tile=jnp.repeat(sgu_b_s[l].T, HEAD_DIM, axis=1)))
        xx = _out_ffn(xx, (y_a, y_b, y_c, y_d), w_out[l].astype(bf16), r2(norm_ffn_g[l]),
                      ffn_w_up[l].astype(bf16), ffn_w_down[l].astype(bf16),
                      r2(norm_final_g) if l == depth - 1 else None)
    return xx[None]
```

```python
import functools

import jax
import jax.numpy as jnp
from jax import lax
from jax.experimental import pallas as pl
from jax.experimental.pallas import tpu as pltpu

f32 = jnp.float32
bf16 = jnp.bfloat16
HI = lax.Precision.HIGHEST

SEQ = 16384
D_MODEL = 1024
GROUP_W = 256
HEAD_DIM = 64
N_HEADS = 4
GLA_KEY = 128
GLA_HEAD_K = 32
D_FF = 4096
EPS = 1e-6
RWKV_GN_EPS = 64e-5
NEG = -1e30

NA, NB, NC, ND = 1152, 1152, 896, 512
N_PAD = NA + NB + NC + ND

VMEM_LIMIT = 56 * 1024 * 1024

NN = (((1,), (0,)), ((), ()))
NT = (((1,), (1,)), ((), ()))
TN = (((0,), (0,)), ((), ()))


def _dot(a, b, dims=NN, hi=False):
    if hi:
        return lax.dot_general(a.astype(f32), b.astype(f32), dims, precision=HI,
                               preferred_element_type=f32)
    return lax.dot_general(a.astype(bf16), b.astype(bf16), dims, preferred_element_type=f32)


def _iota(shape, axis):
    return lax.broadcasted_iota(jnp.int32, shape, axis)


def _segsum(x, ones_bd):
    hi = x.astype(bf16)
    lo = (x - hi.astype(f32)).astype(bf16)
    return (jnp.dot(hi, ones_bd, preferred_element_type=f32)
            + jnp.dot(lo, ones_bd, preferred_element_type=f32))


def _group_ones(n, width):
    return (_iota((n, n), 0) // width == _iota((n, n), 1) // width).astype(bf16)


def _sigmoid(x):
    return 1.0 / (1.0 + jnp.exp(-x))


def _softplus(x):
    return jnp.maximum(x, 0.0) + jnp.log1p(jnp.exp(-jnp.abs(x)))


def _shift_rows(x, carry, s):
    xs = pltpu.roll(x, s, 0)
    fix = pltpu.roll(carry, s, 0)
    first = jnp.where(_iota(carry.shape, 0) < s, fix, xs[0:8])
    return jnp.concatenate([first, xs[8:]], axis=0)


def _tri_masks(c):
    ri, ci = _iota((c, c), 0), _iota((c, c), 1)
    eye = (ri == ci).astype(f32)
    m16 = (ri // 16 == ci // 16).astype(f32)
    mo1 = ((ri // 32 == ci // 32) & (ri // 16 == ci // 16 + 1)).astype(f32)
    mo2 = ((ri // 32 == 1) & (ci // 32 == 0)).astype(f32)
    return eye, m16, mo1, mo2


def _tri_inv(lms, masks):
    eye, m16, mo1, mo2 = masks
    ps = [-(lm * m16) for lm in lms]
    ts = [eye + p for p in ps]
    for _ in range(3):
        ps = [_dot(p, p) for p in ps]
        ts = [t + _dot(t, p) for t, p in zip(ts, ps)]
    for mo in (mo1, mo2):
        xs = [_dot(lm * mo, t) for lm, t in zip(lms, ts)]
        ts = [t - _dot(t, x) for t, x in zip(ts, xs)]
    return ts


def _inproj_kernel(x_ref, g_ref, w_ref, oa_ref, ob_ref, oc_ref, od_ref):
    x = x_ref[...]
    ms = jnp.mean(x * x, axis=-1, keepdims=True)
    h = (x * lax.rsqrt(ms + EPS) * g_ref[...]).astype(bf16)
    off = 0
    for o_ref, n in ((oa_ref, NA), (ob_ref, NB), (oc_ref, NC), (od_ref, ND)):
        o_ref[...] = jnp.dot(h, w_ref[:, off:off + n], preferred_element_type=f32)
        off += n


def _in_proj(x, g, w):
    tm = 512
    return pl.pallas_call(
        _inproj_kernel,
        grid=(SEQ // tm,),
        in_specs=[pl.BlockSpec((tm, D_MODEL), lambda i: (i, 0)),
                  pl.BlockSpec((1, D_MODEL), lambda i: (0, 0)),
                  pl.BlockSpec((D_MODEL, N_PAD), lambda i: (0, 0))],
        out_specs=[pl.BlockSpec((tm, n), lambda i: (i, 0)) for n in (NA, NB, NC, ND)],
        out_shape=[jax.ShapeDtypeStruct((SEQ, n), f32) for n in (NA, NB, NC, ND)],
        compiler_params=pltpu.CompilerParams(dimension_semantics=("arbitrary",),
                                             vmem_limit_bytes=VMEM_LIMIT),
        name="in_proj",
    )(x, g, w)


RW_TB = 256
RW_C = 64


def _rwkv_kernel(*refs, has_vres):
    if has_vres:
        (c_ref, vf_ref, mu_ref, w0_ref, wup_ref, a0_ref, aup_ref, gup_ref, kk_ref, ka_ref, rk_ref,
         lw_ref, lb_ref, v0_ref, vup_ref, y_ref,
         carry_ref, s_ref, r_s, k_s, v_s, lw_s, al_s, be_s, y_s) = refs
    else:
        (c_ref, mu_ref, w0_ref, wup_ref, a0_ref, aup_ref, gup_ref, kk_ref, ka_ref, rk_ref,
         lw_ref, lb_ref, y_ref, vf_out_ref,
         carry_ref, s_ref, r_s, k_s, v_s, lw_s, al_s, be_s, y_s) = refs

    @pl.when(pl.program_id(0) == 0)
    def _():
        carry_ref[...] = jnp.zeros_like(carry_ref)
        s_ref[...] = jnp.zeros_like(s_ref)

    ones_bd = _group_ones(GROUP_W, HEAD_DIM)
    x = c_ref[:, 0:1024]
    x_prev = _shift_rows(x, carry_ref[...], 1)
    carry_ref[...] = x[RW_TB - 8:RW_TB]
    xs = x + (x_prev - x) * mu_ref[...]
    r = xs[:, 0:256]
    k = xs[:, 256:512]
    v = xs[:, 512:768]
    lora = xs[:, 768:896]
    w_pre = w0_ref[...] + _dot(jnp.tanh(lora), wup_ref[...])
    lw = -jnp.exp(-_softplus(-w_pre) - 0.5)
    a = _sigmoid(a0_ref[...] + _dot(lora, aup_ref[...]))
    g = _dot(_sigmoid(xs[:, 896:1024]), gup_ref[...])
    if has_vres:
        mix = _sigmoid(v0_ref[...] + _dot(c_ref[:, 1024:1152], vup_ref[...]))
        v = v + (vf_ref[...] - v) * mix
    else:
        vf_out_ref[...] = v
    kk = k * kk_ref[...]
    kk = kk * lax.rsqrt(_segsum(kk * kk, ones_bd) + 1e-24)
    k = k * (1.0 + (a - 1.0) * ka_ref[...])
    r_s[...] = r
    k_s[...] = k
    v_s[...] = v
    lw_s[...] = lw
    al_s[...] = -kk
    be_s[...] = kk * a

    c = RW_C
    ri, ci = _iota((c, c), 0), _iota((c, c), 1)
    tril_incl = (ri >= ci).astype(f32)
    low_strict = ri > ci
    low_incl = ri >= ci
    masks = _tri_masks(c)

    nchunk = RW_TB // c
    items = [(n, h) for n in range(nchunk) for h in range(N_HEADS)]
    hsl = lambda h: slice(h * HEAD_DIM, (h + 1) * HEAD_DIM)
    pre = []
    for n in range(nchunk):
        rows = slice(n * c, (n + 1) * c)
        lwc = lw_s[rows, :]
        lc = _dot(tril_incl, lwc, hi=True)
        llast = lc[c - 1:c, :]
        e_out = jnp.exp(-lc)
        e_rest = jnp.exp(llast - lc)
        kc, bec = k_s[rows, :], be_s[rows, :]
        pre.append(dict(rt=r_s[rows, :] * jnp.exp(lc), at=al_s[rows, :] * jnp.exp(lc - lwc),
                        bt=bec * e_out, kt=kc * e_out, bw=bec * e_rest, kw=kc * e_rest,
                        v=v_s[rows, :], dlast=jnp.exp(llast)))
    ms = [_dot(jnp.concatenate([pre[n]["at"][:, hsl(h)], pre[n]["rt"][:, hsl(h)]], axis=0),
               jnp.concatenate([pre[n]["bt"][:, hsl(h)], pre[n]["kt"][:, hsl(h)]], axis=0), NT)
          for n, h in items]
    tms = _tri_inv([jnp.where(low_strict, -m[0:c, 0:c], 0.0) for m in ms], masks)
    avs = [_dot(jnp.concatenate([jnp.where(low_strict, m[0:c, c:], 0.0),
                                 jnp.where(low_incl, m[c:, c:], 0.0)], axis=0), pre[n]["v"][:, hsl(h)])
           for m, (n, h) in zip(ms, items)]
    tas = [_dot(tm, pre[n]["at"][:, hsl(h)]) for tm, (n, h) in zip(tms, items)]
    tvs = [_dot(tm, av[0:c]) for tm, av in zip(tms, avs)]
    state = [s_ref[h] for h in range(N_HEADS)]
    for n in range(nchunk):
        p = pre[n]
        idx = [n * N_HEADS + h for h in range(N_HEADS)]
        sas = [_dot(jnp.concatenate([tas[i], p["rt"][:, hsl(h)]], axis=0), state[h], NT)
               for h, i in enumerate(idx)]
        us = [sa[0:c] + tvs[i] for sa, i in zip(sas, idx)]
        upd = [_dot(jnp.concatenate([u, p["v"][:, hsl(h)]], axis=0),
                    jnp.concatenate([p["bw"][:, hsl(h)], p["kw"][:, hsl(h)]], axis=0), TN)
               for h, u in enumerate(us)]
        state = [state[h] * p["dlast"][:, hsl(h)] + upd[h] for h in range(N_HEADS)]
        ys = [sas[h][c:] + _dot(jnp.where(low_incl, ms[i][c:, 0:c], 0.0), us[h]) + avs[i][c:]
              for h, i in enumerate(idx)]
        y_s[n * c:(n + 1) * c, :] = jnp.concatenate(ys, axis=1)
    for h in range(N_HEADS):
        s_ref[h] = state[h]

    y = y_s[...]
    inv_d = 1.0 / HEAD_DIM
    mean = _segsum(y, ones_bd) * inv_d
    yc = y - mean
    var = _segsum(yc * yc, ones_bd) * inv_d
    y = yc * lax.rsqrt(var + RWKV_GN_EPS) * lw_ref[...] + lb_ref[...]
    y = y + _segsum(r * k * rk_ref[...], ones_bd) * v
    y_ref[...] = (y * g).astype(bf16)


def _rwkv(c_a, v_first, p):
    has_vres = v_first is not None
    row = lambda n: pl.BlockSpec((1, n), lambda i: (0, 0))
    full = lambda a, b: pl.BlockSpec((a, b), lambda i: (0, 0))
    blk = lambda n: pl.BlockSpec((RW_TB, n), lambda i: (i, 0))
    in_specs = [blk(NA)]
    args = [c_a]
    if has_vres:
        in_specs.append(blk(GROUP_W))
        args.append(v_first)
    in_specs += [row(1024), row(256), full(128, 256), row(256), full(128, 256), full(128, 256),
                 row(256), row(256), row(256), row(256), row(256)]
    args += [p["mu"], p["w0"], p["w_up"], p["a0"], p["a_up"], p["g_up"], p["k_k"], p["k_a"], p["r_k"],
             p["lnx_w"], p["lnx_b"]]
    if has_vres:
        in_specs += [row(256), full(128, 256)]
        args += [p["v0"], p["vres_up"]]
        out_specs = blk(GROUP_W)
        out_shape = jax.ShapeDtypeStruct((SEQ, GROUP_W), bf16)
    else:
        out_specs = [blk(GROUP_W), blk(GROUP_W)]
        out_shape = [jax.ShapeDtypeStruct((SEQ, GROUP_W), bf16),
                     jax.ShapeDtypeStruct((SEQ, GROUP_W), f32)]
    scratch = [pltpu.VMEM((8, 1024), f32), pltpu.VMEM((N_HEADS, HEAD_DIM, HEAD_DIM), f32)]
    scratch += [pltpu.VMEM((RW_TB, GROUP_W), f32) for _ in range(7)]
    out = pl.pallas_call(
        functools.partial(_rwkv_kernel, has_vres=has_vres),
        grid=(SEQ // RW_TB,),
        in_specs=in_specs,
        out_specs=out_specs,
        out_shape=out_shape,
        scratch_shapes=scratch,
        compiler_params=pltpu.CompilerParams(dimension_semantics=("arbitrary",),
                                             vmem_limit_bytes=VMEM_LIMIT),
        name="rwkv7",
    )(*args)
    if has_vres:
        return out, v_first
    return out[0], out[1]


GD_TB = 256
GD_C = 64


def _gdn_kernel(c_ref, cw_ref, alog_ref, dtb_ref, alogc_ref, dtbc_ref, ng_ref, y_ref,
                carry_ref, s_ref, q_s, k_s, v_s, be_s, g_s, o_s):
    @pl.when(pl.program_id(0) == 0)
    def _():
        carry_ref[...] = jnp.zeros_like(carry_ref)
        s_ref[...] = jnp.zeros_like(s_ref)

    ones_bd = _group_ones(GROUP_W, HEAD_DIM)
    raw = c_ref[:, 0:768]
    carry = carry_ref[...]
    conv = raw * cw_ref[3:4, :]
    for s in range(1, 4):
        conv = conv + _shift_rows(raw, carry, s) * cw_ref[3 - s:4 - s, :]
    carry_ref[...] = raw[GD_TB - 8:GD_TB]
    qkv = conv * _sigmoid(conv)
    q = qkv[:, 0:256]
    k = qkv[:, 256:512]
    q_s[...] = q * lax.rsqrt(_segsum(q * q, ones_bd) + 1e-6) * (HEAD_DIM ** -0.5)
    k_s[...] = k * lax.rsqrt(_segsum(k * k, ones_bd) + 1e-6)
    v_s[...] = qkv[:, 512:768]
    small = c_ref[:, 1024:1152]
    er, ec = _iota((128, GROUP_W), 0), _iota((128, GROUP_W), 1)
    b_exp = _dot(small, (er == ec // HEAD_DIM).astype(f32), hi=True)
    a_exp = _dot(small, (er == ec // HEAD_DIM + N_HEADS).astype(f32), hi=True)
    be_s[...] = _sigmoid(b_exp)
    g_s[...] = -jnp.exp(alog_ref[...]) * _softplus(a_exp + dtb_ref[...])

    c = GD_C
    ri, ci = _iota((c, c), 0), _iota((c, c), 1)
    tril_incl = (ri >= ci).astype(f32)
    low_strict = ri > ci
    low_incl = ri >= ci
    masks = _tri_masks(c)
    sel8 = (_iota((8, 128), 0) == _iota((8, 128), 1)).astype(f32)
    g_rows = -jnp.exp(alogc_ref[...]) * _softplus(_dot(sel8, small, NT, hi=True) + dtbc_ref[...])
    tj, ti = _iota((GD_TB, GD_TB), 0), _iota((GD_TB, GD_TB), 1)
    gc_rows = _dot(g_rows, ((tj // c == ti // c) & (tj <= ti)).astype(f32), hi=True)

    nchunk = GD_TB // c
    items = [(n, h) for n in range(nchunk) for h in range(N_HEADS)]
    hsl = lambda h: slice(h * HEAD_DIM, (h + 1) * HEAD_DIM)
    pre = []
    for n in range(nchunk):
        rows = slice(n * c, (n + 1) * c)
        gc = _dot(tril_incl, g_s[rows, :], hi=True)
        glast = gc[c - 1:c, :]
        egc = jnp.exp(gc)
        kc, bc = k_s[rows, :], be_s[rows, :]
        kb = kc * bc
        pre.append(dict(gc=gc, k=kc, kb=kb, q=q_s[rows, :], vb=v_s[rows, :] * bc, kbe=kb * egc,
                        qe=q_s[rows, :] * egc, kd=kc * jnp.exp(glast - gc), eglast=jnp.exp(glast)))
    dms = [jnp.exp(jnp.where(low_incl, pre[n]["gc"][:, hsl(h)]
                             - gc_rows[N_HEADS + h:N_HEADS + h + 1, n * c:(n + 1) * c], NEG))
           for n, h in items]
    aqs = [_dot(jnp.concatenate([pre[n]["kb"][:, hsl(h)], pre[n]["q"][:, hsl(h)]], axis=0),
                pre[n]["k"][:, hsl(h)], NT) for n, h in items]
    tms = _tri_inv([jnp.where(low_strict, aq[0:c] * dm, 0.0) for aq, dm in zip(aqs, dms)], masks)
    us = [_dot(tm, pre[n]["vb"][:, hsl(h)]) for tm, (n, h) in zip(tms, items)]
    ws = [_dot(tm, pre[n]["kbe"][:, hsl(h)]) for tm, (n, h) in zip(tms, items)]
    state = [s_ref[h] for h in range(N_HEADS)]
    for n in range(nchunk):
        p = pre[n]
        idx = [n * N_HEADS + h for h in range(N_HEADS)]
        wss = [_dot(jnp.concatenate([ws[i], p["qe"][:, hsl(h)]], axis=0), state[h])
               for h, i in enumerate(idx)]
        vns = [us[i] - wss[h][0:c] for h, i in enumerate(idx)]
        upd = [_dot(p["kd"][:, hsl(h)], vns[h], TN) for h in range(N_HEADS)]
        state = [state[h] * p["eglast"][:, hsl(h)] + upd[h] for h in range(N_HEADS)]
        os_ = [wss[h][c:] + _dot(aqs[i][c:] * dms[i], vns[h]) for h, i in enumerate(idx)]
        o_s[n * c:(n + 1) * c, :] = jnp.concatenate(os_, axis=1)
    for h in range(N_HEADS):
        s_ref[h] = state[h]

    o = o_s[...]
    ms = _segsum(o * o, ones_bd) * (1.0 / HEAD_DIM)
    z = c_ref[:, 768:1024]
    y_ref[...] = (o * lax.rsqrt(ms + EPS) * ng_ref[...] * (z * _sigmoid(z))).astype(bf16)


def _gdn(c_b, p):
    row = lambda n: pl.BlockSpec((1, n), lambda i: (0, 0))
    scratch = [pltpu.VMEM((8, 768), f32), pltpu.VMEM((N_HEADS, HEAD_DIM, HEAD_DIM), f32)]
    scratch += [pltpu.VMEM((GD_TB, GROUP_W), f32) for _ in range(6)]
    return pl.pallas_call(
        _gdn_kernel,
        grid=(SEQ // GD_TB,),
        in_specs=[pl.BlockSpec((GD_TB, NB), lambda i: (i, 0)),
                  pl.BlockSpec((4, 768), lambda i: (0, 0)), row(256), row(256),
                  pl.BlockSpec((8, 1), lambda i: (0, 0)), pl.BlockSpec((8, 1), lambda i: (0, 0)),
                  row(256)],
        out_specs=pl.BlockSpec((GD_TB, GROUP_W), lambda i: (i, 0)),
        out_shape=jax.ShapeDtypeStruct((SEQ, GROUP_W), bf16),
        scratch_shapes=scratch,
        compiler_params=pltpu.CompilerParams(dimension_semantics=("arbitrary",),
                                             vmem_limit_bytes=VMEM_LIMIT),
        name="gdn",
    )(c_b, p["conv_w"], p["a_log"], p["dt_bias"], p["a_log_col"], p["dt_bias_col"], p["norm_g"])


GL_TB = 256
GL_C = 16


def _gla_kernel(c_ref, gup_ref, gb_ref, ng_ref, y_ref, st_ref, q_s, k_s, la_s, o_s):
    @pl.when(pl.program_id(0) == 0)
    def _():
        st_ref[...] = jnp.zeros_like(st_ref)

    q_s[...] = c_ref[:, 0:128] * (GLA_HEAD_K ** -0.5)
    k_s[...] = c_ref[:, 128:256]
    pre = _dot(c_ref[:, 768:896], gup_ref[...]) + gb_ref[...]
    la_s[...] = -_softplus(-pre) * (1.0 / 16.0)

    c = GL_C
    tril_incl = (_iota((c, c), 0) >= _iota((c, c), 1)).astype(f32)
    ri = _iota((c, GLA_KEY), 0)
    ind_e = (_iota((GLA_KEY, GROUP_W), 0) // GLA_HEAD_K == _iota((GLA_KEY, GROUP_W), 1) // HEAD_DIM).astype(bf16)
    bd_mask = (_iota((GROUP_W, GLA_KEY), 0) // HEAD_DIM == _iota((GROUP_W, GLA_KEY), 1) // GLA_HEAD_K).astype(f32)

    def chunk(n, _):
        rows = pl.ds(pl.multiple_of(n * c, c), c)
        qc, kc, la = q_s[rows, :], k_s[rows, :], la_s[rows, :]
        vc = c_ref[rows, 256:512]
        b = _dot(tril_incl, la, hi=True)
        blast = b[c - 1:c, :]
        terms = []
        for j in range(c):
            e = jnp.exp(jnp.where(ri >= j, b - b[j:j + 1, :], NEG))
            terms.append(qc * (kc[j:j + 1, :] * e))
        sx = jnp.dot(jnp.concatenate(terms, axis=0).astype(bf16), ind_e, preferred_element_type=f32)
        st = st_ref[...]
        o = _dot(qc * jnp.exp(b), st, NT)
        for j in range(c):
            o = o + sx[j * c:(j + 1) * c, :] * vc[j:j + 1, :]
        o_s[rows, :] = o
        st_ref[...] = st * jnp.exp(blast) + _dot(vc, kc * jnp.exp(blast - b), TN) * bd_mask
        return 0

    lax.fori_loop(0, GL_TB // c, chunk, 0)

    o = o_s[...]
    ms = _segsum(o * o, _group_ones(GROUP_W, HEAD_DIM)) * (1.0 / HEAD_DIM)
    gate = c_ref[:, 512:768]
    y_ref[...] = (o * lax.rsqrt(ms + EPS) * ng_ref[...] * (gate * _sigmoid(gate))).astype(bf16)


def _gla(c_c, p):
    row = lambda n: pl.BlockSpec((1, n), lambda i: (0, 0))
    return pl.pallas_call(
        _gla_kernel,
        grid=(SEQ // GL_TB,),
        in_specs=[pl.BlockSpec((GL_TB, NC), lambda i: (i, 0)),
                  pl.BlockSpec((128, 128), lambda i: (0, 0)), row(128), row(256)],
        out_specs=pl.BlockSpec((GL_TB, GROUP_W), lambda i: (i, 0)),
        out_shape=jax.ShapeDtypeStruct((SEQ, GROUP_W), bf16),
        scratch_shapes=[pltpu.VMEM((GROUP_W, GLA_KEY), f32),
                        pltpu.VMEM((GL_TB, GLA_KEY), f32), pltpu.VMEM((GL_TB, GLA_KEY), f32),
                        pltpu.VMEM((GL_TB, GLA_KEY), f32), pltpu.VMEM((GL_TB, GROUP_W), f32)],
        compiler_params=pltpu.CompilerParams(dimension_semantics=("arbitrary",),
                                             vmem_limit_bytes=VMEM_LIMIT),
        name="gla",
    )(c_c, p["gk_up"], p["gk_bias"], p["norm_g"])


SG_TB = 512
SG_C = 128


def _sgu_kernel(c_ref, lg_ref, lb_ref, w_ref, bias_ref, y_ref):
    x = c_ref[...]
    gx = 0.5 * x * (1.0 + jnp.tanh(0.7978845608028654 * (x + 0.044715 * x * x * x)))
    u = gx[:, 0:256]
    v = gx[:, 256:512]
    mu = jnp.mean(v, axis=-1, keepdims=True)
    vc = v - mu
    var = jnp.mean(vc * vc, axis=-1, keepdims=True)
    v = vc * lax.rsqrt(var + 1e-5) * lg_ref[...] + lb_ref[...]
    wr, wc = _iota((SG_C, 4 * SG_C), 0), _iota((SG_C, 4 * SG_C), 1)
    w = jnp.where(wc % SG_C <= wr, w_ref[...], 0.0).astype(bf16)
    lane_g = _iota((SG_C, GROUP_W), 1) // HEAD_DIM
    outs = []
    for n in range(SG_TB // SG_C):
        vn = v[n * SG_C:(n + 1) * SG_C, :]
        vst = jnp.concatenate([jnp.where(lane_g == g, vn, 0.0) for g in range(4)], axis=0)
        outs.append(jnp.dot(w, vst.astype(bf16), preferred_element_type=f32) + bias_ref[...])
    y_ref[...] = (u * jnp.concatenate(outs, axis=0)).astype(bf16)


def _sgu(c_d, p):
    row = lambda n: pl.BlockSpec((1, n), lambda i: (0, 0))
    return pl.pallas_call(
        _sgu_kernel,
        grid=(SEQ // SG_TB,),
        in_specs=[pl.BlockSpec((SG_TB, ND), lambda i: (i, 0)), row(256), row(256),
                  pl.BlockSpec((SG_C, 4 * SG_C), lambda i: (0, 0)),
                  pl.BlockSpec((SG_C, GROUP_W), lambda i: (0, 0))],
        out_specs=pl.BlockSpec((SG_TB, GROUP_W), lambda i: (i, 0)),
        out_shape=jax.ShapeDtypeStruct((SEQ, GROUP_W), bf16),
        compiler_params=pltpu.CompilerParams(dimension_semantics=("arbitrary",),
                                             vmem_limit_bytes=VMEM_LIMIT),
        name="sgu",
    )(c_d, p["ln_g"], p["ln_b"], p["w_cat"], p["bias_tile"])


FF_TM = 512
FF_TF = 1024


def _outffn_kernel(*refs, final):
    if final:
        (x_ref, ya_ref, yb_ref, yc_ref, yd_ref, wo_ref, g_ref, wu_ref, wd_ref, gf_ref,
         o_ref, acc_ref, h_ref) = refs
    else:
        (x_ref, ya_ref, yb_ref, yc_ref, yd_ref, wo_ref, g_ref, wu_ref, wd_ref,
         o_ref, acc_ref, h_ref) = refs
    kf = pl.program_id(1)

    @pl.when(kf == 0)
    def _():
        x1 = x_ref[...]
        for m, y_ref in enumerate((ya_ref, yb_ref, yc_ref, yd_ref)):
            x1 = x1 + jnp.dot(y_ref[...], wo_ref[m * GROUP_W:(m + 1) * GROUP_W, :],
                              preferred_element_type=f32)
        acc_ref[...] = x1
        ms = jnp.mean(x1 * x1, axis=-1, keepdims=True)
        h_ref[...] = (x1 * lax.rsqrt(ms + EPS) * g_ref[...]).astype(bf16)

    hid = jnp.dot(h_ref[...], wu_ref[...], preferred_element_type=f32)
    hid = jnp.maximum(hid, 0.0)
    acc_ref[...] += jnp.dot((hid * hid).astype(bf16), wd_ref[...], preferred_element_type=f32)

    @pl.when(kf == pl.num_programs(1) - 1)
    def _():
        x2 = acc_ref[...]
        if final:
            ms = jnp.mean(x2 * x2, axis=-1, keepdims=True)
            x2 = x2 * lax.rsqrt(ms + EPS) * gf_ref[...]
        o_ref[...] = x2


def _out_ffn(x, ys, w_out, g, w_up, w_down, g_final):
    final = g_final is not None
    row = pl.BlockSpec((1, D_MODEL), lambda i, k: (0, 0))
    yspec = pl.BlockSpec((FF_TM, GROUP_W), lambda i, k: (i, 0))
    in_specs = [pl.BlockSpec((FF_TM, D_MODEL), lambda i, k: (i, 0)), yspec, yspec, yspec, yspec,
                pl.BlockSpec((D_MODEL, D_MODEL), lambda i, k: (0, 0)), row,
                pl.BlockSpec((D_MODEL, FF_TF), lambda i, k: (0, k)),
                pl.BlockSpec((FF_TF, D_MODEL), lambda i, k: (k, 0))]
    args = [x, *ys, w_out, g, w_up, w_down]
    if final:
        in_specs.append(row)
        args.append(g_final)
    return pl.pallas_call(
        functools.partial(_outffn_kernel, final=final),
        grid=(SEQ // FF_TM, D_FF // FF_TF),
        in_specs=in_specs,
        out_specs=pl.BlockSpec((FF_TM, D_MODEL), lambda i, k: (i, 0)),
        out_shape=jax.ShapeDtypeStruct((SEQ, D_MODEL), f32),
        scratch_shapes=[pltpu.VMEM((FF_TM, D_MODEL), f32), pltpu.VMEM((FF_TM, D_MODEL), bf16)],
        compiler_params=pltpu.CompilerParams(dimension_semantics=("arbitrary", "arbitrary"),
                                             vmem_limit_bytes=VMEM_LIMIT),
        name="out_ffn",
    )(*args)


def _pad_cols(w, n):
    return jnp.pad(w, ((0, 0), (0, n - w.shape[1])))


def _pad_rows(w, top, total):
    return jnp.pad(w, ((top, total - top - w.shape[0]), (0, 0)))


def kernel(x, w_in, w_out, norm_mix_g, norm_ffn_g, norm_final_g, rwkv_mu, rwkv_w0, rwkv_w_up, rwkv_a0, rwkv_a_up, rwkv_g_up, rwkv_k_k, rwkv_k_a, rwkv_r_k, rwkv_lnx_w, rwkv_lnx_b, rwkv_v0, rwkv_vres_down, rwkv_vres_up, gdn_conv_w, gdn_a_log, gdn_dt_bias, gdn_norm_g, gla_gk_up, gla_gk_bias, gla_norm_g, sgu_ln_g, sgu_ln_b, sgu_w_s, sgu_b_s, ffn_w_up, ffn_w_down):
    depth = w_in.shape[0]
    xx = x[0]
    v_first = None
    r2 = lambda a: a.reshape(1, -1)
    per_head = lambda a: jnp.repeat(a, HEAD_DIM).reshape(1, -1)
    for l in range(depth):
        wl = w_in[l]
        w_a = wl[:, 0:1024]
        if l > 0:
            w_a = jnp.concatenate([w_a, rwkv_vres_down[l - 1]], axis=1)
        w_comb = jnp.concatenate([_pad_cols(w_a, NA), _pad_cols(wl[:, 1024:2056], NB),
                                  _pad_cols(wl[:, 2056:2840], NC), wl[:, 2840:3352]], axis=1).astype(bf16)
        c_a, c_b, c_c, c_d = _in_proj(xx, r2(norm_mix_g[l]), w_comb)

        pa = dict(mu=r2(rwkv_mu[l]), w0=r2(rwkv_w0[l]), w_up=_pad_rows(rwkv_w_up[l], 0, 128),
                  a0=r2(rwkv_a0[l]), a_up=_pad_rows(rwkv_a_up[l], 64, 128), g_up=rwkv_g_up[l],
                  k_k=r2(rwkv_k_k[l]), k_a=r2(rwkv_k_a[l]), r_k=r2(rwkv_r_k[l]),
                  lnx_w=r2(rwkv_lnx_w[l]), lnx_b=r2(rwkv_lnx_b[l]))
        if l > 0:
            pa.update(v0=r2(rwkv_v0[l - 1]), vres_up=_pad_rows(rwkv_vres_up[l - 1], 0, 128))
        y_a, v_first = _rwkv(c_a, v_first, pa)

        y_b = _gdn(c_b, dict(conv_w=gdn_conv_w[l], a_log=per_head(gdn_a_log[l]),
                             dt_bias=per_head(gdn_dt_bias[l]),
                             a_log_col=jnp.pad(gdn_a_log[l], (N_HEADS, 0)).reshape(8, 1),
                             dt_bias_col=jnp.pad(gdn_dt_bias[l], (N_HEADS, 0)).reshape(8, 1), norm_g=r2(jnp.tile(gdn_norm_g[l], N_HEADS))))
        y_c = _gla(c_c, dict(gk_up=_pad_rows(gla_gk_up[l], 0, 128), gk_bias=r2(gla_gk_bias[l]),
                             norm_g=r2(jnp.tile(gla_norm_g[l], N_HEADS))))
        y_d = _sgu(c_d, dict(ln_g=r2(sgu_ln_g[l]), ln_b=r2(sgu_ln_b[l]),
                             w_cat=sgu_w_s[l].transpose(1, 0, 2).reshape(SG_C, 4 * SG_C),
                             bias_tile=jnp.repeat(sgu_b_s[l].T, HEAD_DIM, axis=1)))
        xx = _out_ffn(xx, (y_a, y_b, y_c, y_d), w_out[l].astype(bf16), r2(norm_ffn_g[l]),
                      ffn_w_up[l].astype(bf16), ffn_w_down[l].astype(bf16),
                      r2(norm_final_g) if l == depth - 1 else None)
    return xx[None]
```

```python
import functools

import jax
import jax.numpy as jnp
from jax import lax
from jax.experimental import pallas as pl
from jax.experimental.pallas import tpu as pltpu

f32 = jnp.float32
bf16 = jnp.bfloat16
HI = lax.Precision.HIGHEST

SEQ = 16384
D_MODEL = 1024
GROUP_W = 256
HEAD_DIM = 64
N_HEADS = 4
GLA_KEY = 128
GLA_HEAD_K = 32
D_FF = 4096
EPS = 1e-6
RWKV_GN_EPS = 64e-5
NEG = -1e30

NA, NB, NC, ND = 1152, 1152, 896, 512
N_PAD = NA + NB + NC + ND

VMEM_LIMIT = 56 * 1024 * 1024

NN = (((1,), (0,)), ((), ()))
NT = (((1,), (1,)), ((), ()))
TN = (((0,), (0,)), ((), ()))


def _dot(a, b, dims=NN):
    return lax.dot_general(a.astype(bf16), b.astype(bf16), dims, preferred_element_type=f32)


def _iota(shape, axis):
    return lax.broadcasted_iota(jnp.int32, shape, axis)


def _segsum(x, ones_bd):
    hi = x.astype(bf16)
    lo = (x - hi.astype(f32)).astype(bf16)
    return (jnp.dot(hi, ones_bd, preferred_element_type=f32)
            + jnp.dot(lo, ones_bd, preferred_element_type=f32))


def _split3(x):
    hi = x.astype(bf16)
    r1 = x - hi.astype(f32)
    mid = r1.astype(bf16)
    lo = (r1 - mid.astype(f32)).astype(bf16)
    return hi, mid, lo


def _dot_sel_lhs(sel, x, dims=NN):
    return sum(lax.dot_general(sel, t, dims, preferred_element_type=f32) for t in _split3(x))


def _dot_sel_rhs(x, sel, dims=NN):
    return sum(lax.dot_general(t, sel, dims, preferred_element_type=f32) for t in _split3(x))


def _group_ones(n, width):
    return (_iota((n, n), 0) // width == _iota((n, n), 1) // width).astype(bf16)


def _sigmoid(x):
    return 1.0 / (1.0 + jnp.exp(-x))


def _softplus(x):
    return jnp.maximum(x, 0.0) + jnp.log1p(jnp.exp(-jnp.abs(x)))


def _shift_rows(x, carry, s):
    xs = pltpu.roll(x, s, 0)
    fix = pltpu.roll(carry, s, 0)
    first = jnp.where(_iota(carry.shape, 0) < s, fix, xs[0:8])
    return jnp.concatenate([first, xs[8:]], axis=0)


def _tri_masks(c):
    ri, ci = _iota((c, c), 0), _iota((c, c), 1)
    eye = (ri == ci).astype(f32)
    m16 = (ri // 16 == ci // 16).astype(f32)
    mo1 = ((ri // 32 == ci // 32) & (ri // 16 == ci // 16 + 1)).astype(f32)
    mo2 = ((ri // 32 == 1) & (ci // 32 == 0)).astype(f32)
    return eye, m16, mo1, mo2


def _tri_inv(lms, masks):
    eye, m16, mo1, mo2 = masks
    ps = [-(lm * m16) for lm in lms]
    ts = [eye + p for p in ps]
    for _ in range(3):
        ps = [_dot(p, p) for p in ps]
        ts = [t + _dot(t, p) for t, p in zip(ts, ps)]
    for mo in (mo1, mo2):
        xs = [_dot(lm * mo, t) for lm, t in zip(lms, ts)]
        ts = [t - _dot(t, x) for t, x in zip(ts, xs)]
    return ts


def _inproj_kernel(x_ref, g_ref, w_ref, oa_ref, ob_ref, oc_ref, od_ref):
    x = x_ref[...]
    ms = jnp.mean(x * x, axis=-1, keepdims=True)
    h = (x * lax.rsqrt(ms + EPS) * g_ref[...]).astype(bf16)
    off = 0
    for o_ref, n in ((oa_ref, NA), (ob_ref, NB), (oc_ref, NC), (od_ref, ND)):
        o_ref[...] = jnp.dot(h, w_ref[:, off:off + n], preferred_element_type=f32)
        off += n


def _in_proj(x, g, w):
    tm = 512
    return pl.pallas_call(
        _inproj_kernel,
        grid=(SEQ // tm,),
        in_specs=[pl.BlockSpec((tm, D_MODEL), lambda i: (i, 0)),
                  pl.BlockSpec((1, D_MODEL), lambda i: (0, 0)),
                  pl.BlockSpec((D_MODEL, N_PAD), lambda i: (0, 0))],
        out_specs=[pl.BlockSpec((tm, n), lambda i: (i, 0)) for n in (NA, NB, NC, ND)],
        out_shape=[jax.ShapeDtypeStruct((SEQ, n), f32) for n in (NA, NB, NC, ND)],
        compiler_params=pltpu.CompilerParams(dimension_semantics=("arbitrary",),
                                             vmem_limit_bytes=VMEM_LIMIT),
        name="in_proj",
    )(x, g, w)


RW_TB = 256
RW_C = 64


def _rwkv_kernel(*refs, has_vres):
    if has_vres:
        (c_ref, vf_ref, mu_ref, w0_ref, wup_ref, a0_ref, aup_ref, gup_ref, kk_ref, ka_ref, rk_ref,
         lw_ref, lb_ref, v0_ref, vup_ref, y_ref,
         carry_ref, s_ref, r_s, k_s, v_s, lw_s, al_s, be_s, y_s) = refs
    else:
        (c_ref, mu_ref, w0_ref, wup_ref, a0_ref, aup_ref, gup_ref, kk_ref, ka_ref, rk_ref,
         lw_ref, lb_ref, y_ref, vf_out_ref,
         carry_ref, s_ref, r_s, k_s, v_s, lw_s, al_s, be_s, y_s) = refs

    @pl.when(pl.program_id(0) == 0)
    def _():
        carry_ref[...] = jnp.zeros_like(carry_ref)
        s_ref[...] = jnp.zeros_like(s_ref)

    ones_bd = _group_ones(GROUP_W, HEAD_DIM)
    x = c_ref[:, 0:1024]
    x_prev = _shift_rows(x, carry_ref[...], 1)
    carry_ref[...] = x[RW_TB - 8:RW_TB]
    xs = x + (x_prev - x) * mu_ref[...]
    r = xs[:, 0:256]
    k = xs[:, 256:512]
    v = xs[:, 512:768]
    lora = xs[:, 768:896]
    w_pre = w0_ref[...] + _dot(jnp.tanh(lora), wup_ref[...])
    lw = -jnp.exp(-_softplus(-w_pre) - 0.5)
    a = _sigmoid(a0_ref[...] + _dot(lora, aup_ref[...]))
    g = _dot(_sigmoid(xs[:, 896:1024]), gup_ref[...])
    if has_vres:
        mix = _sigmoid(v0_ref[...] + _dot(c_ref[:, 1024:1152], vup_ref[...]))
        v = v + (vf_ref[...] - v) * mix
    else:
        vf_out_ref[...] = v
    kk = k * kk_ref[...]
    kk = kk * lax.rsqrt(_segsum(kk * kk, ones_bd) + 1e-24)
    k = k * (1.0 + (a - 1.0) * ka_ref[...])
    r_s[...] = r
    k_s[...] = k
    v_s[...] = v
    lw_s[...] = lw
    al_s[...] = -kk
    be_s[...] = kk * a

    c = RW_C
    ri, ci = _iota((c, c), 0), _iota((c, c), 1)
    tril_incl = (ri >= ci).astype(bf16)
    low_strict = ri > ci
    low_incl = ri >= ci
    masks = _tri_masks(c)

    nchunk = RW_TB // c
    items = [(n, h) for n in range(nchunk) for h in range(N_HEADS)]
    hsl = lambda h: slice(h * HEAD_DIM, (h + 1) * HEAD_DIM)
    pre = []
    for n in range(nchunk):
        rows = slice(n * c, (n + 1) * c)
        lwc = lw_s[rows, :]
        lc = _dot_sel_lhs(tril_incl, lwc)
        llast = lc[c - 1:c, :]
        e_out = jnp.exp(-lc)
        e_rest = jnp.exp(llast - lc)
        kc, bec = k_s[rows, :], be_s[rows, :]
        pre.append(dict(rt=r_s[rows, :] * jnp.exp(lc), at=al_s[rows, :] * jnp.exp(lc - lwc),
                        bt=bec * e_out, kt=kc * e_out, bw=bec * e_rest, kw=kc * e_rest,
                        v=v_s[rows, :], dlast=jnp.exp(llast)))
    ms = [_dot(jnp.concatenate([pre[n]["at"][:, hsl(h)], pre[n]["rt"][:, hsl(h)]], axis=0),
               jnp.concatenate([pre[n]["bt"][:, hsl(h)], pre[n]["kt"][:, hsl(h)]], axis=0), NT)
          for n, h in items]
    tms = _tri_inv([jnp.where(low_strict, -m[0:c, 0:c], 0.0) for m in ms], masks)
    avs = [_dot(jnp.concatenate([jnp.where(low_strict, m[0:c, c:], 0.0),
                                 jnp.where(low_incl, m[c:, c:], 0.0)], axis=0), pre[n]["v"][:, hsl(h)])
           for m, (n, h) in zip(ms, items)]
    tas = [_dot(tm, pre[n]["at"][:, hsl(h)]) for tm, (n, h) in zip(tms, items)]
    tvs = [_dot(tm, av[0:c]) for tm, av in zip(tms, avs)]
    state = [s_ref[h] for h in range(N_HEADS)]
    for n in range(nchunk):
        p = pre[n]
        idx = [n * N_HEADS + h for h in range(N_HEADS)]
        sas = [_dot(jnp.concatenate([tas[i], p["rt"][:, hsl(h)]], axis=0), state[h], NT)
               for h, i in enumerate(idx)]
        us = [sa[0:c] + tvs[i] for sa, i in zip(sas, idx)]
        upd = [_dot(jnp.concatenate([u, p["v"][:, hsl(h)]], axis=0),
                    jnp.concatenate([p["bw"][:, hsl(h)], p["kw"][:, hsl(h)]], axis=0), TN)
               for h, u in enumerate(us)]
        state = [state[h] * p["dlast"][:, hsl(h)] + upd[h] for h in range(N_HEADS)]
        ys = [sas[h][c:] + _dot(jnp.where(low_incl, ms[i][c:, 0:c], 0.0), us[h]) + avs[i][c:]
              for h, i in enumerate(idx)]
        y_s[n * c:(n + 1) * c, :] = jnp.concatenate(ys, axis=1)
    for h in range(N_HEADS):
        s_ref[h] = state[h]

    y = y_s[...]
    inv_d = 1.0 / HEAD_DIM
    mean = _segsum(y, ones_bd) * inv_d
    yc = y - mean
    var = _segsum(yc * yc, ones_bd) * inv_d
    y = yc * lax.rsqrt(var + RWKV_GN_EPS) * lw_ref[...] + lb_ref[...]
    y = y + _segsum(r * k * rk_ref[...], ones_bd) * v
    y_ref[...] = (y * g).astype(bf16)


def _rwkv(c_a, v_first, p):
    has_vres = v_first is not None
    row = lambda n: pl.BlockSpec((1, n), lambda i: (0, 0))
    full = lambda a, b: pl.BlockSpec((a, b), lambda i: (0, 0))
    blk = lambda n: pl.BlockSpec((RW_TB, n), lambda i: (i, 0))
    in_specs = [blk(NA)]
    args = [c_a]
    if has_vres:
        in_specs.append(blk(GROUP_W))
        args.append(v_first)
    in_specs += [row(1024), row(256), full(128, 256), row(256), full(128, 256), full(128, 256),
                 row(256), row(256), row(256), row(256), row(256)]
    args += [p["mu"], p["w0"], p["w_up"], p["a0"], p["a_up"], p["g_up"], p["k_k"], p["k_a"], p["r_k"],
             p["lnx_w"], p["lnx_b"]]
    if has_vres:
        in_specs += [row(256), full(128, 256)]
        args += [p["v0"], p["vres_up"]]
        out_specs = blk(GROUP_W)
        out_shape = jax.ShapeDtypeStruct((SEQ, GROUP_W), bf16)
    else:
        out_specs = [blk(GROUP_W), blk(GROUP_W)]
        out_shape = [jax.ShapeDtypeStruct((SEQ, GROUP_W), bf16),
                     jax.ShapeDtypeStruct((SEQ, GROUP_W), f32)]
    scratch = [pltpu.VMEM((8, 1024), f32), pltpu.VMEM((N_HEADS, HEAD_DIM, HEAD_DIM), f32)]
    scratch += [pltpu.VMEM((RW_TB, GROUP_W), f32) for _ in range(7)]
    out = pl.pallas_call(
        functools.partial(_rwkv_kernel, has_vres=has_vres),
        grid=(SEQ // RW_TB,),
        in_specs=in_specs,
        out_specs=out_specs,
        out_shape=out_shape,
        scratch_shapes=scratch,
        compiler_params=pltpu.CompilerParams(dimension_semantics=("arbitrary",),
                                             vmem_limit_bytes=VMEM_LIMIT),
        name="rwkv7",
    )(*args)
    if has_vres:
        return out, v_first
    return out[0], out[1]


GD_TB = 256
GD_C = 64


def _gdn_kernel(c_ref, cw_ref, alog_ref, dtb_ref, alogc_ref, dtbc_ref, ng_ref, y_ref,
                carry_ref, s_ref, q_s, k_s, v_s, be_s, g_s, o_s):
    @pl.when(pl.program_id(0) == 0)
    def _():
        carry_ref[...] = jnp.zeros_like(carry_ref)
        s_ref[...] = jnp.zeros_like(s_ref)

    ones_bd = _group_ones(GROUP_W, HEAD_DIM)
    raw = c_ref[:, 0:768]
    carry = carry_ref[...]
    conv = raw * cw_ref[3:4, :]
    for s in range(1, 4):
        conv = conv + _shift_rows(raw, carry, s) * cw_ref[3 - s:4 - s, :]
    carry_ref[...] = raw[GD_TB - 8:GD_TB]
    qkv = conv * _sigmoid(conv)
    q = qkv[:, 0:256]
    k = qkv[:, 256:512]
    q_s[...] = q * lax.rsqrt(_segsum(q * q, ones_bd) + 1e-6) * (HEAD_DIM ** -0.5)
    k_s[...] = k * lax.rsqrt(_segsum(k * k, ones_bd) + 1e-6)
    v_s[...] = qkv[:, 512:768]
    small = c_ref[:, 1024:1152]
    er, ec = _iota((128, GROUP_W), 0), _iota((128, GROUP_W), 1)
    b_exp = _dot_sel_rhs(small, (er == ec // HEAD_DIM).astype(bf16))
    a_exp = _dot_sel_rhs(small, (er == ec // HEAD_DIM + N_HEADS).astype(bf16))
    be_s[...] = _sigmoid(b_exp)
    g_s[...] = -jnp.exp(alog_ref[...]) * _softplus(a_exp + dtb_ref[...])

    c = GD_C
    ri, ci = _iota((c, c), 0), _iota((c, c), 1)
    tril_incl = (ri >= ci).astype(bf16)
    low_strict = ri > ci
    low_incl = ri >= ci
    masks = _tri_masks(c)
    sel8 = (_iota((8, 128), 0) == _iota((8, 128), 1)).astype(bf16)
    g_rows = -jnp.exp(alogc_ref[...]) * _softplus(_dot_sel_lhs(sel8, small, NT) + dtbc_ref[...])
    tj, ti = _iota((GD_TB, GD_TB), 0), _iota((GD_TB, GD_TB), 1)
    gc_rows = _dot_sel_rhs(g_rows, ((tj // c == ti // c) & (tj <= ti)).astype(bf16))

    nchunk = GD_TB // c
    items = [(n, h) for n in range(nchunk) for h in range(N_HEADS)]
    hsl = lambda h: slice(h * HEAD_DIM, (h + 1) * HEAD_DIM)
    pre = []
    for n in range(nchunk):
        rows = slice(n * c, (n + 1) * c)
        gc = _dot_sel_lhs(tril_incl, g_s[rows, :])
        glast = gc[c - 1:c, :]
        egc = jnp.exp(gc)
        kc, bc = k_s[rows, :], be_s[rows, :]
        kb = kc * bc
        pre.append(dict(gc=gc, k=kc, kb=kb, q=q_s[rows, :], vb=v_s[rows, :] * bc, kbe=kb * egc,
                        qe=q_s[rows, :] * egc, kd=kc * jnp.exp(glast - gc), eglast=jnp.exp(glast)))
    dms = [jnp.exp(jnp.where(low_incl, pre[n]["gc"][:, hsl(h)]
                             - gc_rows[N_HEADS + h:N_HEADS + h + 1, n * c:(n + 1) * c], NEG))
           for n, h in items]
    aqs = [_dot(jnp.concatenate([pre[n]["kb"][:, hsl(h)], pre[n]["q"][:, hsl(h)]], axis=0),
                pre[n]["k"][:, hsl(h)], NT) for n, h in items]
    tms = _tri_inv([jnp.where(low_strict, aq[0:c] * dm, 0.0) for aq, dm in zip(aqs, dms)], masks)
    us = [_dot(tm, pre[n]["vb"][:, hsl(h)]) for tm, (n, h) in zip(tms, items)]
    ws = [_dot(tm, pre[n]["kbe"][:, hsl(h)]) for tm, (n, h) in zip(tms, items)]
    state = [s_ref[h] for h in range(N_HEADS)]
    for n in range(nchunk):
        p = pre[n]
        idx = [n * N_HEADS + h for h in range(N_HEADS)]
        wss = [_dot(jnp.concatenate([ws[i], p["qe"][:, hsl(h)]], axis=0), state[h])
               for h, i in enumerate(idx)]
        vns = [us[i] - wss[h][0:c] for h, i in enumerate(idx)]
        upd = [_dot(p["kd"][:, hsl(h)], vns[h], TN) for h in range(N_HEADS)]
        state = [state[h] * p["eglast"][:, hsl(h)] + upd[h] for h in range(N_HEADS)]
        os_ = [wss[h][c:] + _dot(aqs[i][c:] * dms[i], vns[h]) for h, i in enumerate(idx)]
        o_s[n * c:(n + 1) * c, :] = jnp.concatenate(os_, axis=1)
    for h in range(N_HEADS):
        s_ref[h] = state[h]

    o = o_s[...]
    ms = _segsum(o * o, ones_bd) * (1.0 / HEAD_DIM)
    z = c_ref[:, 768:1024]
    y_ref[...] = (o * lax.rsqrt(ms + EPS) * ng_ref[...] * (z * _sigmoid(z))).astype(bf16)


def _gdn(c_b, p):
    row = lambda n: pl.BlockSpec((1, n), lambda i: (0, 0))
    scratch = [pltpu.VMEM((8, 768), f32), pltpu.VMEM((N_HEADS, HEAD_DIM, HEAD_DIM), f32)]
    scratch += [pltpu.VMEM((GD_TB, GROUP_W), f32) for _ in range(6)]
    return pl.pallas_call(
        _gdn_kernel,
        grid=(SEQ // GD_TB,),
        in_specs=[pl.BlockSpec((GD_TB, NB), lambda i: (i, 0)),
                  pl.BlockSpec((4, 768), lambda i: (0, 0)), row(256), row(256),
                  pl.BlockSpec((8, 1), lambda i: (0, 0)), pl.BlockSpec((8, 1), lambda i: (0, 0)),
                  row(256)],
        out_specs=pl.BlockSpec((GD_TB, GROUP_W), lambda i: (i, 0)),
        out_shape=jax.ShapeDtypeStruct((SEQ, GROUP_W), bf16),
        scratch_shapes=scratch,
        compiler_params=pltpu.CompilerParams(dimension_semantics=("arbitrary",),
                                             vmem_limit_bytes=VMEM_LIMIT),
        name="gdn",
    )(c_b, p["conv_w"], p["a_log"], p["dt_bias"], p["a_log_col"], p["dt_bias_col"], p["norm_g"])


GL_TB = 256
GL_C = 16


def _gla_kernel(c_ref, gup_ref, gb_ref, ng_ref, y_ref, st_ref, sx_s, o_s):
    @pl.when(pl.program_id(0) == 0)
    def _():
        st_ref[...] = jnp.zeros_like(st_ref)

    tb, c = GL_TB, GL_C
    q = c_ref[:, 0:128] * (GLA_HEAD_K ** -0.5)
    k = c_ref[:, 128:256]
    pre = _dot(c_ref[:, 768:896], gup_ref[...]) + gb_ref[...]
    la = -_softplus(-pre) * (1.0 / 16.0)
    tj, ti = _iota((tb, tb), 0), _iota((tb, tb), 1)
    b = _dot_sel_lhs(((tj // c == ti // c) & (ti <= tj)).astype(bf16), la)
    qi = q * jnp.exp(b)

    ri = _iota((c, GLA_KEY), 0)
    ind_e = (_iota((GLA_KEY, GROUP_W), 0) // GLA_HEAD_K == _iota((GLA_KEY, GROUP_W), 1) // HEAD_DIM).astype(bf16)
    bd_mask = (_iota((GROUP_W, GLA_KEY), 0) // HEAD_DIM == _iota((GROUP_W, GLA_KEY), 1) // GLA_HEAD_K).astype(f32)
    nchunk = tb // c
    terms, blasts, upds = [], [], []
    for n in range(nchunk):
        rows = slice(n * c, (n + 1) * c)
        bn, qn, kn = b[rows], q[rows], k[rows]
        for j in range(c):
            e = jnp.exp(jnp.where(ri >= j, bn - bn[j:j + 1, :], NEG))
            terms.append((qn * (kn[j:j + 1, :] * e)).astype(bf16))
        blasts.append(bn[c - 1:c, :])
    sx_s[...] = jnp.dot(jnp.concatenate(terms, axis=0), ind_e, preferred_element_type=f32)
    for n in range(nchunk):
        rows = slice(n * c, (n + 1) * c)
        upds.append(_dot(c_ref[rows, 256:512], k[rows] * jnp.exp(blasts[n] - b[rows]), TN) * bd_mask)
    st = st_ref[...]
    for n in range(nchunk):
        rows = slice(n * c, (n + 1) * c)
        o = _dot(qi[rows], st, NT)
        st = st * jnp.exp(blasts[n]) + upds[n]
        for j in range(c):
            r0 = (n * c + j) * c
            o = o + sx_s[r0:r0 + c, :] * c_ref[n * c + j:n * c + j + 1, 256:512]
        o_s[rows, :] = o
    st_ref[...] = st

    o = o_s[...]
    ms = _segsum(o * o, _group_ones(GROUP_W, HEAD_DIM)) * (1.0 / HEAD_DIM)
    gate = c_ref[:, 512:768]
    y_ref[...] = (o * lax.rsqrt(ms + EPS) * ng_ref[...] * (gate * _sigmoid(gate))).astype(bf16)


def _gla(c_c, p):
    row = lambda n: pl.BlockSpec((1, n), lambda i: (0, 0))
    return pl.pallas_call(
        _gla_kernel,
        grid=(SEQ // GL_TB,),
        in_specs=[pl.BlockSpec((GL_TB, NC), lambda i: (i, 0)),
                  pl.BlockSpec((128, 128), lambda i: (0, 0)), row(128), row(256)],
        out_specs=pl.BlockSpec((GL_TB, GROUP_W), lambda i: (i, 0)),
        out_shape=jax.ShapeDtypeStruct((SEQ, GROUP_W), bf16),
        scratch_shapes=[pltpu.VMEM((GROUP_W, GLA_KEY), f32),
                        pltpu.VMEM((GL_TB * GL_C, GROUP_W), f32), pltpu.VMEM((GL_TB, GROUP_W), f32)],
        compiler_params=pltpu.CompilerParams(dimension_semantics=("arbitrary",),
                                             vmem_limit_bytes=VMEM_LIMIT),
        name="gla",
    )(c_c, p["gk_up"], p["gk_bias"], p["norm_g"])


SG_TB = 512
SG_C = 128


def _sgu_kernel(c_ref, lg_ref, lb_ref, w_ref, bias_ref, y_ref):
    x = c_ref[...]
    gx = 0.5 * x * (1.0 + jnp.tanh(0.7978845608028654 * (x + 0.044715 * x * x * x)))
    u = gx[:, 0:256]
    v = gx[:, 256:512]
    mu = jnp.mean(v, axis=-1, keepdims=True)
    vc = v - mu
    var = jnp.mean(vc * vc, axis=-1, keepdims=True)
    v = vc * lax.rsqrt(var + 1e-5) * lg_ref[...] + lb_ref[...]
    wr, wc = _iota((SG_C, 4 * SG_C), 0), _iota((SG_C, 4 * SG_C), 1)
    w = jnp.where(wc % SG_C <= wr, w_ref[...], 0.0).astype(bf16)
    lane_g = _iota((SG_C, GROUP_W), 1) // HEAD_DIM
    outs = []
    for n in range(SG_TB // SG_C):
        vn = v[n * SG_C:(n + 1) * SG_C, :]
        vst = jnp.concatenate([jnp.where(lane_g == g, vn, 0.0) for g in range(4)], axis=0)
        outs.append(jnp.dot(w, vst.astype(bf16), preferred_element_type=f32) + bias_ref[...])
    y_ref[...] = (u * jnp.concatenate(outs, axis=0)).astype(bf16)


def _sgu(c_d, p):
    row = lambda n: pl.BlockSpec((1, n), lambda i: (0, 0))
    return pl.pallas_call(
        _sgu_kernel,
        grid=(SEQ // SG_TB,),
        in_specs=[pl.BlockSpec((SG_TB, ND), lambda i: (i, 0)), row(256), row(256),
                  pl.BlockSpec((SG_C, 4 * SG_C), lambda i: (0, 0)),
                  pl.BlockSpec((SG_C, GROUP_W), lambda i: (0, 0))],
        out_specs=pl.BlockSpec((SG_TB, GROUP_W), lambda i: (i, 0)),
        out_shape=jax.ShapeDtypeStruct((SEQ, GROUP_W), bf16),
        compiler_params=pltpu.CompilerParams(dimension_semantics=("arbitrary",),
                                             vmem_limit_bytes=VMEM_LIMIT),
        name="sgu",
    )(c_d, p["ln_g"], p["ln_b"], p["w_cat"], p["bias_tile"])


FF_TM = 512
FF_TF = 1024


def _outffn_kernel(*refs, final):
    if final:
        (x_ref, ya_ref, yb_ref, yc_ref, yd_ref, wo_ref, g_ref, wu_ref, wd_ref, gf_ref,
         o_ref, acc_ref, h_ref) = refs
    else:
        (x_ref, ya_ref, yb_ref, yc_ref, yd_ref, wo_ref, g_ref, wu_ref, wd_ref,
         o_ref, acc_ref, h_ref) = refs
    kf = pl.program_id(1)

    @pl.when(kf == 0)
    def _():
        x1 = x_ref[...]
        for m, y_ref in enumerate((ya_ref, yb_ref, yc_ref, yd_ref)):
            x1 = x1 + jnp.dot(y_ref[...], wo_ref[m * GROUP_W:(m + 1) * GROUP_W, :],
                              preferred_element_type=f32)
        acc_ref[...] = x1
        ms = jnp.mean(x1 * x1, axis=-1, keepdims=True)
        h_ref[...] = (x1 * lax.rsqrt(ms + EPS) * g_ref[...]).astype(bf16)

    hid = jnp.dot(h_ref[...], wu_ref[...], preferred_element_type=f32)
    hid = jnp.maximum(hid, 0.0)
    acc_ref[...] += jnp.dot((hid * hid).astype(bf16), wd_ref[...], preferred_element_type=f32)

    @pl.when(kf == pl.num_programs(1) - 1)
    def _():
        x2 = acc_ref[...]
        if final:
            ms = jnp.mean(x2 * x2, axis=-1, keepdims=True)
            x2 = x2 * lax.rsqrt(ms + EPS) * gf_ref[...]
        o_ref[...] = x2


def _out_ffn(x, ys, w_out, g, w_up, w_down, g_final):
    final = g_final is not None
    row = pl.BlockSpec((1, D_MODEL), lambda i, k: (0, 0))
    yspec = pl.BlockSpec((FF_TM, GROUP_W), lambda i, k: (i, 0))
    in_specs = [pl.BlockSpec((FF_TM, D_MODEL), lambda i, k: (i, 0)), yspec, yspec, yspec, yspec,
                pl.BlockSpec((D_MODEL, D_MODEL), lambda i, k: (0, 0)), row,
                pl.BlockSpec((D_MODEL, FF_TF), lambda i, k: (0, k)),
                pl.BlockSpec((FF_TF, D_MODEL), lambda i, k: (k, 0))]
    args = [x, *ys, w_out, g, w_up, w_down]
    if final:
        in_specs.append(row)
        args.append(g_final)
    return pl.pallas_call(
        functools.partial(_outffn_kernel, final=final),
        grid=(SEQ // FF_TM, D_FF // FF_TF),
        in_specs=in_specs,
        out_specs=pl.BlockSpec((FF_TM, D_MODEL), lambda i, k: (i, 0)),
        out_shape=jax.ShapeDtypeStruct((SEQ, D_MODEL), f32),
        scratch_shapes=[pltpu.VMEM((FF_TM, D_MODEL), f32), pltpu.VMEM((FF_TM, D_MODEL), bf16)],
        compiler_params=pltpu.CompilerParams(dimension_semantics=("arbitrary", "arbitrary"),
                                             vmem_limit_bytes=VMEM_LIMIT),
        name="out_ffn",
    )(*args)


def _pad_cols(w, n):
    return jnp.pad(w, ((0, 0), (0, n - w.shape[1])))


def _pad_rows(w, top, total):
    return jnp.pad(w, ((top, total - top - w.shape[0]), (0, 0)))


def kernel(x, w_in, w_out, norm_mix_g, norm_ffn_g, norm_final_g, rwkv_mu, rwkv_w0, rwkv_w_up, rwkv_a0, rwkv_a_up, rwkv_g_up, rwkv_k_k, rwkv_k_a, rwkv_r_k, rwkv_lnx_w, rwkv_lnx_b, rwkv_v0, rwkv_vres_down, rwkv_vres_up, gdn_conv_w, gdn_a_log, gdn_dt_bias, gdn_norm_g, gla_gk_up, gla_gk_bias, gla_norm_g, sgu_ln_g, sgu_ln_b, sgu_w_s, sgu_b_s, ffn_w_up, ffn_w_down):
    depth = w_in.shape[0]
    xx = x[0]
    v_first = None
    r2 = lambda a: a.reshape(1, -1)
    per_head = lambda a: jnp.repeat(a, HEAD_DIM).reshape(1, -1)
    for l in range(depth):
        wl = w_in[l]
        w_a = wl[:, 0:1024]
        if l > 0:
            w_a = jnp.concatenate([w_a, rwkv_vres_down[l - 1]], axis=1)
        w_comb = jnp.concatenate([_pad_cols(w_a, NA), _pad_cols(wl[:, 1024:2056], NB),
                                  _pad_cols(wl[:, 2056:2840], NC), wl[:, 2840:3352]], axis=1).astype(bf16)
        c_a, c_b, c_c, c_d = _in_proj(xx, r2(norm_mix_g[l]), w_comb)

        pa = dict(mu=r2(rwkv_mu[l]), w0=r2(rwkv_w0[l]), w_up=_pad_rows(rwkv_w_up[l], 0, 128),
                  a0=r2(rwkv_a0[l]), a_up=_pad_rows(rwkv_a_up[l], 64, 128), g_up=rwkv_g_up[l],
                  k_k=r2(rwkv_k_k[l]), k_a=r2(rwkv_k_a[l]), r_k=r2(rwkv_r_k[l]),
                  lnx_w=r2(rwkv_lnx_w[l]), lnx_b=r2(rwkv_lnx_b[l]))
        if l > 0:
            pa.update(v0=r2(rwkv_v0[l - 1]), vres_up=_pad_rows(rwkv_vres_up[l - 1], 0, 128))
        y_a, v_first = _rwkv(c_a, v_first, pa)

        y_b = _gdn(c_b, dict(conv_w=gdn_conv_w[l], a_log=per_head(gdn_a_log[l]),
                             dt_bias=per_head(gdn_dt_bias[l]),
                             a_log_col=jnp.pad(gdn_a_log[l], (N_HEADS, 0)).reshape(8, 1),
                             dt_bias_col=jnp.pad(gdn_dt_bias[l], (N_HEADS, 0)).reshape(8, 1), norm_g=r2(jnp.tile(gdn_norm_g[l], N_HEADS))))
        y_c = _gla(c_c, dict(gk_up=_pad_rows(gla_gk_up[l], 0, 128), gk_bias=r2(gla_gk_bias[l]),
                             norm_g=r2(jnp.tile(gla_norm_g[l], N_HEADS))))
        y_d = _sgu(c_d, dict(ln_g=r2(sgu_ln_g[l]), ln_b=r2(sgu_ln_b[l]),
                             w_cat=sgu_w_s[l].transpose(1, 0, 2).reshape(SG_C, 4 * SG_C),
                             bias_tile=jnp.repeat(sgu_b_s[l].T, HEAD_DIM, axis=1)))
        xx = _out_ffn(xx, (y_a, y_b, y_c, y_d), w_out[l].astype(bf16), r2(norm_ffn_g[l]),
                      ffn_w_up[l].astype(bf16), ffn_w_down[l].astype(bf16),
                      r2(norm_final_g) if l == depth - 1 else None)
    return xx[None]
```

```python
import functools

import jax
import jax.numpy as jnp
from jax import lax
from jax.experimental import pallas as pl
from jax.experimental.pallas import tpu as pltpu

f32 = jnp.float32
bf16 = jnp.bfloat16

SEQ = 16384
D_MODEL = 1024
GROUP_W = 256
HEAD_DIM = 64
N_HEADS = 4
GLA_KEY = 128
GLA_HEAD_K = 32
D_FF = 4096
EPS = 1e-6
RWKV_GN_EPS = 64e-5
NEG = -1e30

NA, NB, NC, ND = 1152, 1152, 896, 512
N_PAD = NA + NB + NC + ND

VMEM_LIMIT = 56 * 1024 * 1024

NN = (((1,), (0,)), ((), ()))
NT = (((1,), (1,)), ((), ()))
TN = (((0,), (0,)), ((), ()))


def _dot(a, b, dims=NN):
    return lax.dot_general(a.astype(bf16), b.astype(bf16), dims, preferred_element_type=f32)


def _iota(shape, axis):
    return lax.broadcasted_iota(jnp.int32, shape, axis)


def _segsum(x, ones_bd):
    hi = x.astype(bf16)
    lo = (x - hi.astype(f32)).astype(bf16)
    return (jnp.dot(hi, ones_bd, preferred_element_type=f32)
            + jnp.dot(lo, ones_bd, preferred_element_type=f32))


def _split3(x):
    hi = x.astype(bf16)
    r1 = x - hi.astype(f32)
    mid = r1.astype(bf16)
    lo = (r1 - mid.astype(f32)).astype(bf16)
    return hi, mid, lo


def _dot_sel_lhs(sel, x, dims=NN):
    return sum(lax.dot_general(sel, t, dims, preferred_element_type=f32) for t in _split3(x))


def _dot_sel_rhs(x, sel, dims=NN):
    return sum(lax.dot_general(t, sel, dims, preferred_element_type=f32) for t in _split3(x))


def _group_ones(n, width):
    return (_iota((n, n), 0) // width == _iota((n, n), 1) // width).astype(bf16)


def _sigmoid(x):
    return 1.0 / (1.0 + jnp.exp(-x))


def _softplus(x):
    return jnp.maximum(x, 0.0) + jnp.log1p(jnp.exp(-jnp.abs(x)))


def _shift_rows(x, carry, s):
    xs = pltpu.roll(x, s, 0)
    fix = pltpu.roll(carry, s, 0)
    first = jnp.where(_iota(carry.shape, 0) < s, fix, xs[0:8])
    return jnp.concatenate([first, xs[8:]], axis=0)


def _tri_masks(c):
    ri, ci = _iota((c, c), 0), _iota((c, c), 1)
    eye = (ri == ci).astype(f32)
    m16 = (ri // 16 == ci // 16).astype(f32)
    mo1 = ((ri // 32 == ci // 32) & (ri // 16 == ci // 16 + 1)).astype(f32)
    mo2 = ((ri // 32 == 1) & (ci // 32 == 0)).astype(f32)
    return eye, m16, mo1, mo2


def _tri_inv(lms, masks):
    eye, m16, mo1, mo2 = masks
    ps = [-(lm * m16) for lm in lms]
    ts = [eye + p for p in ps]
    for _ in range(3):
        ps = [_dot(p, p) for p in ps]
        ts = [t + _dot(t, p) for t, p in zip(ts, ps)]
    for mo in (mo1, mo2):
        xs = [_dot(lm * mo, t) for lm, t in zip(lms, ts)]
        ts = [t - _dot(t, x) for t, x in zip(ts, xs)]
    return ts


def _inproj_kernel(x_ref, g_ref, w_ref, oa_ref, ob_ref, oc_ref, od_ref):
    x = x_ref[...]
    ms = jnp.mean(x * x, axis=-1, keepdims=True)
    h = (x * lax.rsqrt(ms + EPS) * g_ref[...]).astype(bf16)
    off = 0
    for o_ref, n in ((oa_ref, NA), (ob_ref, NB), (oc_ref, NC), (od_ref, ND)):
        o_ref[...] = jnp.dot(h, w_ref[:, off:off + n], preferred_element_type=f32)
        off += n


def _in_proj(x, g, w):
    tm = 512
    return pl.pallas_call(
        _inproj_kernel,
        grid=(SEQ // tm,),
        in_specs=[pl.BlockSpec((tm, D_MODEL), lambda i: (i, 0)),
                  pl.BlockSpec((1, D_MODEL), lambda i: (0, 0)),
                  pl.BlockSpec((D_MODEL, N_PAD), lambda i: (0, 0))],
        out_specs=[pl.BlockSpec((tm, n), lambda i: (i, 0)) for n in (NA, NB, NC, ND)],
        out_shape=[jax.ShapeDtypeStruct((SEQ, n), f32) for n in (NA, NB, NC, ND)],
        compiler_params=pltpu.CompilerParams(dimension_semantics=("arbitrary",),
                                             vmem_limit_bytes=VMEM_LIMIT),
        name="in_proj",
    )(x, g, w)


RW_TB = 512
RW_C = 64


def _rwkv_kernel(*refs, has_vres):
    if has_vres:
        (c_ref, vf_ref, mu_ref, w0_ref, wup_ref, a0_ref, aup_ref, gup_ref, kk_ref, ka_ref, rk_ref,
         lw_ref, lb_ref, v0_ref, vup_ref, y_ref,
         carry_ref, s_ref, r_s, k_s, v_s, lw_s, al_s, be_s, y_s) = refs
    else:
        (c_ref, mu_ref, w0_ref, wup_ref, a0_ref, aup_ref, gup_ref, kk_ref, ka_ref, rk_ref,
         lw_ref, lb_ref, y_ref, vf_out_ref,
         carry_ref, s_ref, r_s, k_s, v_s, lw_s, al_s, be_s, y_s) = refs

    @pl.when(pl.program_id(0) == 0)
    def _():
        carry_ref[...] = jnp.zeros_like(carry_ref)
        s_ref[...] = jnp.zeros_like(s_ref)

    ones_bd = _group_ones(GROUP_W, HEAD_DIM)
    x = c_ref[:, 0:1024]
    x_prev = _shift_rows(x, carry_ref[...], 1)
    carry_ref[...] = x[RW_TB - 8:RW_TB]
    xs = x + (x_prev - x) * mu_ref[...]
    r = xs[:, 0:256]
    k = xs[:, 256:512]
    v = xs[:, 512:768]
    lora = xs[:, 768:896]
    w_pre = w0_ref[...] + _dot(jnp.tanh(lora), wup_ref[...])
    lw = -jnp.exp(-_softplus(-w_pre) - 0.5)
    a = _sigmoid(a0_ref[...] + _dot(lora, aup_ref[...]))
    g = _dot(_sigmoid(xs[:, 896:1024]), gup_ref[...])
    if has_vres:
        mix = _sigmoid(v0_ref[...] + _dot(c_ref[:, 1024:1152], vup_ref[...]))
        v = v + (vf_ref[...] - v) * mix
    else:
        vf_out_ref[...] = v
    kk = k * kk_ref[...]
    kk = kk * lax.rsqrt(_segsum(kk * kk, ones_bd) + 1e-24)
    k = k * (1.0 + (a - 1.0) * ka_ref[...])
    r_s[...] = r
    k_s[...] = k
    v_s[...] = v
    lw_s[...] = lw
    al_s[...] = -kk
    be_s[...] = kk * a

    c = RW_C
    ri, ci = _iota((c, c), 0), _iota((c, c), 1)
    tril_incl = (ri >= ci).astype(bf16)
    low_strict = ri > ci
    low_incl = ri >= ci
    masks = _tri_masks(c)

    nchunk = RW_TB // c
    items = [(n, h) for n in range(nchunk) for h in range(N_HEADS)]
    hsl = lambda h: slice(h * HEAD_DIM, (h + 1) * HEAD_DIM)
    pre = []
    for n in range(nchunk):
        rows = slice(n * c, (n + 1) * c)
        lwc = lw_s[rows, :]
        lc = _dot_sel_lhs(tril_incl, lwc)
        llast = lc[c - 1:c, :]
        e_out = jnp.exp(-lc)
        e_rest = jnp.exp(llast - lc)
        kc, bec = k_s[rows, :], be_s[rows, :]
        pre.append(dict(rt=r_s[rows, :] * jnp.exp(lc), at=al_s[rows, :] * jnp.exp(lc - lwc),
                        bt=bec * e_out, kt=kc * e_out, bw=bec * e_rest, kw=kc * e_rest,
                        v=v_s[rows, :], dlast=jnp.exp(llast)))
    ms = [_dot(jnp.concatenate([pre[n]["at"][:, hsl(h)], pre[n]["rt"][:, hsl(h)]], axis=0),
               jnp.concatenate([pre[n]["bt"][:, hsl(h)], pre[n]["kt"][:, hsl(h)]], axis=0), NT)
          for n, h in items]
    tms = _tri_inv([jnp.where(low_strict, -m[0:c, 0:c], 0.0) for m in ms], masks)
    avs = [_dot(jnp.concatenate([jnp.where(low_strict, m[0:c, c:], 0.0),
                                 jnp.where(low_incl, m[c:, c:], 0.0)], axis=0), pre[n]["v"][:, hsl(h)])
           for m, (n, h) in zip(ms, items)]
    tas = [_dot(tm, pre[n]["at"][:, hsl(h)]) for tm, (n, h) in zip(tms, items)]
    tvs = [_dot(tm, av[0:c]) for tm, av in zip(tms, avs)]
    state = [s_ref[h] for h in range(N_HEADS)]
    for n in range(nchunk):
        p = pre[n]
        idx = [n * N_HEADS + h for h in range(N_HEADS)]
        sas = [_dot(jnp.concatenate([tas[i], p["rt"][:, hsl(h)]], axis=0), state[h], NT)
               for h, i in enumerate(idx)]
        us = [sa[0:c] + tvs[i] for sa, i in zip(sas, idx)]
        upd = [_dot(jnp.concatenate([u, p["v"][:, hsl(h)]], axis=0),
                    jnp.concatenate([p["bw"][:, hsl(h)], p["kw"][:, hsl(h)]], axis=0), TN)
               for h, u in enumerate(us)]
        state = [state[h] * p["dlast"][:, hsl(h)] + upd[h] for h in range(N_HEADS)]
        ys = [sas[h][c:] + _dot(jnp.where(low_incl, ms[i][c:, 0:c], 0.0), us[h]) + avs[i][c:]
              for h, i in enumerate(idx)]
        y_s[n * c:(n + 1) * c, :] = jnp.concatenate(ys, axis=1)
    for h in range(N_HEADS):
        s_ref[h] = state[h]

    y = y_s[...]
    inv_d = 1.0 / HEAD_DIM
    mean = _segsum(y, ones_bd) * inv_d
    yc = y - mean
    var = _segsum(yc * yc, ones_bd) * inv_d
    y = yc * lax.rsqrt(var + RWKV_GN_EPS) * lw_ref[...] + lb_ref[...]
    y = y + _segsum(r * k * rk_ref[...], ones_bd) * v
    y_ref[...] = (y * g).astype(bf16)


def _rwkv(c_a, v_first, p):
    has_vres = v_first is not None
    row = lambda n: pl.BlockSpec((1, n), lambda i: (0, 0))
    full = lambda a, b: pl.BlockSpec((a, b), lambda i: (0, 0))
    blk = lambda n: pl.BlockSpec((RW_TB, n), lambda i: (i, 0))
    in_specs = [blk(NA)]
    args = [c_a]
    if has_vres:
        in_specs.append(blk(GROUP_W))
        args.append(v_first)
    in_specs += [row(1024), row(256), full(128, 256), row(256), full(128, 256), full(128, 256),
                 row(256), row(256), row(256), row(256), row(256)]
    args += [p["mu"], p["w0"], p["w_up"], p["a0"], p["a_up"], p["g_up"], p["k_k"], p["k_a"], p["r_k"],
             p["lnx_w"], p["lnx_b"]]
    if has_vres:
        in_specs += [row(256), full(128, 256)]
        args += [p["v0"], p["vres_up"]]
        out_specs = blk(GROUP_W)
        out_shape = jax.ShapeDtypeStruct((SEQ, GROUP_W), bf16)
    else:
        out_specs = [blk(GROUP_W), blk(GROUP_W)]
        out_shape = [jax.ShapeDtypeStruct((SEQ, GROUP_W), bf16),
                     jax.ShapeDtypeStruct((SEQ, GROUP_W), f32)]
    scratch = [pltpu.VMEM((8, 1024), f32), pltpu.VMEM((N_HEADS, HEAD_DIM, HEAD_DIM), f32)]
    scratch += [pltpu.VMEM((RW_TB, GROUP_W), f32) for _ in range(7)]
    out = pl.pallas_call(
        functools.partial(_rwkv_kernel, has_vres=has_vres),
        grid=(SEQ // RW_TB,),
        in_specs=in_specs,
        out_specs=out_specs,
        out_shape=out_shape,
        scratch_shapes=scratch,
        compiler_params=pltpu.CompilerParams(dimension_semantics=("arbitrary",),
                                             vmem_limit_bytes=VMEM_LIMIT),
        name="rwkv7",
    )(*args)
    if has_vres:
        return out, v_first
    return out[0], out[1]


GD_TB = 512
GD_C = 64


def _gdn_kernel(c_ref, cw_ref, alog_ref, dtb_ref, alogc_ref, dtbc_ref, ng_ref, y_ref,
                carry_ref, s_ref, q_s, k_s, v_s, be_s, g_s, o_s):
    @pl.when(pl.program_id(0) == 0)
    def _():
        carry_ref[...] = jnp.zeros_like(carry_ref)
        s_ref[...] = jnp.zeros_like(s_ref)

    ones_bd = _group_ones(GROUP_W, HEAD_DIM)
    raw = c_ref[:, 0:768]
    carry = carry_ref[...]
    conv = raw * cw_ref[3:4, :]
    for s in range(1, 4):
        conv = conv + _shift_rows(raw, carry, s) * cw_ref[3 - s:4 - s, :]
    carry_ref[...] = raw[GD_TB - 8:GD_TB]
    qkv = conv * _sigmoid(conv)
    q = qkv[:, 0:256]
    k = qkv[:, 256:512]
    q_s[...] = q * lax.rsqrt(_segsum(q * q, ones_bd) + 1e-6) * (HEAD_DIM ** -0.5)
    k_s[...] = k * lax.rsqrt(_segsum(k * k, ones_bd) + 1e-6)
    v_s[...] = qkv[:, 512:768]
    small = c_ref[:, 1024:1152]
    er, ec = _iota((128, GROUP_W), 0), _iota((128, GROUP_W), 1)
    b_exp = _dot_sel_rhs(small, (er == ec // HEAD_DIM).astype(bf16))
    a_exp = _dot_sel_rhs(small, (er == ec // HEAD_DIM + N_HEADS).astype(bf16))
    be_s[...] = _sigmoid(b_exp)
    g_s[...] = -jnp.exp(alog_ref[...]) * _softplus(a_exp + dtb_ref[...])

    c = GD_C
    ri, ci = _iota((c, c), 0), _iota((c, c), 1)
    tril_incl = (ri >= ci).astype(bf16)
    low_strict = ri > ci
    low_incl = ri >= ci
    masks = _tri_masks(c)
    sel8 = (_iota((8, 128), 0) == _iota((8, 128), 1)).astype(bf16)
    g_rows = -jnp.exp(alogc_ref[...]) * _softplus(_dot_sel_lhs(sel8, small, NT) + dtbc_ref[...])
    tj, ti = _iota((GD_TB, GD_TB), 0), _iota((GD_TB, GD_TB), 1)
    gc_rows = _dot_sel_rhs(g_rows, ((tj // c == ti // c) & (tj <= ti)).astype(bf16))

    nchunk = GD_TB // c
    items = [(n, h) for n in range(nchunk) for h in range(N_HEADS)]
    hsl = lambda h: slice(h * HEAD_DIM, (h + 1) * HEAD_DIM)
    pre = []
    for n in range(nchunk):
        rows = slice(n * c, (n + 1) * c)
        gc = _dot_sel_lhs(tril_incl, g_s[rows, :])
        glast = gc[c - 1:c, :]
        egc = jnp.exp(gc)
        kc, bc = k_s[rows, :], be_s[rows, :]
        kb = kc * bc
        pre.append(dict(gc=gc, k=kc, kb=kb, q=q_s[rows, :], vb=v_s[rows, :] * bc, kbe=kb * egc,
                        qe=q_s[rows, :] * egc, kd=kc * jnp.exp(glast - gc), eglast=jnp.exp(glast)))
    dms = [jnp.exp(jnp.where(low_incl, pre[n]["gc"][:, hsl(h)]
                             - gc_rows[N_HEADS + h:N_HEADS + h + 1, n * c:(n + 1) * c], NEG))
           for n, h in items]
    aqs = [_dot(jnp.concatenate([pre[n]["kb"][:, hsl(h)], pre[n]["q"][:, hsl(h)]], axis=0),
                pre[n]["k"][:, hsl(h)], NT) for n, h in items]
    tms = _tri_inv([jnp.where(low_strict, aq[0:c] * dm, 0.0) for aq, dm in zip(aqs, dms)], masks)
    us = [_dot(tm, pre[n]["vb"][:, hsl(h)]) for tm, (n, h) in zip(tms, items)]
    ws = [_dot(tm, pre[n]["kbe"][:, hsl(h)]) for tm, (n, h) in zip(tms, items)]
    state = [s_ref[h] for h in range(N_HEADS)]
    for n in range(nchunk):
        p = pre[n]
        idx = [n * N_HEADS + h for h in range(N_HEADS)]
        wss = [_dot(jnp.concatenate([ws[i], p["qe"][:, hsl(h)]], axis=0), state[h])
               for h, i in enumerate(idx)]
        vns = [us[i] - wss[h][0:c] for h, i in enumerate(idx)]
        upd = [_dot(p["kd"][:, hsl(h)], vns[h], TN) for h in range(N_HEADS)]
        state = [state[h] * p["eglast"][:, hsl(h)] + upd[h] for h in range(N_HEADS)]
        os_ = [wss[h][c:] + _dot(aqs[i][c:] * dms[i], vns[h]) for h, i in enumerate(idx)]
        o_s[n * c:(n + 1) * c, :] = jnp.concatenate(os_, axis=1)
    for h in range(N_HEADS):
        s_ref[h] = state[h]

    o = o_s[...]
    ms = _segsum(o * o, ones_bd) * (1.0 / HEAD_DIM)
    z = c_ref[:, 768:1024]
    y_ref[...] = (o * lax.rsqrt(ms + EPS) * ng_ref[...] * (z * _sigmoid(z))).astype(bf16)


def _gdn(c_b, p):
    row = lambda n: pl.BlockSpec((1, n), lambda i: (0, 0))
    scratch = [pltpu.VMEM((8, 768), f32), pltpu.VMEM((N_HEADS, HEAD_DIM, HEAD_DIM), f32)]
    scratch += [pltpu.VMEM((GD_TB, GROUP_W), f32) for _ in range(6)]
    return pl.pallas_call(
        _gdn_kernel,
        grid=(SEQ // GD_TB,),
        in_specs=[pl.BlockSpec((GD_TB, NB), lambda i: (i, 0)),
                  pl.BlockSpec((4, 768), lambda i: (0, 0)), row(256), row(256),
                  pl.BlockSpec((8, 1), lambda i: (0, 0)), pl.BlockSpec((8, 1), lambda i: (0, 0)),
                  row(256)],
        out_specs=pl.BlockSpec((GD_TB, GROUP_W), lambda i: (i, 0)),
        out_shape=jax.ShapeDtypeStruct((SEQ, GROUP_W), bf16),
        scratch_shapes=scratch,
        compiler_params=pltpu.CompilerParams(dimension_semantics=("arbitrary",),
                                             vmem_limit_bytes=VMEM_LIMIT),
        name="gdn",
    )(c_b, p["conv_w"], p["a_log"], p["dt_bias"], p["a_log_col"], p["dt_bias_col"], p["norm_g"])


GL_TB = 256
GL_C = 16


def _gla_kernel(c_ref, gup_ref, gb_ref, ng_ref, y_ref, st_ref, sx_s, o_s):
    @pl.when(pl.program_id(0) == 0)
    def _():
        st_ref[...] = jnp.zeros_like(st_ref)

    tb, c = GL_TB, GL_C
    q = c_ref[:, 0:128] * (GLA_HEAD_K ** -0.5)
    k = c_ref[:, 128:256]
    pre = _dot(c_ref[:, 768:896], gup_ref[...]) + gb_ref[...]
    la = -_softplus(-pre) * (1.0 / 16.0)
    tj, ti = _iota((tb, tb), 0), _iota((tb, tb), 1)
    b = _dot_sel_lhs(((tj // c == ti // c) & (ti <= tj)).astype(bf16), la)
    qi = q * jnp.exp(b)

    ri = _iota((c, GLA_KEY), 0)
    ind_e = (_iota((GLA_KEY, GROUP_W), 0) // GLA_HEAD_K == _iota((GLA_KEY, GROUP_W), 1) // HEAD_DIM).astype(bf16)
    bd_mask = (_iota((GROUP_W, GLA_KEY), 0) // HEAD_DIM == _iota((GROUP_W, GLA_KEY), 1) // GLA_HEAD_K).astype(f32)
    nchunk = tb // c
    terms, blasts, upds = [], [], []
    for n in range(nchunk):
        rows = slice(n * c, (n + 1) * c)
        bn, qn, kn = b[rows], q[rows], k[rows]
        for j in range(c):
            e = jnp.exp(jnp.where(ri >= j, bn - bn[j:j + 1, :], NEG))
            terms.append((qn * (kn[j:j + 1, :] * e)).astype(bf16))
        blasts.append(bn[c - 1:c, :])
    sx_s[...] = jnp.dot(jnp.concatenate(terms, axis=0), ind_e, preferred_element_type=f32)
    for n in range(nchunk):
        rows = slice(n * c, (n + 1) * c)
        upds.append(_dot(c_ref[rows, 256:512], k[rows] * jnp.exp(blasts[n] - b[rows]), TN) * bd_mask)
    st = st_ref[...]
    for n in range(nchunk):
        rows = slice(n * c, (n + 1) * c)
        o = _dot(qi[rows], st, NT)
        st = st * jnp.exp(blasts[n]) + upds[n]
        for j in range(c):
            r0 = (n * c + j) * c
            o = o + sx_s[r0:r0 + c, :] * c_ref[n * c + j:n * c + j + 1, 256:512]
        o_s[rows, :] = o
    st_ref[...] = st

    o = o_s[...]
    ms = _segsum(o * o, _group_ones(GROUP_W, HEAD_DIM)) * (1.0 / HEAD_DIM)
    gate = c_ref[:, 512:768]
    y_ref[...] = (o * lax.rsqrt(ms + EPS) * ng_ref[...] * (gate * _sigmoid(gate))).astype(bf16)


def _gla(c_c, p):
    row = lambda n: pl.BlockSpec((1, n), lambda i: (0, 0))
    return pl.pallas_call(
        _gla_kernel,
        grid=(SEQ // GL_TB,),
        in_specs=[pl.BlockSpec((GL_TB, NC), lambda i: (i, 0)),
                  pl.BlockSpec((128, 128), lambda i: (0, 0)), row(128), row(256)],
        out_specs=pl.BlockSpec((GL_TB, GROUP_W), lambda i: (i, 0)),
        out_shape=jax.ShapeDtypeStruct((SEQ, GROUP_W), bf16),
        scratch_shapes=[pltpu.VMEM((GROUP_W, GLA_KEY), f32),
                        pltpu.VMEM((GL_TB * GL_C, GROUP_W), f32), pltpu.VMEM((GL_TB, GROUP_W), f32)],
        compiler_params=pltpu.CompilerParams(dimension_semantics=("arbitrary",),
                                             vmem_limit_bytes=VMEM_LIMIT),
        name="gla",
    )(c_c, p["gk_up"], p["gk_bias"], p["norm_g"])


SG_TB = 512
SG_C = 128


def _sgu_kernel(c_ref, lg_ref, lb_ref, w_ref, bias_ref, y_ref):
    x = c_ref[...]
    gx = 0.5 * x * (1.0 + jnp.tanh(0.7978845608028654 * (x + 0.044715 * x * x * x)))
    u = gx[:, 0:256]
    v = gx[:, 256:512]
    mu = jnp.mean(v, axis=-1, keepdims=True)
    vc = v - mu
    var = jnp.mean(vc * vc, axis=-1, keepdims=True)
    v = vc * lax.rsqrt(var + 1e-5) * lg_ref[...] + lb_ref[...]
    wr, wc = _iota((SG_C, 4 * SG_C), 0), _iota((SG_C, 4 * SG_C), 1)
    w = jnp.where(wc % SG_C <= wr, w_ref[...], 0.0).astype(bf16)
    lane_g = _iota((SG_C, GROUP_W), 1) // HEAD_DIM
    outs = []
    for n in range(SG_TB // SG_C):
        vn = v[n * SG_C:(n + 1) * SG_C, :]
        vst = jnp.concatenate([jnp.where(lane_g == g, vn, 0.0) for g in range(4)], axis=0)
        outs.append(jnp.dot(w, vst.astype(bf16), preferred_element_type=f32) + bias_ref[...])
    y_ref[...] = (u * jnp.concatenate(outs, axis=0)).astype(bf16)


def _sgu(c_d, p):
    row = lambda n: pl.BlockSpec((1, n), lambda i: (0, 0))
    return pl.pallas_call(
        _sgu_kernel,
        grid=(SEQ // SG_TB,),
        in_specs=[pl.BlockSpec((SG_TB, ND), lambda i: (i, 0)), row(256), row(256),
                  pl.BlockSpec((SG_C, 4 * SG_C), lambda i: (0, 0)),
                  pl.BlockSpec((SG_C, GROUP_W), lambda i: (0, 0))],
        out_specs=pl.BlockSpec((SG_TB, GROUP_W), lambda i: (i, 0)),
        out_shape=jax.ShapeDtypeStruct((SEQ, GROUP_W), bf16),
        compiler_params=pltpu.CompilerParams(dimension_semantics=("arbitrary",),
                                             vmem_limit_bytes=VMEM_LIMIT),
        name="sgu",
    )(c_d, p["ln_g"], p["ln_b"], p["w_cat"], p["bias_tile"])


FF_TM = 512
FF_TF = 1024


def _outffn_kernel(*refs, final):
    if final:
        (x_ref, ya_ref, yb_ref, yc_ref, yd_ref, wo_hbm, g_ref, wu_hbm, wd_hbm, gf_ref, o_ref,
         wo_ref, wu_ref, wd_ref, sem) = refs
    else:
        (x_ref, ya_ref, yb_ref, yc_ref, yd_ref, wo_hbm, g_ref, wu_hbm, wd_hbm, o_ref,
         wo_ref, wu_ref, wd_ref, sem) = refs

    @pl.when(pl.program_id(0) == 0)
    def _():
        copies = [pltpu.make_async_copy(src, dst, sem.at[i])
                  for i, (src, dst) in enumerate(((wo_hbm, wo_ref), (wu_hbm, wu_ref), (wd_hbm, wd_ref)))]
        for cp in copies:
            cp.start()
        for cp in copies:
            cp.wait()

    y = jnp.concatenate([ya_ref[...], yb_ref[...], yc_ref[...], yd_ref[...]], axis=1)
    x1 = x_ref[...] + jnp.dot(y, wo_ref[...], preferred_element_type=f32)
    ms = jnp.mean(x1 * x1, axis=-1, keepdims=True)
    h = (x1 * lax.rsqrt(ms + EPS) * g_ref[...]).astype(bf16)
    x2 = x1
    for kf in range(D_FF // FF_TF):
        cols = slice(kf * FF_TF, (kf + 1) * FF_TF)
        hid = jnp.maximum(jnp.dot(h, wu_ref[:, cols], preferred_element_type=f32), 0.0)
        x2 = x2 + jnp.dot((hid * hid).astype(bf16), wd_ref[cols, :], preferred_element_type=f32)
    if final:
        ms = jnp.mean(x2 * x2, axis=-1, keepdims=True)
        x2 = x2 * lax.rsqrt(ms + EPS) * gf_ref[...]
    o_ref[...] = x2


def _out_ffn(x, ys, w_out, g, w_up, w_down, g_final):
    final = g_final is not None
    hbm = pl.BlockSpec(memory_space=pl.ANY)
    row = pl.BlockSpec((1, D_MODEL), lambda i: (0, 0))
    yspec = pl.BlockSpec((FF_TM, GROUP_W), lambda i: (i, 0))
    in_specs = [pl.BlockSpec((FF_TM, D_MODEL), lambda i: (i, 0)), yspec, yspec, yspec, yspec,
                hbm, row, hbm, hbm]
    args = [x, *ys, w_out, g, w_up, w_down]
    if final:
        in_specs.append(row)
        args.append(g_final)
    return pl.pallas_call(
        functools.partial(_outffn_kernel, final=final),
        grid=(SEQ // FF_TM,),
        in_specs=in_specs,
        out_specs=pl.BlockSpec((FF_TM, D_MODEL), lambda i: (i, 0)),
        out_shape=jax.ShapeDtypeStruct((SEQ, D_MODEL), f32),
        scratch_shapes=[pltpu.VMEM((D_MODEL, D_MODEL), bf16), pltpu.VMEM((D_MODEL, D_FF), bf16),
                        pltpu.VMEM((D_FF, D_MODEL), bf16), pltpu.SemaphoreType.DMA((3,))],
        compiler_params=pltpu.CompilerParams(dimension_semantics=("arbitrary",),
                                             vmem_limit_bytes=VMEM_LIMIT),
        name="out_ffn",
    )(*args)


def _pad_cols(w, n):
    return jnp.pad(w, ((0, 0), (0, n - w.shape[1])))


def _pad_rows(w, top, total):
    return jnp.pad(w, ((top, total - top - w.shape[0]), (0, 0)))


def kernel(x, w_in, w_out, norm_mix_g, norm_ffn_g, norm_final_g, rwkv_mu, rwkv_w0, rwkv_w_up, rwkv_a0, rwkv_a_up, rwkv_g_up, rwkv_k_k, rwkv_k_a, rwkv_r_k, rwkv_lnx_w, rwkv_lnx_b, rwkv_v0, rwkv_vres_down, rwkv_vres_up, gdn_conv_w, gdn_a_log, gdn_dt_bias, gdn_norm_g, gla_gk_up, gla_gk_bias, gla_norm_g, sgu_ln_g, sgu_ln_b, sgu_w_s, sgu_b_s, ffn_w_up, ffn_w_down):
    depth = w_in.shape[0]
    xx = x[0]
    v_first = None
    r2 = lambda a: a.reshape(1, -1)
    per_head = lambda a: jnp.repeat(a, HEAD_DIM).reshape(1, -1)
    for l in range(depth):
        wl = w_in[l]
        w_a = wl[:, 0:1024]
        if l > 0:
            w_a = jnp.concatenate([w_a, rwkv_vres_down[l - 1]], axis=1)
        w_comb = jnp.concatenate([_pad_cols(w_a, NA), _pad_cols(wl[:, 1024:2056], NB),
                                  _pad_cols(wl[:, 2056:2840], NC), wl[:, 2840:3352]], axis=1).astype(bf16)
        c_a, c_b, c_c, c_d = _in_proj(xx, r2(norm_mix_g[l]), w_comb)

        pa = dict(mu=r2(rwkv_mu[l]), w0=r2(rwkv_w0[l]), w_up=_pad_rows(rwkv_w_up[l], 0, 128),
                  a0=r2(rwkv_a0[l]), a_up=_pad_rows(rwkv_a_up[l], 64, 128), g_up=rwkv_g_up[l],
                  k_k=r2(rwkv_k_k[l]), k_a=r2(rwkv_k_a[l]), r_k=r2(rwkv_r_k[l]),
                  lnx_w=r2(rwkv_lnx_w[l]), lnx_b=r2(rwkv_lnx_b[l]))
        if l > 0:
            pa.update(v0=r2(rwkv_v0[l - 1]), vres_up=_pad_rows(rwkv_vres_up[l - 1], 0, 128))
        y_a, v_first = _rwkv(c_a, v_first, pa)

        y_b = _gdn(c_b, dict(conv_w=gdn_conv_w[l], a_log=per_head(gdn_a_log[l]),
                             dt_bias=per_head(gdn_dt_bias[l]),
                             a_log_col=jnp.pad(gdn_a_log[l], (N_HEADS, 0)).reshape(8, 1),
                             dt_bias_col=jnp.pad(gdn_dt_bias[l], (N_HEADS, 0)).reshape(8, 1), norm_g=r2(jnp.tile(gdn_norm_g[l], N_HEADS))))
        y_c = _gla(c_c, dict(gk_up=_pad_rows(gla_gk_up[l], 0, 128), gk_bias=r2(gla_gk_bias[l]),
                             norm_g=r2(jnp.tile(gla_norm_g[l], N_HEADS))))
        y_d = _sgu(c_d, dict(ln_g=r2(sgu_ln_g[l]), ln_b=r2(sgu_ln_b[l]),
                             w_cat=sgu_w_s[l].transpose(1, 0, 2).reshape(SG_C, 4 * SG_C),
                             bias_tile=jnp.repeat(sgu_b_s[l].T, HEAD_DIM, axis=1)))
        xx = _out_ffn(xx, (y_a, y_b, y_c, y_d), w_out[l].astype(bf16), r2(norm_ffn_g[l]),
                      ffn_w_up[l].astype(bf16), ffn_w_down[l].astype(bf16),
                      r2(norm_final_g) if l == depth - 1 else None)
    return xx[None]
```

```python
import functools

import jax
import jax.numpy as jnp
from jax import lax
from jax.experimental import pallas as pl
from jax.experimental.pallas import tpu as pltpu

f32 = jnp.float32
bf16 = jnp.bfloat16

SEQ = 16384
D_MODEL = 1024
GROUP_W = 256
HEAD_DIM = 64
N_HEADS = 4
GLA_KEY = 128
GLA_HEAD_K = 32
D_FF = 4096
EPS = 1e-6
RWKV_GN_EPS = 64e-5
NEG = -1e30

NA, NB, NC, ND = 1152, 1152, 896, 512
N_PAD = NA + NB + NC + ND

VMEM_LIMIT = 56 * 1024 * 1024

NN = (((1,), (0,)), ((), ()))
NT = (((1,), (1,)), ((), ()))
TN = (((0,), (0,)), ((), ()))


def _dot(a, b, dims=NN):
    return lax.dot_general(a.astype(bf16), b.astype(bf16), dims, preferred_element_type=f32)


def _iota(shape, axis):
    return lax.broadcasted_iota(jnp.int32, shape, axis)


def _segsum(x, ones_bd):
    hi = x.astype(bf16)
    lo = (x - hi.astype(f32)).astype(bf16)
    return (jnp.dot(hi, ones_bd, preferred_element_type=f32)
            + jnp.dot(lo, ones_bd, preferred_element_type=f32))


def _split3(x):
    hi = x.astype(bf16)
    r1 = x - hi.astype(f32)
    mid = r1.astype(bf16)
    lo = (r1 - mid.astype(f32)).astype(bf16)
    return hi, mid, lo


def _dot_sel_lhs(sel, x, dims=NN):
    return sum(lax.dot_general(sel, t, dims, preferred_element_type=f32) for t in _split3(x))


def _dot_sel_rhs(x, sel, dims=NN):
    return sum(lax.dot_general(t, sel, dims, preferred_element_type=f32) for t in _split3(x))


def _group_ones(n, width):
    return (_iota((n, n), 0) // width == _iota((n, n), 1) // width).astype(bf16)


def _sigmoid(x):
    return 1.0 / (1.0 + jnp.exp(-x))


def _softplus(x):
    return jnp.maximum(x, 0.0) + jnp.log1p(jnp.exp(-jnp.abs(x)))


def _shift_rows(x, carry, s):
    xs = pltpu.roll(x, s, 0)
    fix = pltpu.roll(carry, s, 0)
    first = jnp.where(_iota(carry.shape, 0) < s, fix, xs[0:8])
    return jnp.concatenate([first, xs[8:]], axis=0)


def _bd(xp):
    xb = xp.astype(bf16)
    left = _iota(xb.shape, 1) < HEAD_DIM
    zero = jnp.zeros_like(xb)
    return jnp.concatenate([jnp.where(left, xb, zero), jnp.where(left, zero, xb)], axis=0)


def _pair_masks(c):
    ri, cj = _iota((c, 2 * c), 0), _iota((c, 2 * c), 1) & (c - 1)
    eye = (ri == cj).astype(f32)
    m16 = (ri // 16 == cj // 16).astype(f32)
    mo1 = ((ri // 32 == cj // 32) & (ri // 16 == cj // 16 + 1)).astype(f32)
    mo2 = ((ri // 32 == 1) & (cj // 32 == 0)).astype(f32)
    return dict(eye=eye, m16=m16, mo1=mo1, mo2=mo2, strict=ri > cj, incl=ri >= cj)


def _tri_inv(lms, masks):
    c = RW_C
    ps = [-(lm * masks["m16"]) for lm in lms]
    ts = [masks["eye"] + p for p in ps]
    ps = [_dot(p, _bd(p)) for p in ps]
    for _ in range(2):
        outs = [_dot(jnp.concatenate([t, p], axis=0), _bd(p)) for t, p in zip(ts, ps)]
        ts = [t + o[0:c] for t, o in zip(ts, outs)]
        ps = [o[c:] for o in outs]
    ts = [t + _dot(t, _bd(p)) for t, p in zip(ts, ps)]
    for mo in (masks["mo1"], masks["mo2"]):
        xs = [_dot(lm * mo, _bd(t)) for lm, t in zip(lms, ts)]
        ts = [t - _dot(t, _bd(x)) for t, x in zip(ts, xs)]
    return ts


def _inproj_kernel(x_ref, g_ref, w_ref, oa_ref, ob_ref, oc_ref, od_ref):
    x = x_ref[...]
    ms = jnp.mean(x * x, axis=-1, keepdims=True)
    h = (x * lax.rsqrt(ms + EPS) * g_ref[...]).astype(bf16)
    off = 0
    for o_ref, n in ((oa_ref, NA), (ob_ref, NB), (oc_ref, NC), (od_ref, ND)):
        o_ref[...] = jnp.dot(h, w_ref[:, off:off + n], preferred_element_type=f32)
        off += n


def _in_proj(x, g, w):
    tm = 512
    return pl.pallas_call(
        _inproj_kernel,
        grid=(SEQ // tm,),
        in_specs=[pl.BlockSpec((tm, D_MODEL), lambda i: (i, 0)),
                  pl.BlockSpec((1, D_MODEL), lambda i: (0, 0)),
                  pl.BlockSpec((D_MODEL, N_PAD), lambda i: (0, 0))],
        out_specs=[pl.BlockSpec((tm, n), lambda i: (i, 0)) for n in (NA, NB, NC, ND)],
        out_shape=[jax.ShapeDtypeStruct((SEQ, n), f32) for n in (NA, NB, NC, ND)],
        compiler_params=pltpu.CompilerParams(dimension_semantics=("arbitrary",),
                                             vmem_limit_bytes=VMEM_LIMIT),
        name="in_proj",
    )(x, g, w)


RW_TB = 512
RW_C = 64


def _rwkv_kernel(*refs, has_vres):
    if has_vres:
        (c_ref, vf_ref, mu_ref, w0_ref, wup_ref, a0_ref, aup_ref, gup_ref, kk_ref, ka_ref, rk_ref,
         lw_ref, lb_ref, v0_ref, vup_ref, y_ref,
         carry_ref, s_ref, r_s, k_s, v_s, lw_s, al_s, be_s, y_s) = refs
    else:
        (c_ref, mu_ref, w0_ref, wup_ref, a0_ref, aup_ref, gup_ref, kk_ref, ka_ref, rk_ref,
         lw_ref, lb_ref, y_ref, vf_out_ref,
         carry_ref, s_ref, r_s, k_s, v_s, lw_s, al_s, be_s, y_s) = refs

    @pl.when(pl.program_id(0) == 0)
    def _():
        carry_ref[...] = jnp.zeros_like(carry_ref)
        s_ref[...] = jnp.zeros_like(s_ref)

    ones_bd = _group_ones(GROUP_W, HEAD_DIM)
    x = c_ref[:, 0:1024]
    x_prev = _shift_rows(x, carry_ref[...], 1)
    carry_ref[...] = x[RW_TB - 8:RW_TB]
    xs = x + (x_prev - x) * mu_ref[...]
    r = xs[:, 0:256]
    k = xs[:, 256:512]
    v = xs[:, 512:768]
    lora = xs[:, 768:896]
    w_pre = w0_ref[...] + _dot(jnp.tanh(lora), wup_ref[...])
    lw = -jnp.exp(-_softplus(-w_pre) - 0.5)
    a = _sigmoid(a0_ref[...] + _dot(lora, aup_ref[...]))
    g = _dot(_sigmoid(xs[:, 896:1024]), gup_ref[...])
    if has_vres:
        mix = _sigmoid(v0_ref[...] + _dot(c_ref[:, 1024:1152], vup_ref[...]))
        v = v + (vf_ref[...] - v) * mix
    else:
        vf_out_ref[...] = v
    kk = k * kk_ref[...]
    kk = kk * lax.rsqrt(_segsum(kk * kk, ones_bd) + 1e-24)
    k = k * (1.0 + (a - 1.0) * ka_ref[...])
    r_s[...] = r
    k_s[...] = k
    v_s[...] = v
    lw_s[...] = lw
    al_s[...] = -kk
    be_s[...] = kk * a

    c = RW_C
    tril_incl = (_iota((c, c), 0) >= _iota((c, c), 1)).astype(bf16)
    masks = _pair_masks(c)
    low_strict, low_incl = masks["strict"], masks["incl"]
    bd_mask = (_iota((2 * c, 2 * c), 0) // c == _iota((2 * c, 2 * c), 1) // c).astype(f32)

    nchunk = RW_TB // c
    npair = N_HEADS // 2
    items = [(n, h) for n in range(nchunk) for h in range(npair)]
    hsl = lambda h: slice(2 * h * HEAD_DIM, 2 * (h + 1) * HEAD_DIM)
    pre = []
    for n in range(nchunk):
        rows = slice(n * c, (n + 1) * c)
        lwc = lw_s[rows, :]
        lc = _dot_sel_lhs(tril_incl, lwc)
        llast = lc[c - 1:c, :]
        e_out = jnp.exp(-lc)
        e_rest = jnp.exp(llast - lc)
        kc, bec = k_s[rows, :], be_s[rows, :]
        pre.append(dict(rt=r_s[rows, :] * jnp.exp(lc), at=al_s[rows, :] * jnp.exp(lc - lwc),
                        bt=bec * e_out, kt=kc * e_out, bw=bec * e_rest, kw=kc * e_rest,
                        v=v_s[rows, :], dlast=jnp.exp(llast)))
    ms = [_dot(jnp.concatenate([pre[n]["at"][:, hsl(h)], pre[n]["rt"][:, hsl(h)]], axis=0),
               jnp.concatenate([_bd(pre[n]["bt"][:, hsl(h)]), _bd(pre[n]["kt"][:, hsl(h)])], axis=0), NT)
          for n, h in items]
    tms = _tri_inv([jnp.where(low_strict, -m[0:c, 0:2 * c], 0.0) for m in ms], masks)
    avs = [_dot(jnp.concatenate([jnp.where(low_strict, m[0:c, 2 * c:], 0.0),
                                 jnp.where(low_incl, m[c:, 2 * c:], 0.0)], axis=0),
                _bd(pre[n]["v"][:, hsl(h)]))
           for m, (n, h) in zip(ms, items)]
    tts = [_dot(tm, jnp.concatenate([_bd(pre[n]["at"][:, hsl(h)]), _bd(av[0:c])], axis=1))
           for tm, av, (n, h) in zip(tms, avs, items)]
    state = [s_ref[h] for h in range(npair)]
    for n in range(nchunk):
        p = pre[n]
        idx = [n * npair + h for h in range(npair)]
        sas = [_dot(jnp.concatenate([tts[i][:, 0:2 * c], p["rt"][:, hsl(h)]], axis=0), state[h], NT)
               for h, i in enumerate(idx)]
        us = [sa[0:c] + tts[i][:, 2 * c:] for sa, i in zip(sas, idx)]
        upd = [_dot(jnp.concatenate([u, p["v"][:, hsl(h)]], axis=0),
                    jnp.concatenate([p["bw"][:, hsl(h)], p["kw"][:, hsl(h)]], axis=0), TN) * bd_mask
               for h, u in enumerate(us)]
        state = [state[h] * p["dlast"][:, hsl(h)] + upd[h] for h in range(npair)]
        ys = [sas[h][c:] + _dot(jnp.where(low_incl, ms[i][c:, 0:2 * c], 0.0), _bd(us[h])) + avs[i][c:]
              for h, i in enumerate(idx)]
        y_s[n * c:(n + 1) * c, :] = jnp.concatenate(ys, axis=1)
    for h in range(npair):
        s_ref[h] = state[h]

    y = y_s[...]
    inv_d = 1.0 / HEAD_DIM
    mean = _segsum(y, ones_bd) * inv_d
    yc = y - mean
    var = _segsum(yc * yc, ones_bd) * inv_d
    y = yc * lax.rsqrt(var + RWKV_GN_EPS) * lw_ref[...] + lb_ref[...]
    y = y + _segsum(r * k * rk_ref[...], ones_bd) * v
    y_ref[...] = (y * g).astype(bf16)


def _rwkv(c_a, v_first, p):
    has_vres = v_first is not None
    row = lambda n: pl.BlockSpec((1, n), lambda i: (0, 0))
    full = lambda a, b: pl.BlockSpec((a, b), lambda i: (0, 0))
    blk = lambda n: pl.BlockSpec((RW_TB, n), lambda i: (i, 0))
    in_specs = [blk(NA)]
    args = [c_a]
    if has_vres:
        in_specs.append(blk(GROUP_W))
        args.append(v_first)
    in_specs += [row(1024), row(256), full(128, 256), row(256), full(128, 256), full(128, 256),
                 row(256), row(256), row(256), row(256), row(256)]
    args += [p["mu"], p["w0"], p["w_up"], p["a0"], p["a_up"], p["g_up"], p["k_k"], p["k_a"], p["r_k"],
             p["lnx_w"], p["lnx_b"]]
    if has_vres:
        in_specs += [row(256), full(128, 256)]
        args += [p["v0"], p["vres_up"]]
        out_specs = blk(GROUP_W)
        out_shape = jax.ShapeDtypeStruct((SEQ, GROUP_W), bf16)
    else:
        out_specs = [blk(GROUP_W), blk(GROUP_W)]
        out_shape = [jax.ShapeDtypeStruct((SEQ, GROUP_W), bf16),
                     jax.ShapeDtypeStruct((SEQ, GROUP_W), f32)]
    scratch = [pltpu.VMEM((8, 1024), f32), pltpu.VMEM((N_HEADS // 2, 2 * HEAD_DIM, 2 * HEAD_DIM), f32)]
    scratch += [pltpu.VMEM((RW_TB, GROUP_W), f32) for _ in range(7)]
    out = pl.pallas_call(
        functools.partial(_rwkv_kernel, has_vres=has_vres),
        grid=(SEQ // RW_TB,),
        in_specs=in_specs,
        out_specs=out_specs,
        out_shape=out_shape,
        scratch_shapes=scratch,
        compiler_params=pltpu.CompilerParams(dimension_semantics=("arbitrary",),
                                             vmem_limit_bytes=VMEM_LIMIT),
        name="rwkv7",
    )(*args)
    if has_vres:
        return out, v_first
    return out[0], out[1]


GD_TB = 512
GD_C = 64


def _gdn_kernel(c_ref, cw_ref, alog_ref, dtb_ref, alogc_ref, dtbc_ref, ng_ref, y_ref,
                carry_ref, s_ref, q_s, k_s, v_s, be_s, g_s, o_s):
    @pl.when(pl.program_id(0) == 0)
    def _():
        carry_ref[...] = jnp.zeros_like(carry_ref)
        s_ref[...] = jnp.zeros_like(s_ref)

    c = GD_C
    nchunk = GD_TB // c
    npair = N_HEADS // 2
    hsl = lambda h: slice(2 * h * HEAD_DIM, 2 * (h + 1) * HEAD_DIM)
    bd_mask = (_iota((2 * c, 2 * c), 0) // c == _iota((2 * c, 2 * c), 1) // c).astype(f32)
    ones_bd = _group_ones(GROUP_W, HEAD_DIM)

    raw = c_ref[:, 0:768]
    carry = carry_ref[...]
    conv = raw * cw_ref[3:4, :]
    for s in range(1, 4):
        conv = conv + _shift_rows(raw, carry, s) * cw_ref[3 - s:4 - s, :]
    carry_ref[...] = raw[GD_TB - 8:GD_TB]
    qkv = conv * _sigmoid(conv)
    q = qkv[:, 0:256]
    k = qkv[:, 256:512]
    q_s[...] = q * lax.rsqrt(_segsum(q * q, ones_bd) + 1e-6) * (HEAD_DIM ** -0.5)
    k_s[...] = k * lax.rsqrt(_segsum(k * k, ones_bd) + 1e-6)
    v_s[...] = qkv[:, 512:768]
    small = c_ref[:, 1024:1152]
    er, ec = _iota((128, GROUP_W), 0), _iota((128, GROUP_W), 1)
    b_exp = _dot_sel_rhs(small, (er == ec // HEAD_DIM).astype(bf16))
    a_exp = _dot_sel_rhs(small, (er == ec // HEAD_DIM + N_HEADS).astype(bf16))
    be_s[...] = _sigmoid(b_exp)
    g_s[...] = -jnp.exp(alog_ref[...]) * _softplus(a_exp + dtb_ref[...])

    tril_incl = (_iota((c, c), 0) >= _iota((c, c), 1)).astype(bf16)
    masks = _pair_masks(c)
    low_strict, low_incl = masks["strict"], masks["incl"]
    sel8 = (_iota((8, 128), 0) == _iota((8, 128), 1)).astype(bf16)
    g_rows = -jnp.exp(alogc_ref[...]) * _softplus(_dot_sel_lhs(sel8, small, NT) + dtbc_ref[...])
    tj, ti = _iota((GD_TB, GD_TB), 0), _iota((GD_TB, GD_TB), 1)
    gc_rows = _dot_sel_rhs(g_rows, ((tj // c == ti // c) & (tj <= ti)).astype(bf16))

    items = [(n, h) for n in range(nchunk) for h in range(npair)]
    pre = []
    for n in range(nchunk):
        rows = slice(n * c, (n + 1) * c)
        gc = _dot_sel_lhs(tril_incl, g_s[rows, :])
        glast = gc[c - 1:c, :]
        egc = jnp.exp(gc)
        kc, bc = k_s[rows, :], be_s[rows, :]
        kb = kc * bc
        pre.append(dict(gc=gc, k=kc, kb=kb, q=q_s[rows, :], vb=v_s[rows, :] * bc, kbe=kb * egc,
                        qe=q_s[rows, :] * egc, kd=kc * jnp.exp(glast - gc), eglast=jnp.exp(glast)))

    def gc_row(n, h):
        return jnp.concatenate([gc_rows[N_HEADS + 2 * h + i:N_HEADS + 2 * h + i + 1, n * c:(n + 1) * c]
                                for i in range(2)], axis=1)

    dms = [jnp.exp(jnp.where(low_incl, pre[n]["gc"][:, hsl(h)] - gc_row(n, h), NEG))
           for n, h in items]
    aqs = [_dot(jnp.concatenate([pre[n]["kb"][:, hsl(h)], pre[n]["q"][:, hsl(h)]], axis=0),
                _bd(pre[n]["k"][:, hsl(h)]), NT) for n, h in items]
    tms = _tri_inv([jnp.where(low_strict, aq[0:c] * dm, 0.0) for aq, dm in zip(aqs, dms)], masks)
    uws = [_dot(tm, jnp.concatenate([_bd(pre[n]["vb"][:, hsl(h)]), _bd(pre[n]["kbe"][:, hsl(h)])], axis=1))
           for tm, (n, h) in zip(tms, items)]
    state = [s_ref[h] for h in range(npair)]
    for n in range(nchunk):
        p = pre[n]
        idx = [n * npair + h for h in range(npair)]
        wss = [_dot(jnp.concatenate([uws[i][:, 2 * c:], p["qe"][:, hsl(h)]], axis=0), state[h])
               for h, i in enumerate(idx)]
        vns = [uws[i][:, 0:2 * c] - wss[h][0:c] for h, i in enumerate(idx)]
        upd = [_dot(p["kd"][:, hsl(h)], vns[h], TN) * bd_mask for h in range(npair)]
        state = [state[h] * p["eglast"][:, hsl(h)] + upd[h] for h in range(npair)]
        os_ = [wss[h][c:] + _dot(aqs[i][c:] * dms[i], _bd(vns[h])) for h, i in enumerate(idx)]
        o_s[n * c:(n + 1) * c, :] = jnp.concatenate(os_, axis=1)
    for h in range(npair):
        s_ref[h] = state[h]

    o = o_s[...]
    ms = _segsum(o * o, ones_bd) * (1.0 / HEAD_DIM)
    z = c_ref[:, 768:1024]
    y_ref[...] = (o * lax.rsqrt(ms + EPS) * ng_ref[...] * (z * _sigmoid(z))).astype(bf16)


def _gdn(c_b, p):
    row = lambda n: pl.BlockSpec((1, n), lambda i: (0, 0))
    scratch = [pltpu.VMEM((8, 768), f32), pltpu.VMEM((N_HEADS // 2, 2 * HEAD_DIM, 2 * HEAD_DIM), f32)]
    scratch += [pltpu.VMEM((GD_TB, GROUP_W), f32) for _ in range(6)]
    return pl.pallas_call(
        _gdn_kernel,
        grid=(SEQ // GD_TB,),
        in_specs=[pl.BlockSpec((GD_TB, NB), lambda i: (i, 0)),
                  pl.BlockSpec((4, 768), lambda i: (0, 0)), row(256), row(256),
                  pl.BlockSpec((8, 1), lambda i: (0, 0)), pl.BlockSpec((8, 1), lambda i: (0, 0)),
                  row(256)],
        out_specs=pl.BlockSpec((GD_TB, GROUP_W), lambda i: (i, 0)),
        out_shape=jax.ShapeDtypeStruct((SEQ, GROUP_W), bf16),
        scratch_shapes=scratch,
        compiler_params=pltpu.CompilerParams(dimension_semantics=("arbitrary",),
                                             vmem_limit_bytes=VMEM_LIMIT),
        name="gdn",
    )(c_b, p["conv_w"], p["a_log"], p["dt_bias"], p["a_log_col"], p["dt_bias_col"], p["norm_g"])


GL_TB = 256
GL_C = 16


def _gla_kernel(c_ref, gup_ref, gb_ref, ng_ref, y_ref, st_ref, sx_s, o_s):
    @pl.when(pl.program_id(0) == 0)
    def _():
        st_ref[...] = jnp.zeros_like(st_ref)

    tb, c = GL_TB, GL_C
    q = c_ref[:, 0:128] * (GLA_HEAD_K ** -0.5)
    k = c_ref[:, 128:256]
    pre = _dot(c_ref[:, 768:896], gup_ref[...]) + gb_ref[...]
    la = -_softplus(-pre) * (1.0 / 16.0)
    tj, ti = _iota((tb, tb), 0), _iota((tb, tb), 1)
    b = _dot_sel_lhs(((tj // c == ti // c) & (ti <= tj)).astype(bf16), la)
    qi = q * jnp.exp(b)

    ri = _iota((c, GLA_KEY), 0)
    ind_e = (_iota((GLA_KEY, GROUP_W), 0) // GLA_HEAD_K == _iota((GLA_KEY, GROUP_W), 1) // HEAD_DIM).astype(bf16)
    bd_mask = (_iota((GROUP_W, GLA_KEY), 0) // HEAD_DIM == _iota((GROUP_W, GLA_KEY), 1) // GLA_HEAD_K).astype(f32)
    nchunk = tb // c
    terms, blasts, upds = [], [], []
    for n in range(nchunk):
        rows = slice(n * c, (n + 1) * c)
        bn, qn, kn = b[rows], q[rows], k[rows]
        for j in range(c):
            e = jnp.exp(jnp.where(ri >= j, bn - bn[j:j + 1, :], NEG))
            terms.append((qn * (kn[j:j + 1, :] * e)).astype(bf16))
        blasts.append(bn[c - 1:c, :])
    sx_s[...] = jnp.dot(jnp.concatenate(terms, axis=0), ind_e, preferred_element_type=f32)
    for n in range(nchunk):
        rows = slice(n * c, (n + 1) * c)
        upds.append(_dot(c_ref[rows, 256:512], k[rows] * jnp.exp(blasts[n] - b[rows]), TN) * bd_mask)
    st = st_ref[...]
    for n in range(nchunk):
        rows = slice(n * c, (n + 1) * c)
        o = _dot(qi[rows], st, NT)
        st = st * jnp.exp(blasts[n]) + upds[n]
        for j in range(c):
            r0 = (n * c + j) * c
            o = o + sx_s[r0:r0 + c, :] * c_ref[n * c + j:n * c + j + 1, 256:512]
        o_s[rows, :] = o
    st_ref[...] = st

    o = o_s[...]
    ms = _segsum(o * o, _group_ones(GROUP_W, HEAD_DIM)) * (1.0 / HEAD_DIM)
    gate = c_ref[:, 512:768]
    y_ref[...] = (o * lax.rsqrt(ms + EPS) * ng_ref[...] * (gate * _sigmoid(gate))).astype(bf16)


def _gla(c_c, p):
    row = lambda n: pl.BlockSpec((1, n), lambda i: (0, 0))
    return pl.pallas_call(
        _gla_kernel,
        grid=(SEQ // GL_TB,),
        in_specs=[pl.BlockSpec((GL_TB, NC), lambda i: (i, 0)),
                  pl.BlockSpec((128, 128), lambda i: (0, 0)), row(128), row(256)],
        out_specs=pl.BlockSpec((GL_TB, GROUP_W), lambda i: (i, 0)),
        out_shape=jax.ShapeDtypeStruct((SEQ, GROUP_W), bf16),
        scratch_shapes=[pltpu.VMEM((GROUP_W, GLA_KEY), f32),
                        pltpu.VMEM((GL_TB * GL_C, GROUP_W), f32), pltpu.VMEM((GL_TB, GROUP_W), f32)],
        compiler_params=pltpu.CompilerParams(dimension_semantics=("arbitrary",),
                                             vmem_limit_bytes=VMEM_LIMIT),
        name="gla",
    )(c_c, p["gk_up"], p["gk_bias"], p["norm_g"])


SG_TB = 512
SG_C = 128


def _sgu_kernel(c_ref, lg_ref, lb_ref, w_ref, bias_ref, y_ref):
    x = c_ref[...]
    gx = 0.5 * x * (1.0 + jnp.tanh(0.7978845608028654 * (x + 0.044715 * x * x * x)))
    u = gx[:, 0:256]
    v = gx[:, 256:512]
    mu = jnp.mean(v, axis=-1, keepdims=True)
    vc = v - mu
    var = jnp.mean(vc * vc, axis=-1, keepdims=True)
    v = vc * lax.rsqrt(var + 1e-5) * lg_ref[...] + lb_ref[...]
    wr, wc = _iota((SG_C, 4 * SG_C), 0), _iota((SG_C, 4 * SG_C), 1)
    w = jnp.where(wc % SG_C <= wr, w_ref[...], 0.0).astype(bf16)
    lane_g = _iota((SG_C, GROUP_W), 1) // HEAD_DIM
    outs = []
    for n in range(SG_TB // SG_C):
        vn = v[n * SG_C:(n + 1) * SG_C, :]
        vst = jnp.concatenate([jnp.where(lane_g == g, vn, 0.0) for g in range(4)], axis=0)
        outs.append(jnp.dot(w, vst.astype(bf16), preferred_element_type=f32) + bias_ref[...])
    y_ref[...] = (u * jnp.concatenate(outs, axis=0)).astype(bf16)


def _sgu(c_d, p):
    row = lambda n: pl.BlockSpec((1, n), lambda i: (0, 0))
    return pl.pallas_call(
        _sgu_kernel,
        grid=(SEQ // SG_TB,),
        in_specs=[pl.BlockSpec((SG_TB, ND), lambda i: (i, 0)), row(256), row(256),
                  pl.BlockSpec((SG_C, 4 * SG_C), lambda i: (0, 0)),
                  pl.BlockSpec((SG_C, GROUP_W), lambda i: (0, 0))],
        out_specs=pl.BlockSpec((SG_TB, GROUP_W), lambda i: (i, 0)),
        out_shape=jax.ShapeDtypeStruct((SEQ, GROUP_W), bf16),
        compiler_params=pltpu.CompilerParams(dimension_semantics=("arbitrary",),
                                             vmem_limit_bytes=VMEM_LIMIT),
        name="sgu",
    )(c_d, p["ln_g"], p["ln_b"], p["w_cat"], p["bias_tile"])


FF_TM = 512
FF_TF = 1024


def _outffn_kernel(*refs, final):
    if final:
        (x_ref, ya_ref, yb_ref, yc_ref, yd_ref, wo_hbm, g_ref, wu_hbm, wd_hbm, gf_ref, o_ref,
         wo_ref, wu_ref, wd_ref, sem) = refs
    else:
        (x_ref, ya_ref, yb_ref, yc_ref, yd_ref, wo_hbm, g_ref, wu_hbm, wd_hbm, o_ref,
         wo_ref, wu_ref, wd_ref, sem) = refs

    @pl.when(pl.program_id(0) == 0)
    def _():
        copies = [pltpu.make_async_copy(src, dst, sem.at[i])
                  for i, (src, dst) in enumerate(((wo_hbm, wo_ref), (wu_hbm, wu_ref), (wd_hbm, wd_ref)))]
        for cp in copies:
            cp.start()
        for cp in copies:
            cp.wait()

    y = jnp.concatenate([ya_ref[...], yb_ref[...], yc_ref[...], yd_ref[...]], axis=1)
    x1 = x_ref[...] + jnp.dot(y, wo_ref[...], preferred_element_type=f32)
    ms = jnp.mean(x1 * x1, axis=-1, keepdims=True)
    h = (x1 * lax.rsqrt(ms + EPS) * g_ref[...]).astype(bf16)
    x2 = x1
    for kf in range(D_FF // FF_TF):
        cols = slice(kf * FF_TF, (kf + 1) * FF_TF)
        hid = jnp.maximum(jnp.dot(h, wu_ref[:, cols], preferred_element_type=f32), 0.0)
        x2 = x2 + jnp.dot((hid * hid).astype(bf16), wd_ref[cols, :], preferred_element_type=f32)
    if final:
        ms = jnp.mean(x2 * x2, axis=-1, keepdims=True)
        x2 = x2 * lax.rsqrt(ms + EPS) * gf_ref[...]
    o_ref[...] = x2


def _out_ffn(x, ys, w_out, g, w_up, w_down, g_final):
    final = g_final is not None
    hbm = pl.BlockSpec(memory_space=pl.ANY)
    row = pl.BlockSpec((1, D_MODEL), lambda i: (0, 0))
    yspec = pl.BlockSpec((FF_TM, GROUP_W), lambda i: (i, 0))
    in_specs = [pl.BlockSpec((FF_TM, D_MODEL), lambda i: (i, 0)), yspec, yspec, yspec, yspec,
                hbm, row, hbm, hbm]
    args = [x, *ys, w_out, g, w_up, w_down]
    if final:
        in_specs.append(row)
        args.append(g_final)
    return pl.pallas_call(
        functools.partial(_outffn_kernel, final=final),
        grid=(SEQ // FF_TM,),
        in_specs=in_specs,
        out_specs=pl.BlockSpec((FF_TM, D_MODEL), lambda i: (i, 0)),
        out_shape=jax.ShapeDtypeStruct((SEQ, D_MODEL), f32),
        scratch_shapes=[pltpu.VMEM((D_MODEL, D_MODEL), bf16), pltpu.VMEM((D_MODEL, D_FF), bf16),
                        pltpu.VMEM((D_FF, D_MODEL), bf16), pltpu.SemaphoreType.DMA((3,))],
        compiler_params=pltpu.CompilerParams(dimension_semantics=("arbitrary",),
                                             vmem_limit_bytes=VMEM_LIMIT),
        name="out_ffn",
    )(*args)


def _pad_cols(w, n):
    return jnp.pad(w, ((0, 0), (0, n - w.shape[1])))


def _pad_rows(w, top, total):
    return jnp.pad(w, ((top, total - top - w.shape[0]), (0, 0)))


def kernel(x, w_in, w_out, norm_mix_g, norm_ffn_g, norm_final_g, rwkv_mu, rwkv_w0, rwkv_w_up, rwkv_a0, rwkv_a_up, rwkv_g_up, rwkv_k_k, rwkv_k_a, rwkv_r_k, rwkv_lnx_w, rwkv_lnx_b, rwkv_v0, rwkv_vres_down, rwkv_vres_up, gdn_conv_w, gdn_a_log, gdn_dt_bias, gdn_norm_g, gla_gk_up, gla_gk_bias, gla_norm_g, sgu_ln_g, sgu_ln_b, sgu_w_s, sgu_b_s, ffn_w_up, ffn_w_down):
    depth = w_in.shape[0]
    xx = x[0]
    v_first = None
    r2 = lambda a: a.reshape(1, -1)
    per_head = lambda a: jnp.repeat(a, HEAD_DIM).reshape(1, -1)
    for l in range(depth):
        wl = w_in[l]
        w_a = wl[:, 0:1024]
        if l > 0:
            w_a = jnp.concatenate([w_a, rwkv_vres_down[l - 1]], axis=1)
        w_comb = jnp.concatenate([_pad_cols(w_a, NA), _pad_cols(wl[:, 1024:2056], NB),
                                  _pad_cols(wl[:, 2056:2840], NC), wl[:, 2840:3352]], axis=1).astype(bf16)
        c_a, c_b, c_c, c_d = _in_proj(xx, r2(norm_mix_g[l]), w_comb)

        pa = dict(mu=r2(rwkv_mu[l]), w0=r2(rwkv_w0[l]), w_up=_pad_rows(rwkv_w_up[l], 0, 128),
                  a0=r2(rwkv_a0[l]), a_up=_pad_rows(rwkv_a_up[l], 64, 128), g_up=rwkv_g_up[l],
                  k_k=r2(rwkv_k_k[l]), k_a=r2(rwkv_k_a[l]), r_k=r2(rwkv_r_k[l]),
                  lnx_w=r2(rwkv_lnx_w[l]), lnx_b=r2(rwkv_lnx_b[l]))
        if l > 0:
            pa.update(v0=r2(rwkv_v0[l - 1]), vres_up=_pad_rows(rwkv_vres_up[l - 1], 0, 128))
        y_a, v_first = _rwkv(c_a, v_first, pa)

        y_b = _gdn(c_b, dict(conv_w=gdn_conv_w[l], a_log=per_head(gdn_a_log[l]),
                             dt_bias=per_head(gdn_dt_bias[l]),
                             a_log_col=jnp.pad(gdn_a_log[l], (N_HEADS, 0)).reshape(8, 1),
                             dt_bias_col=jnp.pad(gdn_dt_bias[l], (N_HEADS, 0)).reshape(8, 1), norm_g=r2(jnp.tile(gdn_norm_g[l], N_HEADS))))
        y_c = _gla(c_c, dict(gk_up=_pad_rows(gla_gk_up[l], 0, 128), gk_bias=r2(gla_gk_bias[l]),
                             norm_g=r2(jnp.tile(gla_norm_g[l], N_HEADS))))
        y_d = _sgu(c_d, dict(ln_g=r2(sgu_ln_g[l]), ln_b=r2(sgu_ln_b[l]),
                             w_cat=sgu_w_s[l].transpose(1, 0, 2).reshape(SG_C, 4 * SG_C),
                             bias_tile=jnp.repeat(sgu_b_s[l].T, HEAD_DIM, axis=1)))
        xx = _out_ffn(xx, (y_a, y_b, y_c, y_d), w_out[l].astype(bf16), r2(norm_ffn_g[l]),
                      ffn_w_up[l].astype(bf16), ffn_w_down[l].astype(bf16),
                      r2(norm_final_g) if l == depth - 1 else None)
    return xx[None]
```

```python
import functools

import jax
import jax.numpy as jnp
from jax import lax
from jax.experimental import pallas as pl
from jax.experimental.pallas import tpu as pltpu

f32 = jnp.float32
bf16 = jnp.bfloat16

SEQ = 16384
D_MODEL = 1024
GROUP_W = 256
HEAD_DIM = 64
N_HEADS = 4
GLA_KEY = 128
GLA_HEAD_K = 32
D_FF = 4096
EPS = 1e-6
RWKV_GN_EPS = 64e-5
NEG = -1e30

NA, NB, NC, ND = 1152, 1152, 896, 512
N_PAD = NA + NB + NC + ND

VMEM_LIMIT = 56 * 1024 * 1024

NN = (((1,), (0,)), ((), ()))
NT = (((1,), (1,)), ((), ()))
TN = (((0,), (0,)), ((), ()))


def _dot(a, b, dims=NN):
    return lax.dot_general(a.astype(bf16), b.astype(bf16), dims, preferred_element_type=f32)


def _iota(shape, axis):
    return lax.broadcasted_iota(jnp.int32, shape, axis)


def _layer_spec(a, l):
    return pl.BlockSpec((None,) + a.shape[1:], lambda *_: (l,) + (0,) * (a.ndim - 1))


def _segsum(x, ones_bd):
    hi = x.astype(bf16)
    lo = (x - hi.astype(f32)).astype(bf16)
    return (jnp.dot(hi, ones_bd, preferred_element_type=f32)
            + jnp.dot(lo, ones_bd, preferred_element_type=f32))


def _split3(x):
    hi = x.astype(bf16)
    r1 = x - hi.astype(f32)
    mid = r1.astype(bf16)
    lo = (r1 - mid.astype(f32)).astype(bf16)
    return hi, mid, lo


def _dot_sel_lhs(sel, x, dims=NN):
    return sum(lax.dot_general(sel, t, dims, preferred_element_type=f32) for t in _split3(x))


def _dot_sel_rhs(x, sel, dims=NN):
    return sum(lax.dot_general(t, sel, dims, preferred_element_type=f32) for t in _split3(x))


def _group_ones(n, width):
    return (_iota((n, n), 0) // width == _iota((n, n), 1) // width).astype(bf16)


def _sigmoid(x):
    return 1.0 / (1.0 + jnp.exp(-x))


def _softplus(x):
    return jnp.maximum(x, 0.0) + jnp.log1p(jnp.exp(-jnp.abs(x)))


def _shift_rows(x, carry, s):
    xs = pltpu.roll(x, s, 0)
    fix = pltpu.roll(carry, s, 0)
    first = jnp.where(_iota(carry.shape, 0) < s, fix, xs[0:8])
    return jnp.concatenate([first, xs[8:]], axis=0)


def _bd(xp):
    xb = xp.astype(bf16)
    left = _iota(xb.shape, 1) < HEAD_DIM
    zero = jnp.zeros_like(xb)
    return jnp.concatenate([jnp.where(left, xb, zero), jnp.where(left, zero, xb)], axis=0)


def _pair_masks(c):
    ri, cj = _iota((c, 2 * c), 0), _iota((c, 2 * c), 1) & (c - 1)
    eye = (ri == cj).astype(f32)
    m16 = (ri // 16 == cj // 16).astype(f32)
    mo1 = ((ri // 32 == cj // 32) & (ri // 16 == cj // 16 + 1)).astype(f32)
    mo2 = ((ri // 32 == 1) & (cj // 32 == 0)).astype(f32)
    return dict(eye=eye, m16=m16, mo1=mo1, mo2=mo2, strict=ri > cj, incl=ri >= cj)


def _tri_inv(lms, masks):
    c = RW_C
    ps = [-(lm * masks["m16"]) for lm in lms]
    ts = [masks["eye"] + p for p in ps]
    ps = [_dot(p, _bd(p)) for p in ps]
    for _ in range(2):
        outs = [_dot(jnp.concatenate([t, p], axis=0), _bd(p)) for t, p in zip(ts, ps)]
        ts = [t + o[0:c] for t, o in zip(ts, outs)]
        ps = [o[c:] for o in outs]
    ts = [t + _dot(t, _bd(p)) for t, p in zip(ts, ps)]
    for mo in (masks["mo1"], masks["mo2"]):
        xs = [_dot(lm * mo, _bd(t)) for lm, t in zip(lms, ts)]
        ts = [t - _dot(t, _bd(x)) for t, x in zip(ts, xs)]
    return ts


IN_TM = 512
SG_C = 128


def _sgu_mix(x, lg_ref, lb_ref, w_ref, bias_ref):
    gx = 0.5 * x * (1.0 + jnp.tanh(0.7978845608028654 * (x + 0.044715 * x * x * x)))
    u = gx[:, 0:256]
    v = gx[:, 256:512]
    mu = jnp.mean(v, axis=-1, keepdims=True)
    vc = v - mu
    var = jnp.mean(vc * vc, axis=-1, keepdims=True)
    v = vc * lax.rsqrt(var + 1e-5) * lg_ref[...] + lb_ref[...]
    wr, wc = _iota((SG_C, 4 * SG_C), 0), _iota((SG_C, 4 * SG_C), 1)
    w = jnp.where(wc % SG_C <= wr, w_ref[...], 0.0).astype(bf16)
    lane_g = _iota((SG_C, GROUP_W), 1) // HEAD_DIM
    outs = []
    for n in range(x.shape[0] // SG_C):
        vn = v[n * SG_C:(n + 1) * SG_C, :]
        vst = jnp.concatenate([jnp.where(lane_g == g, vn, 0.0) for g in range(4)], axis=0)
        outs.append(jnp.dot(w, vst.astype(bf16), preferred_element_type=f32) + bias_ref[...])
    return (u * jnp.concatenate(outs, axis=0)).astype(bf16)


def _inproj_kernel(x_ref, g_ref, w_ref, lg_ref, lb_ref, ws_ref, bias_ref, oa_ref, ob_ref, oc_ref, yd_ref):
    x = x_ref[...]
    ms = jnp.mean(x * x, axis=-1, keepdims=True)
    h = (x * lax.rsqrt(ms + EPS) * g_ref[...]).astype(bf16)
    c_d = jnp.dot(h, w_ref[:, NA + NB + NC:], preferred_element_type=f32)
    yd_ref[...] = _sgu_mix(c_d, lg_ref, lb_ref, ws_ref, bias_ref)
    off = 0
    for o_ref, n in ((oa_ref, NA), (ob_ref, NB), (oc_ref, NC)):
        o_ref[...] = jnp.dot(h, w_ref[:, off:off + n], preferred_element_type=f32)
        off += n


def _in_proj(x, p, l):
    tm = IN_TM
    names = ("g", "w", "ln_g", "ln_b", "w_cat", "bias_tile")
    return pl.pallas_call(
        _inproj_kernel,
        grid=(SEQ // tm,),
        in_specs=[pl.BlockSpec((tm, D_MODEL), lambda i: (i, 0))] + [_layer_spec(p[k], l) for k in names],
        out_specs=[pl.BlockSpec((tm, n), lambda i: (i, 0)) for n in (NA, NB, NC, GROUP_W)],
        out_shape=[jax.ShapeDtypeStruct((SEQ, n), f32) for n in (NA, NB, NC)]
        + [jax.ShapeDtypeStruct((SEQ, GROUP_W), bf16)],
        compiler_params=pltpu.CompilerParams(dimension_semantics=("arbitrary",),
                                             vmem_limit_bytes=VMEM_LIMIT),
        name="in_proj",
    )(x, *[p[k] for k in names])


RW_TB = 512
RW_C = 64


def _rwkv_kernel(*refs, has_vres):
    if has_vres:
        (c_ref, vf_ref, mu_ref, w0_ref, wup_ref, a0_ref, aup_ref, gup_ref, kk_ref, ka_ref, rk_ref,
         lw_ref, lb_ref, v0_ref, vup_ref, y_ref,
         carry_ref, s_ref, r_s, k_s, v_s, lw_s, al_s, be_s, y_s) = refs
    else:
        (c_ref, mu_ref, w0_ref, wup_ref, a0_ref, aup_ref, gup_ref, kk_ref, ka_ref, rk_ref,
         lw_ref, lb_ref, y_ref, vf_out_ref,
         carry_ref, s_ref, r_s, k_s, v_s, lw_s, al_s, be_s, y_s) = refs

    @pl.when(pl.program_id(0) == 0)
    def _():
        carry_ref[...] = jnp.zeros_like(carry_ref)
        s_ref[...] = jnp.zeros_like(s_ref)

    ones_bd = _group_ones(GROUP_W, HEAD_DIM)
    x = c_ref[:, 0:1024]
    x_prev = _shift_rows(x, carry_ref[...], 1)
    carry_ref[...] = x[RW_TB - 8:RW_TB]
    xs = x + (x_prev - x) * mu_ref[...]
    r = xs[:, 0:256]
    k = xs[:, 256:512]
    v = xs[:, 512:768]
    lora = xs[:, 768:896]
    w_pre = w0_ref[...] + _dot(jnp.tanh(lora), wup_ref[...])
    lw = -jnp.exp(-_softplus(-w_pre) - 0.5)
    a = _sigmoid(a0_ref[...] + _dot(lora, aup_ref[...]))
    g = _dot(_sigmoid(xs[:, 896:1024]), gup_ref[...])
    if has_vres:
        mix = _sigmoid(v0_ref[...] + _dot(c_ref[:, 1024:1152], vup_ref[...]))
        v = v + (vf_ref[...] - v) * mix
    else:
        vf_out_ref[...] = v
    kk = k * kk_ref[...]
    kk = kk * lax.rsqrt(_segsum(kk * kk, ones_bd) + 1e-24)
    k = k * (1.0 + (a - 1.0) * ka_ref[...])
    r_s[...] = r
    k_s[...] = k
    v_s[...] = v
    lw_s[...] = lw
    al_s[...] = -kk
    be_s[...] = kk * a

    c = RW_C
    tril_incl = (_iota((c, c), 0) >= _iota((c, c), 1)).astype(bf16)
    masks = _pair_masks(c)
    low_strict, low_incl = masks["strict"], masks["incl"]
    bd_mask = (_iota((2 * c, 2 * c), 0) // c == _iota((2 * c, 2 * c), 1) // c).astype(f32)

    nchunk = RW_TB // c
    npair = N_HEADS // 2
    items = [(n, h) for n in range(nchunk) for h in range(npair)]
    hsl = lambda h: slice(2 * h * HEAD_DIM, 2 * (h + 1) * HEAD_DIM)
    pre = []
    for n in range(nchunk):
        rows = slice(n * c, (n + 1) * c)
        lwc = lw_s[rows, :]
        lc = _dot_sel_lhs(tril_incl, lwc)
        llast = lc[c - 1:c, :]
        e_out = jnp.exp(-lc)
        e_rest = jnp.exp(llast - lc)
        kc, bec = k_s[rows, :], be_s[rows, :]
        pre.append(dict(rt=r_s[rows, :] * jnp.exp(lc), at=al_s[rows, :] * jnp.exp(lc - lwc),
                        bt=bec * e_out, kt=kc * e_out, bw=bec * e_rest, kw=kc * e_rest,
                        v=v_s[rows, :], dlast=jnp.exp(llast)))
    ms = [_dot(jnp.concatenate([pre[n]["at"][:, hsl(h)], pre[n]["rt"][:, hsl(h)]], axis=0),
               jnp.concatenate([_bd(pre[n]["bt"][:, hsl(h)]), _bd(pre[n]["kt"][:, hsl(h)])], axis=0), NT)
          for n, h in items]
    tms = _tri_inv([jnp.where(low_strict, -m[0:c, 0:2 * c], 0.0) for m in ms], masks)
    avs = [_dot(jnp.concatenate([jnp.where(low_strict, m[0:c, 2 * c:], 0.0),
                                 jnp.where(low_incl, m[c:, 2 * c:], 0.0)], axis=0),
                _bd(pre[n]["v"][:, hsl(h)]))
           for m, (n, h) in zip(ms, items)]
    tts = [_dot(tm, jnp.concatenate([_bd(pre[n]["at"][:, hsl(h)]), _bd(av[0:c])], axis=1))
           for tm, av, (n, h) in zip(tms, avs, items)]
    state = [s_ref[h] for h in range(npair)]
    for n in range(nchunk):
        p = pre[n]
        idx = [n * npair + h for h in range(npair)]
        sas = [_dot(jnp.concatenate([tts[i][:, 0:2 * c], p["rt"][:, hsl(h)]], axis=0), state[h], NT)
               for h, i in enumerate(idx)]
        us = [sa[0:c] + tts[i][:, 2 * c:] for sa, i in zip(sas, idx)]
        upd = [_dot(jnp.concatenate([u, p["v"][:, hsl(h)]], axis=0),
                    jnp.concatenate([p["bw"][:, hsl(h)], p["kw"][:, hsl(h)]], axis=0), TN) * bd_mask
               for h, u in enumerate(us)]
        state = [state[h] * p["dlast"][:, hsl(h)] + upd[h] for h in range(npair)]
        ys = [sas[h][c:] + _dot(jnp.where(low_incl, ms[i][c:, 0:2 * c], 0.0), _bd(us[h])) + avs[i][c:]
              for h, i in enumerate(idx)]
        y_s[n * c:(n + 1) * c, :] = jnp.concatenate(ys, axis=1)
    for h in range(npair):
        s_ref[h] = state[h]

    y = y_s[...]
    inv_d = 1.0 / HEAD_DIM
    mean = _segsum(y, ones_bd) * inv_d
    yc = y - mean
    var = _segsum(yc * yc, ones_bd) * inv_d
    y = yc * lax.rsqrt(var + RWKV_GN_EPS) * lw_ref[...] + lb_ref[...]
    y = y + _segsum(r * k * rk_ref[...], ones_bd) * v
    y_ref[...] = (y * g).astype(bf16)


def _rwkv(c_a, v_first, p, l):
    has_vres = v_first is not None
    blk = lambda n: pl.BlockSpec((RW_TB, n), lambda i: (i, 0))
    in_specs = [blk(NA)]
    args = [c_a]
    if has_vres:
        in_specs.append(blk(GROUP_W))
        args.append(v_first)
    names = ("mu", "w0", "w_up", "a0", "a_up", "g_up", "k_k", "k_a", "r_k", "lnx_w", "lnx_b")
    in_specs += [_layer_spec(p[k], l) for k in names]
    args += [p[k] for k in names]
    if has_vres:
        in_specs += [_layer_spec(p["v0"], l - 1), _layer_spec(p["vres_up"], l - 1)]
        args += [p["v0"], p["vres_up"]]
        out_specs = blk(GROUP_W)
        out_shape = jax.ShapeDtypeStruct((SEQ, GROUP_W), bf16)
    else:
        out_specs = [blk(GROUP_W), blk(GROUP_W)]
        out_shape = [jax.ShapeDtypeStruct((SEQ, GROUP_W), bf16),
                     jax.ShapeDtypeStruct((SEQ, GROUP_W), f32)]
    scratch = [pltpu.VMEM((8, 1024), f32), pltpu.VMEM((N_HEADS // 2, 2 * HEAD_DIM, 2 * HEAD_DIM), f32)]
    scratch += [pltpu.VMEM((RW_TB, GROUP_W), f32) for _ in range(7)]
    out = pl.pallas_call(
        functools.partial(_rwkv_kernel, has_vres=has_vres),
        grid=(SEQ // RW_TB,),
        in_specs=in_specs,
        out_specs=out_specs,
        out_shape=out_shape,
        scratch_shapes=scratch,
        compiler_params=pltpu.CompilerParams(dimension_semantics=("arbitrary",),
                                             vmem_limit_bytes=VMEM_LIMIT),
        name="rwkv7",
    )(*args)
    if has_vres:
        return out, v_first
    return out[0], out[1]


GD_TB = 512
GD_C = 64


def _gdn_kernel(c_ref, cw_ref, alog_ref, dtb_ref, alogc_ref, dtbc_ref, ng_ref, y_ref,
                carry_ref, s_ref, q_s, k_s, v_s, be_s, g_s, o_s):
    @pl.when(pl.program_id(0) == 0)
    def _():
        carry_ref[...] = jnp.zeros_like(carry_ref)
        s_ref[...] = jnp.zeros_like(s_ref)

    c = GD_C
    nchunk = GD_TB // c
    npair = N_HEADS // 2
    hsl = lambda h: slice(2 * h * HEAD_DIM, 2 * (h + 1) * HEAD_DIM)
    bd_mask = (_iota((2 * c, 2 * c), 0) // c == _iota((2 * c, 2 * c), 1) // c).astype(f32)
    ones_bd = _group_ones(GROUP_W, HEAD_DIM)

    raw = c_ref[:, 0:768]
    carry = carry_ref[...]
    conv = raw * cw_ref[3:4, :]
    for s in range(1, 4):
        conv = conv + _shift_rows(raw, carry, s) * cw_ref[3 - s:4 - s, :]
    carry_ref[...] = raw[GD_TB - 8:GD_TB]
    qkv = conv * _sigmoid(conv)
    q = qkv[:, 0:256]
    k = qkv[:, 256:512]
    q_s[...] = q * lax.rsqrt(_segsum(q * q, ones_bd) + 1e-6) * (HEAD_DIM ** -0.5)
    k_s[...] = k * lax.rsqrt(_segsum(k * k, ones_bd) + 1e-6)
    v_s[...] = qkv[:, 512:768]
    small = c_ref[:, 1024:1152]
    er, ec = _iota((128, GROUP_W), 0), _iota((128, GROUP_W), 1)
    b_exp = _dot_sel_rhs(small, (er == ec // HEAD_DIM).astype(bf16))
    a_exp = _dot_sel_rhs(small, (er == ec // HEAD_DIM + N_HEADS).astype(bf16))
    be_s[...] = _sigmoid(b_exp)
    g_s[...] = -jnp.exp(alog_ref[...]) * _softplus(a_exp + dtb_ref[...])

    tril_incl = (_iota((c, c), 0) >= _iota((c, c), 1)).astype(bf16)
    masks = _pair_masks(c)
    low_strict, low_incl = masks["strict"], masks["incl"]
    sel8 = (_iota((8, 128), 0) == _iota((8, 128), 1)).astype(bf16)
    g_rows = -jnp.exp(alogc_ref[...]) * _softplus(_dot_sel_lhs(sel8, small, NT) + dtbc_ref[...])
    tj, ti = _iota((GD_TB, GD_TB), 0), _iota((GD_TB, GD_TB), 1)
    gc_rows = _dot_sel_rhs(g_rows, ((tj // c == ti // c) & (tj <= ti)).astype(bf16))

    items = [(n, h) for n in range(nchunk) for h in range(npair)]
    pre = []
    for n in range(nchunk):
        rows = slice(n * c, (n + 1) * c)
        gc = _dot_sel_lhs(tril_incl, g_s[rows, :])
        glast = gc[c - 1:c, :]
        egc = jnp.exp(gc)
        kc, bc = k_s[rows, :], be_s[rows, :]
        kb = kc * bc
        pre.append(dict(gc=gc, k=kc, kb=kb, q=q_s[rows, :], vb=v_s[rows, :] * bc, kbe=kb * egc,
                        qe=q_s[rows, :] * egc, kd=kc * jnp.exp(glast - gc), eglast=jnp.exp(glast)))

    def gc_row(n, h):
        return jnp.concatenate([gc_rows[N_HEADS + 2 * h + i:N_HEADS + 2 * h + i + 1, n * c:(n + 1) * c]
                                for i in range(2)], axis=1)

    dms = [jnp.exp(jnp.where(low_incl, pre[n]["gc"][:, hsl(h)] - gc_row(n, h), NEG))
           for n, h in items]
    aqs = [_dot(jnp.concatenate([pre[n]["kb"][:, hsl(h)], pre[n]["q"][:, hsl(h)]], axis=0),
                _bd(pre[n]["k"][:, hsl(h)]), NT) for n, h in items]
    tms = _tri_inv([jnp.where(low_strict, aq[0:c] * dm, 0.0) for aq, dm in zip(aqs, dms)], masks)
    uws = [_dot(tm, jnp.concatenate([_bd(pre[n]["vb"][:, hsl(h)]), _bd(pre[n]["kbe"][:, hsl(h)])], axis=1))
           for tm, (n, h) in zip(tms, items)]
    state = [s_ref[h] for h in range(npair)]
    for n in range(nchunk):
        p = pre[n]
        idx = [n * npair + h for h in range(npair)]
        wss = [_dot(jnp.concatenate([uws[i][:, 2 * c:], p["qe"][:, hsl(h)]], axis=0), state[h])
               for h, i in enumerate(idx)]
        vns = [uws[i][:, 0:2 * c] - wss[h][0:c] for h, i in enumerate(idx)]
        upd = [_dot(p["kd"][:, hsl(h)], vns[h], TN) * bd_mask for h in range(npair)]
        state = [state[h] * p["eglast"][:, hsl(h)] + upd[h] for h in range(npair)]
        os_ = [wss[h][c:] + _dot(aqs[i][c:] * dms[i], _bd(vns[h])) for h, i in enumerate(idx)]
        o_s[n * c:(n + 1) * c, :] = jnp.concatenate(os_, axis=1)
    for h in range(npair):
        s_ref[h] = state[h]

    o = o_s[...]
    ms = _segsum(o * o, ones_bd) * (1.0 / HEAD_DIM)
    z = c_ref[:, 768:1024]
    y_ref[...] = (o * lax.rsqrt(ms + EPS) * ng_ref[...] * (z * _sigmoid(z))).astype(bf16)


def _gdn(c_b, p, l):
    names = ("conv_w", "a_log", "dt_bias", "a_log_col", "dt_bias_col", "norm_g")
    scratch = [pltpu.VMEM((8, 768), f32), pltpu.VMEM((N_HEADS // 2, 2 * HEAD_DIM, 2 * HEAD_DIM), f32)]
    scratch += [pltpu.VMEM((GD_TB, GROUP_W), f32) for _ in range(6)]
    return pl.pallas_call(
        _gdn_kernel,
        grid=(SEQ // GD_TB,),
        in_specs=[pl.BlockSpec((GD_TB, NB), lambda i: (i, 0))] + [_layer_spec(p[k], l) for k in names],
        out_specs=pl.BlockSpec((GD_TB, GROUP_W), lambda i: (i, 0)),
        out_shape=jax.ShapeDtypeStruct((SEQ, GROUP_W), bf16),
        scratch_shapes=scratch,
        compiler_params=pltpu.CompilerParams(dimension_semantics=("arbitrary",),
                                             vmem_limit_bytes=VMEM_LIMIT),
        name="gdn",
    )(c_b, *[p[k] for k in names])


GL_TB = 256
GL_C = 16


def _gla_kernel(c_ref, gup_ref, gb_ref, ng_ref, y_ref, st_ref, sx_s, o_s):
    @pl.when(pl.program_id(0) == 0)
    def _():
        st_ref[...] = jnp.zeros_like(st_ref)

    tb, c = GL_TB, GL_C
    q = c_ref[:, 0:128] * (GLA_HEAD_K ** -0.5)
    k = c_ref[:, 128:256]
    pre = _dot(c_ref[:, 768:896], gup_ref[...]) + gb_ref[...]
    la = -_softplus(-pre) * (1.0 / 16.0)
    tj, ti = _iota((tb, tb), 0), _iota((tb, tb), 1)
    b = _dot_sel_lhs(((tj // c == ti // c) & (ti <= tj)).astype(bf16), la)
    qi = q * jnp.exp(b)

    ri = _iota((c, GLA_KEY), 0)
    ind_e = (_iota((GLA_KEY, GROUP_W), 0) // GLA_HEAD_K == _iota((GLA_KEY, GROUP_W), 1) // HEAD_DIM).astype(bf16)
    bd_mask = (_iota((GROUP_W, GLA_KEY), 0) // HEAD_DIM == _iota((GROUP_W, GLA_KEY), 1) // GLA_HEAD_K).astype(f32)
    nchunk = tb // c
    terms, blasts, upds = [], [], []
    for n in range(nchunk):
        rows = slice(n * c, (n + 1) * c)
        bn, qn, kn = b[rows], q[rows], k[rows]
        for j in range(c):
            e = jnp.exp(jnp.where(ri >= j, bn - bn[j:j + 1, :], NEG))
            terms.append((qn * (kn[j:j + 1, :] * e)).astype(bf16))
        blasts.append(bn[c - 1:c, :])
    sx_s[...] = jnp.dot(jnp.concatenate(terms, axis=0), ind_e, preferred_element_type=f32)
    for n in range(nchunk):
        rows = slice(n * c, (n + 1) * c)
        upds.append(_dot(c_ref[rows, 256:512], k[rows] * jnp.exp(blasts[n] - b[rows]), TN) * bd_mask)
    st = st_ref[...]
    for n in range(nchunk):
        rows = slice(n * c, (n + 1) * c)
        o = _dot(qi[rows], st, NT)
        st = st * jnp.exp(blasts[n]) + upds[n]
        for j in range(c):
            r0 = (n * c + j) * c
            o = o + sx_s[r0:r0 + c, :] * c_ref[n * c + j:n * c + j + 1, 256:512]
        o_s[rows, :] = o
    st_ref[...] = st

    o = o_s[...]
    ms = _segsum(o * o, _group_ones(GROUP_W, HEAD_DIM)) * (1.0 / HEAD_DIM)
    gate = c_ref[:, 512:768]
    y_ref[...] = (o * lax.rsqrt(ms + EPS) * ng_ref[...] * (gate * _sigmoid(gate))).astype(bf16)


def _gla(c_c, p, l):
    names = ("gk_up", "gk_bias", "norm_g")
    return pl.pallas_call(
        _gla_kernel,
        grid=(SEQ // GL_TB,),
        in_specs=[pl.BlockSpec((GL_TB, NC), lambda i: (i, 0))] + [_layer_spec(p[k], l) for k in names],
        out_specs=pl.BlockSpec((GL_TB, GROUP_W), lambda i: (i, 0)),
        out_shape=jax.ShapeDtypeStruct((SEQ, GROUP_W), bf16),
        scratch_shapes=[pltpu.VMEM((GROUP_W, GLA_KEY), f32),
                        pltpu.VMEM((GL_TB * GL_C, GROUP_W), f32), pltpu.VMEM((GL_TB, GROUP_W), f32)],
        compiler_params=pltpu.CompilerParams(dimension_semantics=("arbitrary",),
                                             vmem_limit_bytes=VMEM_LIMIT),
        name="gla",
    )(c_c, *[p[k] for k in names])


FF_TM = 512
FF_TF = 1024


def _outffn_kernel(*refs, final, layer):
    if final:
        (x_ref, ya_ref, yb_ref, yc_ref, yd_ref, wo_hbm, g_ref, wu_hbm, wd_hbm, gf_ref, o_ref,
         wo_ref, wu_ref, wd_ref, sem) = refs
    else:
        (x_ref, ya_ref, yb_ref, yc_ref, yd_ref, wo_hbm, g_ref, wu_hbm, wd_hbm, o_ref,
         wo_ref, wu_ref, wd_ref, sem) = refs

    @pl.when(pl.program_id(0) == 0)
    def _():
        copies = [pltpu.make_async_copy(src.at[layer], dst, sem.at[i])
                  for i, (src, dst) in enumerate(((wo_hbm, wo_ref), (wu_hbm, wu_ref), (wd_hbm, wd_ref)))]
        for cp in copies:
            cp.start()
        for cp in copies:
            cp.wait()

    y = jnp.concatenate([ya_ref[...], yb_ref[...], yc_ref[...], yd_ref[...]], axis=1)
    x1 = x_ref[...] + jnp.dot(y, wo_ref[...], preferred_element_type=f32)
    ms = jnp.mean(x1 * x1, axis=-1, keepdims=True)
    h = (x1 * lax.rsqrt(ms + EPS) * g_ref[...]).astype(bf16)
    x2 = x1
    for kf in range(D_FF // FF_TF):
        cols = slice(kf * FF_TF, (kf + 1) * FF_TF)
        hid = jnp.maximum(jnp.dot(h, wu_ref[:, cols], preferred_element_type=f32), 0.0)
        x2 = x2 + jnp.dot((hid * hid).astype(bf16), wd_ref[cols, :], preferred_element_type=f32)
    if final:
        ms = jnp.mean(x2 * x2, axis=-1, keepdims=True)
        x2 = x2 * lax.rsqrt(ms + EPS) * gf_ref[...]
    o_ref[...] = x2


def _out_ffn(x, ys, p, l, final):
    hbm = pl.BlockSpec(memory_space=pl.ANY)
    yspec = pl.BlockSpec((FF_TM, GROUP_W), lambda i: (i, 0))
    in_specs = [pl.BlockSpec((FF_TM, D_MODEL), lambda i: (i, 0)), yspec, yspec, yspec, yspec,
                hbm, _layer_spec(p["g"], l), hbm, hbm]
    args = [x, *ys, p["w_out"], p["g"], p["w_up"], p["w_down"]]
    if final:
        in_specs.append(pl.BlockSpec((1, D_MODEL), lambda i: (0, 0)))
        args.append(p["g_final"])
    return pl.pallas_call(
        functools.partial(_outffn_kernel, final=final, layer=l),
        grid=(SEQ // FF_TM,),
        in_specs=in_specs,
        out_specs=pl.BlockSpec((FF_TM, D_MODEL), lambda i: (i, 0)),
        out_shape=jax.ShapeDtypeStruct((SEQ, D_MODEL), f32),
        scratch_shapes=[pltpu.VMEM((D_MODEL, D_MODEL), bf16), pltpu.VMEM((D_MODEL, D_FF), bf16),
                        pltpu.VMEM((D_FF, D_MODEL), bf16), pltpu.SemaphoreType.DMA((3,))],
        compiler_params=pltpu.CompilerParams(dimension_semantics=("arbitrary",),
                                             vmem_limit_bytes=VMEM_LIMIT),
        name="out_ffn",
    )(*args)


def kernel(x, w_in, w_out, norm_mix_g, norm_ffn_g, norm_final_g, rwkv_mu, rwkv_w0, rwkv_w_up, rwkv_a0, rwkv_a_up, rwkv_g_up, rwkv_k_k, rwkv_k_a, rwkv_r_k, rwkv_lnx_w, rwkv_lnx_b, rwkv_v0, rwkv_vres_down, rwkv_vres_up, gdn_conv_w, gdn_a_log, gdn_dt_bias, gdn_norm_g, gla_gk_up, gla_gk_bias, gla_norm_g, sgu_ln_g, sgu_ln_b, sgu_w_s, sgu_b_s, ffn_w_up, ffn_w_down):
    depth = w_in.shape[0]
    row = lambda a: a.reshape(depth, 1, -1)
    per_head = lambda a: jnp.repeat(a, HEAD_DIM, axis=-1).reshape(depth, 1, -1)
    pad_cols = lambda w, n: jnp.pad(w, ((0, 0), (0, 0), (0, n - w.shape[2])))
    pad_rows = lambda w, top, total: jnp.pad(w, ((0, 0), (top, total - top - w.shape[1]), (0, 0)))
    vres_down = jnp.pad(rwkv_vres_down, ((1, 0), (0, 0), (0, 0)))
    w_comb = jnp.concatenate(
        [pad_cols(jnp.concatenate([w_in[:, :, 0:1024], vres_down], axis=2), NA),
         pad_cols(w_in[:, :, 1024:2056], NB), pad_cols(w_in[:, :, 2056:2840], NC),
         w_in[:, :, 2840:3352]], axis=2).astype(bf16)
    p_in = dict(g=row(norm_mix_g), w=w_comb, ln_g=row(sgu_ln_g), ln_b=row(sgu_ln_b),
                w_cat=sgu_w_s.transpose(0, 2, 1, 3).reshape(depth, SG_C, 4 * SG_C),
                bias_tile=jnp.repeat(sgu_b_s.transpose(0, 2, 1), HEAD_DIM, axis=2))
    p_rwkv = dict(mu=row(rwkv_mu), w0=row(rwkv_w0), w_up=pad_rows(rwkv_w_up, 0, 128),
                  a0=row(rwkv_a0), a_up=pad_rows(rwkv_a_up, 64, 128), g_up=rwkv_g_up,
                  k_k=row(rwkv_k_k), k_a=row(rwkv_k_a), r_k=row(rwkv_r_k),
                  lnx_w=row(rwkv_lnx_w), lnx_b=row(rwkv_lnx_b),
                  v0=rwkv_v0.reshape(depth - 1, 1, -1), vres_up=pad_rows(rwkv_vres_up, 0, 128))
    p_gdn = dict(conv_w=gdn_conv_w, a_log=per_head(gdn_a_log), dt_bias=per_head(gdn_dt_bias),
                 a_log_col=jnp.pad(gdn_a_log, ((0, 0), (N_HEADS, 0))).reshape(depth, 8, 1),
                 dt_bias_col=jnp.pad(gdn_dt_bias, ((0, 0), (N_HEADS, 0))).reshape(depth, 8, 1),
                 norm_g=row(jnp.tile(gdn_norm_g, (1, N_HEADS))))
    p_gla = dict(gk_up=pad_rows(gla_gk_up, 0, 128), gk_bias=row(gla_gk_bias),
                 norm_g=row(jnp.tile(gla_norm_g, (1, N_HEADS))))
    p_ffn = dict(w_out=w_out.astype(bf16), g=row(norm_ffn_g), w_up=ffn_w_up.astype(bf16),
                 w_down=ffn_w_down.astype(bf16), g_final=norm_final_g.reshape(1, -1))

    xx = x[0]
    v_first = None
    for l in range(depth):
        c_a, c_b, c_c, y_d = _in_proj(xx, p_in, l)
        y_a, v_first = _rwkv(c_a, v_first, p_rwkv, l)
        y_b = _gdn(c_b, p_gdn, l)
        y_c = _gla(c_c, p_gla, l)
        xx = _out_ffn(xx, (y_a, y_b, y_c, y_d), p_ffn, l, final=(l == depth - 1))
    return xx[None]
```

```python
import functools

import jax
import jax.numpy as jnp
from jax import lax
from jax.experimental import pallas as pl
from jax.experimental.pallas import tpu as pltpu

f32 = jnp.float32
bf16 = jnp.bfloat16

SEQ = 16384
D_MODEL = 1024
GROUP_W = 256
HEAD_DIM = 64
N_HEADS = 4
GLA_KEY = 128
GLA_HEAD_K = 32
D_FF = 4096
EPS = 1e-6
RWKV_GN_EPS = 64e-5
NEG = -1e30

NA, NB, NC, ND = 1152, 1152, 896, 512
N_PAD = NA + NB + NC + ND

VMEM_LIMIT = 56 * 1024 * 1024

NN = (((1,), (0,)), ((), ()))
NT = (((1,), (1,)), ((), ()))
TN = (((0,), (0,)), ((), ()))


def _dot(a, b, dims=NN):
    return lax.dot_general(a.astype(bf16), b.astype(bf16), dims, preferred_element_type=f32)


def _iota(shape, axis):
    return lax.broadcasted_iota(jnp.int32, shape, axis)


def _layer_spec(a, l):
    return pl.BlockSpec((None,) + a.shape[1:], lambda *_: (l,) + (0,) * (a.ndim - 1))


def _segsum(x, ones_bd):
    hi = x.astype(bf16)
    lo = (x - hi.astype(f32)).astype(bf16)
    return (jnp.dot(hi, ones_bd, preferred_element_type=f32)
            + jnp.dot(lo, ones_bd, preferred_element_type=f32))


def _split3(x):
    hi = x.astype(bf16)
    r1 = x - hi.astype(f32)
    mid = r1.astype(bf16)
    lo = (r1 - mid.astype(f32)).astype(bf16)
    return hi, mid, lo


def _dot_sel_lhs(sel, x, dims=NN):
    return sum(lax.dot_general(sel, t, dims, preferred_element_type=f32) for t in _split3(x))


def _dot_sel_rhs(x, sel, dims=NN):
    return sum(lax.dot_general(t, sel, dims, preferred_element_type=f32) for t in _split3(x))


def _group_ones(n, width):
    return (_iota((n, n), 0) // width == _iota((n, n), 1) // width).astype(bf16)


def _sigmoid(x):
    return 1.0 / (1.0 + jnp.exp(-x))


def _softplus(x):
    return jnp.maximum(x, 0.0) + jnp.log1p(jnp.exp(-jnp.abs(x)))


def _shift_rows(x, carry, s):
    xs = pltpu.roll(x, s, 0)
    fix = pltpu.roll(carry, s, 0)
    first = jnp.where(_iota(carry.shape, 0) < s, fix, xs[0:8])
    return jnp.concatenate([first, xs[8:]], axis=0)


def _bd(xp):
    xb = xp.astype(bf16)
    left = _iota(xb.shape, 1) < HEAD_DIM
    zero = jnp.zeros_like(xb)
    return jnp.concatenate([jnp.where(left, xb, zero), jnp.where(left, zero, xb)], axis=0)


def _pair_masks(c):
    ri, cj = _iota((c, 2 * c), 0), _iota((c, 2 * c), 1) & (c - 1)
    eye = (ri == cj).astype(f32)
    m16 = (ri // 16 == cj // 16).astype(f32)
    mo1 = ((ri // 32 == cj // 32) & (ri // 16 == cj // 16 + 1)).astype(f32)
    mo2 = ((ri // 32 == 1) & (cj // 32 == 0)).astype(f32)
    return dict(eye=eye, m16=m16, mo1=mo1, mo2=mo2, strict=ri > cj, incl=ri >= cj)


def _tri_inv(lms, masks):
    c = RW_C
    ps = [-(lm * masks["m16"]) for lm in lms]
    ts = [masks["eye"] + p for p in ps]
    ps = [_dot(p, _bd(p)) for p in ps]
    for _ in range(2):
        outs = [_dot(jnp.concatenate([t, p], axis=0), _bd(p)) for t, p in zip(ts, ps)]
        ts = [t + o[0:c] for t, o in zip(ts, outs)]
        ps = [o[c:] for o in outs]
    ts = [t + _dot(t, _bd(p)) for t, p in zip(ts, ps)]
    for mo in (masks["mo1"], masks["mo2"]):
        xs = [_dot(lm * mo, _bd(t)) for lm, t in zip(lms, ts)]
        ts = [t - _dot(t, _bd(x)) for t, x in zip(ts, xs)]
    return ts


IN_TM = 512
SG_C = 128


def _sgu_mix(x, lg_ref, lb_ref, w_ref, bias_ref):
    gx = 0.5 * x * (1.0 + jnp.tanh(0.7978845608028654 * (x + 0.044715 * x * x * x)))
    u = gx[:, 0:256]
    v = gx[:, 256:512]
    mu = jnp.mean(v, axis=-1, keepdims=True)
    vc = v - mu
    var = jnp.mean(vc * vc, axis=-1, keepdims=True)
    v = vc * lax.rsqrt(var + 1e-5) * lg_ref[...] + lb_ref[...]
    wr, wc = _iota((SG_C, 4 * SG_C), 0), _iota((SG_C, 4 * SG_C), 1)
    w = jnp.where(wc % SG_C <= wr, w_ref[...], 0.0).astype(bf16)
    lane_g = _iota((SG_C, GROUP_W), 1) // HEAD_DIM
    outs = []
    for n in range(x.shape[0] // SG_C):
        vn = v[n * SG_C:(n + 1) * SG_C, :]
        vst = jnp.concatenate([jnp.where(lane_g == g, vn, 0.0) for g in range(4)], axis=0)
        outs.append(jnp.dot(w, vst.astype(bf16), preferred_element_type=f32) + bias_ref[...])
    return (u * jnp.concatenate(outs, axis=0)).astype(bf16)


def _inproj_kernel(x_ref, g_ref, w_ref, lg_ref, lb_ref, ws_ref, bias_ref, oa_ref, ob_ref, oc_ref, yd_ref):
    x = x_ref[...]
    ms = jnp.mean(x * x, axis=-1, keepdims=True)
    h = (x * lax.rsqrt(ms + EPS) * g_ref[...]).astype(bf16)
    c_d = jnp.dot(h, w_ref[:, NA + NB + NC:], preferred_element_type=f32)
    yd_ref[...] = _sgu_mix(c_d, lg_ref, lb_ref, ws_ref, bias_ref)
    off = 0
    for o_ref, n in ((oa_ref, NA), (ob_ref, NB), (oc_ref, NC)):
        o_ref[...] = jnp.dot(h, w_ref[:, off:off + n], preferred_element_type=f32)
        off += n


def _in_proj(x, p, l):
    tm = IN_TM
    names = ("g", "w", "ln_g", "ln_b", "w_cat", "bias_tile")
    return pl.pallas_call(
        _inproj_kernel,
        grid=(SEQ // tm,),
        in_specs=[pl.BlockSpec((tm, D_MODEL), lambda i: (i, 0))] + [_layer_spec(p[k], l) for k in names],
        out_specs=[pl.BlockSpec((tm, n), lambda i: (i, 0)) for n in (NA, NB, NC, GROUP_W)],
        out_shape=[jax.ShapeDtypeStruct((SEQ, n), f32) for n in (NA, NB, NC)]
        + [jax.ShapeDtypeStruct((SEQ, GROUP_W), bf16)],
        compiler_params=pltpu.CompilerParams(dimension_semantics=("arbitrary",),
                                             vmem_limit_bytes=VMEM_LIMIT),
        name="in_proj",
    )(x, *[p[k] for k in names])


RW_TB = 512
RW_C = 64


def _rwkv_kernel(*refs, has_vres):
    if has_vres:
        (c_ref, vf_ref, mu_ref, w0_ref, wup_ref, a0_ref, aup_ref, gup_ref, kk_ref, ka_ref, rk_ref,
         lw_ref, lb_ref, v0_ref, vup_ref, y_ref,
         carry_ref, s_ref, r_s, k_s, v_s, lw_s, al_s, be_s, y_s) = refs
    else:
        (c_ref, mu_ref, w0_ref, wup_ref, a0_ref, aup_ref, gup_ref, kk_ref, ka_ref, rk_ref,
         lw_ref, lb_ref, y_ref, vf_out_ref,
         carry_ref, s_ref, r_s, k_s, v_s, lw_s, al_s, be_s, y_s) = refs

    @pl.when(pl.program_id(0) == 0)
    def _():
        carry_ref[...] = jnp.zeros_like(carry_ref)
        s_ref[...] = jnp.zeros_like(s_ref)

    ones_bd = _group_ones(GROUP_W, HEAD_DIM)
    x = c_ref[:, 0:1024]
    x_prev = _shift_rows(x, carry_ref[...], 1)
    carry_ref[...] = x[RW_TB - 8:RW_TB]
    xs = x + (x_prev - x) * mu_ref[...]
    r = xs[:, 0:256]
    k = xs[:, 256:512]
    v = xs[:, 512:768]
    lora = xs[:, 768:896]
    w_pre = w0_ref[...] + _dot(jnp.tanh(lora), wup_ref[...])
    lw = -jnp.exp(-_softplus(-w_pre) - 0.5)
    a = _sigmoid(a0_ref[...] + _dot(lora, aup_ref[...]))
    g = _dot(_sigmoid(xs[:, 896:1024]), gup_ref[...])
    if has_vres:
        mix = _sigmoid(v0_ref[...] + _dot(c_ref[:, 1024:1152], vup_ref[...]))
        v = v + (vf_ref[...] - v) * mix
    else:
        vf_out_ref[...] = v
    kk = k * kk_ref[...]
    kk = kk * lax.rsqrt(_segsum(kk * kk, ones_bd) + 1e-24)
    k = k * (1.0 + (a - 1.0) * ka_ref[...])
    r_s[...] = r
    k_s[...] = k
    v_s[...] = v
    lw_s[...] = lw
    al_s[...] = -kk
    be_s[...] = kk * a

    c = RW_C
    tril_incl = (_iota((c, c), 0) >= _iota((c, c), 1)).astype(bf16)
    masks = _pair_masks(c)
    low_strict, low_incl = masks["strict"], masks["incl"]
    bd_mask = (_iota((2 * c, 2 * c), 0) // c == _iota((2 * c, 2 * c), 1) // c).astype(f32)

    nchunk = RW_TB // c
    npair = N_HEADS // 2
    items = [(n, h) for n in range(nchunk) for h in range(npair)]
    hsl = lambda h: slice(2 * h * HEAD_DIM, 2 * (h + 1) * HEAD_DIM)
    pre = []
    for n in range(nchunk):
        rows = slice(n * c, (n + 1) * c)
        lwc = lw_s[rows, :]
        lc = _dot_sel_lhs(tril_incl, lwc)
        llast = lc[c - 1:c, :]
        e_out = jnp.exp(-lc)
        e_rest = jnp.exp(llast - lc)
        kc, bec = k_s[rows, :], be_s[rows, :]
        pre.append(dict(rt=r_s[rows, :] * jnp.exp(lc), at=al_s[rows, :] * jnp.exp(lc - lwc),
                        bt=bec * e_out, kt=kc * e_out, bw=bec * e_rest, kw=kc * e_rest,
                        v=v_s[rows, :], dlast=jnp.exp(llast)))
    ms = [_dot(jnp.concatenate([pre[n]["at"][:, hsl(h)], pre[n]["rt"][:, hsl(h)]], axis=0),
               jnp.concatenate([_bd(pre[n]["bt"][:, hsl(h)]), _bd(pre[n]["kt"][:, hsl(h)])], axis=0), NT)
          for n, h in items]
    tms = _tri_inv([jnp.where(low_strict, -m[0:c, 0:2 * c], 0.0) for m in ms], masks)
    avs = [_dot(jnp.concatenate([jnp.where(low_strict, m[0:c, 2 * c:], 0.0),
                                 jnp.where(low_incl, m[c:, 2 * c:], 0.0)], axis=0),
                _bd(pre[n]["v"][:, hsl(h)]))
           for m, (n, h) in zip(ms, items)]
    tts = [_dot(tm, jnp.concatenate([_bd(pre[n]["at"][:, hsl(h)]), _bd(av[0:c])], axis=1))
           for tm, av, (n, h) in zip(tms, avs, items)]
    state = [s_ref[h] for h in range(npair)]
    for n in range(nchunk):
        p = pre[n]
        idx = [n * npair + h for h in range(npair)]
        sas = [_dot(jnp.concatenate([tts[i][:, 0:2 * c], p["rt"][:, hsl(h)]], axis=0), state[h], NT)
               for h, i in enumerate(idx)]
        us = [sa[0:c] + tts[i][:, 2 * c:] for sa, i in zip(sas, idx)]
        upd = [_dot(jnp.concatenate([u, p["v"][:, hsl(h)]], axis=0),
                    jnp.concatenate([p["bw"][:, hsl(h)], p["kw"][:, hsl(h)]], axis=0), TN) * bd_mask
               for h, u in enumerate(us)]
        state = [state[h] * p["dlast"][:, hsl(h)] + upd[h] for h in range(npair)]
        ys = [sas[h][c:] + _dot(jnp.where(low_incl, ms[i][c:, 0:2 * c], 0.0), _bd(us[h])) + avs[i][c:]
              for h, i in enumerate(idx)]
        y_s[n * c:(n + 1) * c, :] = jnp.concatenate(ys, axis=1)
    for h in range(npair):
        s_ref[h] = state[h]

    y = y_s[...]
    inv_d = 1.0 / HEAD_DIM
    mean = _segsum(y, ones_bd) * inv_d
    yc = y - mean
    var = _segsum(yc * yc, ones_bd) * inv_d
    y = yc * lax.rsqrt(var + RWKV_GN_EPS) * lw_ref[...] + lb_ref[...]
    y = y + _segsum(r * k * rk_ref[...], ones_bd) * v
    y_ref[...] = (y * g).astype(bf16)


def _rwkv(c_a, v_first, p, l):
    has_vres = v_first is not None
    blk = lambda n: pl.BlockSpec((RW_TB, n), lambda i: (i, 0))
    in_specs = [blk(NA)]
    args = [c_a]
    if has_vres:
        in_specs.append(blk(GROUP_W))
        args.append(v_first)
    names = ("mu", "w0", "w_up", "a0", "a_up", "g_up", "k_k", "k_a", "r_k", "lnx_w", "lnx_b")
    in_specs += [_layer_spec(p[k], l) for k in names]
    args += [p[k] for k in names]
    if has_vres:
        in_specs += [_layer_spec(p["v0"], l - 1), _layer_spec(p["vres_up"], l - 1)]
        args += [p["v0"], p["vres_up"]]
        out_specs = blk(GROUP_W)
        out_shape = jax.ShapeDtypeStruct((SEQ, GROUP_W), bf16)
    else:
        out_specs = [blk(GROUP_W), blk(GROUP_W)]
        out_shape = [jax.ShapeDtypeStruct((SEQ, GROUP_W), bf16),
                     jax.ShapeDtypeStruct((SEQ, GROUP_W), f32)]
    scratch = [pltpu.VMEM((8, 1024), f32), pltpu.VMEM((N_HEADS // 2, 2 * HEAD_DIM, 2 * HEAD_DIM), f32)]
    scratch += [pltpu.VMEM((RW_TB, GROUP_W), f32) for _ in range(7)]
    out = pl.pallas_call(
        functools.partial(_rwkv_kernel, has_vres=has_vres),
        grid=(SEQ // RW_TB,),
        in_specs=in_specs,
        out_specs=out_specs,
        out_shape=out_shape,
        scratch_shapes=scratch,
        compiler_params=pltpu.CompilerParams(dimension_semantics=("arbitrary",),
                                             vmem_limit_bytes=VMEM_LIMIT),
        name="rwkv7",
    )(*args)
    if has_vres:
        return out, v_first
    return out[0], out[1]


GD_TB = 512
GD_C = 64


def _gdn_kernel(c_ref, cw_ref, alog_ref, dtb_ref, alogc_ref, dtbc_ref, ng_ref, y_ref,
                carry_ref, s_ref, q_s, k_s, v_s, be_s, g_s, o_s):
    @pl.when(pl.program_id(0) == 0)
    def _():
        carry_ref[...] = jnp.zeros_like(carry_ref)
        s_ref[...] = jnp.zeros_like(s_ref)

    c = GD_C
    nchunk = GD_TB // c
    npair = N_HEADS // 2
    hsl = lambda h: slice(2 * h * HEAD_DIM, 2 * (h + 1) * HEAD_DIM)
    bd_mask = (_iota((2 * c, 2 * c), 0) // c == _iota((2 * c, 2 * c), 1) // c).astype(f32)
    ones_bd = _group_ones(GROUP_W, HEAD_DIM)

    raw = c_ref[:, 0:768]
    carry = carry_ref[...]
    conv = raw * cw_ref[3:4, :]
    for s in range(1, 4):
        conv = conv + _shift_rows(raw, carry, s) * cw_ref[3 - s:4 - s, :]
    carry_ref[...] = raw[GD_TB - 8:GD_TB]
    qkv = conv * _sigmoid(conv)
    q = qkv[:, 0:256]
    k = qkv[:, 256:512]
    q_s[...] = q * lax.rsqrt(_segsum(q * q, ones_bd) + 1e-6) * (HEAD_DIM ** -0.5)
    k_s[...] = k * lax.rsqrt(_segsum(k * k, ones_bd) + 1e-6)
    v_s[...] = qkv[:, 512:768]
    small = c_ref[:, 1024:1152]
    er, ec = _iota((128, GROUP_W), 0), _iota((128, GROUP_W), 1)
    b_exp = _dot_sel_rhs(small, (er == ec // HEAD_DIM).astype(bf16))
    a_exp = _dot_sel_rhs(small, (er == ec // HEAD_DIM + N_HEADS).astype(bf16))
    be_s[...] = _sigmoid(b_exp)
    g_s[...] = -jnp.exp(alog_ref[...]) * _softplus(a_exp + dtb_ref[...])

    tril_incl = (_iota((c, c), 0) >= _iota((c, c), 1)).astype(bf16)
    masks = _pair_masks(c)
    low_strict, low_incl = masks["strict"], masks["incl"]
    sel8 = (_iota((8, 128), 0) == _iota((8, 128), 1)).astype(bf16)
    g_rows = -jnp.exp(alogc_ref[...]) * _softplus(_dot_sel_lhs(sel8, small, NT) + dtbc_ref[...])
    tj, ti = _iota((GD_TB, GD_TB), 0), _iota((GD_TB, GD_TB), 1)
    gc_rows = _dot_sel_rhs(g_rows, ((tj // c == ti // c) & (tj <= ti)).astype(bf16))

    items = [(n, h) for n in range(nchunk) for h in range(npair)]
    pre = []
    for n in range(nchunk):
        rows = slice(n * c, (n + 1) * c)
        gc = _dot_sel_lhs(tril_incl, g_s[rows, :])
        glast = gc[c - 1:c, :]
        egc = jnp.exp(gc)
        kc, bc = k_s[rows, :], be_s[rows, :]
        kb = kc * bc
        pre.append(dict(gc=gc, k=kc, kb=kb, q=q_s[rows, :], vb=v_s[rows, :] * bc, kbe=kb * egc,
                        qe=q_s[rows, :] * egc, kd=kc * jnp.exp(glast - gc), eglast=jnp.exp(glast)))

    def gc_row(n, h):
        return jnp.concatenate([gc_rows[N_HEADS + 2 * h + i:N_HEADS + 2 * h + i + 1, n * c:(n + 1) * c]
                                for i in range(2)], axis=1)

    dms = [jnp.exp(jnp.where(low_incl, pre[n]["gc"][:, hsl(h)] - gc_row(n, h), NEG))
           for n, h in items]
    aqs = [_dot(jnp.concatenate([pre[n]["kb"][:, hsl(h)], pre[n]["q"][:, hsl(h)]], axis=0),
                _bd(pre[n]["k"][:, hsl(h)]), NT) for n, h in items]
    tms = _tri_inv([jnp.where(low_strict, aq[0:c] * dm, 0.0) for aq, dm in zip(aqs, dms)], masks)
    uws = [_dot(tm, jnp.concatenate([_bd(pre[n]["vb"][:, hsl(h)]), _bd(pre[n]["kbe"][:, hsl(h)])], axis=1))
           for tm, (n, h) in zip(tms, items)]
    state = [s_ref[h] for h in range(npair)]
    for n in range(nchunk):
        p = pre[n]
        idx = [n * npair + h for h in range(npair)]
        wss = [_dot(jnp.concatenate([uws[i][:, 2 * c:], p["qe"][:, hsl(h)]], axis=0), state[h])
               for h, i in enumerate(idx)]
        vns = [uws[i][:, 0:2 * c] - wss[h][0:c] for h, i in enumerate(idx)]
        upd = [_dot(p["kd"][:, hsl(h)], vns[h], TN) * bd_mask for h in range(npair)]
        state = [state[h] * p["eglast"][:, hsl(h)] + upd[h] for h in range(npair)]
        os_ = [wss[h][c:] + _dot(aqs[i][c:] * dms[i], _bd(vns[h])) for h, i in enumerate(idx)]
        o_s[n * c:(n + 1) * c, :] = jnp.concatenate(os_, axis=1)
    for h in range(npair):
        s_ref[h] = state[h]

    o = o_s[...]
    ms = _segsum(o * o, ones_bd) * (1.0 / HEAD_DIM)
    z = c_ref[:, 768:1024]
    y_ref[...] = (o * lax.rsqrt(ms + EPS) * ng_ref[...] * (z * _sigmoid(z))).astype(bf16)


def _gdn(c_b, p, l):
    names = ("conv_w", "a_log", "dt_bias", "a_log_col", "dt_bias_col", "norm_g")
    scratch = [pltpu.VMEM((8, 768), f32), pltpu.VMEM((N_HEADS // 2, 2 * HEAD_DIM, 2 * HEAD_DIM), f32)]
    scratch += [pltpu.VMEM((GD_TB, GROUP_W), f32) for _ in range(6)]
    return pl.pallas_call(
        _gdn_kernel,
        grid=(SEQ // GD_TB,),
        in_specs=[pl.BlockSpec((GD_TB, NB), lambda i: (i, 0))] + [_layer_spec(p[k], l) for k in names],
        out_specs=pl.BlockSpec((GD_TB, GROUP_W), lambda i: (i, 0)),
        out_shape=jax.ShapeDtypeStruct((SEQ, GROUP_W), bf16),
        scratch_shapes=scratch,
        compiler_params=pltpu.CompilerParams(dimension_semantics=("arbitrary",),
                                             vmem_limit_bytes=VMEM_LIMIT),
        name="gdn",
    )(c_b, *[p[k] for k in names])


GL_TB = 256
GL_C = 16
GL_S = 8


def _gla_kernel(c_ref, gup_ref, gb_ref, ng_ref, y_ref, st_ref, sx_s, o_s):
    @pl.when(pl.program_id(0) == 0)
    def _():
        st_ref[...] = jnp.zeros_like(st_ref)

    tb, c, s = GL_TB, GL_C, GL_S
    nchunk, nsub = tb // c, tb // s
    q = c_ref[:, 0:128] * (GLA_HEAD_K ** -0.5)
    k = c_ref[:, 128:256]
    v = c_ref[:, 256:512]
    pre = _dot(c_ref[:, 768:896], gup_ref[...]) + gb_ref[...]
    la = -_softplus(-pre) * (1.0 / 16.0)
    tj, ti = _iota((tb, tb), 0), _iota((tb, tb), 1)
    b = _dot_sel_lhs(((tj // c == ti // c) & (ti <= tj)).astype(bf16), la)
    qi = q * jnp.exp(b)

    ind_e = (_iota((GLA_KEY, GROUP_W), 0) // GLA_HEAD_K == _iota((GLA_KEY, GROUP_W), 1) // HEAD_DIM).astype(bf16)
    bd_mask = (_iota((GROUP_W, GLA_KEY), 0) // HEAD_DIM == _iota((GROUP_W, GLA_KEY), 1) // GLA_HEAD_K).astype(f32)

    b3, q3, k3 = (t.reshape(nsub, s, GLA_KEY) for t in (b, q, k))
    ri = _iota((nsub, s, GLA_KEY), 1)
    terms = []
    for j in range(s):
        e = jnp.exp(jnp.where(ri >= j, b3 - b3[:, j:j + 1, :], NEG))
        terms.append((q3 * (k3[:, j:j + 1, :] * e)).reshape(tb, GLA_KEY).astype(bf16))
    sx_s[...] = jnp.dot(jnp.concatenate(terms, axis=0), ind_e, preferred_element_type=f32)
    v3 = v.reshape(nsub, s, GROUP_W)
    o3 = sx_s[0:tb, :].reshape(nsub, s, GROUP_W) * v3[:, 0:1, :]
    for j in range(1, s):
        o3 = o3 + sx_s[j * tb:(j + 1) * tb, :].reshape(nsub, s, GROUP_W) * v3[:, j:j + 1, :]

    b4, q4, k4 = (t.reshape(nchunk, 2, s, GLA_KEY) for t in (b, q, k))
    bref = b4[:, 0, s - 1:s, :]
    qd = (q4[:, 1] * jnp.exp(b4[:, 1] - bref)).reshape(nchunk * s, GLA_KEY)
    kd = (k4[:, 0] * jnp.exp(bref - b4[:, 0])).reshape(nchunk * s, GLA_KEY)
    v0 = v.reshape(nchunk, 2, s, GROUP_W)[:, 0].reshape(nchunk * s, GROUP_W)
    head_k = _iota((nchunk * s, GLA_KEY), 1) // GLA_HEAD_K
    head_v = _iota((nchunk * s, GROUP_W), 1) // HEAD_DIM
    ks = jnp.concatenate([jnp.where(head_k == h, kd, 0.0) for h in range(N_HEADS)], axis=0)
    vs = jnp.concatenate([jnp.where(head_v == h, v0, 0.0) for h in range(N_HEADS)], axis=0)
    sc = _dot(qd, ks, NT)
    sr, scol = _iota(sc.shape, 0), _iota(sc.shape, 1)
    sc = jnp.where(sr // s == (scol % (nchunk * s)) // s, sc, 0.0)
    o_off = _dot(sc, vs).reshape(nchunk, 1, s, GROUP_W)
    o4 = o3.reshape(nchunk, 2, s, GROUP_W)
    o_intra = jnp.concatenate([o4[:, 0:1], o4[:, 1:2] + o_off], axis=1).reshape(tb, GROUP_W)

    blasts = [b[(n + 1) * c - 1:(n + 1) * c, :] for n in range(nchunk)]
    upds = [_dot(v[n * c:(n + 1) * c], k[n * c:(n + 1) * c] * jnp.exp(blasts[n] - b[n * c:(n + 1) * c]), TN)
            * bd_mask for n in range(nchunk)]
    st = st_ref[...]
    for n in range(nchunk):
        rows = slice(n * c, (n + 1) * c)
        o_s[rows, :] = o_intra[rows] + _dot(qi[rows], st, NT)
        st = st * jnp.exp(blasts[n]) + upds[n]
    st_ref[...] = st

    o = o_s[...]
    ms = _segsum(o * o, _group_ones(GROUP_W, HEAD_DIM)) * (1.0 / HEAD_DIM)
    gate = c_ref[:, 512:768]
    y_ref[...] = (o * lax.rsqrt(ms + EPS) * ng_ref[...] * (gate * _sigmoid(gate))).astype(bf16)


def _gla(c_c, p, l):
    names = ("gk_up", "gk_bias", "norm_g")
    return pl.pallas_call(
        _gla_kernel,
        grid=(SEQ // GL_TB,),
        in_specs=[pl.BlockSpec((GL_TB, NC), lambda i: (i, 0))] + [_layer_spec(p[k], l) for k in names],
        out_specs=pl.BlockSpec((GL_TB, GROUP_W), lambda i: (i, 0)),
        out_shape=jax.ShapeDtypeStruct((SEQ, GROUP_W), bf16),
        scratch_shapes=[pltpu.VMEM((GROUP_W, GLA_KEY), f32),
                        pltpu.VMEM((GL_TB * GL_S, GROUP_W), f32), pltpu.VMEM((GL_TB, GROUP_W), f32)],
        compiler_params=pltpu.CompilerParams(dimension_semantics=("arbitrary",),
                                             vmem_limit_bytes=VMEM_LIMIT),
        name="gla",
    )(c_c, *[p[k] for k in names])


FF_TM = 512
FF_TF = 1024


def _outffn_kernel(*refs, final, layer):
    if final:
        (x_ref, ya_ref, yb_ref, yc_ref, yd_ref, wo_hbm, g_ref, wu_hbm, wd_hbm, gf_ref, o_ref,
         wo_ref, wu_ref, wd_ref, sem) = refs
    else:
        (x_ref, ya_ref, yb_ref, yc_ref, yd_ref, wo_hbm, g_ref, wu_hbm, wd_hbm, o_ref,
         wo_ref, wu_ref, wd_ref, sem) = refs

    @pl.when(pl.program_id(0) == 0)
    def _():
        copies = [pltpu.make_async_copy(src.at[layer], dst, sem.at[i])
                  for i, (src, dst) in enumerate(((wo_hbm, wo_ref), (wu_hbm, wu_ref), (wd_hbm, wd_ref)))]
        for cp in copies:
            cp.start()
        for cp in copies:
            cp.wait()

    y = jnp.concatenate([ya_ref[...], yb_ref[...], yc_ref[...], yd_ref[...]], axis=1)
    x1 = x_ref[...] + jnp.dot(y, wo_ref[...], preferred_element_type=f32)
    ms = jnp.mean(x1 * x1, axis=-1, keepdims=True)
    h = (x1 * lax.rsqrt(ms + EPS) * g_ref[...]).astype(bf16)
    x2 = x1
    for kf in range(D_FF // FF_TF):
        cols = slice(kf * FF_TF, (kf + 1) * FF_TF)
        hid = jnp.maximum(jnp.dot(h, wu_ref[:, cols], preferred_element_type=f32), 0.0)
        x2 = x2 + jnp.dot((hid * hid).astype(bf16), wd_ref[cols, :], preferred_element_type=f32)
    if final:
        ms = jnp.mean(x2 * x2, axis=-1, keepdims=True)
        x2 = x2 * lax.rsqrt(ms + EPS) * gf_ref[...]
    o_ref[...] = x2


def _out_ffn(x, ys, p, l, final):
    hbm = pl.BlockSpec(memory_space=pl.ANY)
    yspec = pl.BlockSpec((FF_TM, GROUP_W), lambda i: (i, 0))
    in_specs = [pl.BlockSpec((FF_TM, D_MODEL), lambda i: (i, 0)), yspec, yspec, yspec, yspec,
                hbm, _layer_spec(p["g"], l), hbm, hbm]
    args = [x, *ys, p["w_out"], p["g"], p["w_up"], p["w_down"]]
    if final:
        in_specs.append(pl.BlockSpec((1, D_MODEL), lambda i: (0, 0)))
        args.append(p["g_final"])
    return pl.pallas_call(
        functools.partial(_outffn_kernel, final=final, layer=l),
        grid=(SEQ // FF_TM,),
        in_specs=in_specs,
        out_specs=pl.BlockSpec((FF_TM, D_MODEL), lambda i: (i, 0)),
        out_shape=jax.ShapeDtypeStruct((SEQ, D_MODEL), f32),
        scratch_shapes=[pltpu.VMEM((D_MODEL, D_MODEL), bf16), pltpu.VMEM((D_MODEL, D_FF), bf16),
                        pltpu.VMEM((D_FF, D_MODEL), bf16), pltpu.SemaphoreType.DMA((3,))],
        compiler_params=pltpu.CompilerParams(dimension_semantics=("arbitrary",),
                                             vmem_limit_bytes=VMEM_LIMIT),
        name="out_ffn",
    )(*args)


def kernel(x, w_in, w_out, norm_mix_g, norm_ffn_g, norm_final_g, rwkv_mu, rwkv_w0, rwkv_w_up, rwkv_a0, rwkv_a_up, rwkv_g_up, rwkv_k_k, rwkv_k_a, rwkv_r_k, rwkv_lnx_w, rwkv_lnx_b, rwkv_v0, rwkv_vres_down, rwkv_vres_up, gdn_conv_w, gdn_a_log, gdn_dt_bias, gdn_norm_g, gla_gk_up, gla_gk_bias, gla_norm_g, sgu_ln_g, sgu_ln_b, sgu_w_s, sgu_b_s, ffn_w_up, ffn_w_down):
    depth = w_in.shape[0]
    row = lambda a: a.reshape(depth, 1, -1)
    per_head = lambda a: jnp.repeat(a, HEAD_DIM, axis=-1).reshape(depth, 1, -1)
    pad_cols = lambda w, n: jnp.pad(w, ((0, 0), (0, 0), (0, n - w.shape[2])))
    pad_rows = lambda w, top, total: jnp.pad(w, ((0, 0), (top, total - top - w.shape[1]), (0, 0)))
    vres_down = jnp.pad(rwkv_vres_down, ((1, 0), (0, 0), (0, 0)))
    w_comb = jnp.concatenate(
        [pad_cols(jnp.concatenate([w_in[:, :, 0:1024], vres_down], axis=2), NA),
         pad_cols(w_in[:, :, 1024:2056], NB), pad_cols(w_in[:, :, 2056:2840], NC),
         w_in[:, :, 2840:3352]], axis=2).astype(bf16)
    p_in = dict(g=row(norm_mix_g), w=w_comb, ln_g=row(sgu_ln_g), ln_b=row(sgu_ln_b),
                w_cat=sgu_w_s.transpose(0, 2, 1, 3).reshape(depth, SG_C, 4 * SG_C),
                bias_tile=jnp.repeat(sgu_b_s.transpose(0, 2, 1), HEAD_DIM, axis=2))
    p_rwkv = dict(mu=row(rwkv_mu), w0=row(rwkv_w0), w_up=pad_rows(rwkv_w_up, 0, 128),
                  a0=row(rwkv_a0), a_up=pad_rows(rwkv_a_up, 64, 128), g_up=rwkv_g_up,
                  k_k=row(rwkv_k_k), k_a=row(rwkv_k_a), r_k=row(rwkv_r_k),
                  lnx_w=row(rwkv_lnx_w), lnx_b=row(rwkv_lnx_b),
                  v0=rwkv_v0.reshape(depth - 1, 1, -1), vres_up=pad_rows(rwkv_vres_up, 0, 128))
    p_gdn = dict(conv_w=gdn_conv_w, a_log=per_head(gdn_a_log), dt_bias=per_head(gdn_dt_bias),
                 a_log_col=jnp.pad(gdn_a_log, ((0, 0), (N_HEADS, 0))).reshape(depth, 8, 1),
                 dt_bias_col=jnp.pad(gdn_dt_bias, ((0, 0), (N_HEADS, 0))).reshape(depth, 8, 1),
                 norm_g=row(jnp.tile(gdn_norm_g, (1, N_HEADS))))
    p_gla = dict(gk_up=pad_rows(gla_gk_up, 0, 128), gk_bias=row(gla_gk_bias),
                 norm_g=row(jnp.tile(gla_norm_g, (1, N_HEADS))))
    p_ffn = dict(w_out=w_out.astype(bf16), g=row(norm_ffn_g), w_up=ffn_w_up.astype(bf16),
                 w_down=ffn_w_down.astype(bf16), g_final=norm_final_g.reshape(1, -1))

    xx = x[0]
    v_first = None
    for l in range(depth):
        c_a, c_b, c_c, y_d = _in_proj(xx, p_in, l)
        y_a, v_first = _rwkv(c_a, v_first, p_rwkv, l)
        y_b = _gdn(c_b, p_gdn, l)
        y_c = _gla(c_c, p_gla, l)
        xx = _out_ffn(xx, (y_a, y_b, y_c, y_d), p_ffn, l, final=(l == depth - 1))
    return xx[None]
```

```python
import functools

import jax
import jax.numpy as jnp
from jax import lax
from jax.experimental import pallas as pl
from jax.experimental.pallas import tpu as pltpu

f32 = jnp.float32
bf16 = jnp.bfloat16

SEQ = 16384
D_MODEL = 1024
GROUP_W = 256
HEAD_DIM = 64
N_HEADS = 4
GLA_KEY = 128
GLA_HEAD_K = 32
D_FF = 4096
EPS = 1e-6
RWKV_GN_EPS = 64e-5
NEG = -1e30

NA, NB, NC, ND = 1152, 1152, 896, 512
N_PAD = NA + NB + NC + ND

VMEM_LIMIT = 56 * 1024 * 1024

NN = (((1,), (0,)), ((), ()))
NT = (((1,), (1,)), ((), ()))
TN = (((0,), (0,)), ((), ()))


def _dot(a, b, dims=NN):
    return lax.dot_general(a.astype(bf16), b.astype(bf16), dims, preferred_element_type=f32)


def _iota(shape, axis):
    return lax.broadcasted_iota(jnp.int32, shape, axis)


def _layer_spec(a, l):
    return pl.BlockSpec((None,) + a.shape[1:], lambda *_: (l,) + (0,) * (a.ndim - 1))


def _segsum(x, ones_bd):
    hi = x.astype(bf16)
    lo = (x - hi.astype(f32)).astype(bf16)
    return (jnp.dot(hi, ones_bd, preferred_element_type=f32)
            + jnp.dot(lo, ones_bd, preferred_element_type=f32))


def _split3(x):
    hi = x.astype(bf16)
    r1 = x - hi.astype(f32)
    mid = r1.astype(bf16)
    lo = (r1 - mid.astype(f32)).astype(bf16)
    return hi, mid, lo


def _dot_sel_lhs(sel, x, dims=NN):
    return sum(lax.dot_general(sel, t, dims, preferred_element_type=f32) for t in _split3(x))


def _dot_sel_rhs(x, sel, dims=NN):
    return sum(lax.dot_general(t, sel, dims, preferred_element_type=f32) for t in _split3(x))


def _group_ones(n, width):
    return (_iota((n, n), 0) // width == _iota((n, n), 1) // width).astype(bf16)


def _sigmoid(x):
    return 1.0 / (1.0 + jnp.exp(-x))


def _softplus(x):
    return jnp.maximum(x, 0.0) + jnp.log1p(jnp.exp(-jnp.abs(x)))


def _shift_rows(x, carry, s):
    xs = pltpu.roll(x, s, 0)
    fix = pltpu.roll(carry, s, 0)
    first = jnp.where(_iota(carry.shape, 0) < s, fix, xs[0:8])
    return jnp.concatenate([first, xs[8:]], axis=0)


def _bd(xp):
    xb = xp.astype(bf16)
    left = _iota(xb.shape, 1) < HEAD_DIM
    zero = jnp.zeros_like(xb)
    return jnp.concatenate([jnp.where(left, xb, zero), jnp.where(left, zero, xb)], axis=0)


def _pair_masks(c):
    ri, cj = _iota((c, 2 * c), 0), _iota((c, 2 * c), 1) & (c - 1)
    eye = (ri == cj).astype(f32)
    m16 = (ri // 16 == cj // 16).astype(f32)
    mo1 = ((ri // 32 == cj // 32) & (ri // 16 == cj // 16 + 1)).astype(f32)
    mo2 = ((ri // 32 == 1) & (cj // 32 == 0)).astype(f32)
    return dict(eye=eye, m16=m16, mo1=mo1, mo2=mo2, strict=ri > cj, incl=ri >= cj)


def _tri_inv(lms, masks, tick=lambda: None):
    c = RW_C

    def each(fn, *lists):
        out = []
        for args in zip(*lists):
            out.append(fn(*args))
            tick()
        return out

    ps = [-(lm * masks["m16"]) for lm in lms]
    ts = [masks["eye"] + p for p in ps]
    ps = each(lambda p: _dot(p, _bd(p)), ps)
    for _ in range(2):
        outs = each(lambda t, p: _dot(jnp.concatenate([t, p], axis=0), _bd(p)), ts, ps)
        ts = [t + o[0:c] for t, o in zip(ts, outs)]
        ps = [o[c:] for o in outs]
    ts = each(lambda t, p: t + _dot(t, _bd(p)), ts, ps)
    for mo in (masks["mo1"], masks["mo2"]):
        xs = each(lambda lm, t: _dot(lm * mo, _bd(t)), lms, ts)
        ts = each(lambda t, x: t - _dot(t, _bd(x)), ts, xs)
    return ts


IN_TM = 512
SG_C = 128


def _sgu_mix(x, lg_ref, lb_ref, w_ref, bias_ref):
    gx = 0.5 * x * (1.0 + jnp.tanh(0.7978845608028654 * (x + 0.044715 * x * x * x)))
    u = gx[:, 0:256]
    v = gx[:, 256:512]
    mu = jnp.mean(v, axis=-1, keepdims=True)
    vc = v - mu
    var = jnp.mean(vc * vc, axis=-1, keepdims=True)
    v = vc * lax.rsqrt(var + 1e-5) * lg_ref[...] + lb_ref[...]
    wr, wc = _iota((SG_C, 4 * SG_C), 0), _iota((SG_C, 4 * SG_C), 1)
    w = jnp.where(wc % SG_C <= wr, w_ref[...], 0.0).astype(bf16)
    lane_g = _iota((SG_C, GROUP_W), 1) // HEAD_DIM
    outs = []
    for n in range(x.shape[0] // SG_C):
        vn = v[n * SG_C:(n + 1) * SG_C, :]
        vst = jnp.concatenate([jnp.where(lane_g == g, vn, 0.0) for g in range(4)], axis=0)
        outs.append(jnp.dot(w, vst.astype(bf16), preferred_element_type=f32) + bias_ref[...])
    return (u * jnp.concatenate(outs, axis=0)).astype(bf16)


def _inproj_kernel(x_ref, g_ref, w_ref, lg_ref, lb_ref, ws_ref, bias_ref, oa_ref, ob_ref, oc_ref, yd_ref):
    x = x_ref[...]
    ms = jnp.mean(x * x, axis=-1, keepdims=True)
    h = (x * lax.rsqrt(ms + EPS) * g_ref[...]).astype(bf16)
    c_d = jnp.dot(h, w_ref[:, NA + NB + NC:], preferred_element_type=f32)
    yd_ref[...] = _sgu_mix(c_d, lg_ref, lb_ref, ws_ref, bias_ref)
    off = 0
    for o_ref, n in ((oa_ref, NA), (ob_ref, NB), (oc_ref, NC)):
        o_ref[...] = jnp.dot(h, w_ref[:, off:off + n], preferred_element_type=f32)
        off += n


def _in_proj(x, p, l):
    tm = IN_TM
    names = ("g", "w", "ln_g", "ln_b", "w_cat", "bias_tile")
    return pl.pallas_call(
        _inproj_kernel,
        grid=(SEQ // tm,),
        in_specs=[pl.BlockSpec((tm, D_MODEL), lambda i: (i, 0))] + [_layer_spec(p[k], l) for k in names],
        out_specs=[pl.BlockSpec((tm, n), lambda i: (i, 0)) for n in (NA, NB, NC, GROUP_W)],
        out_shape=[jax.ShapeDtypeStruct((SEQ, n), f32) for n in (NA, NB, NC)]
        + [jax.ShapeDtypeStruct((SEQ, GROUP_W), bf16)],
        compiler_params=pltpu.CompilerParams(dimension_semantics=("arbitrary",),
                                             vmem_limit_bytes=VMEM_LIMIT),
        name="in_proj",
    )(x, *[p[k] for k in names])


RW_TB = 512
RW_C = 64


def _rwkv_kernel(*refs, has_vres):
    if has_vres:
        (c_ref, vf_ref, mu_ref, w0_ref, wup_ref, a0_ref, aup_ref, gup_ref, kk_ref, ka_ref, rk_ref,
         lw_ref, lb_ref, v0_ref, vup_ref, y_ref,
         carry_ref, s_ref, r_s, k_s, v_s, lw_s, al_s, be_s, y_s) = refs
    else:
        (c_ref, mu_ref, w0_ref, wup_ref, a0_ref, aup_ref, gup_ref, kk_ref, ka_ref, rk_ref,
         lw_ref, lb_ref, y_ref, vf_out_ref,
         carry_ref, s_ref, r_s, k_s, v_s, lw_s, al_s, be_s, y_s) = refs

    @pl.when(pl.program_id(0) == 0)
    def _():
        carry_ref[...] = jnp.zeros_like(carry_ref)
        s_ref[...] = jnp.zeros_like(s_ref)

    ones_bd = _group_ones(GROUP_W, HEAD_DIM)
    x = c_ref[:, 0:1024]
    x_prev = _shift_rows(x, carry_ref[...], 1)
    carry_ref[...] = x[RW_TB - 8:RW_TB]
    xs = x + (x_prev - x) * mu_ref[...]
    r = xs[:, 0:256]
    k = xs[:, 256:512]
    v = xs[:, 512:768]
    lora = xs[:, 768:896]
    w_pre = w0_ref[...] + _dot(jnp.tanh(lora), wup_ref[...])
    lw = -jnp.exp(-_softplus(-w_pre) - 0.5)
    a = _sigmoid(a0_ref[...] + _dot(lora, aup_ref[...]))
    g = _dot(_sigmoid(xs[:, 896:1024]), gup_ref[...])
    if has_vres:
        mix = _sigmoid(v0_ref[...] + _dot(c_ref[:, 1024:1152], vup_ref[...]))
        v = v + (vf_ref[...] - v) * mix
    else:
        vf_out_ref[...] = v
    kk = k * kk_ref[...]
    kk = kk * lax.rsqrt(_segsum(kk * kk, ones_bd) + 1e-24)
    k = k * (1.0 + (a - 1.0) * ka_ref[...])
    r_s[...] = r
    k_s[...] = k
    v_s[...] = v
    lw_s[...] = lw
    al_s[...] = -kk
    be_s[...] = kk * a

    c = RW_C
    tril_incl = (_iota((c, c), 0) >= _iota((c, c), 1)).astype(bf16)
    masks = _pair_masks(c)
    low_strict, low_incl = masks["strict"], masks["incl"]
    bd_mask = (_iota((2 * c, 2 * c), 0) // c == _iota((2 * c, 2 * c), 1) // c).astype(f32)

    nchunk = RW_TB // c
    npair = N_HEADS // 2
    items = [(n, h) for n in range(nchunk) for h in range(npair)]
    hsl = lambda h: slice(2 * h * HEAD_DIM, 2 * (h + 1) * HEAD_DIM)
    pre = []
    for n in range(nchunk):
        rows = slice(n * c, (n + 1) * c)
        lwc = lw_s[rows, :]
        lc = _dot_sel_lhs(tril_incl, lwc)
        llast = lc[c - 1:c, :]
        e_out = jnp.exp(-lc)
        e_rest = jnp.exp(llast - lc)
        kc, bec = k_s[rows, :], be_s[rows, :]
        pre.append(dict(rt=r_s[rows, :] * jnp.exp(lc), at=al_s[rows, :] * jnp.exp(lc - lwc),
                        bt=bec * e_out, kt=kc * e_out, bw=bec * e_rest, kw=kc * e_rest,
                        v=v_s[rows, :], dlast=jnp.exp(llast)))
    ms = [_dot(jnp.concatenate([pre[n]["at"][:, hsl(h)], pre[n]["rt"][:, hsl(h)]], axis=0),
               jnp.concatenate([_bd(pre[n]["bt"][:, hsl(h)]), _bd(pre[n]["kt"][:, hsl(h)])], axis=0), NT)
          for n, h in items]
    tms = _tri_inv([jnp.where(low_strict, -m[0:c, 0:2 * c], 0.0) for m in ms], masks)
    avs = [_dot(jnp.concatenate([jnp.where(low_strict, m[0:c, 2 * c:], 0.0),
                                 jnp.where(low_incl, m[c:, 2 * c:], 0.0)], axis=0),
                _bd(pre[n]["v"][:, hsl(h)]))
           for m, (n, h) in zip(ms, items)]
    tts = [_dot(tm, jnp.concatenate([_bd(pre[n]["at"][:, hsl(h)]), _bd(av[0:c])], axis=1))
           for tm, av, (n, h) in zip(tms, avs, items)]
    state = [s_ref[h] for h in range(npair)]
    for n in range(nchunk):
        p = pre[n]
        idx = [n * npair + h for h in range(npair)]
        sas = [_dot(jnp.concatenate([tts[i][:, 0:2 * c], p["rt"][:, hsl(h)]], axis=0), state[h], NT)
               for h, i in enumerate(idx)]
        us = [sa[0:c] + tts[i][:, 2 * c:] for sa, i in zip(sas, idx)]
        upd = [_dot(jnp.concatenate([u, p["v"][:, hsl(h)]], axis=0),
                    jnp.concatenate([p["bw"][:, hsl(h)], p["kw"][:, hsl(h)]], axis=0), TN) * bd_mask
               for h, u in enumerate(us)]
        state = [state[h] * p["dlast"][:, hsl(h)] + upd[h] for h in range(npair)]
        ys = [sas[h][c:] + _dot(jnp.where(low_incl, ms[i][c:, 0:2 * c], 0.0), _bd(us[h])) + avs[i][c:]
              for h, i in enumerate(idx)]
        y_s[n * c:(n + 1) * c, :] = jnp.concatenate(ys, axis=1)
    for h in range(npair):
        s_ref[h] = state[h]

    y = y_s[...]
    inv_d = 1.0 / HEAD_DIM
    mean = _segsum(y, ones_bd) * inv_d
    yc = y - mean
    var = _segsum(yc * yc, ones_bd) * inv_d
    y = yc * lax.rsqrt(var + RWKV_GN_EPS) * lw_ref[...] + lb_ref[...]
    y = y + _segsum(r * k * rk_ref[...], ones_bd) * v
    y_ref[...] = (y * g).astype(bf16)


def _rwkv(c_a, v_first, p, l):
    has_vres = v_first is not None
    blk = lambda n: pl.BlockSpec((RW_TB, n), lambda i: (i, 0))
    in_specs = [blk(NA)]
    args = [c_a]
    if has_vres:
        in_specs.append(blk(GROUP_W))
        args.append(v_first)
    names = ("mu", "w0", "w_up", "a0", "a_up", "g_up", "k_k", "k_a", "r_k", "lnx_w", "lnx_b")
    in_specs += [_layer_spec(p[k], l) for k in names]
    args += [p[k] for k in names]
    if has_vres:
        in_specs += [_layer_spec(p["v0"], l - 1), _layer_spec(p["vres_up"], l - 1)]
        args += [p["v0"], p["vres_up"]]
        out_specs = blk(GROUP_W)
        out_shape = jax.ShapeDtypeStruct((SEQ, GROUP_W), bf16)
    else:
        out_specs = [blk(GROUP_W), blk(GROUP_W)]
        out_shape = [jax.ShapeDtypeStruct((SEQ, GROUP_W), bf16),
                     jax.ShapeDtypeStruct((SEQ, GROUP_W), f32)]
    scratch = [pltpu.VMEM((8, 1024), f32), pltpu.VMEM((N_HEADS // 2, 2 * HEAD_DIM, 2 * HEAD_DIM), f32)]
    scratch += [pltpu.VMEM((RW_TB, GROUP_W), f32) for _ in range(7)]
    out = pl.pallas_call(
        functools.partial(_rwkv_kernel, has_vres=has_vres),
        grid=(SEQ // RW_TB,),
        in_specs=in_specs,
        out_specs=out_specs,
        out_shape=out_shape,
        scratch_shapes=scratch,
        compiler_params=pltpu.CompilerParams(dimension_semantics=("arbitrary",),
                                             vmem_limit_bytes=VMEM_LIMIT),
        name="rwkv7",
    )(*args)
    if has_vres:
        return out, v_first
    return out[0], out[1]


GD_TB = 512
GD_C = 64
GD_TICKS = 7


def _gdn_kernel(c_ref, cw_ref, alog_ref, dtb_ref, alogc_ref, dtbc_ref, ng_ref, y_ref,
                carry_ref, s_ref, q_s, k_s, v_s, be_s, g_s, o_s,
                u_p, w_p, qe_p, kd_p, qk_p, gz_p, egl_p):
    step = pl.program_id(0)

    @pl.when(step == 0)
    def _():
        for ref in (carry_ref, s_ref, u_p, w_p, qe_p, kd_p, qk_p, gz_p, egl_p):
            ref[...] = jnp.zeros_like(ref)

    wr = step % 2
    rd = 1 - wr
    c = GD_C
    nchunk = GD_TB // c
    npair = N_HEADS // 2
    hsl = lambda h: slice(2 * h * HEAD_DIM, 2 * (h + 1) * HEAD_DIM)
    bd_mask = (_iota((2 * c, 2 * c), 0) // c == _iota((2 * c, 2 * c), 1) // c).astype(f32)
    ones_bd = _group_ones(GROUP_W, HEAD_DIM)
    state = [s_ref[h] for h in range(npair)]

    def recurrence():
        for n in range(nchunk):
            rows = slice(n * c, (n + 1) * c)
            wss = [jnp.dot(jnp.concatenate([w_p[rd, rows, hsl(h)], qe_p[rd, rows, hsl(h)]], axis=0),
                           state[h].astype(bf16), preferred_element_type=f32)
                   for h in range(npair)]
            yield
            vns = [u_p[rd, rows, hsl(h)] - wss[h][0:c] for h in range(npair)]
            upd = [_dot(kd_p[rd, rows, hsl(h)], vns[h], TN) * bd_mask for h in range(npair)]
            yield
            egl = egl_p[rd, n:n + 1, :]
            for h in range(npair):
                state[h] = state[h] * egl[:, hsl(h)] + upd[h]
            os_ = [wss[h][c:] + _dot(qk_p[rd, rows, hsl(h)], _bd(vns[h])) for h in range(npair)]
            o_s[rows, :] = jnp.concatenate(os_, axis=1)
            yield

    chain = recurrence()
    calls = [0]

    def tick():
        calls[0] += 1
        if calls[0] % GD_TICKS == 0:
            next(chain, None)

    raw = c_ref[:, 0:768]
    carry = carry_ref[...]
    conv = raw * cw_ref[3:4, :]
    for s in range(1, 4):
        conv = conv + _shift_rows(raw, carry, s) * cw_ref[3 - s:4 - s, :]
    carry_ref[...] = raw[GD_TB - 8:GD_TB]
    qkv = conv * _sigmoid(conv)
    next(chain, None)
    q = qkv[:, 0:256]
    k = qkv[:, 256:512]
    q_s[...] = q * lax.rsqrt(_segsum(q * q, ones_bd) + 1e-6) * (HEAD_DIM ** -0.5)
    next(chain, None)
    k_s[...] = k * lax.rsqrt(_segsum(k * k, ones_bd) + 1e-6)
    v_s[...] = qkv[:, 512:768]
    z = c_ref[:, 768:1024]
    gz_p[wr] = z * _sigmoid(z)
    next(chain, None)
    small = c_ref[:, 1024:1152]
    er, ec = _iota((128, GROUP_W), 0), _iota((128, GROUP_W), 1)
    b_exp = _dot_sel_rhs(small, (er == ec // HEAD_DIM).astype(bf16))
    a_exp = _dot_sel_rhs(small, (er == ec // HEAD_DIM + N_HEADS).astype(bf16))
    be_s[...] = _sigmoid(b_exp)
    g_s[...] = -jnp.exp(alog_ref[...]) * _softplus(a_exp + dtb_ref[...])

    tril_incl = (_iota((c, c), 0) >= _iota((c, c), 1)).astype(bf16)
    masks = _pair_masks(c)
    low_strict, low_incl = masks["strict"], masks["incl"]
    sel8 = (_iota((8, 128), 0) == _iota((8, 128), 1)).astype(bf16)
    g_rows = -jnp.exp(alogc_ref[...]) * _softplus(_dot_sel_lhs(sel8, small, NT) + dtbc_ref[...])
    tj, ti = _iota((GD_TB, GD_TB), 0), _iota((GD_TB, GD_TB), 1)
    gc_rows = _dot_sel_rhs(g_rows, ((tj // c == ti // c) & (tj <= ti)).astype(bf16))

    items = [(n, h) for n in range(nchunk) for h in range(npair)]
    pre = []
    for n in range(nchunk):
        rows = slice(n * c, (n + 1) * c)
        gc = _dot_sel_lhs(tril_incl, g_s[rows, :])
        glast = gc[c - 1:c, :]
        egc = jnp.exp(gc)
        kc, bc = k_s[rows, :], be_s[rows, :]
        kb = kc * bc
        pre.append(dict(gc=gc, k=kc, kb=kb, q=q_s[rows, :], vb=v_s[rows, :] * bc, kbe=kb * egc))
        qe_p[wr, rows, :] = (q_s[rows, :] * egc).astype(bf16)
        kd_p[wr, rows, :] = (kc * jnp.exp(glast - gc)).astype(bf16)
        egl_p[wr, n:n + 1, :] = jnp.exp(glast)
        tick()

    def gc_row(n, h):
        return jnp.concatenate([gc_rows[N_HEADS + 2 * h + i:N_HEADS + 2 * h + i + 1, n * c:(n + 1) * c]
                                for i in range(2)], axis=1)

    dms, aqs = [], []
    for n, h in items:
        dms.append(jnp.exp(jnp.where(low_incl, pre[n]["gc"][:, hsl(h)] - gc_row(n, h), NEG)))
        aqs.append(_dot(jnp.concatenate([pre[n]["kb"][:, hsl(h)], pre[n]["q"][:, hsl(h)]], axis=0),
                        _bd(pre[n]["k"][:, hsl(h)]), NT))
        qk_p[wr, n * c:(n + 1) * c, hsl(h)] = (aqs[-1][c:] * dms[-1]).astype(bf16)
        tick()
    tms = _tri_inv([jnp.where(low_strict, aq[0:c] * dm, 0.0) for aq, dm in zip(aqs, dms)], masks, tick)
    for tm, (n, h) in zip(tms, items):
        uw = _dot(tm, jnp.concatenate([_bd(pre[n]["vb"][:, hsl(h)]), _bd(pre[n]["kbe"][:, hsl(h)])], axis=1))
        u_p[wr, n * c:(n + 1) * c, hsl(h)] = uw[:, 0:2 * c]
        w_p[wr, n * c:(n + 1) * c, hsl(h)] = uw[:, 2 * c:].astype(bf16)
        tick()
    for _ in chain:
        pass
    for h in range(npair):
        s_ref[h] = state[h]

    o = o_s[...]
    ms = _segsum(o * o, ones_bd) * (1.0 / HEAD_DIM)
    y_ref[...] = (o * lax.rsqrt(ms + EPS) * ng_ref[...] * gz_p[rd]).astype(bf16)


def _gdn(c_b, p, l):
    names = ("conv_w", "a_log", "dt_bias", "a_log_col", "dt_bias_col", "norm_g")
    nblk = SEQ // GD_TB
    scratch = [pltpu.VMEM((8, 768), f32), pltpu.VMEM((N_HEADS // 2, 2 * HEAD_DIM, 2 * HEAD_DIM), f32)]
    scratch += [pltpu.VMEM((GD_TB, GROUP_W), f32) for _ in range(6)]
    scratch += [pltpu.VMEM((2, GD_TB, GROUP_W), f32)]
    scratch += [pltpu.VMEM((2, GD_TB, GROUP_W), bf16) for _ in range(4)]
    scratch += [pltpu.VMEM((2, GD_TB, GROUP_W), f32), pltpu.VMEM((2, GD_TB // GD_C, GROUP_W), f32)]
    return pl.pallas_call(
        _gdn_kernel,
        grid=(nblk + 1,),
        in_specs=[pl.BlockSpec((GD_TB, NB), lambda i: (jnp.minimum(i, nblk - 1), 0))]
        + [_layer_spec(p[k], l) for k in names],
        out_specs=pl.BlockSpec((GD_TB, GROUP_W), lambda i: (jnp.maximum(i - 1, 0), 0)),
        out_shape=jax.ShapeDtypeStruct((SEQ, GROUP_W), bf16),
        scratch_shapes=scratch,
        compiler_params=pltpu.CompilerParams(dimension_semantics=("arbitrary",),
                                             vmem_limit_bytes=VMEM_LIMIT),
        name="gdn",
    )(c_b, *[p[k] for k in names])


GL_TB = 256
GL_C = 16
GL_S = 8


def _gla_kernel(c_ref, gup_ref, gb_ref, ng_ref, y_ref, st_ref, sx_s, o_s):
    @pl.when(pl.program_id(0) == 0)
    def _():
        st_ref[...] = jnp.zeros_like(st_ref)

    tb, c, s = GL_TB, GL_C, GL_S
    nchunk, nsub = tb // c, tb // s
    q = c_ref[:, 0:128] * (GLA_HEAD_K ** -0.5)
    k = c_ref[:, 128:256]
    v = c_ref[:, 256:512]
    pre = _dot(c_ref[:, 768:896], gup_ref[...]) + gb_ref[...]
    la = -_softplus(-pre) * (1.0 / 16.0)
    tj, ti = _iota((tb, tb), 0), _iota((tb, tb), 1)
    b = _dot_sel_lhs(((tj // c == ti // c) & (ti <= tj)).astype(bf16), la)
    qi = q * jnp.exp(b)

    ind_e = (_iota((GLA_KEY, GROUP_W), 0) // GLA_HEAD_K == _iota((GLA_KEY, GROUP_W), 1) // HEAD_DIM).astype(bf16)
    bd_mask = (_iota((GROUP_W, GLA_KEY), 0) // HEAD_DIM == _iota((GROUP_W, GLA_KEY), 1) // GLA_HEAD_K).astype(f32)

    b3, q3, k3 = (t.reshape(nsub, s, GLA_KEY) for t in (b, q, k))
    ri = _iota((nsub, s, GLA_KEY), 1)
    terms = []
    for j in range(s):
        e = jnp.exp(jnp.where(ri >= j, b3 - b3[:, j:j + 1, :], NEG))
        terms.append((q3 * (k3[:, j:j + 1, :] * e)).reshape(tb, GLA_KEY).astype(bf16))
    sx_s[...] = jnp.dot(jnp.concatenate(terms, axis=0), ind_e, preferred_element_type=f32)
    v3 = v.reshape(nsub, s, GROUP_W)
    o3 = sx_s[0:tb, :].reshape(nsub, s, GROUP_W) * v3[:, 0:1, :]
    for j in range(1, s):
        o3 = o3 + sx_s[j * tb:(j + 1) * tb, :].reshape(nsub, s, GROUP_W) * v3[:, j:j + 1, :]

    b4, q4, k4 = (t.reshape(nchunk, 2, s, GLA_KEY) for t in (b, q, k))
    bref = b4[:, 0, s - 1:s, :]
    qd = (q4[:, 1] * jnp.exp(b4[:, 1] - bref)).reshape(nchunk * s, GLA_KEY)
    kd = (k4[:, 0] * jnp.exp(bref - b4[:, 0])).reshape(nchunk * s, GLA_KEY)
    v0 = v.reshape(nchunk, 2, s, GROUP_W)[:, 0].reshape(nchunk * s, GROUP_W)
    head_k = _iota((nchunk * s, GLA_KEY), 1) // GLA_HEAD_K
    head_v = _iota((nchunk * s, GROUP_W), 1) // HEAD_DIM
    ks = jnp.concatenate([jnp.where(head_k == h, kd, 0.0) for h in range(N_HEADS)], axis=0)
    vs = jnp.concatenate([jnp.where(head_v == h, v0, 0.0) for h in range(N_HEADS)], axis=0)
    sc = _dot(qd, ks, NT)
    sr, scol = _iota(sc.shape, 0), _iota(sc.shape, 1)
    sc = jnp.where(sr // s == (scol % (nchunk * s)) // s, sc, 0.0)
    o_off = _dot(sc, vs).reshape(nchunk, 1, s, GROUP_W)
    o4 = o3.reshape(nchunk, 2, s, GROUP_W)
    o_intra = jnp.concatenate([o4[:, 0:1], o4[:, 1:2] + o_off], axis=1).reshape(tb, GROUP_W)

    blasts = [b[(n + 1) * c - 1:(n + 1) * c, :] for n in range(nchunk)]
    upds = [_dot(v[n * c:(n + 1) * c], k[n * c:(n + 1) * c] * jnp.exp(blasts[n] - b[n * c:(n + 1) * c]), TN)
            * bd_mask for n in range(nchunk)]
    st = st_ref[...]
    for n in range(nchunk):
        rows = slice(n * c, (n + 1) * c)
        o_s[rows, :] = o_intra[rows] + _dot(qi[rows], st, NT)
        st = st * jnp.exp(blasts[n]) + upds[n]
    st_ref[...] = st

    o = o_s[...]
    ms = _segsum(o * o, _group_ones(GROUP_W, HEAD_DIM)) * (1.0 / HEAD_DIM)
    gate = c_ref[:, 512:768]
    y_ref[...] = (o * lax.rsqrt(ms + EPS) * ng_ref[...] * (gate * _sigmoid(gate))).astype(bf16)


def _gla(c_c, p, l):
    names = ("gk_up", "gk_bias", "norm_g")
    return pl.pallas_call(
        _gla_kernel,
        grid=(SEQ // GL_TB,),
        in_specs=[pl.BlockSpec((GL_TB, NC), lambda i: (i, 0))] + [_layer_spec(p[k], l) for k in names],
        out_specs=pl.BlockSpec((GL_TB, GROUP_W), lambda i: (i, 0)),
        out_shape=jax.ShapeDtypeStruct((SEQ, GROUP_W), bf16),
        scratch_shapes=[pltpu.VMEM((GROUP_W, GLA_KEY), f32),
                        pltpu.VMEM((GL_TB * GL_S, GROUP_W), f32), pltpu.VMEM((GL_TB, GROUP_W), f32)],
        compiler_params=pltpu.CompilerParams(dimension_semantics=("arbitrary",),
                                             vmem_limit_bytes=VMEM_LIMIT),
        name="gla",
    )(c_c, *[p[k] for k in names])


FF_TM = 512
FF_TF = 1024


def _outffn_kernel(*refs, final, layer):
    if final:
        (x_ref, ya_ref, yb_ref, yc_ref, yd_ref, wo_hbm, g_ref, wu_hbm, wd_hbm, gf_ref, o_ref,
         wo_ref, wu_ref, wd_ref, sem) = refs
    else:
        (x_ref, ya_ref, yb_ref, yc_ref, yd_ref, wo_hbm, g_ref, wu_hbm, wd_hbm, o_ref,
         wo_ref, wu_ref, wd_ref, sem) = refs

    @pl.when(pl.program_id(0) == 0)
    def _():
        copies = [pltpu.make_async_copy(src.at[layer], dst, sem.at[i])
                  for i, (src, dst) in enumerate(((wo_hbm, wo_ref), (wu_hbm, wu_ref), (wd_hbm, wd_ref)))]
        for cp in copies:
            cp.start()
        for cp in copies:
            cp.wait()

    y = jnp.concatenate([ya_ref[...], yb_ref[...], yc_ref[...], yd_ref[...]], axis=1)
    x1 = x_ref[...] + jnp.dot(y, wo_ref[...], preferred_element_type=f32)
    ms = jnp.mean(x1 * x1, axis=-1, keepdims=True)
    h = (x1 * lax.rsqrt(ms + EPS) * g_ref[...]).astype(bf16)
    x2 = x1
    for kf in range(D_FF // FF_TF):
        cols = slice(kf * FF_TF, (kf + 1) * FF_TF)
        hid = jnp.maximum(jnp.dot(h, wu_ref[:, cols], preferred_element_type=f32), 0.0)
        x2 = x2 + jnp.dot((hid * hid).astype(bf16), wd_ref[cols, :], preferred_element_type=f32)
    if final:
        ms = jnp.mean(x2 * x2, axis=-1, keepdims=True)
        x2 = x2 * lax.rsqrt(ms + EPS) * gf_ref[...]
    o_ref[...] = x2


def _out_ffn(x, ys, p, l, final):
    hbm = pl.BlockSpec(memory_space=pl.ANY)
    yspec = pl.BlockSpec((FF_TM, GROUP_W), lambda i: (i, 0))
    in_specs = [pl.BlockSpec((FF_TM, D_MODEL), lambda i: (i, 0)), yspec, yspec, yspec, yspec,
                hbm, _layer_spec(p["g"], l), hbm, hbm]
    args = [x, *ys, p["w_out"], p["g"], p["w_up"], p["w_down"]]
    if final:
        in_specs.append(pl.BlockSpec((1, D_MODEL), lambda i: (0, 0)))
        args.append(p["g_final"])
    return pl.pallas_call(
        functools.partial(_outffn_kernel, final=final, layer=l),
        grid=(SEQ // FF_TM,),
        in_specs=in_specs,
        out_specs=pl.BlockSpec((FF_TM, D_MODEL), lambda i: (i, 0)),
        out_shape=jax.ShapeDtypeStruct((SEQ, D_MODEL), f32),
        scratch_shapes=[pltpu.VMEM((D_MODEL, D_MODEL), bf16), pltpu.VMEM((D_MODEL, D_FF), bf16),
                        pltpu.VMEM((D_FF, D_MODEL), bf16), pltpu.SemaphoreType.DMA((3,))],
        compiler_params=pltpu.CompilerParams(dimension_semantics=("arbitrary",),
                                             vmem_limit_bytes=VMEM_LIMIT),
        name="out_ffn",
    )(*args)


def kernel(x, w_in, w_out, norm_mix_g, norm_ffn_g, norm_final_g, rwkv_mu, rwkv_w0, rwkv_w_up, rwkv_a0, rwkv_a_up, rwkv_g_up, rwkv_k_k, rwkv_k_a, rwkv_r_k, rwkv_lnx_w, rwkv_lnx_b, rwkv_v0, rwkv_vres_down, rwkv_vres_up, gdn_conv_w, gdn_a_log, gdn_dt_bias, gdn_norm_g, gla_gk_up, gla_gk_bias, gla_norm_g, sgu_ln_g, sgu_ln_b, sgu_w_s, sgu_b_s, ffn_w_up, ffn_w_down):
    depth = w_in.shape[0]
    row = lambda a: a.reshape(depth, 1, -1)
    per_head = lambda a: jnp.repeat(a, HEAD_DIM, axis=-1).reshape(depth, 1, -1)
    pad_cols = lambda w, n: jnp.pad(w, ((0, 0), (0, 0), (0, n - w.shape[2])))
    pad_rows = lambda w, top, total: jnp.pad(w, ((0, 0), (top, total - top - w.shape[1]), (0, 0)))
    vres_down = jnp.pad(rwkv_vres_down, ((1, 0), (0, 0), (0, 0)))
    w_comb = jnp.concatenate(
        [pad_cols(jnp.concatenate([w_in[:, :, 0:1024], vres_down], axis=2), NA),
         pad_cols(w_in[:, :, 1024:2056], NB), pad_cols(w_in[:, :, 2056:2840], NC),
         w_in[:, :, 2840:3352]], axis=2).astype(bf16)
    p_in = dict(g=row(norm_mix_g), w=w_comb, ln_g=row(sgu_ln_g), ln_b=row(sgu_ln_b),
                w_cat=sgu_w_s.transpose(0, 2, 1, 3).reshape(depth, SG_C, 4 * SG_C),
                bias_tile=jnp.repeat(sgu_b_s.transpose(0, 2, 1), HEAD_DIM, axis=2))
    p_rwkv = dict(mu=row(rwkv_mu), w0=row(rwkv_w0), w_up=pad_rows(rwkv_w_up, 0, 128),
                  a0=row(rwkv_a0), a_up=pad_rows(rwkv_a_up, 64, 128), g_up=rwkv_g_up,
                  k_k=row(rwkv_k_k), k_a=row(rwkv_k_a), r_k=row(rwkv_r_k),
                  lnx_w=row(rwkv_lnx_w), lnx_b=row(rwkv_lnx_b),
                  v0=rwkv_v0.reshape(depth - 1, 1, -1), vres_up=pad_rows(rwkv_vres_up, 0, 128))
    p_gdn = dict(conv_w=gdn_conv_w, a_log=per_head(gdn_a_log), dt_bias=per_head(gdn_dt_bias),
                 a_log_col=jnp.pad(gdn_a_log, ((0, 0), (N_HEADS, 0))).reshape(depth, 8, 1),
                 dt_bias_col=jnp.pad(gdn_dt_bias, ((0, 0), (N_HEADS, 0))).reshape(depth, 8, 1),
                 norm_g=row(jnp.tile(gdn_norm_g, (1, N_HEADS))))
    p_gla = dict(gk_up=pad_rows(gla_gk_up, 0, 128), gk_bias=row(gla_gk_bias),
                 norm_g=row(jnp.tile(gla_norm_g, (1, N_HEADS))))
    p_ffn = dict(w_out=w_out.astype(bf16), g=row(norm_ffn_g), w_up=ffn_w_up.astype(bf16),
                 w_down=ffn_w_down.astype(bf16), g_final=norm_final_g.reshape(1, -1))

    xx = x[0]
    v_first = None
    for l in range(depth):
        c_a, c_b, c_c, y_d = _in_proj(xx, p_in, l)
        y_a, v_first = _rwkv(c_a, v_first, p_rwkv, l)
        y_b = _gdn(c_b, p_gdn, l)
        y_c = _gla(c_c, p_gla, l)
        xx = _out_ffn(xx, (y_a, y_b, y_c, y_d), p_ffn, l, final=(l == depth - 1))
    return xx[None]
```

```python
import functools

import jax
import jax.numpy as jnp
from jax import lax
from jax.experimental import pallas as pl
from jax.experimental.pallas import tpu as pltpu

f32 = jnp.float32
bf16 = jnp.bfloat16

SEQ = 16384
D_MODEL = 1024
GROUP_W = 256
HEAD_DIM = 64
N_HEADS = 4
GLA_KEY = 128
GLA_HEAD_K = 32
D_FF = 4096
EPS = 1e-6
RWKV_GN_EPS = 64e-5
NEG = -1e30

NA, NB, NC, ND = 1152, 1152, 896, 512
N_PAD = NA + NB + NC + ND

VMEM_LIMIT = 56 * 1024 * 1024

NN = (((1,), (0,)), ((), ()))
NT = (((1,), (1,)), ((), ()))
TN = (((0,), (0,)), ((), ()))


def _dot(a, b, dims=NN):
    return lax.dot_general(a.astype(bf16), b.astype(bf16), dims, preferred_element_type=f32)


def _iota(shape, axis):
    return lax.broadcasted_iota(jnp.int32, shape, axis)


def _layer_spec(a, l):
    return pl.BlockSpec((None,) + a.shape[1:], lambda *_: (l,) + (0,) * (a.ndim - 1))


def _segsum(x, ones_bd):
    hi = x.astype(bf16)
    lo = (x - hi.astype(f32)).astype(bf16)
    return (jnp.dot(hi, ones_bd, preferred_element_type=f32)
            + jnp.dot(lo, ones_bd, preferred_element_type=f32))


def _split3(x):
    hi = x.astype(bf16)
    r1 = x - hi.astype(f32)
    mid = r1.astype(bf16)
    lo = (r1 - mid.astype(f32)).astype(bf16)
    return hi, mid, lo


def _dot_sel_lhs(sel, x, dims=NN):
    return sum(lax.dot_general(sel, t, dims, preferred_element_type=f32) for t in _split3(x))


def _dot_sel_rhs(x, sel, dims=NN):
    return sum(lax.dot_general(t, sel, dims, preferred_element_type=f32) for t in _split3(x))


def _group_ones(n, width):
    return (_iota((n, n), 0) // width == _iota((n, n), 1) // width).astype(bf16)


def _sigmoid(x):
    return 1.0 / (1.0 + jnp.exp(-x))


def _softplus(x):
    return jnp.maximum(x, 0.0) + jnp.log1p(jnp.exp(-jnp.abs(x)))


def _shift_rows(x, carry, s):
    xs = pltpu.roll(x, s, 0)
    fix = pltpu.roll(carry, s, 0)
    first = jnp.where(_iota(carry.shape, 0) < s, fix, xs[0:8])
    return jnp.concatenate([first, xs[8:]], axis=0)


def _bd(xp):
    xb = xp.astype(bf16)
    left = _iota(xb.shape, 1) < HEAD_DIM
    zero = jnp.zeros_like(xb)
    return jnp.concatenate([jnp.where(left, xb, zero), jnp.where(left, zero, xb)], axis=0)


def _pair_masks(c):
    ri, cj = _iota((c, 2 * c), 0), _iota((c, 2 * c), 1) & (c - 1)
    eye = (ri == cj).astype(f32)
    m16 = (ri // 16 == cj // 16).astype(f32)
    mo1 = ((ri // 32 == cj // 32) & (ri // 16 == cj // 16 + 1)).astype(f32)
    mo2 = ((ri // 32 == 1) & (cj // 32 == 0)).astype(f32)
    return dict(eye=eye, m16=m16, mo1=mo1, mo2=mo2, strict=ri > cj, incl=ri >= cj)


def _tri_inv(lms, masks, tick=lambda: None):
    c = RW_C

    def each(fn, *lists):
        out = []
        for args in zip(*lists):
            out.append(fn(*args))
            tick()
        return out

    ps = [-(lm * masks["m16"]) for lm in lms]
    ts = [masks["eye"] + p for p in ps]
    ps = each(lambda p: _dot(p, _bd(p)), ps)
    for _ in range(2):
        outs = each(lambda t, p: _dot(jnp.concatenate([t, p], axis=0), _bd(p)), ts, ps)
        ts = [t + o[0:c] for t, o in zip(ts, outs)]
        ps = [o[c:] for o in outs]
    ts = each(lambda t, p: t + _dot(t, _bd(p)), ts, ps)
    for mo in (masks["mo1"], masks["mo2"]):
        xs = each(lambda lm, t: _dot(lm * mo, _bd(t)), lms, ts)
        ts = each(lambda t, x: t - _dot(t, _bd(x)), ts, xs)
    return ts


IN_TM = 512
SG_C = 128


def _sgu_mix(x, lg_ref, lb_ref, w_ref, bias_ref):
    gx = 0.5 * x * (1.0 + jnp.tanh(0.7978845608028654 * (x + 0.044715 * x * x * x)))
    u = gx[:, 0:256]
    v = gx[:, 256:512]
    mu = jnp.mean(v, axis=-1, keepdims=True)
    vc = v - mu
    var = jnp.mean(vc * vc, axis=-1, keepdims=True)
    v = vc * lax.rsqrt(var + 1e-5) * lg_ref[...] + lb_ref[...]
    wr, wc = _iota((SG_C, 4 * SG_C), 0), _iota((SG_C, 4 * SG_C), 1)
    w = jnp.where(wc % SG_C <= wr, w_ref[...], 0.0).astype(bf16)
    lane_g = _iota((SG_C, GROUP_W), 1) // HEAD_DIM
    outs = []
    for n in range(x.shape[0] // SG_C):
        vn = v[n * SG_C:(n + 1) * SG_C, :]
        vst = jnp.concatenate([jnp.where(lane_g == g, vn, 0.0) for g in range(4)], axis=0)
        outs.append(jnp.dot(w, vst.astype(bf16), preferred_element_type=f32) + bias_ref[...])
    return (u * jnp.concatenate(outs, axis=0)).astype(bf16)


def _inproj_kernel(x_ref, g_ref, w_ref, lg_ref, lb_ref, ws_ref, bias_ref, oa_ref, ob_ref, oc_ref, yd_ref):
    x = x_ref[...]
    ms = jnp.mean(x * x, axis=-1, keepdims=True)
    h = (x * lax.rsqrt(ms + EPS) * g_ref[...]).astype(bf16)
    c_d = jnp.dot(h, w_ref[:, NA + NB + NC:], preferred_element_type=f32)
    yd_ref[...] = _sgu_mix(c_d, lg_ref, lb_ref, ws_ref, bias_ref)
    off = 0
    for o_ref, n in ((oa_ref, NA), (ob_ref, NB), (oc_ref, NC)):
        o_ref[...] = jnp.dot(h, w_ref[:, off:off + n], preferred_element_type=f32)
        off += n


def _in_proj(x, p, l):
    tm = IN_TM
    names = ("g", "w", "ln_g", "ln_b", "w_cat", "bias_tile")
    return pl.pallas_call(
        _inproj_kernel,
        grid=(SEQ // tm,),
        in_specs=[pl.BlockSpec((tm, D_MODEL), lambda i: (i, 0))] + [_layer_spec(p[k], l) for k in names],
        out_specs=[pl.BlockSpec((tm, n), lambda i: (i, 0)) for n in (NA, NB, NC, GROUP_W)],
        out_shape=[jax.ShapeDtypeStruct((SEQ, n), f32) for n in (NA, NB, NC)]
        + [jax.ShapeDtypeStruct((SEQ, GROUP_W), bf16)],
        compiler_params=pltpu.CompilerParams(dimension_semantics=("arbitrary",),
                                             vmem_limit_bytes=VMEM_LIMIT),
        name="in_proj",
    )(x, *[p[k] for k in names])


RW_TB = 512
RW_C = 64
RW_PRO = 2
RW_TICKS = 13


def _rwkv_kernel(*refs, has_vres):
    if has_vres:
        (c_ref, vf_ref, mu_ref, w0_ref, wup_ref, a0_ref, aup_ref, gup_ref, kk_ref, ka_ref, rk_ref,
         lw_ref, lb_ref, v0_ref, vup_ref, y_ref,
         carry_ref, s_ref, r_s, k_s, v_s, lw_s, al_s, be_s, y_s,
         ta_p, rt_p, vb_p, bw_p, kw_p, arb_p, tv_p, av_p, bonus_p, g_p, dl_p) = refs
    else:
        (c_ref, mu_ref, w0_ref, wup_ref, a0_ref, aup_ref, gup_ref, kk_ref, ka_ref, rk_ref,
         lw_ref, lb_ref, y_ref, vf_out_ref,
         carry_ref, s_ref, r_s, k_s, v_s, lw_s, al_s, be_s, y_s,
         ta_p, rt_p, vb_p, bw_p, kw_p, arb_p, tv_p, av_p, bonus_p, g_p, dl_p) = refs
    step = pl.program_id(0)

    @pl.when(step == 0)
    def _():
        for ref in (carry_ref, s_ref, ta_p, rt_p, vb_p, bw_p, kw_p, arb_p, tv_p, av_p, bonus_p, g_p, dl_p):
            ref[...] = jnp.zeros_like(ref)

    wr = step % 2
    rd = 1 - wr
    c = RW_C
    nchunk = RW_TB // c
    npair = N_HEADS // 2
    hsl = lambda h: slice(2 * h * HEAD_DIM, 2 * (h + 1) * HEAD_DIM)
    bd_mask = (_iota((2 * c, 2 * c), 0) // c == _iota((2 * c, 2 * c), 1) // c).astype(f32)
    state = [s_ref[h] for h in range(npair)]

    def recurrence():
        for n in range(nchunk):
            rows = slice(n * c, (n + 1) * c)
            sas = [lax.dot_general(jnp.concatenate([ta_p[rd, rows, hsl(h)], rt_p[rd, rows, hsl(h)]], axis=0),
                                   state[h].astype(bf16), NT, preferred_element_type=f32)
                   for h in range(npair)]
            yield
            us = [sas[h][0:c] + tv_p[rd, rows, hsl(h)] for h in range(npair)]
            upd = [_dot(jnp.concatenate([us[h].astype(bf16), vb_p[rd, rows, hsl(h)]], axis=0),
                        jnp.concatenate([bw_p[rd, rows, hsl(h)], kw_p[rd, rows, hsl(h)]], axis=0), TN) * bd_mask
                   for h in range(npair)]
            yield
            dl = dl_p[rd, n:n + 1, :]
            for h in range(npair):
                state[h] = state[h] * dl[:, hsl(h)] + upd[h]
            ys = [sas[h][c:] + _dot(arb_p[rd, rows, hsl(h)], _bd(us[h])) + av_p[rd, rows, hsl(h)]
                  for h in range(npair)]
            y_s[rows, :] = jnp.concatenate(ys, axis=1)
            yield

    chain = recurrence()
    calls = [0]

    def advance(pieces=1):
        for _ in range(pieces):
            next(chain, None)

    def tick():
        calls[0] += 1
        if calls[0] % RW_TICKS == 0:
            advance()

    ones_bd = _group_ones(GROUP_W, HEAD_DIM)
    x = c_ref[:, 0:1024]
    x_prev = _shift_rows(x, carry_ref[...], 1)
    carry_ref[...] = x[RW_TB - 8:RW_TB]
    xs = x + (x_prev - x) * mu_ref[...]
    advance(RW_PRO)
    r = xs[:, 0:256]
    k = xs[:, 256:512]
    v = xs[:, 512:768]
    lora = xs[:, 768:896]
    w_pre = w0_ref[...] + _dot(jnp.tanh(lora), wup_ref[...])
    lw = -jnp.exp(-_softplus(-w_pre) - 0.5)
    advance(RW_PRO)
    a = _sigmoid(a0_ref[...] + _dot(lora, aup_ref[...]))
    g_p[wr] = _dot(_sigmoid(xs[:, 896:1024]), gup_ref[...])
    advance(RW_PRO)
    if has_vres:
        mix = _sigmoid(v0_ref[...] + _dot(c_ref[:, 1024:1152], vup_ref[...]))
        v = v + (vf_ref[...] - v) * mix
    else:
        @pl.when(step < SEQ // RW_TB)
        def _():
            vf_out_ref[...] = v
    kk = k * kk_ref[...]
    kk = kk * lax.rsqrt(_segsum(kk * kk, ones_bd) + 1e-24)
    advance(RW_PRO)
    k = k * (1.0 + (a - 1.0) * ka_ref[...])
    bonus_p[wr] = _segsum(r * k * rk_ref[...], ones_bd) * v
    advance(RW_PRO)
    r_s[...] = r
    k_s[...] = k
    v_s[...] = v
    lw_s[...] = lw
    al_s[...] = -kk
    be_s[...] = kk * a

    tril_incl = (_iota((c, c), 0) >= _iota((c, c), 1)).astype(bf16)
    masks = _pair_masks(c)
    low_strict, low_incl = masks["strict"], masks["incl"]

    items = [(n, h) for n in range(nchunk) for h in range(npair)]
    pre = []
    for n in range(nchunk):
        rows = slice(n * c, (n + 1) * c)
        lwc = lw_s[rows, :]
        lc = _dot_sel_lhs(tril_incl, lwc)
        llast = lc[c - 1:c, :]
        e_out = jnp.exp(-lc)
        e_rest = jnp.exp(llast - lc)
        kc, bec = k_s[rows, :], be_s[rows, :]
        rt = r_s[rows, :] * jnp.exp(lc)
        pre.append(dict(rt=rt, at=al_s[rows, :] * jnp.exp(lc - lwc), bt=bec * e_out, kt=kc * e_out,
                        v=v_s[rows, :]))
        rt_p[wr, rows, :] = rt.astype(bf16)
        vb_p[wr, rows, :] = v_s[rows, :].astype(bf16)
        bw_p[wr, rows, :] = (bec * e_rest).astype(bf16)
        kw_p[wr, rows, :] = (kc * e_rest).astype(bf16)
        dl_p[wr, n:n + 1, :] = jnp.exp(llast)
        tick()
    ms = []
    for n, h in items:
        ms.append(_dot(jnp.concatenate([pre[n]["at"][:, hsl(h)], pre[n]["rt"][:, hsl(h)]], axis=0),
                       jnp.concatenate([_bd(pre[n]["bt"][:, hsl(h)]), _bd(pre[n]["kt"][:, hsl(h)])], axis=0),
                       NT))
        arb_p[wr, n * c:(n + 1) * c, hsl(h)] = jnp.where(low_incl, ms[-1][c:, 0:2 * c], 0.0).astype(bf16)
        tick()
    tms = _tri_inv([jnp.where(low_strict, -m[0:c, 0:2 * c], 0.0) for m in ms], masks, tick)
    avs = []
    for m, (n, h) in zip(ms, items):
        avs.append(_dot(jnp.concatenate([jnp.where(low_strict, m[0:c, 2 * c:], 0.0),
                                         jnp.where(low_incl, m[c:, 2 * c:], 0.0)], axis=0),
                        _bd(pre[n]["v"][:, hsl(h)])))
        av_p[wr, n * c:(n + 1) * c, hsl(h)] = avs[-1][c:]
        tick()
    for tm, av, (n, h) in zip(tms, avs, items):
        tt = _dot(tm, jnp.concatenate([_bd(pre[n]["at"][:, hsl(h)]), _bd(av[0:c])], axis=1))
        ta_p[wr, n * c:(n + 1) * c, hsl(h)] = tt[:, 0:2 * c].astype(bf16)
        tv_p[wr, n * c:(n + 1) * c, hsl(h)] = tt[:, 2 * c:]
        tick()
    for _ in chain:
        pass
    for h in range(npair):
        s_ref[h] = state[h]

    y = y_s[...]
    inv_d = 1.0 / HEAD_DIM
    mean = _segsum(y, ones_bd) * inv_d
    yc = y - mean
    var = _segsum(yc * yc, ones_bd) * inv_d
    y = yc * lax.rsqrt(var + RWKV_GN_EPS) * lw_ref[...] + lb_ref[...]
    y_ref[...] = ((y + bonus_p[rd]) * g_p[rd]).astype(bf16)


def _rwkv(c_a, v_first, p, l):
    has_vres = v_first is not None
    nblk = SEQ // RW_TB
    blk = lambda n: pl.BlockSpec((RW_TB, n), lambda i: (jnp.minimum(i, nblk - 1), 0))
    blk_prev = lambda n: pl.BlockSpec((RW_TB, n), lambda i: (jnp.maximum(i - 1, 0), 0))
    in_specs = [blk(NA)]
    args = [c_a]
    if has_vres:
        in_specs.append(blk(GROUP_W))
        args.append(v_first)
    names = ("mu", "w0", "w_up", "a0", "a_up", "g_up", "k_k", "k_a", "r_k", "lnx_w", "lnx_b")
    in_specs += [_layer_spec(p[k], l) for k in names]
    args += [p[k] for k in names]
    if has_vres:
        in_specs += [_layer_spec(p["v0"], l - 1), _layer_spec(p["vres_up"], l - 1)]
        args += [p["v0"], p["vres_up"]]
        out_specs = blk_prev(GROUP_W)
        out_shape = jax.ShapeDtypeStruct((SEQ, GROUP_W), bf16)
    else:
        out_specs = [blk_prev(GROUP_W), blk(GROUP_W)]
        out_shape = [jax.ShapeDtypeStruct((SEQ, GROUP_W), bf16),
                     jax.ShapeDtypeStruct((SEQ, GROUP_W), f32)]
    scratch = [pltpu.VMEM((8, 1024), f32), pltpu.VMEM((N_HEADS // 2, 2 * HEAD_DIM, 2 * HEAD_DIM), f32)]
    scratch += [pltpu.VMEM((RW_TB, GROUP_W), f32) for _ in range(7)]
    scratch += [pltpu.VMEM((2, RW_TB, GROUP_W), bf16) for _ in range(6)]
    scratch += [pltpu.VMEM((2, RW_TB, GROUP_W), f32) for _ in range(4)]
    scratch += [pltpu.VMEM((2, RW_TB // RW_C, GROUP_W), f32)]
    out = pl.pallas_call(
        functools.partial(_rwkv_kernel, has_vres=has_vres),
        grid=(nblk + 1,),
        in_specs=in_specs,
        out_specs=out_specs,
        out_shape=out_shape,
        scratch_shapes=scratch,
        compiler_params=pltpu.CompilerParams(dimension_semantics=("arbitrary",),
                                             vmem_limit_bytes=VMEM_LIMIT),
        name="rwkv7",
    )(*args)
    if has_vres:
        return out, v_first
    return out[0], out[1]


GD_TB = 512
GD_C = 64
GD_PRO = 2
GD_TICKS = 9


def _gdn_kernel(c_ref, cw_ref, alog_ref, dtb_ref, alogc_ref, dtbc_ref, ng_ref, y_ref,
                carry_ref, s_ref, q_s, k_s, v_s, be_s, g_s, o_s,
                u_p, w_p, qe_p, kd_p, qk_p, gz_p, egl_p):
    step = pl.program_id(0)

    @pl.when(step == 0)
    def _():
        for ref in (carry_ref, s_ref, u_p, w_p, qe_p, kd_p, qk_p, gz_p, egl_p):
            ref[...] = jnp.zeros_like(ref)

    wr = step % 2
    rd = 1 - wr
    c = GD_C
    nchunk = GD_TB // c
    npair = N_HEADS // 2
    hsl = lambda h: slice(2 * h * HEAD_DIM, 2 * (h + 1) * HEAD_DIM)
    bd_mask = (_iota((2 * c, 2 * c), 0) // c == _iota((2 * c, 2 * c), 1) // c).astype(f32)
    ones_bd = _group_ones(GROUP_W, HEAD_DIM)
    state = [s_ref[h] for h in range(npair)]

    def recurrence():
        for n in range(nchunk):
            rows = slice(n * c, (n + 1) * c)
            wss = [jnp.dot(jnp.concatenate([w_p[rd, rows, hsl(h)], qe_p[rd, rows, hsl(h)]], axis=0),
                           state[h].astype(bf16), preferred_element_type=f32)
                   for h in range(npair)]
            yield
            vns = [u_p[rd, rows, hsl(h)] - wss[h][0:c] for h in range(npair)]
            upd = [_dot(kd_p[rd, rows, hsl(h)], vns[h], TN) * bd_mask for h in range(npair)]
            yield
            egl = egl_p[rd, n:n + 1, :]
            for h in range(npair):
                state[h] = state[h] * egl[:, hsl(h)] + upd[h]
            os_ = [wss[h][c:] + _dot(qk_p[rd, rows, hsl(h)], _bd(vns[h])) for h in range(npair)]
            o_s[rows, :] = jnp.concatenate(os_, axis=1)
            yield

    chain = recurrence()
    calls = [0]

    def tick():
        calls[0] += 1
        if calls[0] % GD_TICKS == 0:
            next(chain, None)

    raw = c_ref[:, 0:768]
    carry = carry_ref[...]
    conv = raw * cw_ref[3:4, :]
    for s in range(1, 4):
        conv = conv + _shift_rows(raw, carry, s) * cw_ref[3 - s:4 - s, :]
    carry_ref[...] = raw[GD_TB - 8:GD_TB]
    qkv = conv * _sigmoid(conv)
    for _ in range(GD_PRO):
        next(chain, None)
    q = qkv[:, 0:256]
    k = qkv[:, 256:512]
    q_s[...] = q * lax.rsqrt(_segsum(q * q, ones_bd) + 1e-6) * (HEAD_DIM ** -0.5)
    for _ in range(GD_PRO):
        next(chain, None)
    k_s[...] = k * lax.rsqrt(_segsum(k * k, ones_bd) + 1e-6)
    v_s[...] = qkv[:, 512:768]
    z = c_ref[:, 768:1024]
    gz_p[wr] = z * _sigmoid(z)
    for _ in range(GD_PRO):
        next(chain, None)
    small = c_ref[:, 1024:1152]
    er, ec = _iota((128, GROUP_W), 0), _iota((128, GROUP_W), 1)
    b_exp = _dot_sel_rhs(small, (er == ec // HEAD_DIM).astype(bf16))
    a_exp = _dot_sel_rhs(small, (er == ec // HEAD_DIM + N_HEADS).astype(bf16))
    be_s[...] = _sigmoid(b_exp)
    g_s[...] = -jnp.exp(alog_ref[...]) * _softplus(a_exp + dtb_ref[...])

    tril_incl = (_iota((c, c), 0) >= _iota((c, c), 1)).astype(bf16)
    masks = _pair_masks(c)
    low_strict, low_incl = masks["strict"], masks["incl"]
    sel8 = (_iota((8, 128), 0) == _iota((8, 128), 1)).astype(bf16)
    g_rows = -jnp.exp(alogc_ref[...]) * _softplus(_dot_sel_lhs(sel8, small, NT) + dtbc_ref[...])
    tj, ti = _iota((GD_TB, GD_TB), 0), _iota((GD_TB, GD_TB), 1)
    gc_rows = _dot_sel_rhs(g_rows, ((tj // c == ti // c) & (tj <= ti)).astype(bf16))

    items = [(n, h) for n in range(nchunk) for h in range(npair)]
    pre = []
    for n in range(nchunk):
        rows = slice(n * c, (n + 1) * c)
        gc = _dot_sel_lhs(tril_incl, g_s[rows, :])
        glast = gc[c - 1:c, :]
        egc = jnp.exp(gc)
        kc, bc = k_s[rows, :], be_s[rows, :]
        kb = kc * bc
        pre.append(dict(gc=gc, k=kc, kb=kb, q=q_s[rows, :], vb=v_s[rows, :] * bc, kbe=kb * egc))
        qe_p[wr, rows, :] = (q_s[rows, :] * egc).astype(bf16)
        kd_p[wr, rows, :] = (kc * jnp.exp(glast - gc)).astype(bf16)
        egl_p[wr, n:n + 1, :] = jnp.exp(glast)
        tick()

    def gc_row(n, h):
        return jnp.concatenate([gc_rows[N_HEADS + 2 * h + i:N_HEADS + 2 * h + i + 1, n * c:(n + 1) * c]
                                for i in range(2)], axis=1)

    dms, aqs = [], []
    for n, h in items:
        dms.append(jnp.exp(jnp.where(low_incl, pre[n]["gc"][:, hsl(h)] - gc_row(n, h), NEG)))
        aqs.append(_dot(jnp.concatenate([pre[n]["kb"][:, hsl(h)], pre[n]["q"][:, hsl(h)]], axis=0),
                        _bd(pre[n]["k"][:, hsl(h)]), NT))
        qk_p[wr, n * c:(n + 1) * c, hsl(h)] = (aqs[-1][c:] * dms[-1]).astype(bf16)
        tick()
    tms = _tri_inv([jnp.where(low_strict, aq[0:c] * dm, 0.0) for aq, dm in zip(aqs, dms)], masks, tick)
    for tm, (n, h) in zip(tms, items):
        uw = _dot(tm, jnp.concatenate([_bd(pre[n]["vb"][:, hsl(h)]), _bd(pre[n]["kbe"][:, hsl(h)])], axis=1))
        u_p[wr, n * c:(n + 1) * c, hsl(h)] = uw[:, 0:2 * c]
        w_p[wr, n * c:(n + 1) * c, hsl(h)] = uw[:, 2 * c:].astype(bf16)
        tick()
    for _ in chain:
        pass
    for h in range(npair):
        s_ref[h] = state[h]

    o = o_s[...]
    ms = _segsum(o * o, ones_bd) * (1.0 / HEAD_DIM)
    y_ref[...] = (o * lax.rsqrt(ms + EPS) * ng_ref[...] * gz_p[rd]).astype(bf16)


def _gdn(c_b, p, l):
    names = ("conv_w", "a_log", "dt_bias", "a_log_col", "dt_bias_col", "norm_g")
    nblk = SEQ // GD_TB
    scratch = [pltpu.VMEM((8, 768), f32), pltpu.VMEM((N_HEADS // 2, 2 * HEAD_DIM, 2 * HEAD_DIM), f32)]
    scratch += [pltpu.VMEM((GD_TB, GROUP_W), f32) for _ in range(6)]
    scratch += [pltpu.VMEM((2, GD_TB, GROUP_W), f32)]
    scratch += [pltpu.VMEM((2, GD_TB, GROUP_W), bf16) for _ in range(4)]
    scratch += [pltpu.VMEM((2, GD_TB, GROUP_W), f32), pltpu.VMEM((2, GD_TB // GD_C, GROUP_W), f32)]
    return pl.pallas_call(
        _gdn_kernel,
        grid=(nblk + 1,),
        in_specs=[pl.BlockSpec((GD_TB, NB), lambda i: (jnp.minimum(i, nblk - 1), 0))]
        + [_layer_spec(p[k], l) for k in names],
        out_specs=pl.BlockSpec((GD_TB, GROUP_W), lambda i: (jnp.maximum(i - 1, 0), 0)),
        out_shape=jax.ShapeDtypeStruct((SEQ, GROUP_W), bf16),
        scratch_shapes=scratch,
        compiler_params=pltpu.CompilerParams(dimension_semantics=("arbitrary",),
                                             vmem_limit_bytes=VMEM_LIMIT),
        name="gdn",
    )(c_b, *[p[k] for k in names])


GL_TB = 256
GL_C = 16
GL_S = 8


def _gla_kernel(c_ref, gup_ref, gb_ref, ng_ref, y_ref, st_ref, sx_s, o_s):
    @pl.when(pl.program_id(0) == 0)
    def _():
        st_ref[...] = jnp.zeros_like(st_ref)

    tb, c, s = GL_TB, GL_C, GL_S
    nchunk, nsub = tb // c, tb // s
    q = c_ref[:, 0:128] * (GLA_HEAD_K ** -0.5)
    k = c_ref[:, 128:256]
    v = c_ref[:, 256:512]
    pre = _dot(c_ref[:, 768:896], gup_ref[...]) + gb_ref[...]
    la = -_softplus(-pre) * (1.0 / 16.0)
    tj, ti = _iota((tb, tb), 0), _iota((tb, tb), 1)
    b = _dot_sel_lhs(((tj // c == ti // c) & (ti <= tj)).astype(bf16), la)
    qi = q * jnp.exp(b)

    ind_e = (_iota((GLA_KEY, GROUP_W), 0) // GLA_HEAD_K == _iota((GLA_KEY, GROUP_W), 1) // HEAD_DIM).astype(bf16)
    bd_mask = (_iota((GROUP_W, GLA_KEY), 0) // HEAD_DIM == _iota((GROUP_W, GLA_KEY), 1) // GLA_HEAD_K).astype(f32)

    b3, q3, k3 = (t.reshape(nsub, s, GLA_KEY) for t in (b, q, k))
    ri = _iota((nsub, s, GLA_KEY), 1)
    terms = []
    for j in range(s):
        e = jnp.exp(jnp.where(ri >= j, b3 - b3[:, j:j + 1, :], NEG))
        terms.append((q3 * (k3[:, j:j + 1, :] * e)).reshape(tb, GLA_KEY).astype(bf16))
    sx_s[...] = jnp.dot(jnp.concatenate(terms, axis=0), ind_e, preferred_element_type=f32)
    v3 = v.reshape(nsub, s, GROUP_W)
    o3 = sx_s[0:tb, :].reshape(nsub, s, GROUP_W) * v3[:, 0:1, :]
    for j in range(1, s):
        o3 = o3 + sx_s[j * tb:(j + 1) * tb, :].reshape(nsub, s, GROUP_W) * v3[:, j:j + 1, :]

    b4, q4, k4 = (t.reshape(nchunk, 2, s, GLA_KEY) for t in (b, q, k))
    bref = b4[:, 0, s - 1:s, :]
    qd = (q4[:, 1] * jnp.exp(b4[:, 1] - bref)).reshape(nchunk * s, GLA_KEY)
    kd = (k4[:, 0] * jnp.exp(bref - b4[:, 0])).reshape(nchunk * s, GLA_KEY)
    v0 = v.reshape(nchunk, 2, s, GROUP_W)[:, 0].reshape(nchunk * s, GROUP_W)
    head_k = _iota((nchunk * s, GLA_KEY), 1) // GLA_HEAD_K
    head_v = _iota((nchunk * s, GROUP_W), 1) // HEAD_DIM
    ks = jnp.concatenate([jnp.where(head_k == h, kd, 0.0) for h in range(N_HEADS)], axis=0)
    vs = jnp.concatenate([jnp.where(head_v == h, v0, 0.0) for h in range(N_HEADS)], axis=0)
    sc = _dot(qd, ks, NT)
    sr, scol = _iota(sc.shape, 0), _iota(sc.shape, 1)
    sc = jnp.where(sr // s == (scol % (nchunk * s)) // s, sc, 0.0)
    o_off = _dot(sc, vs).reshape(nchunk, 1, s, GROUP_W)
    o4 = o3.reshape(nchunk, 2, s, GROUP_W)
    o_intra = jnp.concatenate([o4[:, 0:1], o4[:, 1:2] + o_off], axis=1).reshape(tb, GROUP_W)

    blasts = [b[(n + 1) * c - 1:(n + 1) * c, :] for n in range(nchunk)]
    upds = [_dot(v[n * c:(n + 1) * c], k[n * c:(n + 1) * c] * jnp.exp(blasts[n] - b[n * c:(n + 1) * c]), TN)
            * bd_mask for n in range(nchunk)]
    st = st_ref[...]
    for n in range(nchunk):
        rows = slice(n * c, (n + 1) * c)
        o_s[rows, :] = o_intra[rows] + _dot(qi[rows], st, NT)
        st = st * jnp.exp(blasts[n]) + upds[n]
    st_ref[...] = st

    o = o_s[...]
    ms = _segsum(o * o, _group_ones(GROUP_W, HEAD_DIM)) * (1.0 / HEAD_DIM)
    gate = c_ref[:, 512:768]
    y_ref[...] = (o * lax.rsqrt(ms + EPS) * ng_ref[...] * (gate * _sigmoid(gate))).astype(bf16)


def _gla(c_c, p, l):
    names = ("gk_up", "gk_bias", "norm_g")
    return pl.pallas_call(
        _gla_kernel,
        grid=(SEQ // GL_TB,),
        in_specs=[pl.BlockSpec((GL_TB, NC), lambda i: (i, 0))] + [_layer_spec(p[k], l) for k in names],
        out_specs=pl.BlockSpec((GL_TB, GROUP_W), lambda i: (i, 0)),
        out_shape=jax.ShapeDtypeStruct((SEQ, GROUP_W), bf16),
        scratch_shapes=[pltpu.VMEM((GROUP_W, GLA_KEY), f32),
                        pltpu.VMEM((GL_TB * GL_S, GROUP_W), f32), pltpu.VMEM((GL_TB, GROUP_W), f32)],
        compiler_params=pltpu.CompilerParams(dimension_semantics=("arbitrary",),
                                             vmem_limit_bytes=VMEM_LIMIT),
        name="gla",
    )(c_c, *[p[k] for k in names])


FF_TM = 512
FF_TF = 1024


def _outffn_kernel(*refs, final, layer):
    if final:
        (x_ref, ya_ref, yb_ref, yc_ref, yd_ref, wo_hbm, g_ref, wu_hbm, wd_hbm, gf_ref, o_ref,
         wo_ref, wu_ref, wd_ref, sem) = refs
    else:
        (x_ref, ya_ref, yb_ref, yc_ref, yd_ref, wo_hbm, g_ref, wu_hbm, wd_hbm, o_ref,
         wo_ref, wu_ref, wd_ref, sem) = refs

    @pl.when(pl.program_id(0) == 0)
    def _():
        copies = [pltpu.make_async_copy(src.at[layer], dst, sem.at[i])
                  for i, (src, dst) in enumerate(((wo_hbm, wo_ref), (wu_hbm, wu_ref), (wd_hbm, wd_ref)))]
        for cp in copies:
            cp.start()
        for cp in copies:
            cp.wait()

    y = jnp.concatenate([ya_ref[...], yb_ref[...], yc_ref[...], yd_ref[...]], axis=1)
    x1 = x_ref[...] + jnp.dot(y, wo_ref[...], preferred_element_type=f32)
    ms = jnp.mean(x1 * x1, axis=-1, keepdims=True)
    h = (x1 * lax.rsqrt(ms + EPS) * g_ref[...]).astype(bf16)
    x2 = x1
    for kf in range(D_FF // FF_TF):
        cols = slice(kf * FF_TF, (kf + 1) * FF_TF)
        hid = jnp.maximum(jnp.dot(h, wu_ref[:, cols], preferred_element_type=f32), 0.0)
        x2 = x2 + jnp.dot((hid * hid).astype(bf16), wd_ref[cols, :], preferred_element_type=f32)
    if final:
        ms = jnp.mean(x2 * x2, axis=-1, keepdims=True)
        x2 = x2 * lax.rsqrt(ms + EPS) * gf_ref[...]
    o_ref[...] = x2


def _out_ffn(x, ys, p, l, final):
    hbm = pl.BlockSpec(memory_space=pl.ANY)
    yspec = pl.BlockSpec((FF_TM, GROUP_W), lambda i: (i, 0))
    in_specs = [pl.BlockSpec((FF_TM, D_MODEL), lambda i: (i, 0)), yspec, yspec, yspec, yspec,
                hbm, _layer_spec(p["g"], l), hbm, hbm]
    args = [x, *ys, p["w_out"], p["g"], p["w_up"], p["w_down"]]
    if final:
        in_specs.append(pl.BlockSpec((1, D_MODEL), lambda i: (0, 0)))
        args.append(p["g_final"])
    return pl.pallas_call(
        functools.partial(_outffn_kernel, final=final, layer=l),
        grid=(SEQ // FF_TM,),
        in_specs=in_specs,
        out_specs=pl.BlockSpec((FF_TM, D_MODEL), lambda i: (i, 0)),
        out_shape=jax.ShapeDtypeStruct((SEQ, D_MODEL), f32),
        scratch_shapes=[pltpu.VMEM((D_MODEL, D_MODEL), bf16), pltpu.VMEM((D_MODEL, D_FF), bf16),
                        pltpu.VMEM((D_FF, D_MODEL), bf16), pltpu.SemaphoreType.DMA((3,))],
        compiler_params=pltpu.CompilerParams(dimension_semantics=("arbitrary",),
                                             vmem_limit_bytes=VMEM_LIMIT),
        name="out_ffn",
    )(*args)


def kernel(x, w_in, w_out, norm_mix_g, norm_ffn_g, norm_final_g, rwkv_mu, rwkv_w0, rwkv_w_up, rwkv_a0, rwkv_a_up, rwkv_g_up, rwkv_k_k, rwkv_k_a, rwkv_r_k, rwkv_lnx_w, rwkv_lnx_b, rwkv_v0, rwkv_vres_down, rwkv_vres_up, gdn_conv_w, gdn_a_log, gdn_dt_bias, gdn_norm_g, gla_gk_up, gla_gk_bias, gla_norm_g, sgu_ln_g, sgu_ln_b, sgu_w_s, sgu_b_s, ffn_w_up, ffn_w_down):
    depth = w_in.shape[0]
    row = lambda a: a.reshape(depth, 1, -1)
    per_head = lambda a: jnp.repeat(a, HEAD_DIM, axis=-1).reshape(depth, 1, -1)
    pad_cols = lambda w, n: jnp.pad(w, ((0, 0), (0, 0), (0, n - w.shape[2])))
    pad_rows = lambda w, top, total: jnp.pad(w, ((0, 0), (top, total - top - w.shape[1]), (0, 0)))
    vres_down = jnp.pad(rwkv_vres_down, ((1, 0), (0, 0), (0, 0)))
    w_comb = jnp.concatenate(
        [pad_cols(jnp.concatenate([w_in[:, :, 0:1024], vres_down], axis=2), NA),
         pad_cols(w_in[:, :, 1024:2056], NB), pad_cols(w_in[:, :, 2056:2840], NC),
         w_in[:, :, 2840:3352]], axis=2).astype(bf16)
    p_in = dict(g=row(norm_mix_g), w=w_comb, ln_g=row(sgu_ln_g), ln_b=row(sgu_ln_b),
                w_cat=sgu_w_s.transpose(0, 2, 1, 3).reshape(depth, SG_C, 4 * SG_C),
                bias_tile=jnp.repeat(sgu_b_s.transpose(0, 2, 1), HEAD_DIM, axis=2))
    p_rwkv = dict(mu=row(rwkv_mu), w0=row(rwkv_w0), w_up=pad_rows(rwkv_w_up, 0, 128),
                  a0=row(rwkv_a0), a_up=pad_rows(rwkv_a_up, 64, 128), g_up=rwkv_g_up,
                  k_k=row(rwkv_k_k), k_a=row(rwkv_k_a), r_k=row(rwkv_r_k),
                  lnx_w=row(rwkv_lnx_w), lnx_b=row(rwkv_lnx_b),
                  v0=rwkv_v0.reshape(depth - 1, 1, -1), vres_up=pad_rows(rwkv_vres_up, 0, 128))
    p_gdn = dict(conv_w=gdn_conv_w, a_log=per_head(gdn_a_log), dt_bias=per_head(gdn_dt_bias),
                 a_log_col=jnp.pad(gdn_a_log, ((0, 0), (N_HEADS, 0))).reshape(depth, 8, 1),
                 dt_bias_col=jnp.pad(gdn_dt_bias, ((0, 0), (N_HEADS, 0))).reshape(depth, 8, 1),
                 norm_g=row(jnp.tile(gdn_norm_g, (1, N_HEADS))))
    p_gla = dict(gk_up=pad_rows(gla_gk_up, 0, 128), gk_bias=row(gla_gk_bias),
                 norm_g=row(jnp.tile(gla_norm_g, (1, N_HEADS))))
    p_ffn = dict(w_out=w_out.astype(bf16), g=row(norm_ffn_g), w_up=ffn_w_up.astype(bf16),
                 w_down=ffn_w_down.astype(bf16), g_final=norm_final_g.reshape(1, -1))

    xx = x[0]
    v_first = None
    for l in range(depth):
        c_a, c_b, c_c, y_d = _in_proj(xx, p_in, l)
        y_a, v_first = _rwkv(c_a, v_first, p_rwkv, l)
        y_b = _gdn(c_b, p_gdn, l)
        y_c = _gla(c_c, p_gla, l)
        xx = _out_ffn(xx, (y_a, y_b, y_c, y_d), p_ffn, l, final=(l == depth - 1))
    return xx[None]
```

```python
import functools

import jax
import jax.numpy as jnp
from jax import lax
from jax.experimental import pallas as pl
from jax.experimental.pallas import tpu as pltpu

f32 = jnp.float32
bf16 = jnp.bfloat16

SEQ = 16384
D_MODEL = 1024
GROUP_W = 256
HEAD_DIM = 64
N_HEADS = 4
GLA_KEY = 128
GLA_HEAD_K = 32
D_FF = 4096
EPS = 1e-6
RWKV_GN_EPS = 64e-5
NEG = -1e30

NA, NB, NC, ND = 1152, 1152, 896, 512
N_PAD = NA + NB + NC + ND

VMEM_LIMIT = 56 * 1024 * 1024

NN = (((1,), (0,)), ((), ()))
NT = (((1,), (1,)), ((), ()))
TN = (((0,), (0,)), ((), ()))


def _dot(a, b, dims=NN):
    return lax.dot_general(a.astype(bf16), b.astype(bf16), dims, preferred_element_type=f32)


def _iota(shape, axis):
    return lax.broadcasted_iota(jnp.int32, shape, axis)


def _layer_spec(a, l):
    return pl.BlockSpec((None,) + a.shape[1:], lambda *_: (l,) + (0,) * (a.ndim - 1))


def _segsum(x, ones_bd):
    return jnp.dot(x.astype(bf16), ones_bd, preferred_element_type=f32)


def _split3(x):
    hi = x.astype(bf16)
    r1 = x - hi.astype(f32)
    mid = r1.astype(bf16)
    lo = (r1 - mid.astype(f32)).astype(bf16)
    return hi, mid, lo


def _dot_sel_lhs(sel, x, dims=NN):
    return sum(lax.dot_general(sel, t, dims, preferred_element_type=f32) for t in _split3(x))


def _dot_sel_rhs(x, sel, dims=NN):
    return sum(lax.dot_general(t, sel, dims, preferred_element_type=f32) for t in _split3(x))


def _group_ones(n, width):
    return (_iota((n, n), 0) // width == _iota((n, n), 1) // width).astype(bf16)


def _sigmoid(x):
    return 1.0 / (1.0 + jnp.exp(-x))


def _softplus(x):
    return jnp.maximum(x, 0.0) + jnp.log1p(jnp.exp(-jnp.abs(x)))


def _shift_rows(x, carry, s):
    xs = pltpu.roll(x, s, 0)
    fix = pltpu.roll(carry, s, 0)
    first = jnp.where(_iota(carry.shape, 0) < s, fix, xs[0:8])
    return jnp.concatenate([first, xs[8:]], axis=0)


def _bd(xp):
    xb = xp.astype(bf16)
    left = _iota(xb.shape, 1) < HEAD_DIM
    zero = jnp.zeros_like(xb)
    return jnp.concatenate([jnp.where(left, xb, zero), jnp.where(left, zero, xb)], axis=0)


def _pair_masks(c):
    ri, cj = _iota((c, 2 * c), 0), _iota((c, 2 * c), 1) & (c - 1)
    eye = (ri == cj).astype(f32)
    m16 = (ri // 16 == cj // 16).astype(f32)
    mo1 = ((ri // 32 == cj // 32) & (ri // 16 == cj // 16 + 1)).astype(f32)
    mo2 = ((ri // 32 == 1) & (cj // 32 == 0)).astype(f32)
    return dict(eye=eye, m16=m16, mo1=mo1, mo2=mo2, strict=ri > cj, incl=ri >= cj)


def _tri_inv(lms, masks, tick=lambda: None):
    c = RW_C

    def each(fn, *lists):
        out = []
        for args in zip(*lists):
            out.append(fn(*args))
            tick()
        return out

    ps = [-(lm * masks["m16"]) for lm in lms]
    ts = [masks["eye"] + p for p in ps]
    ps = each(lambda p: _dot(p, _bd(p)), ps)
    for _ in range(2):
        outs = each(lambda t, p: _dot(jnp.concatenate([t, p], axis=0), _bd(p)), ts, ps)
        ts = [t + o[0:c] for t, o in zip(ts, outs)]
        ps = [o[c:] for o in outs]
    ts = each(lambda t, p: t + _dot(t, _bd(p)), ts, ps)
    for mo in (masks["mo1"], masks["mo2"]):
        xs = each(lambda lm, t: _dot(lm * mo, _bd(t)), lms, ts)
        ts = each(lambda t, x: t - _dot(t, _bd(x)), ts, xs)
    return ts


IN_TM = 1024
SG_C = 128


def _sgu_mix(x, lg_ref, lb_ref, w_ref, bias_ref):
    gx = 0.5 * x * (1.0 + jnp.tanh(0.7978845608028654 * (x + 0.044715 * x * x * x)))
    u = gx[:, 0:256]
    v = gx[:, 256:512]
    mu = jnp.mean(v, axis=-1, keepdims=True)
    vc = v - mu
    var = jnp.mean(vc * vc, axis=-1, keepdims=True)
    v = vc * lax.rsqrt(var + 1e-5) * lg_ref[...] + lb_ref[...]
    wr, wc = _iota((SG_C, 4 * SG_C), 0), _iota((SG_C, 4 * SG_C), 1)
    w = jnp.where(wc % SG_C <= wr, w_ref[...], 0.0).astype(bf16)
    lane_g = _iota((SG_C, GROUP_W), 1) // HEAD_DIM
    outs = []
    for n in range(x.shape[0] // SG_C):
        vn = v[n * SG_C:(n + 1) * SG_C, :]
        vst = jnp.concatenate([jnp.where(lane_g == g, vn, 0.0) for g in range(4)], axis=0)
        outs.append(jnp.dot(w, vst.astype(bf16), preferred_element_type=f32) + bias_ref[...])
    return (u * jnp.concatenate(outs, axis=0)).astype(bf16)


def _inproj_kernel(x_ref, g_ref, w_hbm, lg_ref, lb_ref, ws_ref, bias_ref, oa_ref, ob_ref, oc_ref, yd_ref,
                   w_ref, sem, *, layer):
    @pl.when(pl.program_id(0) == 0)
    def _():
        cp = pltpu.make_async_copy(w_hbm.at[layer], w_ref, sem.at[0])
        cp.start()
        cp.wait()

    x = x_ref[...]
    ms = jnp.mean(x * x, axis=-1, keepdims=True)
    h = (x * lax.rsqrt(ms + EPS) * g_ref[...]).astype(bf16)
    c_d = jnp.dot(h, w_ref[:, NA + NB + NC:], preferred_element_type=f32)
    yd_ref[...] = _sgu_mix(c_d, lg_ref, lb_ref, ws_ref, bias_ref)
    off = 0
    for o_ref, n in ((oa_ref, NA), (ob_ref, NB), (oc_ref, NC)):
        o_ref[...] = jnp.dot(h, w_ref[:, off:off + n], preferred_element_type=f32)
        off += n


def _in_proj(x, p, l):
    tm = IN_TM
    names = ("g", "w", "ln_g", "ln_b", "w_cat", "bias_tile")
    specs = [pl.BlockSpec(memory_space=pl.ANY) if k == "w" else _layer_spec(p[k], l) for k in names]
    return pl.pallas_call(
        functools.partial(_inproj_kernel, layer=l),
        grid=(SEQ // tm,),
        in_specs=[pl.BlockSpec((tm, D_MODEL), lambda i: (i, 0))] + specs,
        out_specs=[pl.BlockSpec((tm, n), lambda i: (i, 0)) for n in (NA, NB, NC, GROUP_W)],
        out_shape=[jax.ShapeDtypeStruct((SEQ, n), f32) for n in (NA, NB, NC)]
        + [jax.ShapeDtypeStruct((SEQ, GROUP_W), bf16)],
        scratch_shapes=[pltpu.VMEM((D_MODEL, N_PAD), bf16), pltpu.SemaphoreType.DMA((1,))],
        compiler_params=pltpu.CompilerParams(dimension_semantics=("arbitrary",),
                                             vmem_limit_bytes=VMEM_LIMIT),
        name="in_proj",
    )(x, *[p[k] for k in names])


RW_TB = 512
RW_C = 64
RW_PRO = 2
RW_TICKS = 13


def _rwkv_kernel(*refs, has_vres):
    if has_vres:
        (c_ref, vf_ref, mu_ref, w0_ref, wup_ref, a0_ref, aup_ref, gup_ref, kk_ref, ka_ref, rk_ref,
         lw_ref, lb_ref, v0_ref, vup_ref, y_ref,
         carry_ref, s_ref, r_s, k_s, v_s, lw_s, al_s, be_s, y_s,
         ta_p, rt_p, vb_p, bw_p, kw_p, arb_p, tv_p, av_p, bonus_p, g_p, dl_p) = refs
    else:
        (c_ref, mu_ref, w0_ref, wup_ref, a0_ref, aup_ref, gup_ref, kk_ref, ka_ref, rk_ref,
         lw_ref, lb_ref, y_ref, vf_out_ref,
         carry_ref, s_ref, r_s, k_s, v_s, lw_s, al_s, be_s, y_s,
         ta_p, rt_p, vb_p, bw_p, kw_p, arb_p, tv_p, av_p, bonus_p, g_p, dl_p) = refs
    step = pl.program_id(0)

    @pl.when(step == 0)
    def _():
        for ref in (carry_ref, s_ref, ta_p, rt_p, vb_p, bw_p, kw_p, arb_p, tv_p, av_p, bonus_p, g_p, dl_p):
            ref[...] = jnp.zeros_like(ref)

    wr = step % 2
    rd = 1 - wr
    c = RW_C
    nchunk = RW_TB // c
    npair = N_HEADS // 2
    hsl = lambda h: slice(2 * h * HEAD_DIM, 2 * (h + 1) * HEAD_DIM)
    bd_mask = (_iota((2 * c, 2 * c), 0) // c == _iota((2 * c, 2 * c), 1) // c).astype(f32)
    state = [s_ref[h] for h in range(npair)]

    def recurrence():
        for n in range(nchunk):
            rows = slice(n * c, (n + 1) * c)
            sas = [lax.dot_general(jnp.concatenate([ta_p[rd, rows, hsl(h)], rt_p[rd, rows, hsl(h)]], axis=0),
                                   state[h].astype(bf16), NT, preferred_element_type=f32)
                   for h in range(npair)]
            yield
            us = [sas[h][0:c] + tv_p[rd, rows, hsl(h)] for h in range(npair)]
            upd = [_dot(jnp.concatenate([us[h].astype(bf16), vb_p[rd, rows, hsl(h)]], axis=0),
                        jnp.concatenate([bw_p[rd, rows, hsl(h)], kw_p[rd, rows, hsl(h)]], axis=0), TN) * bd_mask
                   for h in range(npair)]
            yield
            dl = dl_p[rd, n:n + 1, :]
            for h in range(npair):
                state[h] = state[h] * dl[:, hsl(h)] + upd[h]
            ys = [sas[h][c:] + _dot(arb_p[rd, rows, hsl(h)], _bd(us[h])) + av_p[rd, rows, hsl(h)]
                  for h in range(npair)]
            y_s[rows, :] = jnp.concatenate(ys, axis=1)
            yield

    chain = recurrence()
    calls = [0]

    def advance(pieces=1):
        for _ in range(pieces):
            next(chain, None)

    def tick():
        calls[0] += 1
        if calls[0] % RW_TICKS == 0:
            advance()

    ones_bd = _group_ones(GROUP_W, HEAD_DIM)
    x = c_ref[:, 0:1024]
    x_prev = _shift_rows(x, carry_ref[...], 1)
    carry_ref[...] = x[RW_TB - 8:RW_TB]
    xs = x + (x_prev - x) * mu_ref[...]
    advance(RW_PRO)
    r = xs[:, 0:256]
    k = xs[:, 256:512]
    v = xs[:, 512:768]
    lora = xs[:, 768:896]
    w_pre = w0_ref[...] + _dot(jnp.tanh(lora), wup_ref[...])
    lw = -jnp.exp(-_softplus(-w_pre) - 0.5)
    advance(RW_PRO)
    a = _sigmoid(a0_ref[...] + _dot(lora, aup_ref[...]))
    g_p[wr] = _dot(_sigmoid(xs[:, 896:1024]), gup_ref[...])
    advance(RW_PRO)
    if has_vres:
        mix = _sigmoid(v0_ref[...] + _dot(c_ref[:, 1024:1152], vup_ref[...]))
        v = v + (vf_ref[...] - v) * mix
    else:
        @pl.when(step < SEQ // RW_TB)
        def _():
            vf_out_ref[...] = v
    kk = k * kk_ref[...]
    kk = kk * lax.rsqrt(_segsum(kk * kk, ones_bd) + 1e-24)
    advance(RW_PRO)
    k = k * (1.0 + (a - 1.0) * ka_ref[...])
    bonus_p[wr] = _segsum(r * k * rk_ref[...], ones_bd) * v
    advance(RW_PRO)
    r_s[...] = r
    k_s[...] = k
    v_s[...] = v
    lw_s[...] = lw
    al_s[...] = -kk
    be_s[...] = kk * a

    tril_incl = (_iota((c, c), 0) >= _iota((c, c), 1)).astype(bf16)
    masks = _pair_masks(c)
    low_strict, low_incl = masks["strict"], masks["incl"]

    items = [(n, h) for n in range(nchunk) for h in range(npair)]
    pre = []
    for n in range(nchunk):
        rows = slice(n * c, (n + 1) * c)
        lwc = lw_s[rows, :]
        lc = _dot_sel_lhs(tril_incl, lwc)
        llast = lc[c - 1:c, :]
        e_out = jnp.exp(-lc)
        e_rest = jnp.exp(llast - lc)
        kc, bec = k_s[rows, :], be_s[rows, :]
        rt = r_s[rows, :] * jnp.exp(lc)
        pre.append(dict(rt=rt, at=al_s[rows, :] * jnp.exp(lc - lwc), bt=bec * e_out, kt=kc * e_out,
                        v=v_s[rows, :]))
        rt_p[wr, rows, :] = rt.astype(bf16)
        vb_p[wr, rows, :] = v_s[rows, :].astype(bf16)
        bw_p[wr, rows, :] = (bec * e_rest).astype(bf16)
        kw_p[wr, rows, :] = (kc * e_rest).astype(bf16)
        dl_p[wr, n:n + 1, :] = jnp.exp(llast)
        tick()
    ms = []
    for n, h in items:
        ms.append(_dot(jnp.concatenate([pre[n]["at"][:, hsl(h)], pre[n]["rt"][:, hsl(h)]], axis=0),
                       jnp.concatenate([_bd(pre[n]["bt"][:, hsl(h)]), _bd(pre[n]["kt"][:, hsl(h)])], axis=0),
                       NT))
        arb_p[wr, n * c:(n + 1) * c, hsl(h)] = jnp.where(low_incl, ms[-1][c:, 0:2 * c], 0.0).astype(bf16)
        tick()
    tms = _tri_inv([jnp.where(low_strict, -m[0:c, 0:2 * c], 0.0) for m in ms], masks, tick)
    avs = []
    for m, (n, h) in zip(ms, items):
        avs.append(_dot(jnp.concatenate([jnp.where(low_strict, m[0:c, 2 * c:], 0.0),
                                         jnp.where(low_incl, m[c:, 2 * c:], 0.0)], axis=0),
                        _bd(pre[n]["v"][:, hsl(h)])))
        av_p[wr, n * c:(n + 1) * c, hsl(h)] = avs[-1][c:]
        tick()
    for tm, av, (n, h) in zip(tms, avs, items):
        tt = _dot(tm, jnp.concatenate([_bd(pre[n]["at"][:, hsl(h)]), _bd(av[0:c])], axis=1))
        ta_p[wr, n * c:(n + 1) * c, hsl(h)] = tt[:, 0:2 * c].astype(bf16)
        tv_p[wr, n * c:(n + 1) * c, hsl(h)] = tt[:, 2 * c:]
        tick()
    for _ in chain:
        pass
    for h in range(npair):
        s_ref[h] = state[h]

    y = y_s[...]
    inv_d = 1.0 / HEAD_DIM
    mean = _segsum(y, ones_bd) * inv_d
    yc = y - mean
    var = _segsum(yc * yc, ones_bd) * inv_d
    y = yc * lax.rsqrt(var + RWKV_GN_EPS) * lw_ref[...] + lb_ref[...]
    y_ref[...] = ((y + bonus_p[rd]) * g_p[rd]).astype(bf16)


def _rwkv(c_a, v_first, p, l):
    has_vres = v_first is not None
    nblk = SEQ // RW_TB
    blk = lambda n: pl.BlockSpec((RW_TB, n), lambda i: (jnp.minimum(i, nblk - 1), 0))
    blk_prev = lambda n: pl.BlockSpec((RW_TB, n), lambda i: (jnp.maximum(i - 1, 0), 0))
    in_specs = [blk(NA)]
    args = [c_a]
    if has_vres:
        in_specs.append(blk(GROUP_W))
        args.append(v_first)
    names = ("mu", "w0", "w_up", "a0", "a_up", "g_up", "k_k", "k_a", "r_k", "lnx_w", "lnx_b")
    in_specs += [_layer_spec(p[k], l) for k in names]
    args += [p[k] for k in names]
    if has_vres:
        in_specs += [_layer_spec(p["v0"], l - 1), _layer_spec(p["vres_up"], l - 1)]
        args += [p["v0"], p["vres_up"]]
        out_specs = blk_prev(GROUP_W)
        out_shape = jax.ShapeDtypeStruct((SEQ, GROUP_W), bf16)
    else:
        out_specs = [blk_prev(GROUP_W), blk(GROUP_W)]
        out_shape = [jax.ShapeDtypeStruct((SEQ, GROUP_W), bf16),
                     jax.ShapeDtypeStruct((SEQ, GROUP_W), f32)]
    scratch = [pltpu.VMEM((8, 1024), f32), pltpu.VMEM((N_HEADS // 2, 2 * HEAD_DIM, 2 * HEAD_DIM), f32)]
    scratch += [pltpu.VMEM((RW_TB, GROUP_W), f32) for _ in range(7)]
    scratch += [pltpu.VMEM((2, RW_TB, GROUP_W), bf16) for _ in range(6)]
    scratch += [pltpu.VMEM((2, RW_TB, GROUP_W), f32) for _ in range(4)]
    scratch += [pltpu.VMEM((2, RW_TB // RW_C, GROUP_W), f32)]
    out = pl.pallas_call(
        functools.partial(_rwkv_kernel, has_vres=has_vres),
        grid=(nblk + 1,),
        in_specs=in_specs,
        out_specs=out_specs,
        out_shape=out_shape,
        scratch_shapes=scratch,
        compiler_params=pltpu.CompilerParams(dimension_semantics=("arbitrary",),
                                             vmem_limit_bytes=VMEM_LIMIT),
        name="rwkv7",
    )(*args)
    if has_vres:
        return out, v_first
    return out[0], out[1]


GD_TB = 512
GD_C = 64
GD_PRO = 2
GD_TICKS = 9


def _gdn_kernel(c_ref, cw_ref, alog_ref, dtb_ref, alogc_ref, dtbc_ref, ng_ref, y_ref,
                carry_ref, s_ref, q_s, k_s, v_s, be_s, g_s, o_s,
                u_p, w_p, qe_p, kd_p, qk_p, gz_p, egl_p):
    step = pl.program_id(0)

    @pl.when(step == 0)
    def _():
        for ref in (carry_ref, s_ref, u_p, w_p, qe_p, kd_p, qk_p, gz_p, egl_p):
            ref[...] = jnp.zeros_like(ref)

    wr = step % 2
    rd = 1 - wr
    c = GD_C
    nchunk = GD_TB // c
    npair = N_HEADS // 2
    hsl = lambda h: slice(2 * h * HEAD_DIM, 2 * (h + 1) * HEAD_DIM)
    bd_mask = (_iota((2 * c, 2 * c), 0) // c == _iota((2 * c, 2 * c), 1) // c).astype(f32)
    ones_bd = _group_ones(GROUP_W, HEAD_DIM)
    state = [s_ref[h] for h in range(npair)]

    def recurrence():
        for n in range(nchunk):
            rows = slice(n * c, (n + 1) * c)
            wss = [jnp.dot(jnp.concatenate([w_p[rd, rows, hsl(h)], qe_p[rd, rows, hsl(h)]], axis=0),
                           state[h].astype(bf16), preferred_element_type=f32)
                   for h in range(npair)]
            yield
            vns = [u_p[rd, rows, hsl(h)] - wss[h][0:c] for h in range(npair)]
            upd = [_dot(kd_p[rd, rows, hsl(h)], vns[h], TN) * bd_mask for h in range(npair)]
            yield
            egl = egl_p[rd, n:n + 1, :]
            for h in range(npair):
                state[h] = state[h] * egl[:, hsl(h)] + upd[h]
            os_ = [wss[h][c:] + _dot(qk_p[rd, rows, hsl(h)], _bd(vns[h])) for h in range(npair)]
            o_s[rows, :] = jnp.concatenate(os_, axis=1)
            yield

    chain = recurrence()
    calls = [0]

    def tick():
        calls[0] += 1
        if calls[0] % GD_TICKS == 0:
            next(chain, None)

    raw = c_ref[:, 0:768]
    carry = carry_ref[...]
    conv = raw * cw_ref[3:4, :]
    for s in range(1, 4):
        conv = conv + _shift_rows(raw, carry, s) * cw_ref[3 - s:4 - s, :]
    carry_ref[...] = raw[GD_TB - 8:GD_TB]
    qkv = conv * _sigmoid(conv)
    for _ in range(GD_PRO):
        next(chain, None)
    q = qkv[:, 0:256]
    k = qkv[:, 256:512]
    q_s[...] = q * lax.rsqrt(_segsum(q * q, ones_bd) + 1e-6) * (HEAD_DIM ** -0.5)
    for _ in range(GD_PRO):
        next(chain, None)
    k_s[...] = k * lax.rsqrt(_segsum(k * k, ones_bd) + 1e-6)
    v_s[...] = qkv[:, 512:768]
    z = c_ref[:, 768:1024]
    gz_p[wr] = z * _sigmoid(z)
    for _ in range(GD_PRO):
        next(chain, None)
    small = c_ref[:, 1024:1152]
    er, ec = _iota((128, GROUP_W), 0), _iota((128, GROUP_W), 1)
    b_exp = _dot_sel_rhs(small, (er == ec // HEAD_DIM).astype(bf16))
    a_exp = _dot_sel_rhs(small, (er == ec // HEAD_DIM + N_HEADS).astype(bf16))
    be_s[...] = _sigmoid(b_exp)
    g_s[...] = -jnp.exp(alog_ref[...]) * _softplus(a_exp + dtb_ref[...])

    tril_incl = (_iota((c, c), 0) >= _iota((c, c), 1)).astype(bf16)
    masks = _pair_masks(c)
    low_strict, low_incl = masks["strict"], masks["incl"]
    sel8 = (_iota((8, 128), 0) == _iota((8, 128), 1)).astype(bf16)
    g_rows = -jnp.exp(alogc_ref[...]) * _softplus(_dot_sel_lhs(sel8, small, NT) + dtbc_ref[...])
    tj, ti = _iota((GD_TB, GD_TB), 0), _iota((GD_TB, GD_TB), 1)
    gc_rows = _dot_sel_rhs(g_rows, ((tj // c == ti // c) & (tj <= ti)).astype(bf16))

    items = [(n, h) for n in range(nchunk) for h in range(npair)]
    pre = []
    for n in range(nchunk):
        rows = slice(n * c, (n + 1) * c)
        gc = _dot_sel_lhs(tril_incl, g_s[rows, :])
        glast = gc[c - 1:c, :]
        egc = jnp.exp(gc)
        kc, bc = k_s[rows, :], be_s[rows, :]
        kb = kc * bc
        pre.append(dict(gc=gc, k=kc, kb=kb, q=q_s[rows, :], vb=v_s[rows, :] * bc, kbe=kb * egc))
        qe_p[wr, rows, :] = (q_s[rows, :] * egc).astype(bf16)
        kd_p[wr, rows, :] = (kc * jnp.exp(glast - gc)).astype(bf16)
        egl_p[wr, n:n + 1, :] = jnp.exp(glast)
        tick()

    def gc_row(n, h):
        return jnp.concatenate([gc_rows[N_HEADS + 2 * h + i:N_HEADS + 2 * h + i + 1, n * c:(n + 1) * c]
                                for i in range(2)], axis=1)

    dms, aqs = [], []
    for n, h in items:
        dms.append(jnp.exp(jnp.where(low_incl, pre[n]["gc"][:, hsl(h)] - gc_row(n, h), NEG)))
        aqs.append(_dot(jnp.concatenate([pre[n]["kb"][:, hsl(h)], pre[n]["q"][:, hsl(h)]], axis=0),
                        _bd(pre[n]["k"][:, hsl(h)]), NT))
        qk_p[wr, n * c:(n + 1) * c, hsl(h)] = (aqs[-1][c:] * dms[-1]).astype(bf16)
        tick()
    tms = _tri_inv([jnp.where(low_strict, aq[0:c] * dm, 0.0) for aq, dm in zip(aqs, dms)], masks, tick)
    for tm, (n, h) in zip(tms, items):
        uw = _dot(tm, jnp.concatenate([_bd(pre[n]["vb"][:, hsl(h)]), _bd(pre[n]["kbe"][:, hsl(h)])], axis=1))
        u_p[wr, n * c:(n + 1) * c, hsl(h)] = uw[:, 0:2 * c]
        w_p[wr, n * c:(n + 1) * c, hsl(h)] = uw[:, 2 * c:].astype(bf16)
        tick()
    for _ in chain:
        pass
    for h in range(npair):
        s_ref[h] = state[h]

    o = o_s[...]
    ms = _segsum(o * o, ones_bd) * (1.0 / HEAD_DIM)
    y_ref[...] = (o * lax.rsqrt(ms + EPS) * ng_ref[...] * gz_p[rd]).astype(bf16)


def _gdn(c_b, p, l):
    names = ("conv_w", "a_log", "dt_bias", "a_log_col", "dt_bias_col", "norm_g")
    nblk = SEQ // GD_TB
    scratch = [pltpu.VMEM((8, 768), f32), pltpu.VMEM((N_HEADS // 2, 2 * HEAD_DIM, 2 * HEAD_DIM), f32)]
    scratch += [pltpu.VMEM((GD_TB, GROUP_W), f32) for _ in range(6)]
    scratch += [pltpu.VMEM((2, GD_TB, GROUP_W), f32)]
    scratch += [pltpu.VMEM((2, GD_TB, GROUP_W), bf16) for _ in range(4)]
    scratch += [pltpu.VMEM((2, GD_TB, GROUP_W), f32), pltpu.VMEM((2, GD_TB // GD_C, GROUP_W), f32)]
    return pl.pallas_call(
        _gdn_kernel,
        grid=(nblk + 1,),
        in_specs=[pl.BlockSpec((GD_TB, NB), lambda i: (jnp.minimum(i, nblk - 1), 0))]
        + [_layer_spec(p[k], l) for k in names],
        out_specs=pl.BlockSpec((GD_TB, GROUP_W), lambda i: (jnp.maximum(i - 1, 0), 0)),
        out_shape=jax.ShapeDtypeStruct((SEQ, GROUP_W), bf16),
        scratch_shapes=scratch,
        compiler_params=pltpu.CompilerParams(dimension_semantics=("arbitrary",),
                                             vmem_limit_bytes=VMEM_LIMIT),
        name="gdn",
    )(c_b, *[p[k] for k in names])


GL_TB = 256
GL_C = 16
GL_S = 8


def _gla_kernel(c_ref, gup_ref, gb_ref, ng_ref, y_ref, st_ref, sx_s, o_s):
    @pl.when(pl.program_id(0) == 0)
    def _():
        st_ref[...] = jnp.zeros_like(st_ref)

    tb, c, s = GL_TB, GL_C, GL_S
    nchunk, nsub = tb // c, tb // s
    q = c_ref[:, 0:128] * (GLA_HEAD_K ** -0.5)
    k = c_ref[:, 128:256]
    v = c_ref[:, 256:512]
    pre = _dot(c_ref[:, 768:896], gup_ref[...]) + gb_ref[...]
    la = -_softplus(-pre) * (1.0 / 16.0)
    tj, ti = _iota((tb, tb), 0), _iota((tb, tb), 1)
    b = _dot_sel_lhs(((tj // c == ti // c) & (ti <= tj)).astype(bf16), la)
    qi = q * jnp.exp(b)

    ind_e = (_iota((GLA_KEY, GROUP_W), 0) // GLA_HEAD_K == _iota((GLA_KEY, GROUP_W), 1) // HEAD_DIM).astype(bf16)
    bd_mask = (_iota((GROUP_W, GLA_KEY), 0) // HEAD_DIM == _iota((GROUP_W, GLA_KEY), 1) // GLA_HEAD_K).astype(f32)

    b3, q3, k3 = (t.reshape(nsub, s, GLA_KEY) for t in (b, q, k))
    ri = _iota((nsub, s, GLA_KEY), 1)
    terms = []
    for j in range(s):
        e = jnp.exp(jnp.where(ri >= j, b3 - b3[:, j:j + 1, :], NEG))
        terms.append((q3 * (k3[:, j:j + 1, :] * e)).reshape(tb, GLA_KEY).astype(bf16))
    sx_s[...] = jnp.dot(jnp.concatenate(terms, axis=0), ind_e, preferred_element_type=f32)
    v3 = v.reshape(nsub, s, GROUP_W)
    o3 = sx_s[0:tb, :].reshape(nsub, s, GROUP_W) * v3[:, 0:1, :]
    for j in range(1, s):
        o3 = o3 + sx_s[j * tb:(j + 1) * tb, :].reshape(nsub, s, GROUP_W) * v3[:, j:j + 1, :]

    b4, q4, k4 = (t.reshape(nchunk, 2, s, GLA_KEY) for t in (b, q, k))
    bref = b4[:, 0, s - 1:s, :]
    qd = (q4[:, 1] * jnp.exp(b4[:, 1] - bref)).reshape(nchunk * s, GLA_KEY)
    kd = (k4[:, 0] * jnp.exp(bref - b4[:, 0])).reshape(nchunk * s, GLA_KEY)
    v0 = v.reshape(nchunk, 2, s, GROUP_W)[:, 0].reshape(nchunk * s, GROUP_W)
    head_k = _iota((nchunk * s, GLA_KEY), 1) // GLA_HEAD_K
    head_v = _iota((nchunk * s, GROUP_W), 1) // HEAD_DIM
    ks = jnp.concatenate([jnp.where(head_k == h, kd, 0.0) for h in range(N_HEADS)], axis=0)
    vs = jnp.concatenate([jnp.where(head_v == h, v0, 0.0) for h in range(N_HEADS)], axis=0)
    sc = _dot(qd, ks, NT)
    sr, scol = _iota(sc.shape, 0), _iota(sc.shape, 1)
    sc = jnp.where(sr // s == (scol % (nchunk * s)) // s, sc, 0.0)
    o_off = _dot(sc, vs).reshape(nchunk, 1, s, GROUP_W)
    o4 = o3.reshape(nchunk, 2, s, GROUP_W)
    o_intra = jnp.concatenate([o4[:, 0:1], o4[:, 1:2] + o_off], axis=1).reshape(tb, GROUP_W)

    blasts = [b[(n + 1) * c - 1:(n + 1) * c, :] for n in range(nchunk)]
    upds = [_dot(v[n * c:(n + 1) * c], k[n * c:(n + 1) * c] * jnp.exp(blasts[n] - b[n * c:(n + 1) * c]), TN)
            * bd_mask for n in range(nchunk)]
    st = st_ref[...]
    for n in range(nchunk):
        rows = slice(n * c, (n + 1) * c)
        o_s[rows, :] = o_intra[rows] + _dot(qi[rows], st, NT)
        st = st * jnp.exp(blasts[n]) + upds[n]
    st_ref[...] = st

    o = o_s[...]
    ms = _segsum(o * o, _group_ones(GROUP_W, HEAD_DIM)) * (1.0 / HEAD_DIM)
    gate = c_ref[:, 512:768]
    y_ref[...] = (o * lax.rsqrt(ms + EPS) * ng_ref[...] * (gate * _sigmoid(gate))).astype(bf16)


def _gla(c_c, p, l):
    names = ("gk_up", "gk_bias", "norm_g")
    return pl.pallas_call(
        _gla_kernel,
        grid=(SEQ // GL_TB,),
        in_specs=[pl.BlockSpec((GL_TB, NC), lambda i: (i, 0))] + [_layer_spec(p[k], l) for k in names],
        out_specs=pl.BlockSpec((GL_TB, GROUP_W), lambda i: (i, 0)),
        out_shape=jax.ShapeDtypeStruct((SEQ, GROUP_W), bf16),
        scratch_shapes=[pltpu.VMEM((GROUP_W, GLA_KEY), f32),
                        pltpu.VMEM((GL_TB * GL_S, GROUP_W), f32), pltpu.VMEM((GL_TB, GROUP_W), f32)],
        compiler_params=pltpu.CompilerParams(dimension_semantics=("arbitrary",),
                                             vmem_limit_bytes=VMEM_LIMIT),
        name="gla",
    )(c_c, *[p[k] for k in names])


FF_TM = 512
FF_TF = 1024


def _outffn_kernel(*refs, final, layer):
    if final:
        (x_ref, ya_ref, yb_ref, yc_ref, yd_ref, wo_hbm, g_ref, wu_hbm, wd_hbm, gf_ref, o_ref,
         wo_ref, wu_ref, wd_ref, sem) = refs
    else:
        (x_ref, ya_ref, yb_ref, yc_ref, yd_ref, wo_hbm, g_ref, wu_hbm, wd_hbm, o_ref,
         wo_ref, wu_ref, wd_ref, sem) = refs

    @pl.when(pl.program_id(0) == 0)
    def _():
        copies = [pltpu.make_async_copy(src.at[layer], dst, sem.at[i])
                  for i, (src, dst) in enumerate(((wo_hbm, wo_ref), (wu_hbm, wu_ref), (wd_hbm, wd_ref)))]
        for cp in copies:
            cp.start()
        for cp in copies:
            cp.wait()

    y = jnp.concatenate([ya_ref[...], yb_ref[...], yc_ref[...], yd_ref[...]], axis=1)
    x1 = x_ref[...] + jnp.dot(y, wo_ref[...], preferred_element_type=f32)
    ms = jnp.mean(x1 * x1, axis=-1, keepdims=True)
    h = (x1 * lax.rsqrt(ms + EPS) * g_ref[...]).astype(bf16)
    x2 = x1
    for kf in range(D_FF // FF_TF):
        cols = slice(kf * FF_TF, (kf + 1) * FF_TF)
        hid = jnp.maximum(jnp.dot(h, wu_ref[:, cols], preferred_element_type=f32), 0.0)
        x2 = x2 + jnp.dot((hid * hid).astype(bf16), wd_ref[cols, :], preferred_element_type=f32)
    if final:
        ms = jnp.mean(x2 * x2, axis=-1, keepdims=True)
        x2 = x2 * lax.rsqrt(ms + EPS) * gf_ref[...]
    o_ref[...] = x2


def _out_ffn(x, ys, p, l, final):
    hbm = pl.BlockSpec(memory_space=pl.ANY)
    yspec = pl.BlockSpec((FF_TM, GROUP_W), lambda i: (i, 0))
    in_specs = [pl.BlockSpec((FF_TM, D_MODEL), lambda i: (i, 0)), yspec, yspec, yspec, yspec,
                hbm, _layer_spec(p["g"], l), hbm, hbm]
    args = [x, *ys, p["w_out"], p["g"], p["w_up"], p["w_down"]]
    if final:
        in_specs.append(pl.BlockSpec((1, D_MODEL), lambda i: (0, 0)))
        args.append(p["g_final"])
    return pl.pallas_call(
        functools.partial(_outffn_kernel, final=final, layer=l),
        grid=(SEQ // FF_TM,),
        in_specs=in_specs,
        out_specs=pl.BlockSpec((FF_TM, D_MODEL), lambda i: (i, 0)),
        out_shape=jax.ShapeDtypeStruct((SEQ, D_MODEL), f32),
        scratch_shapes=[pltpu.VMEM((D_MODEL, D_MODEL), bf16), pltpu.VMEM((D_MODEL, D_FF), bf16),
                        pltpu.VMEM((D_FF, D_MODEL), bf16), pltpu.SemaphoreType.DMA((3,))],
        compiler_params=pltpu.CompilerParams(dimension_semantics=("arbitrary",),
                                             vmem_limit_bytes=VMEM_LIMIT),
        name="out_ffn",
    )(*args)


def kernel(x, w_in, w_out, norm_mix_g, norm_ffn_g, norm_final_g, rwkv_mu, rwkv_w0, rwkv_w_up, rwkv_a0, rwkv_a_up, rwkv_g_up, rwkv_k_k, rwkv_k_a, rwkv_r_k, rwkv_lnx_w, rwkv_lnx_b, rwkv_v0, rwkv_vres_down, rwkv_vres_up, gdn_conv_w, gdn_a_log, gdn_dt_bias, gdn_norm_g, gla_gk_up, gla_gk_bias, gla_norm_g, sgu_ln_g, sgu_ln_b, sgu_w_s, sgu_b_s, ffn_w_up, ffn_w_down):
    depth = w_in.shape[0]
    row = lambda a: a.reshape(depth, 1, -1)
    per_head = lambda a: jnp.repeat(a, HEAD_DIM, axis=-1).reshape(depth, 1, -1)
    pad_cols = lambda w, n: jnp.pad(w, ((0, 0), (0, 0), (0, n - w.shape[2])))
    pad_rows = lambda w, top, total: jnp.pad(w, ((0, 0), (top, total - top - w.shape[1]), (0, 0)))
    vres_down = jnp.pad(rwkv_vres_down, ((1, 0), (0, 0), (0, 0)))
    w_comb = jnp.concatenate(
        [pad_cols(jnp.concatenate([w_in[:, :, 0:1024], vres_down], axis=2), NA),
         pad_cols(w_in[:, :, 1024:2056], NB), pad_cols(w_in[:, :, 2056:2840], NC),
         w_in[:, :, 2840:3352]], axis=2).astype(bf16)
    p_in = dict(g=row(norm_mix_g), w=w_comb, ln_g=row(sgu_ln_g), ln_b=row(sgu_ln_b),
                w_cat=sgu_w_s.transpose(0, 2, 1, 3).reshape(depth, SG_C, 4 * SG_C),
                bias_tile=jnp.repeat(sgu_b_s.transpose(0, 2, 1), HEAD_DIM, axis=2))
    p_rwkv = dict(mu=row(rwkv_mu), w0=row(rwkv_w0), w_up=pad_rows(rwkv_w_up, 0, 128),
                  a0=row(rwkv_a0), a_up=pad_rows(rwkv_a_up, 64, 128), g_up=rwkv_g_up,
                  k_k=row(rwkv_k_k), k_a=row(rwkv_k_a), r_k=row(rwkv_r_k),
                  lnx_w=row(rwkv_lnx_w), lnx_b=row(rwkv_lnx_b),
                  v0=rwkv_v0.reshape(depth - 1, 1, -1), vres_up=pad_rows(rwkv_vres_up, 0, 128))
    p_gdn = dict(conv_w=gdn_conv_w, a_log=per_head(gdn_a_log), dt_bias=per_head(gdn_dt_bias),
                 a_log_col=jnp.pad(gdn_a_log, ((0, 0), (N_HEADS, 0))).reshape(depth, 8, 1),
                 dt_bias_col=jnp.pad(gdn_dt_bias, ((0, 0), (N_HEADS, 0))).reshape(depth, 8, 1),
                 norm_g=row(jnp.tile(gdn_norm_g, (1, N_HEADS))))
    p_gla = dict(gk_up=pad_rows(gla_gk_up, 0, 128), gk_bias=row(gla_gk_bias),
                 norm_g=row(jnp.tile(gla_norm_g, (1, N_HEADS))))
    p_ffn = dict(w_out=w_out.astype(bf16), g=row(norm_ffn_g), w_up=ffn_w_up.astype(bf16),
                 w_down=ffn_w_down.astype(bf16), g_final=norm_final_g.reshape(1, -1))

    xx = x[0]
    v_first = None
    for l in range(depth):
        c_a, c_b, c_c, y_d = _in_proj(xx, p_in, l)
        y_a, v_first = _rwkv(c_a, v_first, p_rwkv, l)
        y_b = _gdn(c_b, p_gdn, l)
        y_c = _gla(c_c, p_gla, l)
        xx = _out_ffn(xx, (y_a, y_b, y_c, y_d), p_ffn, l, final=(l == depth - 1))
    return xx[None]
```

```python
import functools

import jax
import jax.numpy as jnp
from jax import lax
from jax.experimental import pallas as pl
from jax.experimental.pallas import tpu as pltpu

f32 = jnp.float32
bf16 = jnp.bfloat16

SEQ = 16384
D_MODEL = 1024
GROUP_W = 256
HEAD_DIM = 64
N_HEADS = 4
GLA_KEY = 128
GLA_HEAD_K = 32
D_FF = 4096
EPS = 1e-6
RWKV_GN_EPS = 64e-5
NEG = -1e30

NA, NB, NC, ND = 1152, 1152, 896, 512
N_PAD = NA + NB + NC + ND

VMEM_LIMIT = 56 * 1024 * 1024

NN = (((1,), (0,)), ((), ()))
NT = (((1,), (1,)), ((), ()))
TN = (((0,), (0,)), ((), ()))


def _dot(a, b, dims=NN):
    return lax.dot_general(a.astype(bf16), b.astype(bf16), dims, preferred_element_type=f32)


def _iota(shape, axis):
    return lax.broadcasted_iota(jnp.int32, shape, axis)


def _layer_spec(a, l):
    return pl.BlockSpec((None,) + a.shape[1:], lambda *_: (l,) + (0,) * (a.ndim - 1))


def _segsum(x, ones_bd):
    return jnp.dot(x.astype(bf16), ones_bd, preferred_element_type=f32)


def _split2(x):
    hi = x.astype(bf16)
    lo = (x - hi.astype(f32)).astype(bf16)
    return hi, lo


def _dot_sel_lhs(sel, x, dims=NN):
    return sum(lax.dot_general(sel, t, dims, preferred_element_type=f32) for t in _split2(x))


def _dot_sel_rhs(x, sel, dims=NN):
    return sum(lax.dot_general(t, sel, dims, preferred_element_type=f32) for t in _split2(x))


def _group_ones(n, width):
    return (_iota((n, n), 0) // width == _iota((n, n), 1) // width).astype(bf16)


def _sigmoid(x):
    return 1.0 / (1.0 + jnp.exp(-x))


def _softplus(x):
    return jnp.maximum(x, 0.0) + jnp.log1p(jnp.exp(-jnp.abs(x)))


def _shift_rows(x, carry, s):
    xs = pltpu.roll(x, s, 0)
    fix = pltpu.roll(carry, s, 0)
    first = jnp.where(_iota(carry.shape, 0) < s, fix, xs[0:8])
    return jnp.concatenate([first, xs[8:]], axis=0)


def _bd(xp):
    xb = xp.astype(bf16)
    left = _iota(xb.shape, 1) < HEAD_DIM
    zero = jnp.zeros_like(xb)
    return jnp.concatenate([jnp.where(left, xb, zero), jnp.where(left, zero, xb)], axis=0)


def _pair_masks(c):
    ri, cj = _iota((c, 2 * c), 0), _iota((c, 2 * c), 1) & (c - 1)
    eye = (ri == cj).astype(f32)
    m16 = (ri // 16 == cj // 16).astype(f32)
    mo1 = ((ri // 32 == cj // 32) & (ri // 16 == cj // 16 + 1)).astype(f32)
    mo2 = ((ri // 32 == 1) & (cj // 32 == 0)).astype(f32)
    return dict(eye=eye, m16=m16, mo1=mo1, mo2=mo2, strict=ri > cj, incl=ri >= cj)


def _tri_inv(lms, masks, tick=lambda: None):
    c = RW_C

    def each(fn, *lists):
        out = []
        for args in zip(*lists):
            out.append(fn(*args))
            tick()
        return out

    ps = [-(lm * masks["m16"]) for lm in lms]
    ts = [masks["eye"] + p for p in ps]
    ps = each(lambda p: _dot(p, _bd(p)), ps)
    for _ in range(2):
        outs = each(lambda t, p: _dot(jnp.concatenate([t, p], axis=0), _bd(p)), ts, ps)
        ts = [t + o[0:c] for t, o in zip(ts, outs)]
        ps = [o[c:] for o in outs]
    ts = each(lambda t, p: t + _dot(t, _bd(p)), ts, ps)
    for mo in (masks["mo1"], masks["mo2"]):
        xs = each(lambda lm, t: _dot(lm * mo, _bd(t)), lms, ts)
        ts = each(lambda t, x: t - _dot(t, _bd(x)), ts, xs)
    return ts


IN_TM = 1024
SG_C = 128


def _sgu_mix(x, lg_ref, lb_ref, w_ref, bias_ref):
    gx = 0.5 * x * (1.0 + jnp.tanh(0.7978845608028654 * (x + 0.044715 * x * x * x)))
    u = gx[:, 0:256]
    v = gx[:, 256:512]
    mu = jnp.mean(v, axis=-1, keepdims=True)
    vc = v - mu
    var = jnp.mean(vc * vc, axis=-1, keepdims=True)
    v = vc * lax.rsqrt(var + 1e-5) * lg_ref[...] + lb_ref[...]
    wr, wc = _iota((SG_C, 4 * SG_C), 0), _iota((SG_C, 4 * SG_C), 1)
    w = jnp.where(wc % SG_C <= wr, w_ref[...], 0.0).astype(bf16)
    lane_g = _iota((SG_C, GROUP_W), 1) // HEAD_DIM
    outs = []
    for n in range(x.shape[0] // SG_C):
        vn = v[n * SG_C:(n + 1) * SG_C, :]
        vst = jnp.concatenate([jnp.where(lane_g == g, vn, 0.0) for g in range(4)], axis=0)
        outs.append(jnp.dot(w, vst.astype(bf16), preferred_element_type=f32) + bias_ref[...])
    return (u * jnp.concatenate(outs, axis=0)).astype(bf16)


def _inproj_kernel(x_ref, g_ref, w_hbm, lg_ref, lb_ref, ws_ref, bias_ref, oa_ref, ob_ref, oc_ref, yd_ref,
                   w_ref, sem, *, layer):
    @pl.when(pl.program_id(0) == 0)
    def _():
        cp = pltpu.make_async_copy(w_hbm.at[layer], w_ref, sem.at[0])
        cp.start()
        cp.wait()

    x = x_ref[...]
    ms = jnp.mean(x * x, axis=-1, keepdims=True)
    h = (x * lax.rsqrt(ms + EPS) * g_ref[...]).astype(bf16)
    c_d = jnp.dot(h, w_ref[:, NA + NB + NC:], preferred_element_type=f32)
    yd_ref[...] = _sgu_mix(c_d, lg_ref, lb_ref, ws_ref, bias_ref)
    off = 0
    for o_ref, n in ((oa_ref, NA), (ob_ref, NB), (oc_ref, NC)):
        o_ref[...] = jnp.dot(h, w_ref[:, off:off + n], preferred_element_type=f32)
        off += n


def _in_proj(x, p, l):
    tm = IN_TM
    names = ("g", "w", "ln_g", "ln_b", "w_cat", "bias_tile")
    specs = [pl.BlockSpec(memory_space=pl.ANY) if k == "w" else _layer_spec(p[k], l) for k in names]
    return pl.pallas_call(
        functools.partial(_inproj_kernel, layer=l),
        grid=(SEQ // tm,),
        in_specs=[pl.BlockSpec((tm, D_MODEL), lambda i: (i, 0))] + specs,
        out_specs=[pl.BlockSpec((tm, n), lambda i: (i, 0)) for n in (NA, NB, NC, GROUP_W)],
        out_shape=[jax.ShapeDtypeStruct((SEQ, n), f32) for n in (NA, NB, NC)]
        + [jax.ShapeDtypeStruct((SEQ, GROUP_W), bf16)],
        scratch_shapes=[pltpu.VMEM((D_MODEL, N_PAD), bf16), pltpu.SemaphoreType.DMA((1,))],
        compiler_params=pltpu.CompilerParams(dimension_semantics=("arbitrary",),
                                             vmem_limit_bytes=VMEM_LIMIT),
        name="in_proj",
    )(x, *[p[k] for k in names])


RW_TB = 512
RW_C = 64
RW_PRO = 2
RW_TICKS = 13


def _rwkv_kernel(*refs, has_vres):
    if has_vres:
        (c_ref, vf_ref, mu_ref, w0_ref, wup_ref, a0_ref, aup_ref, gup_ref, kk_ref, ka_ref, rk_ref,
         lw_ref, lb_ref, v0_ref, vup_ref, y_ref,
         carry_ref, s_ref, r_s, k_s, v_s, lw_s, al_s, be_s, y_s,
         ta_p, rt_p, vb_p, bw_p, kw_p, arb_p, tv_p, av_p, bonus_p, g_p, dl_p) = refs
    else:
        (c_ref, mu_ref, w0_ref, wup_ref, a0_ref, aup_ref, gup_ref, kk_ref, ka_ref, rk_ref,
         lw_ref, lb_ref, y_ref, vf_out_ref,
         carry_ref, s_ref, r_s, k_s, v_s, lw_s, al_s, be_s, y_s,
         ta_p, rt_p, vb_p, bw_p, kw_p, arb_p, tv_p, av_p, bonus_p, g_p, dl_p) = refs
    step = pl.program_id(0)

    @pl.when(step == 0)
    def _():
        for ref in (carry_ref, s_ref, ta_p, rt_p, vb_p, bw_p, kw_p, arb_p, tv_p, av_p, bonus_p, g_p, dl_p):
            ref[...] = jnp.zeros_like(ref)

    wr = step % 2
    rd = 1 - wr
    c = RW_C
    nchunk = RW_TB // c
    npair = N_HEADS // 2
    hsl = lambda h: slice(2 * h * HEAD_DIM, 2 * (h + 1) * HEAD_DIM)
    bd_mask = (_iota((2 * c, 2 * c), 0) // c == _iota((2 * c, 2 * c), 1) // c).astype(f32)
    state = [s_ref[h] for h in range(npair)]

    def recurrence():
        for n in range(nchunk):
            rows = slice(n * c, (n + 1) * c)
            sas = [lax.dot_general(jnp.concatenate([ta_p[rd, rows, hsl(h)], rt_p[rd, rows, hsl(h)]], axis=0),
                                   state[h].astype(bf16), NT, preferred_element_type=f32)
                   for h in range(npair)]
            yield
            us = [sas[h][0:c] + tv_p[rd, rows, hsl(h)] for h in range(npair)]
            upd = [_dot(jnp.concatenate([us[h].astype(bf16), vb_p[rd, rows, hsl(h)]], axis=0),
                        jnp.concatenate([bw_p[rd, rows, hsl(h)], kw_p[rd, rows, hsl(h)]], axis=0), TN) * bd_mask
                   for h in range(npair)]
            yield
            dl = dl_p[rd, n:n + 1, :]
            for h in range(npair):
                state[h] = state[h] * dl[:, hsl(h)] + upd[h]
            ys = [sas[h][c:] + _dot(arb_p[rd, rows, hsl(h)], _bd(us[h])) + av_p[rd, rows, hsl(h)]
                  for h in range(npair)]
            y_s[rows, :] = jnp.concatenate(ys, axis=1)
            yield

    chain = recurrence()
    calls = [0]

    def advance(pieces=1):
        for _ in range(pieces):
            next(chain, None)

    def tick():
        calls[0] += 1
        if calls[0] % RW_TICKS == 0:
            advance()

    ones_bd = _group_ones(GROUP_W, HEAD_DIM)
    x = c_ref[:, 0:1024]
    x_prev = _shift_rows(x, carry_ref[...], 1)
    carry_ref[...] = x[RW_TB - 8:RW_TB]
    xs = x + (x_prev - x) * mu_ref[...]
    advance(RW_PRO)
    r = xs[:, 0:256]
    k = xs[:, 256:512]
    v = xs[:, 512:768]
    lora = xs[:, 768:896]
    w_pre = w0_ref[...] + _dot(jnp.tanh(lora), wup_ref[...])
    lw = -jnp.exp(-_softplus(-w_pre) - 0.5)
    advance(RW_PRO)
    a = _sigmoid(a0_ref[...] + _dot(lora, aup_ref[...]))
    g_p[wr] = _dot(_sigmoid(xs[:, 896:1024]), gup_ref[...])
    advance(RW_PRO)
    if has_vres:
        mix = _sigmoid(v0_ref[...] + _dot(c_ref[:, 1024:1152], vup_ref[...]))
        v = v + (vf_ref[...] - v) * mix
    else:
        @pl.when(step < SEQ // RW_TB)
        def _():
            vf_out_ref[...] = v
    kk = k * kk_ref[...]
    kk = kk * lax.rsqrt(_segsum(kk * kk, ones_bd) + 1e-24)
    advance(RW_PRO)
    k = k * (1.0 + (a - 1.0) * ka_ref[...])
    bonus_p[wr] = _segsum(r * k * rk_ref[...], ones_bd) * v
    advance(RW_PRO)
    r_s[...] = r
    k_s[...] = k
    v_s[...] = v
    lw_s[...] = lw
    al_s[...] = -kk
    be_s[...] = kk * a

    tril_incl = (_iota((c, c), 0) >= _iota((c, c), 1)).astype(bf16)
    masks = _pair_masks(c)
    low_strict, low_incl = masks["strict"], masks["incl"]

    items = [(n, h) for n in range(nchunk) for h in range(npair)]
    pre = []
    for n in range(nchunk):
        rows = slice(n * c, (n + 1) * c)
        lwc = lw_s[rows, :]
        lc = _dot_sel_lhs(tril_incl, lwc)
        llast = lc[c - 1:c, :]
        e_out = jnp.exp(-lc)
        e_rest = jnp.exp(llast - lc)
        kc, bec = k_s[rows, :], be_s[rows, :]
        rt = r_s[rows, :] * jnp.exp(lc)
        pre.append(dict(rt=rt, at=al_s[rows, :] * jnp.exp(lc - lwc), bt=bec * e_out, kt=kc * e_out,
                        v=v_s[rows, :]))
        rt_p[wr, rows, :] = rt.astype(bf16)
        vb_p[wr, rows, :] = v_s[rows, :].astype(bf16)
        bw_p[wr, rows, :] = (bec * e_rest).astype(bf16)
        kw_p[wr, rows, :] = (kc * e_rest).astype(bf16)
        dl_p[wr, n:n + 1, :] = jnp.exp(llast)
        tick()
    ms = []
    for n, h in items:
        ms.append(_dot(jnp.concatenate([pre[n]["at"][:, hsl(h)], pre[n]["rt"][:, hsl(h)]], axis=0),
                       jnp.concatenate([_bd(pre[n]["bt"][:, hsl(h)]), _bd(pre[n]["kt"][:, hsl(h)])], axis=0),
                       NT))
        arb_p[wr, n * c:(n + 1) * c, hsl(h)] = jnp.where(low_incl, ms[-1][c:, 0:2 * c], 0.0).astype(bf16)
        tick()
    tms = _tri_inv([jnp.where(low_strict, -m[0:c, 0:2 * c], 0.0) for m in ms], masks, tick)
    avs = []
    for m, (n, h) in zip(ms, items):
        avs.append(_dot(jnp.concatenate([jnp.where(low_strict, m[0:c, 2 * c:], 0.0),
                                         jnp.where(low_incl, m[c:, 2 * c:], 0.0)], axis=0),
                        _bd(pre[n]["v"][:, hsl(h)])))
        av_p[wr, n * c:(n + 1) * c, hsl(h)] = avs[-1][c:]
        tick()
    for tm, av, (n, h) in zip(tms, avs, items):
        tt = _dot(tm, jnp.concatenate([_bd(pre[n]["at"][:, hsl(h)]), _bd(av[0:c])], axis=1))
        ta_p[wr, n * c:(n + 1) * c, hsl(h)] = tt[:, 0:2 * c].astype(bf16)
        tv_p[wr, n * c:(n + 1) * c, hsl(h)] = tt[:, 2 * c:]
        tick()
    for _ in chain:
        pass
    for h in range(npair):
        s_ref[h] = state[h]

    y = y_s[...]
    inv_d = 1.0 / HEAD_DIM
    mean = _segsum(y, ones_bd) * inv_d
    yc = y - mean
    var = _segsum(yc * yc, ones_bd) * inv_d
    y = yc * lax.rsqrt(var + RWKV_GN_EPS) * lw_ref[...] + lb_ref[...]
    y_ref[...] = ((y + bonus_p[rd]) * g_p[rd]).astype(bf16)


def _rwkv(c_a, v_first, p, l):
    has_vres = v_first is not None
    nblk = SEQ // RW_TB
    blk = lambda n: pl.BlockSpec((RW_TB, n), lambda i: (jnp.minimum(i, nblk - 1), 0))
    blk_prev = lambda n: pl.BlockSpec((RW_TB, n), lambda i: (jnp.maximum(i - 1, 0), 0))
    in_specs = [blk(NA)]
    args = [c_a]
    if has_vres:
        in_specs.append(blk(GROUP_W))
        args.append(v_first)
    names = ("mu", "w0", "w_up", "a0", "a_up", "g_up", "k_k", "k_a", "r_k", "lnx_w", "lnx_b")
    in_specs += [_layer_spec(p[k], l) for k in names]
    args += [p[k] for k in names]
    if has_vres:
        in_specs += [_layer_spec(p["v0"], l - 1), _layer_spec(p["vres_up"], l - 1)]
        args += [p["v0"], p["vres_up"]]
        out_specs = blk_prev(GROUP_W)
        out_shape = jax.ShapeDtypeStruct((SEQ, GROUP_W), bf16)
    else:
        out_specs = [blk_prev(GROUP_W), blk(GROUP_W)]
        out_shape = [jax.ShapeDtypeStruct((SEQ, GROUP_W), bf16),
                     jax.ShapeDtypeStruct((SEQ, GROUP_W), f32)]
    scratch = [pltpu.VMEM((8, 1024), f32), pltpu.VMEM((N_HEADS // 2, 2 * HEAD_DIM, 2 * HEAD_DIM), f32)]
    scratch += [pltpu.VMEM((RW_TB, GROUP_W), f32) for _ in range(7)]
    scratch += [pltpu.VMEM((2, RW_TB, GROUP_W), bf16) for _ in range(6)]
    scratch += [pltpu.VMEM((2, RW_TB, GROUP_W), f32) for _ in range(4)]
    scratch += [pltpu.VMEM((2, RW_TB // RW_C, GROUP_W), f32)]
    out = pl.pallas_call(
        functools.partial(_rwkv_kernel, has_vres=has_vres),
        grid=(nblk + 1,),
        in_specs=in_specs,
        out_specs=out_specs,
        out_shape=out_shape,
        scratch_shapes=scratch,
        compiler_params=pltpu.CompilerParams(dimension_semantics=("arbitrary",),
                                             vmem_limit_bytes=VMEM_LIMIT),
        name="rwkv7",
    )(*args)
    if has_vres:
        return out, v_first
    return out[0], out[1]


GD_TB = 512
GD_C = 64
GD_PRO = 2
GD_TICKS = 9


def _gdn_kernel(c_ref, cw_ref, alog_ref, dtb_ref, alogc_ref, dtbc_ref, ng_ref, y_ref,
                carry_ref, s_ref, q_s, k_s, v_s, be_s, g_s, o_s,
                u_p, w_p, qe_p, kd_p, qk_p, gz_p, egl_p):
    step = pl.program_id(0)

    @pl.when(step == 0)
    def _():
        for ref in (carry_ref, s_ref, u_p, w_p, qe_p, kd_p, qk_p, gz_p, egl_p):
            ref[...] = jnp.zeros_like(ref)

    wr = step % 2
    rd = 1 - wr
    c = GD_C
    nchunk = GD_TB // c
    npair = N_HEADS // 2
    hsl = lambda h: slice(2 * h * HEAD_DIM, 2 * (h + 1) * HEAD_DIM)
    bd_mask = (_iota((2 * c, 2 * c), 0) // c == _iota((2 * c, 2 * c), 1) // c).astype(f32)
    ones_bd = _group_ones(GROUP_W, HEAD_DIM)
    state = [s_ref[h] for h in range(npair)]

    def recurrence():
        for n in range(nchunk):
            rows = slice(n * c, (n + 1) * c)
            wss = [jnp.dot(jnp.concatenate([w_p[rd, rows, hsl(h)], qe_p[rd, rows, hsl(h)]], axis=0),
                           state[h].astype(bf16), preferred_element_type=f32)
                   for h in range(npair)]
            yield
            vns = [u_p[rd, rows, hsl(h)] - wss[h][0:c] for h in range(npair)]
            upd = [_dot(kd_p[rd, rows, hsl(h)], vns[h], TN) * bd_mask for h in range(npair)]
            yield
            egl = egl_p[rd, n:n + 1, :]
            for h in range(npair):
                state[h] = state[h] * egl[:, hsl(h)] + upd[h]
            os_ = [wss[h][c:] + _dot(qk_p[rd, rows, hsl(h)], _bd(vns[h])) for h in range(npair)]
            o_s[rows, :] = jnp.concatenate(os_, axis=1)
            yield

    chain = recurrence()
    calls = [0]

    def tick():
        calls[0] += 1
        if calls[0] % GD_TICKS == 0:
            next(chain, None)

    raw = c_ref[:, 0:768]
    carry = carry_ref[...]
    conv = raw * cw_ref[3:4, :]
    for s in range(1, 4):
        conv = conv + _shift_rows(raw, carry, s) * cw_ref[3 - s:4 - s, :]
    carry_ref[...] = raw[GD_TB - 8:GD_TB]
    qkv = conv * _sigmoid(conv)
    for _ in range(GD_PRO):
        next(chain, None)
    q = qkv[:, 0:256]
    k = qkv[:, 256:512]
    q_s[...] = q * lax.rsqrt(_segsum(q * q, ones_bd) + 1e-6) * (HEAD_DIM ** -0.5)
    for _ in range(GD_PRO):
        next(chain, None)
    k_s[...] = k * lax.rsqrt(_segsum(k * k, ones_bd) + 1e-6)
    v_s[...] = qkv[:, 512:768]
    z = c_ref[:, 768:1024]
    gz_p[wr] = z * _sigmoid(z)
    for _ in range(GD_PRO):
        next(chain, None)
    small = c_ref[:, 1024:1152]
    er, ec = _iota((128, GROUP_W), 0), _iota((128, GROUP_W), 1)
    b_exp = _dot_sel_rhs(small, (er == ec // HEAD_DIM).astype(bf16))
    a_exp = _dot_sel_rhs(small, (er == ec // HEAD_DIM + N_HEADS).astype(bf16))
    be_s[...] = _sigmoid(b_exp)
    g_s[...] = -jnp.exp(alog_ref[...]) * _softplus(a_exp + dtb_ref[...])

    tril_incl = (_iota((c, c), 0) >= _iota((c, c), 1)).astype(bf16)
    masks = _pair_masks(c)
    low_strict, low_incl = masks["strict"], masks["incl"]
    sel8 = (_iota((8, 128), 0) == _iota((8, 128), 1)).astype(bf16)
    g_rows = -jnp.exp(alogc_ref[...]) * _softplus(_dot_sel_lhs(sel8, small, NT) + dtbc_ref[...])
    tj, ti = _iota((GD_TB, GD_TB), 0), _iota((GD_TB, GD_TB), 1)
    gc_rows = _dot_sel_rhs(g_rows, ((tj // c == ti // c) & (tj <= ti)).astype(bf16))

    items = [(n, h) for n in range(nchunk) for h in range(npair)]
    pre = []
    for n in range(nchunk):
        rows = slice(n * c, (n + 1) * c)
        gc = _dot_sel_lhs(tril_incl, g_s[rows, :])
        glast = gc[c - 1:c, :]
        egc = jnp.exp(gc)
        kc, bc = k_s[rows, :], be_s[rows, :]
        kb = kc * bc
        pre.append(dict(gc=gc, k=kc, kb=kb, q=q_s[rows, :], vb=v_s[rows, :] * bc, kbe=kb * egc))
        qe_p[wr, rows, :] = (q_s[rows, :] * egc).astype(bf16)
        kd_p[wr, rows, :] = (kc * jnp.exp(glast - gc)).astype(bf16)
        egl_p[wr, n:n + 1, :] = jnp.exp(glast)
        tick()

    def gc_row(n, h):
        return jnp.concatenate([gc_rows[N_HEADS + 2 * h + i:N_HEADS + 2 * h + i + 1, n * c:(n + 1) * c]
                                for i in range(2)], axis=1)

    dms, aqs = [], []
    for n, h in items:
        dms.append(jnp.exp(jnp.where(low_incl, pre[n]["gc"][:, hsl(h)] - gc_row(n, h), NEG)))
        aqs.append(_dot(jnp.concatenate([pre[n]["kb"][:, hsl(h)], pre[n]["q"][:, hsl(h)]], axis=0),
                        _bd(pre[n]["k"][:, hsl(h)]), NT))
        qk_p[wr, n * c:(n + 1) * c, hsl(h)] = (aqs[-1][c:] * dms[-1]).astype(bf16)
        tick()
    tms = _tri_inv([jnp.where(low_strict, aq[0:c] * dm, 0.0) for aq, dm in zip(aqs, dms)], masks, tick)
    for tm, (n, h) in zip(tms, items):
        uw = _dot(tm, jnp.concatenate([_bd(pre[n]["vb"][:, hsl(h)]), _bd(pre[n]["kbe"][:, hsl(h)])], axis=1))
        u_p[wr, n * c:(n + 1) * c, hsl(h)] = uw[:, 0:2 * c]
        w_p[wr, n * c:(n + 1) * c, hsl(h)] = uw[:, 2 * c:].astype(bf16)
        tick()
    for _ in chain:
        pass
    for h in range(npair):
        s_ref[h] = state[h]

    o = o_s[...]
    ms = _segsum(o * o, ones_bd) * (1.0 / HEAD_DIM)
    y_ref[...] = (o * lax.rsqrt(ms + EPS) * ng_ref[...] * gz_p[rd]).astype(bf16)


def _gdn(c_b, p, l):
    names = ("conv_w", "a_log", "dt_bias", "a_log_col", "dt_bias_col", "norm_g")
    nblk = SEQ // GD_TB
    scratch = [pltpu.VMEM((8, 768), f32), pltpu.VMEM((N_HEADS // 2, 2 * HEAD_DIM, 2 * HEAD_DIM), f32)]
    scratch += [pltpu.VMEM((GD_TB, GROUP_W), f32) for _ in range(6)]
    scratch += [pltpu.VMEM((2, GD_TB, GROUP_W), f32)]
    scratch += [pltpu.VMEM((2, GD_TB, GROUP_W), bf16) for _ in range(4)]
    scratch += [pltpu.VMEM((2, GD_TB, GROUP_W), f32), pltpu.VMEM((2, GD_TB // GD_C, GROUP_W), f32)]
    return pl.pallas_call(
        _gdn_kernel,
        grid=(nblk + 1,),
        in_specs=[pl.BlockSpec((GD_TB, NB), lambda i: (jnp.minimum(i, nblk - 1), 0))]
        + [_layer_spec(p[k], l) for k in names],
        out_specs=pl.BlockSpec((GD_TB, GROUP_W), lambda i: (jnp.maximum(i - 1, 0), 0)),
        out_shape=jax.ShapeDtypeStruct((SEQ, GROUP_W), bf16),
        scratch_shapes=scratch,
        compiler_params=pltpu.CompilerParams(dimension_semantics=("arbitrary",),
                                             vmem_limit_bytes=VMEM_LIMIT),
        name="gdn",
    )(c_b, *[p[k] for k in names])


GL_TB = 256
GL_C = 16
GL_S = 8


def _gla_kernel(c_ref, gup_ref, gb_ref, ng_ref, y_ref, st_ref, sx_s, o_s):
    @pl.when(pl.program_id(0) == 0)
    def _():
        st_ref[...] = jnp.zeros_like(st_ref)

    tb, c, s = GL_TB, GL_C, GL_S
    nchunk, nsub = tb // c, tb // s
    q = c_ref[:, 0:128] * (GLA_HEAD_K ** -0.5)
    k = c_ref[:, 128:256]
    v = c_ref[:, 256:512]
    pre = _dot(c_ref[:, 768:896], gup_ref[...]) + gb_ref[...]
    la = -_softplus(-pre) * (1.0 / 16.0)
    tj, ti = _iota((tb, tb), 0), _iota((tb, tb), 1)
    b = _dot_sel_lhs(((tj // c == ti // c) & (ti <= tj)).astype(bf16), la)
    qi = q * jnp.exp(b)

    ind_e = (_iota((GLA_KEY, GROUP_W), 0) // GLA_HEAD_K == _iota((GLA_KEY, GROUP_W), 1) // HEAD_DIM).astype(bf16)
    bd_mask = (_iota((GROUP_W, GLA_KEY), 0) // HEAD_DIM == _iota((GROUP_W, GLA_KEY), 1) // GLA_HEAD_K).astype(f32)

    b3, q3, k3 = (t.reshape(nsub, s, GLA_KEY) for t in (b, q, k))
    ri = _iota((nsub, s, GLA_KEY), 1)
    terms = []
    for j in range(s):
        e = jnp.exp(jnp.where(ri >= j, b3 - b3[:, j:j + 1, :], NEG))
        terms.append((q3 * (k3[:, j:j + 1, :] * e)).reshape(tb, GLA_KEY).astype(bf16))
    sx_s[...] = jnp.dot(jnp.concatenate(terms, axis=0), ind_e, preferred_element_type=f32)
    v3 = v.reshape(nsub, s, GROUP_W)
    o3 = sx_s[0:tb, :].reshape(nsub, s, GROUP_W) * v3[:, 0:1, :]
    for j in range(1, s):
        o3 = o3 + sx_s[j * tb:(j + 1) * tb, :].reshape(nsub, s, GROUP_W) * v3[:, j:j + 1, :]

    b4, q4, k4 = (t.reshape(nchunk, 2, s, GLA_KEY) for t in (b, q, k))
    bref = b4[:, 0, s - 1:s, :]
    qd = (q4[:, 1] * jnp.exp(b4[:, 1] - bref)).reshape(nchunk * s, GLA_KEY)
    kd = (k4[:, 0] * jnp.exp(bref - b4[:, 0])).reshape(nchunk * s, GLA_KEY)
    v0 = v.reshape(nchunk, 2, s, GROUP_W)[:, 0].reshape(nchunk * s, GROUP_W)
    head_k = _iota((nchunk * s, GLA_KEY), 1) // GLA_HEAD_K
    head_v = _iota((nchunk * s, GROUP_W), 1) // HEAD_DIM
    ks = jnp.concatenate([jnp.where(head_k == h, kd, 0.0) for h in range(N_HEADS)], axis=0)
    vs = jnp.concatenate([jnp.where(head_v == h, v0, 0.0) for h in range(N_HEADS)], axis=0)
    sc = _dot(qd, ks, NT)
    sr, scol = _iota(sc.shape, 0), _iota(sc.shape, 1)
    sc = jnp.where(sr // s == (scol % (nchunk * s)) // s, sc, 0.0)
    o_off = _dot(sc, vs).reshape(nchunk, 1, s, GROUP_W)
    o4 = o3.reshape(nchunk, 2, s, GROUP_W)
    o_intra = jnp.concatenate([o4[:, 0:1], o4[:, 1:2] + o_off], axis=1).reshape(tb, GROUP_W)

    blasts = [b[(n + 1) * c - 1:(n + 1) * c, :] for n in range(nchunk)]
    upds = [_dot(v[n * c:(n + 1) * c], k[n * c:(n + 1) * c] * jnp.exp(blasts[n] - b[n * c:(n + 1) * c]), TN)
            * bd_mask for n in range(nchunk)]
    st = st_ref[...]
    for n in range(nchunk):
        rows = slice(n * c, (n + 1) * c)
        o_s[rows, :] = o_intra[rows] + _dot(qi[rows], st, NT)
        st = st * jnp.exp(blasts[n]) + upds[n]
    st_ref[...] = st

    o = o_s[...]
    ms = _segsum(o * o, _group_ones(GROUP_W, HEAD_DIM)) * (1.0 / HEAD_DIM)
    gate = c_ref[:, 512:768]
    y_ref[...] = (o * lax.rsqrt(ms + EPS) * ng_ref[...] * (gate * _sigmoid(gate))).astype(bf16)


def _gla(c_c, p, l):
    names = ("gk_up", "gk_bias", "norm_g")
    return pl.pallas_call(
        _gla_kernel,
        grid=(SEQ // GL_TB,),
        in_specs=[pl.BlockSpec((GL_TB, NC), lambda i: (i, 0))] + [_layer_spec(p[k], l) for k in names],
        out_specs=pl.BlockSpec((GL_TB, GROUP_W), lambda i: (i, 0)),
        out_shape=jax.ShapeDtypeStruct((SEQ, GROUP_W), bf16),
        scratch_shapes=[pltpu.VMEM((GROUP_W, GLA_KEY), f32),
                        pltpu.VMEM((GL_TB * GL_S, GROUP_W), f32), pltpu.VMEM((GL_TB, GROUP_W), f32)],
        compiler_params=pltpu.CompilerParams(dimension_semantics=("arbitrary",),
                                             vmem_limit_bytes=VMEM_LIMIT),
        name="gla",
    )(c_c, *[p[k] for k in names])


FF_TM = 512
FF_TF = 1024


def _outffn_kernel(*refs, final, layer):
    if final:
        (x_ref, ya_ref, yb_ref, yc_ref, yd_ref, wo_hbm, g_ref, wu_hbm, wd_hbm, gf_ref, o_ref,
         wo_ref, wu_ref, wd_ref, sem) = refs
    else:
        (x_ref, ya_ref, yb_ref, yc_ref, yd_ref, wo_hbm, g_ref, wu_hbm, wd_hbm, o_ref,
         wo_ref, wu_ref, wd_ref, sem) = refs

    @pl.when(pl.program_id(0) == 0)
    def _():
        copies = [pltpu.make_async_copy(src.at[layer], dst, sem.at[i])
                  for i, (src, dst) in enumerate(((wo_hbm, wo_ref), (wu_hbm, wu_ref), (wd_hbm, wd_ref)))]
        for cp in copies:
            cp.start()
        for cp in copies:
            cp.wait()

    y = jnp.concatenate([ya_ref[...], yb_ref[...], yc_ref[...], yd_ref[...]], axis=1)
    x1 = x_ref[...] + jnp.dot(y, wo_ref[...], preferred_element_type=f32)
    ms = jnp.mean(x1 * x1, axis=-1, keepdims=True)
    h = (x1 * lax.rsqrt(ms + EPS) * g_ref[...]).astype(bf16)
    x2 = x1
    for kf in range(D_FF // FF_TF):
        cols = slice(kf * FF_TF, (kf + 1) * FF_TF)
        hid = jnp.maximum(jnp.dot(h, wu_ref[:, cols], preferred_element_type=f32), 0.0)
        x2 = x2 + jnp.dot((hid * hid).astype(bf16), wd_ref[cols, :], preferred_element_type=f32)
    if final:
        ms = jnp.mean(x2 * x2, axis=-1, keepdims=True)
        x2 = x2 * lax.rsqrt(ms + EPS) * gf_ref[...]
    o_ref[...] = x2


def _out_ffn(x, ys, p, l, final):
    hbm = pl.BlockSpec(memory_space=pl.ANY)
    yspec = pl.BlockSpec((FF_TM, GROUP_W), lambda i: (i, 0))
    in_specs = [pl.BlockSpec((FF_TM, D_MODEL), lambda i: (i, 0)), yspec, yspec, yspec, yspec,
                hbm, _layer_spec(p["g"], l), hbm, hbm]
    args = [x, *ys, p["w_out"], p["g"], p["w_up"], p["w_down"]]
    if final:
        in_specs.append(pl.BlockSpec((1, D_MODEL), lambda i: (0, 0)))
        args.append(p["g_final"])
    return pl.pallas_call(
        functools.partial(_outffn_kernel, final=final, layer=l),
        grid=(SEQ // FF_TM,),
        in_specs=in_specs,
        out_specs=pl.BlockSpec((FF_TM, D_MODEL), lambda i: (i, 0)),
        out_shape=jax.ShapeDtypeStruct((SEQ, D_MODEL), f32),
        scratch_shapes=[pltpu.VMEM((D_MODEL, D_MODEL), bf16), pltpu.VMEM((D_MODEL, D_FF), bf16),
                        pltpu.VMEM((D_FF, D_MODEL), bf16), pltpu.SemaphoreType.DMA((3,))],
        compiler_params=pltpu.CompilerParams(dimension_semantics=("arbitrary",),
                                             vmem_limit_bytes=VMEM_LIMIT),
        name="out_ffn",
    )(*args)


def kernel(x, w_in, w_out, norm_mix_g, norm_ffn_g, norm_final_g, rwkv_mu, rwkv_w0, rwkv_w_up, rwkv_a0, rwkv_a_up, rwkv_g_up, rwkv_k_k, rwkv_k_a, rwkv_r_k, rwkv_lnx_w, rwkv_lnx_b, rwkv_v0, rwkv_vres_down, rwkv_vres_up, gdn_conv_w, gdn_a_log, gdn_dt_bias, gdn_norm_g, gla_gk_up, gla_gk_bias, gla_norm_g, sgu_ln_g, sgu_ln_b, sgu_w_s, sgu_b_s, ffn_w_up, ffn_w_down):
    depth = w_in.shape[0]
    row = lambda a: a.reshape(depth, 1, -1)
    per_head = lambda a: jnp.repeat(a, HEAD_DIM, axis=-1).reshape(depth, 1, -1)
    pad_cols = lambda w, n: jnp.pad(w, ((0, 0), (0, 0), (0, n - w.shape[2])))
    pad_rows = lambda w, top, total: jnp.pad(w, ((0, 0), (top, total - top - w.shape[1]), (0, 0)))
    vres_down = jnp.pad(rwkv_vres_down, ((1, 0), (0, 0), (0, 0)))
    w_comb = jnp.concatenate(
        [pad_cols(jnp.concatenate([w_in[:, :, 0:1024], vres_down], axis=2), NA),
         pad_cols(w_in[:, :, 1024:2056], NB), pad_cols(w_in[:, :, 2056:2840], NC),
         w_in[:, :, 2840:3352]], axis=2).astype(bf16)
    p_in = dict(g=row(norm_mix_g), w=w_comb, ln_g=row(sgu_ln_g), ln_b=row(sgu_ln_b),
                w_cat=sgu_w_s.transpose(0, 2, 1, 3).reshape(depth, SG_C, 4 * SG_C),
                bias_tile=jnp.repeat(sgu_b_s.transpose(0, 2, 1), HEAD_DIM, axis=2))
    p_rwkv = dict(mu=row(rwkv_mu), w0=row(rwkv_w0), w_up=pad_rows(rwkv_w_up, 0, 128),
                  a0=row(rwkv_a0), a_up=pad_rows(rwkv_a_up, 64, 128), g_up=rwkv_g_up,
                  k_k=row(rwkv_k_k), k_a=row(rwkv_k_a), r_k=row(rwkv_r_k),
                  lnx_w=row(rwkv_lnx_w), lnx_b=row(rwkv_lnx_b),
                  v0=rwkv_v0.reshape(depth - 1, 1, -1), vres_up=pad_rows(rwkv_vres_up, 0, 128))
    p_gdn = dict(conv_w=gdn_conv_w, a_log=per_head(gdn_a_log), dt_bias=per_head(gdn_dt_bias),
                 a_log_col=jnp.pad(gdn_a_log, ((0, 0), (N_HEADS, 0))).reshape(depth, 8, 1),
                 dt_bias_col=jnp.pad(gdn_dt_bias, ((0, 0), (N_HEADS, 0))).reshape(depth, 8, 1),
                 norm_g=row(jnp.tile(gdn_norm_g, (1, N_HEADS))))
    p_gla = dict(gk_up=pad_rows(gla_gk_up, 0, 128), gk_bias=row(gla_gk_bias),
                 norm_g=row(jnp.tile(gla_norm_g, (1, N_HEADS))))
    p_ffn = dict(w_out=w_out.astype(bf16), g=row(norm_ffn_g), w_up=ffn_w_up.astype(bf16),
                 w_down=ffn_w_down.astype(bf16), g_final=norm_final_g.reshape(1, -1))

    xx = x[0]
    v_first = None
    for l in range(depth):
        c_a, c_b, c_c, y_d = _in_proj(xx, p_in, l)
        y_a, v_first = _rwkv(c_a, v_first, p_rwkv, l)
        y_b = _gdn(c_b, p_gdn, l)
        y_c = _gla(c_c, p_gla, l)
        xx = _out_ffn(xx, (y_a, y_b, y_c, y_d), p_ffn, l, final=(l == depth - 1))
    return xx[None]
```

```python
import functools

import jax
import jax.numpy as jnp
from jax import lax
from jax.experimental import pallas as pl
from jax.experimental.pallas import tpu as pltpu

f32 = jnp.float32
bf16 = jnp.bfloat16

SEQ = 16384
D_MODEL = 1024
GROUP_W = 256
HEAD_DIM = 64
N_HEADS = 4
GLA_KEY = 128
GLA_HEAD_K = 32
D_FF = 4096
EPS = 1e-6
RWKV_GN_EPS = 64e-5
NEG = -1e30

NA, NB, NC, NM, ND = 1024, 1024, 768, 128, 512
N_PAD = NA + NB + NC + NM + ND
M_VRES, M_GDN, M_GLA = 0, 32, 40

VMEM_LIMIT = 56 * 1024 * 1024

NN = (((1,), (0,)), ((), ()))
NT = (((1,), (1,)), ((), ()))
TN = (((0,), (0,)), ((), ()))


def _dot(a, b, dims=NN):
    return lax.dot_general(a.astype(bf16), b.astype(bf16), dims, preferred_element_type=f32)


def _iota(shape, axis):
    return lax.broadcasted_iota(jnp.int32, shape, axis)


def _layer_spec(a, l):
    return pl.BlockSpec((None,) + a.shape[1:], lambda *_: (l,) + (0,) * (a.ndim - 1))


def _segsum(x, ones_bd):
    return jnp.dot(x.astype(bf16), ones_bd, preferred_element_type=f32)


def _split2(x):
    hi = x.astype(bf16)
    lo = (x - hi.astype(f32)).astype(bf16)
    return hi, lo


def _dot_sel_lhs(sel, x, dims=NN):
    return sum(lax.dot_general(sel, t, dims, preferred_element_type=f32) for t in _split2(x))


def _dot_sel_rhs(x, sel, dims=NN):
    return sum(lax.dot_general(t, sel, dims, preferred_element_type=f32) for t in _split2(x))


def _group_ones(n, width):
    return (_iota((n, n), 0) // width == _iota((n, n), 1) // width).astype(bf16)


def _sigmoid(x):
    return 1.0 / (1.0 + jnp.exp(-x))


def _softplus(x):
    return jnp.maximum(x, 0.0) + jnp.log1p(jnp.exp(-jnp.abs(x)))


def _shift_rows(x, carry, s):
    xs = pltpu.roll(x, s, 0)
    fix = pltpu.roll(carry, s, 0)
    first = jnp.where(_iota(carry.shape, 0) < s, fix, xs[0:8])
    return jnp.concatenate([first, xs[8:]], axis=0)


def _bd(xp):
    xb = xp.astype(bf16)
    left = _iota(xb.shape, 1) < HEAD_DIM
    zero = jnp.zeros_like(xb)
    return jnp.concatenate([jnp.where(left, xb, zero), jnp.where(left, zero, xb)], axis=0)


def _pair_masks(c):
    ri, cj = _iota((c, 2 * c), 0), _iota((c, 2 * c), 1) & (c - 1)
    eye = (ri == cj).astype(f32)
    m16 = (ri // 16 == cj // 16).astype(f32)
    mo1 = ((ri // 32 == cj // 32) & (ri // 16 == cj // 16 + 1)).astype(f32)
    mo2 = ((ri // 32 == 1) & (cj // 32 == 0)).astype(f32)
    return dict(eye=eye, m16=m16, mo1=mo1, mo2=mo2, strict=ri > cj, incl=ri >= cj)


def _tri_inv(lms, masks, tick=lambda: None):
    c = RW_C

    def each(fn, *lists):
        out = []
        for args in zip(*lists):
            out.append(fn(*args))
            tick()
        return out

    ps = [-(lm * masks["m16"]) for lm in lms]
    ts = [masks["eye"] + p for p in ps]
    ps = each(lambda p: _dot(p, _bd(p)), ps)
    for _ in range(2):
        outs = each(lambda t, p: _dot(jnp.concatenate([t, p], axis=0), _bd(p)), ts, ps)
        ts = [t + o[0:c] for t, o in zip(ts, outs)]
        ps = [o[c:] for o in outs]
    ts = each(lambda t, p: t + _dot(t, _bd(p)), ts, ps)
    for mo in (masks["mo1"], masks["mo2"]):
        xs = each(lambda lm, t: _dot(lm * mo, _bd(t)), lms, ts)
        ts = each(lambda t, x: t - _dot(t, _bd(x)), ts, xs)
    return ts


IN_TM = 1024
SG_C = 128


def _sgu_mix(x, lg_ref, lb_ref, w_ref, bias_ref):
    gx = 0.5 * x * (1.0 + jnp.tanh(0.7978845608028654 * (x + 0.044715 * x * x * x)))
    u = gx[:, 0:256]
    v = gx[:, 256:512]
    mu = jnp.mean(v, axis=-1, keepdims=True)
    vc = v - mu
    var = jnp.mean(vc * vc, axis=-1, keepdims=True)
    v = vc * lax.rsqrt(var + 1e-5) * lg_ref[...] + lb_ref[...]
    wr, wc = _iota((SG_C, 4 * SG_C), 0), _iota((SG_C, 4 * SG_C), 1)
    w = jnp.where(wc % SG_C <= wr, w_ref[...], 0.0).astype(bf16)
    lane_g = _iota((SG_C, GROUP_W), 1) // HEAD_DIM
    outs = []
    for n in range(x.shape[0] // SG_C):
        vn = v[n * SG_C:(n + 1) * SG_C, :]
        vst = jnp.concatenate([jnp.where(lane_g == g, vn, 0.0) for g in range(4)], axis=0)
        outs.append(jnp.dot(w, vst.astype(bf16), preferred_element_type=f32) + bias_ref[...])
    return (u * jnp.concatenate(outs, axis=0)).astype(bf16)


def _inproj_kernel(x_ref, g_ref, w_hbm, lg_ref, lb_ref, ws_ref, bias_ref, oa_ref, ob_ref, oc_ref, om_ref,
                   yd_ref, w_ref, sem, *, layer):
    @pl.when(pl.program_id(0) == 0)
    def _():
        cp = pltpu.make_async_copy(w_hbm.at[layer], w_ref, sem.at[0])
        cp.start()
        cp.wait()

    x = x_ref[...]
    ms = jnp.mean(x * x, axis=-1, keepdims=True)
    h = (x * lax.rsqrt(ms + EPS) * g_ref[...]).astype(bf16)
    c_d = jnp.dot(h, w_ref[:, NA + NB + NC + NM:], preferred_element_type=f32)
    yd_ref[...] = _sgu_mix(c_d, lg_ref, lb_ref, ws_ref, bias_ref)
    off = 0
    for o_ref, n in ((oa_ref, NA), (ob_ref, NB), (oc_ref, NC), (om_ref, NM)):
        o_ref[...] = jnp.dot(h, w_ref[:, off:off + n], preferred_element_type=f32)
        off += n


def _in_proj(x, p, l):
    tm = IN_TM
    names = ("g", "w", "ln_g", "ln_b", "w_cat", "bias_tile")
    specs = [pl.BlockSpec(memory_space=pl.ANY) if k == "w" else _layer_spec(p[k], l) for k in names]
    return pl.pallas_call(
        functools.partial(_inproj_kernel, layer=l),
        grid=(SEQ // tm,),
        in_specs=[pl.BlockSpec((tm, D_MODEL), lambda i: (i, 0))] + specs,
        out_specs=[pl.BlockSpec((tm, n), lambda i: (i, 0)) for n in (NA, NB, NC, NM, GROUP_W)],
        out_shape=[jax.ShapeDtypeStruct((SEQ, n), f32) for n in (NA, NB, NC, NM)]
        + [jax.ShapeDtypeStruct((SEQ, GROUP_W), bf16)],
        scratch_shapes=[pltpu.VMEM((D_MODEL, N_PAD), bf16), pltpu.SemaphoreType.DMA((1,))],
        compiler_params=pltpu.CompilerParams(dimension_semantics=("arbitrary",),
                                             vmem_limit_bytes=VMEM_LIMIT),
        name="in_proj",
    )(x, *[p[k] for k in names])


RW_TB = 512
RW_C = 64
RW_PRO = 2
RW_TICKS = 13


def _rwkv_kernel(*refs, has_vres):
    if has_vres:
        (c_ref, m_ref, vf_ref, mu_ref, w0_ref, wup_ref, a0_ref, aup_ref, gup_ref, kk_ref, ka_ref, rk_ref,
         lw_ref, lb_ref, v0_ref, vup_ref, y_ref,
         carry_ref, s_ref, r_s, k_s, v_s, lw_s, al_s, be_s, y_s,
         ta_p, rt_p, vb_p, bw_p, kw_p, arb_p, tv_p, av_p, bonus_p, g_p, dl_p) = refs
    else:
        (c_ref, mu_ref, w0_ref, wup_ref, a0_ref, aup_ref, gup_ref, kk_ref, ka_ref, rk_ref,
         lw_ref, lb_ref, y_ref, vf_out_ref,
         carry_ref, s_ref, r_s, k_s, v_s, lw_s, al_s, be_s, y_s,
         ta_p, rt_p, vb_p, bw_p, kw_p, arb_p, tv_p, av_p, bonus_p, g_p, dl_p) = refs
    step = pl.program_id(0)

    @pl.when(step == 0)
    def _():
        for ref in (carry_ref, s_ref, ta_p, rt_p, vb_p, bw_p, kw_p, arb_p, tv_p, av_p, bonus_p, g_p, dl_p):
            ref[...] = jnp.zeros_like(ref)

    wr = step % 2
    rd = 1 - wr
    c = RW_C
    nchunk = RW_TB // c
    npair = N_HEADS // 2
    hsl = lambda h: slice(2 * h * HEAD_DIM, 2 * (h + 1) * HEAD_DIM)
    bd_mask = (_iota((2 * c, 2 * c), 0) // c == _iota((2 * c, 2 * c), 1) // c).astype(f32)
    state = [s_ref[h] for h in range(npair)]

    def recurrence():
        for n in range(nchunk):
            rows = slice(n * c, (n + 1) * c)
            sas = [lax.dot_general(jnp.concatenate([ta_p[rd, rows, hsl(h)], rt_p[rd, rows, hsl(h)]], axis=0),
                                   state[h].astype(bf16), NT, preferred_element_type=f32)
                   for h in range(npair)]
            yield
            us = [sas[h][0:c] + tv_p[rd, rows, hsl(h)] for h in range(npair)]
            upd = [_dot(jnp.concatenate([us[h].astype(bf16), vb_p[rd, rows, hsl(h)]], axis=0),
                        jnp.concatenate([bw_p[rd, rows, hsl(h)], kw_p[rd, rows, hsl(h)]], axis=0), TN) * bd_mask
                   for h in range(npair)]
            yield
            dl = dl_p[rd, n:n + 1, :]
            for h in range(npair):
                state[h] = state[h] * dl[:, hsl(h)] + upd[h]
            ys = [sas[h][c:] + _dot(arb_p[rd, rows, hsl(h)], _bd(us[h])) + av_p[rd, rows, hsl(h)]
                  for h in range(npair)]
            y_s[rows, :] = jnp.concatenate(ys, axis=1)
            yield

    chain = recurrence()
    calls = [0]

    def advance(pieces=1):
        for _ in range(pieces):
            next(chain, None)

    def tick():
        calls[0] += 1
        if calls[0] % RW_TICKS == 0:
            advance()

    ones_bd = _group_ones(GROUP_W, HEAD_DIM)
    x = c_ref[:, 0:1024]
    x_prev = _shift_rows(x, carry_ref[...], 1)
    carry_ref[...] = x[RW_TB - 8:RW_TB]
    xs = x + (x_prev - x) * mu_ref[...]
    advance(RW_PRO)
    r = xs[:, 0:256]
    k = xs[:, 256:512]
    v = xs[:, 512:768]
    lora = xs[:, 768:896]
    w_pre = w0_ref[...] + _dot(jnp.tanh(lora), wup_ref[...])
    lw = -jnp.exp(-_softplus(-w_pre) - 0.5)
    advance(RW_PRO)
    a = _sigmoid(a0_ref[...] + _dot(lora, aup_ref[...]))
    g_p[wr] = _dot(_sigmoid(xs[:, 896:1024]), gup_ref[...])
    advance(RW_PRO)
    if has_vres:
        mix = _sigmoid(v0_ref[...] + _dot(m_ref[...], vup_ref[...]))
        v = v + (vf_ref[...] - v) * mix
    else:
        @pl.when(step < SEQ // RW_TB)
        def _():
            vf_out_ref[...] = v
    kk = k * kk_ref[...]
    kk = kk * lax.rsqrt(_segsum(kk * kk, ones_bd) + 1e-24)
    advance(RW_PRO)
    k = k * (1.0 + (a - 1.0) * ka_ref[...])
    bonus_p[wr] = _segsum(r * k * rk_ref[...], ones_bd) * v
    advance(RW_PRO)
    r_s[...] = r
    k_s[...] = k
    v_s[...] = v
    lw_s[...] = lw
    al_s[...] = -kk
    be_s[...] = kk * a

    tril_incl = (_iota((c, c), 0) >= _iota((c, c), 1)).astype(bf16)
    masks = _pair_masks(c)
    low_strict, low_incl = masks["strict"], masks["incl"]

    items = [(n, h) for n in range(nchunk) for h in range(npair)]
    pre = []
    for n in range(nchunk):
        rows = slice(n * c, (n + 1) * c)
        lwc = lw_s[rows, :]
        lc = _dot_sel_lhs(tril_incl, lwc)
        llast = lc[c - 1:c, :]
        e_out = jnp.exp(-lc)
        e_rest = jnp.exp(llast - lc)
        kc, bec = k_s[rows, :], be_s[rows, :]
        rt = r_s[rows, :] * jnp.exp(lc)
        pre.append(dict(rt=rt, at=al_s[rows, :] * jnp.exp(lc - lwc), bt=bec * e_out, kt=kc * e_out,
                        v=v_s[rows, :]))
        rt_p[wr, rows, :] = rt.astype(bf16)
        vb_p[wr, rows, :] = v_s[rows, :].astype(bf16)
        bw_p[wr, rows, :] = (bec * e_rest).astype(bf16)
        kw_p[wr, rows, :] = (kc * e_rest).astype(bf16)
        dl_p[wr, n:n + 1, :] = jnp.exp(llast)
        tick()
    ms = []
    for n, h in items:
        ms.append(_dot(jnp.concatenate([pre[n]["at"][:, hsl(h)], pre[n]["rt"][:, hsl(h)]], axis=0),
                       jnp.concatenate([_bd(pre[n]["bt"][:, hsl(h)]), _bd(pre[n]["kt"][:, hsl(h)])], axis=0),
                       NT))
        arb_p[wr, n * c:(n + 1) * c, hsl(h)] = jnp.where(low_incl, ms[-1][c:, 0:2 * c], 0.0).astype(bf16)
        tick()
    tms = _tri_inv([jnp.where(low_strict, -m[0:c, 0:2 * c], 0.0) for m in ms], masks, tick)
    avs = []
    for m, (n, h) in zip(ms, items):
        avs.append(_dot(jnp.concatenate([jnp.where(low_strict, m[0:c, 2 * c:], 0.0),
                                         jnp.where(low_incl, m[c:, 2 * c:], 0.0)], axis=0),
                        _bd(pre[n]["v"][:, hsl(h)])))
        av_p[wr, n * c:(n + 1) * c, hsl(h)] = avs[-1][c:]
        tick()
    for tm, av, (n, h) in zip(tms, avs, items):
        tt = _dot(tm, jnp.concatenate([_bd(pre[n]["at"][:, hsl(h)]), _bd(av[0:c])], axis=1))
        ta_p[wr, n * c:(n + 1) * c, hsl(h)] = tt[:, 0:2 * c].astype(bf16)
        tv_p[wr, n * c:(n + 1) * c, hsl(h)] = tt[:, 2 * c:]
        tick()
    for _ in chain:
        pass
    for h in range(npair):
        s_ref[h] = state[h]

    y = y_s[...]
    inv_d = 1.0 / HEAD_DIM
    mean = _segsum(y, ones_bd) * inv_d
    yc = y - mean
    var = _segsum(yc * yc, ones_bd) * inv_d
    y = yc * lax.rsqrt(var + RWKV_GN_EPS) * lw_ref[...] + lb_ref[...]
    y_ref[...] = ((y + bonus_p[rd]) * g_p[rd]).astype(bf16)


def _rwkv(c_a, c_m, v_first, p, l):
    has_vres = v_first is not None
    nblk = SEQ // RW_TB
    blk = lambda n: pl.BlockSpec((RW_TB, n), lambda i: (jnp.minimum(i, nblk - 1), 0))
    blk_prev = lambda n: pl.BlockSpec((RW_TB, n), lambda i: (jnp.maximum(i - 1, 0), 0))
    in_specs = [blk(NA)]
    args = [c_a]
    if has_vres:
        in_specs += [blk(NM), blk(GROUP_W)]
        args += [c_m, v_first]
    names = ("mu", "w0", "w_up", "a0", "a_up", "g_up", "k_k", "k_a", "r_k", "lnx_w", "lnx_b")
    in_specs += [_layer_spec(p[k], l) for k in names]
    args += [p[k] for k in names]
    if has_vres:
        in_specs += [_layer_spec(p["v0"], l - 1), _layer_spec(p["vres_up"], l - 1)]
        args += [p["v0"], p["vres_up"]]
        out_specs = blk_prev(GROUP_W)
        out_shape = jax.ShapeDtypeStruct((SEQ, GROUP_W), bf16)
    else:
        out_specs = [blk_prev(GROUP_W), blk(GROUP_W)]
        out_shape = [jax.ShapeDtypeStruct((SEQ, GROUP_W), bf16),
                     jax.ShapeDtypeStruct((SEQ, GROUP_W), f32)]
    scratch = [pltpu.VMEM((8, 1024), f32), pltpu.VMEM((N_HEADS // 2, 2 * HEAD_DIM, 2 * HEAD_DIM), f32)]
    scratch += [pltpu.VMEM((RW_TB, GROUP_W), f32) for _ in range(7)]
    scratch += [pltpu.VMEM((2, RW_TB, GROUP_W), bf16) for _ in range(6)]
    scratch += [pltpu.VMEM((2, RW_TB, GROUP_W), f32) for _ in range(4)]
    scratch += [pltpu.VMEM((2, RW_TB // RW_C, GROUP_W), f32)]
    out = pl.pallas_call(
        functools.partial(_rwkv_kernel, has_vres=has_vres),
        grid=(nblk + 1,),
        in_specs=in_specs,
        out_specs=out_specs,
        out_shape=out_shape,
        scratch_shapes=scratch,
        compiler_params=pltpu.CompilerParams(dimension_semantics=("arbitrary",),
                                             vmem_limit_bytes=VMEM_LIMIT),
        name="rwkv7",
    )(*args)
    if has_vres:
        return out, v_first
    return out[0], out[1]


GD_TB = 512
GD_C = 64
GD_PRO = 2
GD_TICKS = 9


def _gdn_kernel(c_ref, m_ref, cw_ref, alog_ref, dtb_ref, alogc_ref, dtbc_ref, ng_ref, y_ref,
                carry_ref, s_ref, q_s, k_s, v_s, be_s, g_s, o_s,
                u_p, w_p, qe_p, kd_p, qk_p, gz_p, egl_p):
    step = pl.program_id(0)

    @pl.when(step == 0)
    def _():
        for ref in (carry_ref, s_ref, u_p, w_p, qe_p, kd_p, qk_p, gz_p, egl_p):
            ref[...] = jnp.zeros_like(ref)

    wr = step % 2
    rd = 1 - wr
    c = GD_C
    nchunk = GD_TB // c
    npair = N_HEADS // 2
    hsl = lambda h: slice(2 * h * HEAD_DIM, 2 * (h + 1) * HEAD_DIM)
    bd_mask = (_iota((2 * c, 2 * c), 0) // c == _iota((2 * c, 2 * c), 1) // c).astype(f32)
    ones_bd = _group_ones(GROUP_W, HEAD_DIM)
    state = [s_ref[h] for h in range(npair)]

    def recurrence():
        for n in range(nchunk):
            rows = slice(n * c, (n + 1) * c)
            wss = [jnp.dot(jnp.concatenate([w_p[rd, rows, hsl(h)], qe_p[rd, rows, hsl(h)]], axis=0),
                           state[h].astype(bf16), preferred_element_type=f32)
                   for h in range(npair)]
            yield
            vns = [u_p[rd, rows, hsl(h)] - wss[h][0:c] for h in range(npair)]
            upd = [_dot(kd_p[rd, rows, hsl(h)], vns[h], TN) * bd_mask for h in range(npair)]
            yield
            egl = egl_p[rd, n:n + 1, :]
            for h in range(npair):
                state[h] = state[h] * egl[:, hsl(h)] + upd[h]
            os_ = [wss[h][c:] + _dot(qk_p[rd, rows, hsl(h)], _bd(vns[h])) for h in range(npair)]
            o_s[rows, :] = jnp.concatenate(os_, axis=1)
            yield

    chain = recurrence()
    calls = [0]

    def tick():
        calls[0] += 1
        if calls[0] % GD_TICKS == 0:
            next(chain, None)

    raw = c_ref[:, 0:768]
    carry = carry_ref[...]
    conv = raw * cw_ref[3:4, :]
    for s in range(1, 4):
        conv = conv + _shift_rows(raw, carry, s) * cw_ref[3 - s:4 - s, :]
    carry_ref[...] = raw[GD_TB - 8:GD_TB]
    qkv = conv * _sigmoid(conv)
    for _ in range(GD_PRO):
        next(chain, None)
    q = qkv[:, 0:256]
    k = qkv[:, 256:512]
    q_s[...] = q * lax.rsqrt(_segsum(q * q, ones_bd) + 1e-6) * (HEAD_DIM ** -0.5)
    for _ in range(GD_PRO):
        next(chain, None)
    k_s[...] = k * lax.rsqrt(_segsum(k * k, ones_bd) + 1e-6)
    v_s[...] = qkv[:, 512:768]
    z = c_ref[:, 768:1024]
    gz_p[wr] = z * _sigmoid(z)
    for _ in range(GD_PRO):
        next(chain, None)
    small = m_ref[...]
    er, ec = _iota((NM, GROUP_W), 0), _iota((NM, GROUP_W), 1)
    b_exp = _dot_sel_rhs(small, (er == ec // HEAD_DIM + M_GDN).astype(bf16))
    a_exp = _dot_sel_rhs(small, (er == ec // HEAD_DIM + M_GDN + N_HEADS).astype(bf16))
    be_s[...] = _sigmoid(b_exp)
    g_s[...] = -jnp.exp(alog_ref[...]) * _softplus(a_exp + dtb_ref[...])

    tril_incl = (_iota((c, c), 0) >= _iota((c, c), 1)).astype(bf16)
    masks = _pair_masks(c)
    low_strict, low_incl = masks["strict"], masks["incl"]
    sel8 = (_iota((8, NM), 0) + M_GDN == _iota((8, NM), 1)).astype(bf16)
    g_rows = -jnp.exp(alogc_ref[...]) * _softplus(_dot_sel_lhs(sel8, small, NT) + dtbc_ref[...])
    tj, ti = _iota((GD_TB, GD_TB), 0), _iota((GD_TB, GD_TB), 1)
    gc_rows = _dot_sel_rhs(g_rows, ((tj // c == ti // c) & (tj <= ti)).astype(bf16))

    items = [(n, h) for n in range(nchunk) for h in range(npair)]
    pre = []
    for n in range(nchunk):
        rows = slice(n * c, (n + 1) * c)
        gc = _dot_sel_lhs(tril_incl, g_s[rows, :])
        glast = gc[c - 1:c, :]
        egc = jnp.exp(gc)
        kc, bc = k_s[rows, :], be_s[rows, :]
        kb = kc * bc
        pre.append(dict(gc=gc, k=kc, kb=kb, q=q_s[rows, :], vb=v_s[rows, :] * bc, kbe=kb * egc))
        qe_p[wr, rows, :] = (q_s[rows, :] * egc).astype(bf16)
        kd_p[wr, rows, :] = (kc * jnp.exp(glast - gc)).astype(bf16)
        egl_p[wr, n:n + 1, :] = jnp.exp(glast)
        tick()

    def gc_row(n, h):
        return jnp.concatenate([gc_rows[N_HEADS + 2 * h + i:N_HEADS + 2 * h + i + 1, n * c:(n + 1) * c]
                                for i in range(2)], axis=1)

    dms, aqs = [], []
    for n, h in items:
        dms.append(jnp.exp(jnp.where(low_incl, pre[n]["gc"][:, hsl(h)] - gc_row(n, h), NEG)))
        aqs.append(_dot(jnp.concatenate([pre[n]["kb"][:, hsl(h)], pre[n]["q"][:, hsl(h)]], axis=0),
                        _bd(pre[n]["k"][:, hsl(h)]), NT))
        qk_p[wr, n * c:(n + 1) * c, hsl(h)] = (aqs[-1][c:] * dms[-1]).astype(bf16)
        tick()
    tms = _tri_inv([jnp.where(low_strict, aq[0:c] * dm, 0.0) for aq, dm in zip(aqs, dms)], masks, tick)
    for tm, (n, h) in zip(tms, items):
        uw = _dot(tm, jnp.concatenate([_bd(pre[n]["vb"][:, hsl(h)]), _bd(pre[n]["kbe"][:, hsl(h)])], axis=1))
        u_p[wr, n * c:(n + 1) * c, hsl(h)] = uw[:, 0:2 * c]
        w_p[wr, n * c:(n + 1) * c, hsl(h)] = uw[:, 2 * c:].astype(bf16)
        tick()
    for _ in chain:
        pass
    for h in range(npair):
        s_ref[h] = state[h]

    o = o_s[...]
    ms = _segsum(o * o, ones_bd) * (1.0 / HEAD_DIM)
    y_ref[...] = (o * lax.rsqrt(ms + EPS) * ng_ref[...] * gz_p[rd]).astype(bf16)


def _gdn(c_b, c_m, p, l):
    names = ("conv_w", "a_log", "dt_bias", "a_log_col", "dt_bias_col", "norm_g")
    nblk = SEQ // GD_TB
    scratch = [pltpu.VMEM((8, 768), f32), pltpu.VMEM((N_HEADS // 2, 2 * HEAD_DIM, 2 * HEAD_DIM), f32)]
    scratch += [pltpu.VMEM((GD_TB, GROUP_W), f32) for _ in range(6)]
    scratch += [pltpu.VMEM((2, GD_TB, GROUP_W), f32)]
    scratch += [pltpu.VMEM((2, GD_TB, GROUP_W), bf16) for _ in range(4)]
    scratch += [pltpu.VMEM((2, GD_TB, GROUP_W), f32), pltpu.VMEM((2, GD_TB // GD_C, GROUP_W), f32)]
    return pl.pallas_call(
        _gdn_kernel,
        grid=(nblk + 1,),
        in_specs=[pl.BlockSpec((GD_TB, n), lambda i: (jnp.minimum(i, nblk - 1), 0)) for n in (NB, NM)]
        + [_layer_spec(p[k], l) for k in names],
        out_specs=pl.BlockSpec((GD_TB, GROUP_W), lambda i: (jnp.maximum(i - 1, 0), 0)),
        out_shape=jax.ShapeDtypeStruct((SEQ, GROUP_W), bf16),
        scratch_shapes=scratch,
        compiler_params=pltpu.CompilerParams(dimension_semantics=("arbitrary",),
                                             vmem_limit_bytes=VMEM_LIMIT),
        name="gdn",
    )(c_b, c_m, *[p[k] for k in names])


GL_TB = 256
GL_C = 16
GL_S = 8


def _gla_kernel(c_ref, m_ref, gup_ref, gb_ref, ng_ref, y_ref, st_ref, sx_s, o_s):
    @pl.when(pl.program_id(0) == 0)
    def _():
        st_ref[...] = jnp.zeros_like(st_ref)

    tb, c, s = GL_TB, GL_C, GL_S
    nchunk, nsub = tb // c, tb // s
    q = c_ref[:, 0:128] * (GLA_HEAD_K ** -0.5)
    k = c_ref[:, 128:256]
    v = c_ref[:, 256:512]
    pre = _dot(m_ref[...], gup_ref[...]) + gb_ref[...]
    la = -_softplus(-pre) * (1.0 / 16.0)
    tj, ti = _iota((tb, tb), 0), _iota((tb, tb), 1)
    b = _dot_sel_lhs(((tj // c == ti // c) & (ti <= tj)).astype(bf16), la)
    qi = q * jnp.exp(b)

    ind_e = (_iota((GLA_KEY, GROUP_W), 0) // GLA_HEAD_K == _iota((GLA_KEY, GROUP_W), 1) // HEAD_DIM).astype(bf16)
    bd_mask = (_iota((GROUP_W, GLA_KEY), 0) // HEAD_DIM == _iota((GROUP_W, GLA_KEY), 1) // GLA_HEAD_K).astype(f32)

    b3, q3, k3 = (t.reshape(nsub, s, GLA_KEY) for t in (b, q, k))
    ri = _iota((nsub, s, GLA_KEY), 1)
    terms = []
    for j in range(s):
        e = jnp.exp(jnp.where(ri >= j, b3 - b3[:, j:j + 1, :], NEG))
        terms.append((q3 * (k3[:, j:j + 1, :] * e)).reshape(tb, GLA_KEY).astype(bf16))
    sx_s[...] = jnp.dot(jnp.concatenate(terms, axis=0), ind_e, preferred_element_type=f32)
    v3 = v.reshape(nsub, s, GROUP_W)
    o3 = sx_s[0:tb, :].reshape(nsub, s, GROUP_W) * v3[:, 0:1, :]
    for j in range(1, s):
        o3 = o3 + sx_s[j * tb:(j + 1) * tb, :].reshape(nsub, s, GROUP_W) * v3[:, j:j + 1, :]

    b4, q4, k4 = (t.reshape(nchunk, 2, s, GLA_KEY) for t in (b, q, k))
    bref = b4[:, 0, s - 1:s, :]
    qd = (q4[:, 1] * jnp.exp(b4[:, 1] - bref)).reshape(nchunk * s, GLA_KEY)
    kd = (k4[:, 0] * jnp.exp(bref - b4[:, 0])).reshape(nchunk * s, GLA_KEY)
    v0 = v.reshape(nchunk, 2, s, GROUP_W)[:, 0].reshape(nchunk * s, GROUP_W)
    head_k = _iota((nchunk * s, GLA_KEY), 1) // GLA_HEAD_K
    head_v = _iota((nchunk * s, GROUP_W), 1) // HEAD_DIM
    ks = jnp.concatenate([jnp.where(head_k == h, kd, 0.0) for h in range(N_HEADS)], axis=0)
    vs = jnp.concatenate([jnp.where(head_v == h, v0, 0.0) for h in range(N_HEADS)], axis=0)
    sc = _dot(qd, ks, NT)
    sr, scol = _iota(sc.shape, 0), _iota(sc.shape, 1)
    sc = jnp.where(sr // s == (scol % (nchunk * s)) // s, sc, 0.0)
    o_off = _dot(sc, vs).reshape(nchunk, 1, s, GROUP_W)
    o4 = o3.reshape(nchunk, 2, s, GROUP_W)
    o_intra = jnp.concatenate([o4[:, 0:1], o4[:, 1:2] + o_off], axis=1).reshape(tb, GROUP_W)

    blasts = [b[(n + 1) * c - 1:(n + 1) * c, :] for n in range(nchunk)]
    upds = [_dot(v[n * c:(n + 1) * c], k[n * c:(n + 1) * c] * jnp.exp(blasts[n] - b[n * c:(n + 1) * c]), TN)
            * bd_mask for n in range(nchunk)]
    st = st_ref[...]
    for n in range(nchunk):
        rows = slice(n * c, (n + 1) * c)
        o_s[rows, :] = o_intra[rows] + _dot(qi[rows], st, NT)
        st = st * jnp.exp(blasts[n]) + upds[n]
    st_ref[...] = st

    o = o_s[...]
    ms = _segsum(o * o, _group_ones(GROUP_W, HEAD_DIM)) * (1.0 / HEAD_DIM)
    gate = c_ref[:, 512:768]
    y_ref[...] = (o * lax.rsqrt(ms + EPS) * ng_ref[...] * (gate * _sigmoid(gate))).astype(bf16)


def _gla(c_c, c_m, p, l):
    names = ("gk_up", "gk_bias", "norm_g")
    return pl.pallas_call(
        _gla_kernel,
        grid=(SEQ // GL_TB,),
        in_specs=[pl.BlockSpec((GL_TB, n), lambda i: (i, 0)) for n in (NC, NM)]
        + [_layer_spec(p[k], l) for k in names],
        out_specs=pl.BlockSpec((GL_TB, GROUP_W), lambda i: (i, 0)),
        out_shape=jax.ShapeDtypeStruct((SEQ, GROUP_W), bf16),
        scratch_shapes=[pltpu.VMEM((GROUP_W, GLA_KEY), f32),
                        pltpu.VMEM((GL_TB * GL_S, GROUP_W), f32), pltpu.VMEM((GL_TB, GROUP_W), f32)],
        compiler_params=pltpu.CompilerParams(dimension_semantics=("arbitrary",),
                                             vmem_limit_bytes=VMEM_LIMIT),
        name="gla",
    )(c_c, c_m, *[p[k] for k in names])


FF_TM = 512
FF_TF = 1024


def _outffn_kernel(*refs, final, layer):
    if final:
        (x_ref, ya_ref, yb_ref, yc_ref, yd_ref, wo_hbm, g_ref, wu_hbm, wd_hbm, gf_ref, o_ref,
         wo_ref, wu_ref, wd_ref, sem) = refs
    else:
        (x_ref, ya_ref, yb_ref, yc_ref, yd_ref, wo_hbm, g_ref, wu_hbm, wd_hbm, o_ref,
         wo_ref, wu_ref, wd_ref, sem) = refs

    @pl.when(pl.program_id(0) == 0)
    def _():
        copies = [pltpu.make_async_copy(src.at[layer], dst, sem.at[i])
                  for i, (src, dst) in enumerate(((wo_hbm, wo_ref), (wu_hbm, wu_ref), (wd_hbm, wd_ref)))]
        for cp in copies:
            cp.start()
        for cp in copies:
            cp.wait()

    y = jnp.concatenate([ya_ref[...], yb_ref[...], yc_ref[...], yd_ref[...]], axis=1)
    x1 = x_ref[...] + jnp.dot(y, wo_ref[...], preferred_element_type=f32)
    ms = jnp.mean(x1 * x1, axis=-1, keepdims=True)
    h = (x1 * lax.rsqrt(ms + EPS) * g_ref[...]).astype(bf16)
    x2 = x1
    for kf in range(D_FF // FF_TF):
        cols = slice(kf * FF_TF, (kf + 1) * FF_TF)
        hid = jnp.maximum(jnp.dot(h, wu_ref[:, cols], preferred_element_type=f32), 0.0)
        x2 = x2 + jnp.dot((hid * hid).astype(bf16), wd_ref[cols, :], preferred_element_type=f32)
    if final:
        ms = jnp.mean(x2 * x2, axis=-1, keepdims=True)
        x2 = x2 * lax.rsqrt(ms + EPS) * gf_ref[...]
    o_ref[...] = x2


def _out_ffn(x, ys, p, l, final):
    hbm = pl.BlockSpec(memory_space=pl.ANY)
    yspec = pl.BlockSpec((FF_TM, GROUP_W), lambda i: (i, 0))
    in_specs = [pl.BlockSpec((FF_TM, D_MODEL), lambda i: (i, 0)), yspec, yspec, yspec, yspec,
                hbm, _layer_spec(p["g"], l), hbm, hbm]
    args = [x, *ys, p["w_out"], p["g"], p["w_up"], p["w_down"]]
    if final:
        in_specs.append(pl.BlockSpec((1, D_MODEL), lambda i: (0, 0)))
        args.append(p["g_final"])
    return pl.pallas_call(
        functools.partial(_outffn_kernel, final=final, layer=l),
        grid=(SEQ // FF_TM,),
        in_specs=in_specs,
        out_specs=pl.BlockSpec((FF_TM, D_MODEL), lambda i: (i, 0)),
        out_shape=jax.ShapeDtypeStruct((SEQ, D_MODEL), f32),
        scratch_shapes=[pltpu.VMEM((D_MODEL, D_MODEL), bf16), pltpu.VMEM((D_MODEL, D_FF), bf16),
                        pltpu.VMEM((D_FF, D_MODEL), bf16), pltpu.SemaphoreType.DMA((3,))],
        compiler_params=pltpu.CompilerParams(dimension_semantics=("arbitrary",),
                                             vmem_limit_bytes=VMEM_LIMIT),
        name="out_ffn",
    )(*args)


def kernel(x, w_in, w_out, norm_mix_g, norm_ffn_g, norm_final_g, rwkv_mu, rwkv_w0, rwkv_w_up, rwkv_a0, rwkv_a_up, rwkv_g_up, rwkv_k_k, rwkv_k_a, rwkv_r_k, rwkv_lnx_w, rwkv_lnx_b, rwkv_v0, rwkv_vres_down, rwkv_vres_up, gdn_conv_w, gdn_a_log, gdn_dt_bias, gdn_norm_g, gla_gk_up, gla_gk_bias, gla_norm_g, sgu_ln_g, sgu_ln_b, sgu_w_s, sgu_b_s, ffn_w_up, ffn_w_down):
    depth = w_in.shape[0]
    row = lambda a: a.reshape(depth, 1, -1)
    per_head = lambda a: jnp.repeat(a, HEAD_DIM, axis=-1).reshape(depth, 1, -1)
    pad_cols = lambda w, n: jnp.pad(w, ((0, 0), (0, 0), (0, n - w.shape[2])))
    pad_rows = lambda w, top, total: jnp.pad(w, ((0, 0), (top, total - top - w.shape[1]), (0, 0)))
    vres_down = jnp.pad(rwkv_vres_down, ((1, 0), (0, 0), (0, 0)))
    narrow = pad_cols(jnp.concatenate([vres_down, w_in[:, :, 2048:2056], w_in[:, :, 2824:2840]], axis=2), NM)
    w_comb = jnp.concatenate(
        [w_in[:, :, 0:1024], w_in[:, :, 1024:2048], w_in[:, :, 2056:2824], narrow,
         w_in[:, :, 2840:3352]], axis=2).astype(bf16)
    p_in = dict(g=row(norm_mix_g), w=w_comb, ln_g=row(sgu_ln_g), ln_b=row(sgu_ln_b),
                w_cat=sgu_w_s.transpose(0, 2, 1, 3).reshape(depth, SG_C, 4 * SG_C),
                bias_tile=jnp.repeat(sgu_b_s.transpose(0, 2, 1), HEAD_DIM, axis=2))
    p_rwkv = dict(mu=row(rwkv_mu), w0=row(rwkv_w0), w_up=pad_rows(rwkv_w_up, 0, 128),
                  a0=row(rwkv_a0), a_up=pad_rows(rwkv_a_up, 64, 128), g_up=rwkv_g_up,
                  k_k=row(rwkv_k_k), k_a=row(rwkv_k_a), r_k=row(rwkv_r_k),
                  lnx_w=row(rwkv_lnx_w), lnx_b=row(rwkv_lnx_b),
                  v0=rwkv_v0.reshape(depth - 1, 1, -1), vres_up=pad_rows(rwkv_vres_up, M_VRES, NM))
    p_gdn = dict(conv_w=gdn_conv_w, a_log=per_head(gdn_a_log), dt_bias=per_head(gdn_dt_bias),
                 a_log_col=jnp.pad(gdn_a_log, ((0, 0), (N_HEADS, 0))).reshape(depth, 8, 1),
                 dt_bias_col=jnp.pad(gdn_dt_bias, ((0, 0), (N_HEADS, 0))).reshape(depth, 8, 1),
                 norm_g=row(jnp.tile(gdn_norm_g, (1, N_HEADS))))
    p_gla = dict(gk_up=pad_rows(gla_gk_up, M_GLA, NM), gk_bias=row(gla_gk_bias),
                 norm_g=row(jnp.tile(gla_norm_g, (1, N_HEADS))))
    p_ffn = dict(w_out=w_out.astype(bf16), g=row(norm_ffn_g), w_up=ffn_w_up.astype(bf16),
                 w_down=ffn_w_down.astype(bf16), g_final=norm_final_g.reshape(1, -1))

    xx = x[0]
    v_first = None
    for l in range(depth):
        c_a, c_b, c_c, c_m, y_d = _in_proj(xx, p_in, l)
        y_a, v_first = _rwkv(c_a, c_m, v_first, p_rwkv, l)
        y_b = _gdn(c_b, c_m, p_gdn, l)
        y_c = _gla(c_c, c_m, p_gla, l)
        xx = _out_ffn(xx, (y_a, y_b, y_c, y_d), p_ffn, l, final=(l == depth - 1))
    return xx[None]
```

```python
import functools

import jax
import jax.numpy as jnp
from jax import lax
from jax.experimental import pallas as pl
from jax.experimental.pallas import tpu as pltpu

f32 = jnp.float32
bf16 = jnp.bfloat16

SEQ = 16384
D_MODEL = 1024
GROUP_W = 256
HEAD_DIM = 64
N_HEADS = 4
GLA_KEY = 128
GLA_HEAD_K = 32
D_FF = 4096
EPS = 1e-6
RWKV_GN_EPS = 64e-5
NEG = -1e30

NA, NB, NC, NM, ND = 1024, 1024, 768, 128, 512
N_PAD = NA + NB + NC + NM + ND
M_VRES, M_GDN, M_GLA = 0, 32, 40

VMEM_LIMIT = 56 * 1024 * 1024

NN = (((1,), (0,)), ((), ()))
NT = (((1,), (1,)), ((), ()))
TN = (((0,), (0,)), ((), ()))


def _dot(a, b, dims=NN):
    return lax.dot_general(a.astype(bf16), b.astype(bf16), dims, preferred_element_type=f32)


def _iota(shape, axis):
    return lax.broadcasted_iota(jnp.int32, shape, axis)


def _layer_spec(a, l):
    return pl.BlockSpec((None,) + a.shape[1:], lambda *_: (l,) + (0,) * (a.ndim - 1))


def _segsum(x, ones_bd):
    return jnp.dot(x.astype(bf16), ones_bd, preferred_element_type=f32)


def _split2(x):
    hi = x.astype(bf16)
    lo = (x - hi.astype(f32)).astype(bf16)
    return hi, lo


def _dot_sel_lhs(sel, x, dims=NN):
    return sum(lax.dot_general(sel, t, dims, preferred_element_type=f32) for t in _split2(x))


def _dot_sel_rhs(x, sel, dims=NN):
    return sum(lax.dot_general(t, sel, dims, preferred_element_type=f32) for t in _split2(x))


def _group_ones(n, width):
    return (_iota((n, n), 0) // width == _iota((n, n), 1) // width).astype(bf16)


def _sigmoid(x):
    return 1.0 / (1.0 + jnp.exp(-x))


def _softplus(x):
    return jnp.maximum(x, 0.0) + jnp.log1p(jnp.exp(-jnp.abs(x)))


def _shift_rows(x, carry, s):
    xs = pltpu.roll(x, s, 0)
    fix = pltpu.roll(carry, s, 0)
    first = jnp.where(_iota(carry.shape, 0) < s, fix, xs[0:8])
    return jnp.concatenate([first, xs[8:]], axis=0)


def _bd(xp):
    xb = xp.astype(bf16)
    left = _iota(xb.shape, 1) < HEAD_DIM
    zero = jnp.zeros_like(xb)
    return jnp.concatenate([jnp.where(left, xb, zero), jnp.where(left, zero, xb)], axis=0)


def _pair_masks(c):
    ri, cj = _iota((c, 2 * c), 0), _iota((c, 2 * c), 1) & (c - 1)
    eye = (ri == cj).astype(f32)
    m16 = (ri // 16 == cj // 16).astype(f32)
    mo1 = ((ri // 32 == cj // 32) & (ri // 16 == cj // 16 + 1)).astype(f32)
    mo2 = ((ri // 32 == 1) & (cj // 32 == 0)).astype(f32)
    return dict(eye=eye, m16=m16, mo1=mo1, mo2=mo2, strict=ri > cj, incl=ri >= cj)


def _tri_inv(lms, masks, tick=lambda: None):
    c = RW_C

    def each(fn, *lists):
        out = []
        for args in zip(*lists):
            out.append(fn(*args))
            tick()
        return out

    ps = [-(lm * masks["m16"]) for lm in lms]
    ts = [masks["eye"] + p for p in ps]
    ps = each(lambda p: _dot(p, _bd(p)), ps)
    for _ in range(2):
        outs = each(lambda t, p: _dot(jnp.concatenate([t, p], axis=0), _bd(p)), ts, ps)
        ts = [t + o[0:c] for t, o in zip(ts, outs)]
        ps = [o[c:] for o in outs]
    ts = each(lambda t, p: t + _dot(t, _bd(p)), ts, ps)
    for mo in (masks["mo1"], masks["mo2"]):
        xs = each(lambda lm, t: _dot(lm * mo, _bd(t)), lms, ts)
        ts = each(lambda t, x: t - _dot(t, _bd(x)), ts, xs)
    return ts


IN_TM = 1024
SG_C = 128


def _sgu_mix(x, lg_ref, lb_ref, w_ref, bias_ref):
    gx = 0.5 * x * (1.0 + jnp.tanh(0.7978845608028654 * (x + 0.044715 * x * x * x)))
    u = gx[:, 0:256]
    v = gx[:, 256:512]
    mu = jnp.mean(v, axis=-1, keepdims=True)
    vc = v - mu
    var = jnp.mean(vc * vc, axis=-1, keepdims=True)
    v = vc * lax.rsqrt(var + 1e-5) * lg_ref[...] + lb_ref[...]
    wr, wc = _iota((SG_C, 4 * SG_C), 0), _iota((SG_C, 4 * SG_C), 1)
    w = jnp.where(wc % SG_C <= wr, w_ref[...], 0.0).astype(bf16)
    lane_g = _iota((SG_C, GROUP_W), 1) // HEAD_DIM
    outs = []
    for n in range(x.shape[0] // SG_C):
        vn = v[n * SG_C:(n + 1) * SG_C, :]
        vst = jnp.concatenate([jnp.where(lane_g == g, vn, 0.0) for g in range(4)], axis=0)
        outs.append(jnp.dot(w, vst.astype(bf16), preferred_element_type=f32) + bias_ref[...])
    return (u * jnp.concatenate(outs, axis=0)).astype(bf16)


def _inproj_kernel(x_ref, g_ref, w_hbm, lg_ref, lb_ref, ws_ref, bias_ref, oa_ref, ob_ref, oc_ref, om_ref,
                   yd_ref, w_ref, sem, *, layer):
    @pl.when(pl.program_id(0) == 0)
    def _():
        cp = pltpu.make_async_copy(w_hbm.at[layer], w_ref, sem.at[0])
        cp.start()
        cp.wait()

    x = x_ref[...]
    ms = jnp.mean(x * x, axis=-1, keepdims=True)
    h = (x * lax.rsqrt(ms + EPS) * g_ref[...]).astype(bf16)
    c_d = jnp.dot(h, w_ref[:, NA + NB + NC + NM:], preferred_element_type=f32)
    yd_ref[...] = _sgu_mix(c_d, lg_ref, lb_ref, ws_ref, bias_ref)
    off = 0
    for o_ref, n in ((oa_ref, NA), (ob_ref, NB), (oc_ref, NC), (om_ref, NM)):
        o_ref[...] = jnp.dot(h, w_ref[:, off:off + n], preferred_element_type=f32)
        off += n


def _in_proj(x, p, l):
    tm = IN_TM
    names = ("g", "w", "ln_g", "ln_b", "w_cat", "bias_tile")
    specs = [pl.BlockSpec(memory_space=pl.ANY) if k == "w" else _layer_spec(p[k], l) for k in names]
    return pl.pallas_call(
        functools.partial(_inproj_kernel, layer=l),
        grid=(SEQ // tm,),
        in_specs=[pl.BlockSpec((tm, D_MODEL), lambda i: (i, 0))] + specs,
        out_specs=[pl.BlockSpec((tm, n), lambda i: (i, 0)) for n in (NA, NB, NC, NM, GROUP_W)],
        out_shape=[jax.ShapeDtypeStruct((SEQ, n), f32) for n in (NA, NB, NC, NM)]
        + [jax.ShapeDtypeStruct((SEQ, GROUP_W), bf16)],
        scratch_shapes=[pltpu.VMEM((D_MODEL, N_PAD), bf16), pltpu.SemaphoreType.DMA((1,))],
        compiler_params=pltpu.CompilerParams(dimension_semantics=("arbitrary",),
                                             vmem_limit_bytes=VMEM_LIMIT),
        name="in_proj",
    )(x, *[p[k] for k in names])


RW_TB = 512
RW_C = 64
RW_PRO = 2
RW_TICKS = 13


def _rwkv_kernel(*refs, has_vres):
    if has_vres:
        (c_ref, m_ref, vf_ref, mu_ref, w0_ref, wup_ref, a0_ref, aup_ref, gup_ref, kk_ref, ka_ref, rk_ref,
         lw_ref, lb_ref, v0_ref, vup_ref, y_ref,
         carry_ref, s_ref, r_s, k_s, v_s, lw_s, al_s, be_s, y_s,
         ta_p, rt_p, vb_p, bw_p, kw_p, arb_p, tv_p, av_p, bonus_p, g_p, dl_p) = refs
    else:
        (c_ref, mu_ref, w0_ref, wup_ref, a0_ref, aup_ref, gup_ref, kk_ref, ka_ref, rk_ref,
         lw_ref, lb_ref, y_ref, vf_out_ref,
         carry_ref, s_ref, r_s, k_s, v_s, lw_s, al_s, be_s, y_s,
         ta_p, rt_p, vb_p, bw_p, kw_p, arb_p, tv_p, av_p, bonus_p, g_p, dl_p) = refs
    step = pl.program_id(0)

    @pl.when(step == 0)
    def _():
        for ref in (carry_ref, s_ref, ta_p, rt_p, vb_p, bw_p, kw_p, arb_p, tv_p, av_p, bonus_p, g_p, dl_p):
            ref[...] = jnp.zeros_like(ref)

    wr = step % 2
    rd = 1 - wr
    c = RW_C
    nchunk = RW_TB // c
    npair = N_HEADS // 2
    hsl = lambda h: slice(2 * h * HEAD_DIM, 2 * (h + 1) * HEAD_DIM)
    bd_mask = (_iota((2 * c, 2 * c), 0) // c == _iota((2 * c, 2 * c), 1) // c).astype(f32)
    state = [s_ref[h] for h in range(npair)]

    def recurrence():
        for n in range(nchunk):
            rows = slice(n * c, (n + 1) * c)
            sas = [lax.dot_general(jnp.concatenate([ta_p[rd, rows, hsl(h)], rt_p[rd, rows, hsl(h)]], axis=0),
                                   state[h].astype(bf16), NT, preferred_element_type=f32)
                   for h in range(npair)]
            yield
            us = [sas[h][0:c] + tv_p[rd, rows, hsl(h)] for h in range(npair)]
            upd = [_dot(jnp.concatenate([us[h].astype(bf16), vb_p[rd, rows, hsl(h)]], axis=0),
                        jnp.concatenate([bw_p[rd, rows, hsl(h)], kw_p[rd, rows, hsl(h)]], axis=0), TN) * bd_mask
                   for h in range(npair)]
            yield
            dl = dl_p[rd, n:n + 1, :]
            for h in range(npair):
                state[h] = state[h] * dl[:, hsl(h)] + upd[h]
            ys = [sas[h][c:] + _dot(arb_p[rd, rows, hsl(h)], _bd(us[h])) + av_p[rd, rows, hsl(h)]
                  for h in range(npair)]
            y_s[rows, :] = jnp.concatenate(ys, axis=1)
            yield

    chain = recurrence()
    calls = [0]

    def advance(pieces=1):
        for _ in range(pieces):
            next(chain, None)

    def tick():
        calls[0] += 1
        if calls[0] % RW_TICKS == 0:
            advance()

    ones_bd = _group_ones(GROUP_W, HEAD_DIM)
    x = c_ref[:, 0:1024]
    x_prev = _shift_rows(x, carry_ref[...], 1)
    carry_ref[...] = x[RW_TB - 8:RW_TB]
    xs = x + (x_prev - x) * mu_ref[...]
    advance(RW_PRO)
    r = xs[:, 0:256]
    k = xs[:, 256:512]
    v = xs[:, 512:768]
    lora = xs[:, 768:896]
    w_pre = w0_ref[...] + _dot(jnp.tanh(lora), wup_ref[...])
    lw = -jnp.exp(-_softplus(-w_pre) - 0.5)
    advance(RW_PRO)
    a = _sigmoid(a0_ref[...] + _dot(lora, aup_ref[...]))
    g_p[wr] = _dot(_sigmoid(xs[:, 896:1024]), gup_ref[...])
    advance(RW_PRO)
    if has_vres:
        mix = _sigmoid(v0_ref[...] + _dot(m_ref[...], vup_ref[...]))
        v = v + (vf_ref[...] - v) * mix
    else:
        @pl.when(step < SEQ // RW_TB)
        def _():
            vf_out_ref[...] = v
    kk = k * kk_ref[...]
    kk = kk * lax.rsqrt(_segsum(kk * kk, ones_bd) + 1e-24)
    advance(RW_PRO)
    k = k * (1.0 + (a - 1.0) * ka_ref[...])
    bonus_p[wr] = _segsum(r * k * rk_ref[...], ones_bd) * v
    advance(RW_PRO)
    r_s[...] = r
    k_s[...] = k
    v_s[...] = v
    lw_s[...] = lw
    al_s[...] = -kk
    be_s[...] = kk * a

    tril_incl = (_iota((c, c), 0) >= _iota((c, c), 1)).astype(bf16)
    masks = _pair_masks(c)
    low_strict, low_incl = masks["strict"], masks["incl"]

    items = [(n, h) for n in range(nchunk) for h in range(npair)]
    pre = []
    for n in range(nchunk):
        rows = slice(n * c, (n + 1) * c)
        lwc = lw_s[rows, :]
        lc = _dot_sel_lhs(tril_incl, lwc)
        llast = lc[c - 1:c, :]
        e_out = jnp.exp(-lc)
        e_rest = jnp.exp(llast - lc)
        kc, bec = k_s[rows, :], be_s[rows, :]
        rt = r_s[rows, :] * jnp.exp(lc)
        pre.append(dict(rt=rt, at=al_s[rows, :] * jnp.exp(lc - lwc), bt=bec * e_out, kt=kc * e_out,
                        v=v_s[rows, :]))
        rt_p[wr, rows, :] = rt.astype(bf16)
        vb_p[wr, rows, :] = v_s[rows, :].astype(bf16)
        bw_p[wr, rows, :] = (bec * e_rest).astype(bf16)
        kw_p[wr, rows, :] = (kc * e_rest).astype(bf16)
        dl_p[wr, n:n + 1, :] = jnp.exp(llast)
        tick()
    ms = []
    for n, h in items:
        ms.append(_dot(jnp.concatenate([pre[n]["at"][:, hsl(h)], pre[n]["rt"][:, hsl(h)]], axis=0),
                       jnp.concatenate([_bd(pre[n]["bt"][:, hsl(h)]), _bd(pre[n]["kt"][:, hsl(h)])], axis=0),
                       NT))
        arb_p[wr, n * c:(n + 1) * c, hsl(h)] = jnp.where(low_incl, ms[-1][c:, 0:2 * c], 0.0).astype(bf16)
        tick()
    tms = _tri_inv([jnp.where(low_strict, -m[0:c, 0:2 * c], 0.0) for m in ms], masks, tick)
    avs = []
    for m, (n, h) in zip(ms, items):
        avs.append(_dot(jnp.concatenate([jnp.where(low_strict, m[0:c, 2 * c:], 0.0),
                                         jnp.where(low_incl, m[c:, 2 * c:], 0.0)], axis=0),
                        _bd(pre[n]["v"][:, hsl(h)])))
        av_p[wr, n * c:(n + 1) * c, hsl(h)] = avs[-1][c:]
        tick()
    for tm, av, (n, h) in zip(tms, avs, items):
        tt = _dot(tm, jnp.concatenate([_bd(pre[n]["at"][:, hsl(h)]), _bd(av[0:c])], axis=1))
        ta_p[wr, n * c:(n + 1) * c, hsl(h)] = tt[:, 0:2 * c].astype(bf16)
        tv_p[wr, n * c:(n + 1) * c, hsl(h)] = tt[:, 2 * c:]
        tick()
    for _ in chain:
        pass
    for h in range(npair):
        s_ref[h] = state[h]

    y = y_s[...]
    inv_d = 1.0 / HEAD_DIM
    mean = _segsum(y, ones_bd) * inv_d
    yc = y - mean
    var = _segsum(yc * yc, ones_bd) * inv_d
    y = yc * lax.rsqrt(var + RWKV_GN_EPS) * lw_ref[...] + lb_ref[...]
    y_ref[...] = ((y + bonus_p[rd]) * g_p[rd]).astype(bf16)


def _rwkv(c_a, c_m, v_first, p, l):
    has_vres = v_first is not None
    nblk = SEQ // RW_TB
    blk = lambda n: pl.BlockSpec((RW_TB, n), lambda i: (jnp.minimum(i, nblk - 1), 0))
    blk_prev = lambda n: pl.BlockSpec((RW_TB, n), lambda i: (jnp.maximum(i - 1, 0), 0))
    in_specs = [blk(NA)]
    args = [c_a]
    if has_vres:
        in_specs += [blk(NM), blk(GROUP_W)]
        args += [c_m, v_first]
    names = ("mu", "w0", "w_up", "a0", "a_up", "g_up", "k_k", "k_a", "r_k", "lnx_w", "lnx_b")
    in_specs += [_layer_spec(p[k], l) for k in names]
    args += [p[k] for k in names]
    if has_vres:
        in_specs += [_layer_spec(p["v0"], l - 1), _layer_spec(p["vres_up"], l - 1)]
        args += [p["v0"], p["vres_up"]]
        out_specs = blk_prev(GROUP_W)
        out_shape = jax.ShapeDtypeStruct((SEQ, GROUP_W), bf16)
    else:
        out_specs = [blk_prev(GROUP_W), blk(GROUP_W)]
        out_shape = [jax.ShapeDtypeStruct((SEQ, GROUP_W), bf16),
                     jax.ShapeDtypeStruct((SEQ, GROUP_W), f32)]
    scratch = [pltpu.VMEM((8, 1024), f32), pltpu.VMEM((N_HEADS // 2, 2 * HEAD_DIM, 2 * HEAD_DIM), f32)]
    scratch += [pltpu.VMEM((RW_TB, GROUP_W), f32) for _ in range(7)]
    scratch += [pltpu.VMEM((2, RW_TB, GROUP_W), bf16) for _ in range(6)]
    scratch += [pltpu.VMEM((2, RW_TB, GROUP_W), f32) for _ in range(4)]
    scratch += [pltpu.VMEM((2, RW_TB // RW_C, GROUP_W), f32)]
    out = pl.pallas_call(
        functools.partial(_rwkv_kernel, has_vres=has_vres),
        grid=(nblk + 1,),
        in_specs=in_specs,
        out_specs=out_specs,
        out_shape=out_shape,
        scratch_shapes=scratch,
        compiler_params=pltpu.CompilerParams(dimension_semantics=("arbitrary",),
                                             vmem_limit_bytes=VMEM_LIMIT),
        name="rwkv7",
    )(*args)
    if has_vres:
        return out, v_first
    return out[0], out[1]


GD_TB = 512
GD_C = 64
GD_PRO = 2
GD_TICKS = 9


def _gdn_kernel(c_ref, m_ref, cw_ref, alog_ref, dtb_ref, alogc_ref, dtbc_ref, ng_ref, y_ref,
                carry_ref, s_ref, q_s, k_s, v_s, be_s, g_s, o_s,
                u_p, w_p, qe_p, kd_p, qk_p, gz_p, egl_p):
    step = pl.program_id(0)

    @pl.when(step == 0)
    def _():
        for ref in (carry_ref, s_ref, u_p, w_p, qe_p, kd_p, qk_p, gz_p, egl_p):
            ref[...] = jnp.zeros_like(ref)

    wr = step % 2
    rd = 1 - wr
    c = GD_C
    nchunk = GD_TB // c
    npair = N_HEADS // 2
    hsl = lambda h: slice(2 * h * HEAD_DIM, 2 * (h + 1) * HEAD_DIM)
    bd_mask = (_iota((2 * c, 2 * c), 0) // c == _iota((2 * c, 2 * c), 1) // c).astype(f32)
    ones_bd = _group_ones(GROUP_W, HEAD_DIM)
    state = [s_ref[h] for h in range(npair)]

    def recurrence():
        for n in range(nchunk):
            rows = slice(n * c, (n + 1) * c)
            wss = [jnp.dot(jnp.concatenate([w_p[rd, rows, hsl(h)], qe_p[rd, rows, hsl(h)]], axis=0),
                           state[h].astype(bf16), preferred_element_type=f32)
                   for h in range(npair)]
            yield
            vns = [u_p[rd, rows, hsl(h)] - wss[h][0:c] for h in range(npair)]
            upd = [_dot(kd_p[rd, rows, hsl(h)], vns[h], TN) * bd_mask for h in range(npair)]
            yield
            egl = egl_p[rd, n:n + 1, :]
            for h in range(npair):
                state[h] = state[h] * egl[:, hsl(h)] + upd[h]
            os_ = [wss[h][c:] + _dot(qk_p[rd, rows, hsl(h)], _bd(vns[h])) for h in range(npair)]
            o_s[rows, :] = jnp.concatenate(os_, axis=1)
            yield

    chain = recurrence()
    calls = [0]

    def tick():
        calls[0] += 1
        if calls[0] % GD_TICKS == 0:
            next(chain, None)

    raw = c_ref[:, 0:768]
    carry = carry_ref[...]
    conv = raw * cw_ref[3:4, :]
    for s in range(1, 4):
        conv = conv + _shift_rows(raw, carry, s) * cw_ref[3 - s:4 - s, :]
    carry_ref[...] = raw[GD_TB - 8:GD_TB]
    qkv = conv * _sigmoid(conv)
    for _ in range(GD_PRO):
        next(chain, None)
    q = qkv[:, 0:256]
    k = qkv[:, 256:512]
    q_s[...] = q * lax.rsqrt(_segsum(q * q, ones_bd) + 1e-6) * (HEAD_DIM ** -0.5)
    for _ in range(GD_PRO):
        next(chain, None)
    k_s[...] = k * lax.rsqrt(_segsum(k * k, ones_bd) + 1e-6)
    v_s[...] = qkv[:, 512:768]
    z = c_ref[:, 768:1024]
    gz_p[wr] = z * _sigmoid(z)
    for _ in range(GD_PRO):
        next(chain, None)
    small = m_ref[...]
    er, ec = _iota((NM, GROUP_W), 0), _iota((NM, GROUP_W), 1)
    b_exp = _dot_sel_rhs(small, (er == ec // HEAD_DIM + M_GDN).astype(bf16))
    a_exp = _dot_sel_rhs(small, (er == ec // HEAD_DIM + M_GDN + N_HEADS).astype(bf16))
    be_s[...] = _sigmoid(b_exp)
    g_s[...] = -jnp.exp(alog_ref[...]) * _softplus(a_exp + dtb_ref[...])

    tril_incl = (_iota((c, c), 0) >= _iota((c, c), 1)).astype(bf16)
    masks = _pair_masks(c)
    low_strict, low_incl = masks["strict"], masks["incl"]
    sel8 = (_iota((8, NM), 0) + M_GDN == _iota((8, NM), 1)).astype(bf16)
    g_rows = -jnp.exp(alogc_ref[...]) * _softplus(_dot_sel_lhs(sel8, small, NT) + dtbc_ref[...])
    tj, ti = _iota((GD_TB, GD_TB), 0), _iota((GD_TB, GD_TB), 1)
    gc_rows = _dot_sel_rhs(g_rows, ((tj // c == ti // c) & (tj <= ti)).astype(bf16))

    items = [(n, h) for n in range(nchunk) for h in range(npair)]
    pre = []
    for n in range(nchunk):
        rows = slice(n * c, (n + 1) * c)
        gc = _dot_sel_lhs(tril_incl, g_s[rows, :])
        glast = gc[c - 1:c, :]
        egc = jnp.exp(gc)
        kc, bc = k_s[rows, :], be_s[rows, :]
        kb = kc * bc
        pre.append(dict(gc=gc, k=kc, kb=kb, q=q_s[rows, :], vb=v_s[rows, :] * bc, kbe=kb * egc))
        qe_p[wr, rows, :] = (q_s[rows, :] * egc).astype(bf16)
        kd_p[wr, rows, :] = (kc * jnp.exp(glast - gc)).astype(bf16)
        egl_p[wr, n:n + 1, :] = jnp.exp(glast)
        tick()

    def gc_row(n, h):
        return jnp.concatenate([gc_rows[N_HEADS + 2 * h + i:N_HEADS + 2 * h + i + 1, n * c:(n + 1) * c]
                                for i in range(2)], axis=1)

    dms, aqs = [], []
    for n, h in items:
        dms.append(jnp.exp(jnp.where(low_incl, pre[n]["gc"][:, hsl(h)] - gc_row(n, h), NEG)))
        aqs.append(_dot(jnp.concatenate([pre[n]["kb"][:, hsl(h)], pre[n]["q"][:, hsl(h)]], axis=0),
                        _bd(pre[n]["k"][:, hsl(h)]), NT))
        qk_p[wr, n * c:(n + 1) * c, hsl(h)] = (aqs[-1][c:] * dms[-1]).astype(bf16)
        tick()
    tms = _tri_inv([jnp.where(low_strict, aq[0:c] * dm, 0.0) for aq, dm in zip(aqs, dms)], masks, tick)
    for tm, (n, h) in zip(tms, items):
        uw = _dot(tm, jnp.concatenate([_bd(pre[n]["vb"][:, hsl(h)]), _bd(pre[n]["kbe"][:, hsl(h)])], axis=1))
        u_p[wr, n * c:(n + 1) * c, hsl(h)] = uw[:, 0:2 * c]
        w_p[wr, n * c:(n + 1) * c, hsl(h)] = uw[:, 2 * c:].astype(bf16)
        tick()
    for _ in chain:
        pass
    for h in range(npair):
        s_ref[h] = state[h]

    o = o_s[...]
    ms = _segsum(o * o, ones_bd) * (1.0 / HEAD_DIM)
    y_ref[...] = (o * lax.rsqrt(ms + EPS) * ng_ref[...] * gz_p[rd]).astype(bf16)


def _gdn(c_b, c_m, p, l):
    names = ("conv_w", "a_log", "dt_bias", "a_log_col", "dt_bias_col", "norm_g")
    nblk = SEQ // GD_TB
    scratch = [pltpu.VMEM((8, 768), f32), pltpu.VMEM((N_HEADS // 2, 2 * HEAD_DIM, 2 * HEAD_DIM), f32)]
    scratch += [pltpu.VMEM((GD_TB, GROUP_W), f32) for _ in range(6)]
    scratch += [pltpu.VMEM((2, GD_TB, GROUP_W), f32)]
    scratch += [pltpu.VMEM((2, GD_TB, GROUP_W), bf16) for _ in range(4)]
    scratch += [pltpu.VMEM((2, GD_TB, GROUP_W), f32), pltpu.VMEM((2, GD_TB // GD_C, GROUP_W), f32)]
    return pl.pallas_call(
        _gdn_kernel,
        grid=(nblk + 1,),
        in_specs=[pl.BlockSpec((GD_TB, n), lambda i: (jnp.minimum(i, nblk - 1), 0)) for n in (NB, NM)]
        + [_layer_spec(p[k], l) for k in names],
        out_specs=pl.BlockSpec((GD_TB, GROUP_W), lambda i: (jnp.maximum(i - 1, 0), 0)),
        out_shape=jax.ShapeDtypeStruct((SEQ, GROUP_W), bf16),
        scratch_shapes=scratch,
        compiler_params=pltpu.CompilerParams(dimension_semantics=("arbitrary",),
                                             vmem_limit_bytes=VMEM_LIMIT),
        name="gdn",
    )(c_b, c_m, *[p[k] for k in names])


GL_TB = 256
GL_C = 16
GL_S = 8


def _gla_kernel(c_ref, m_ref, gup_ref, gb_ref, ng_ref, y_ref, st_ref, sx_s, o_s):
    @pl.when(pl.program_id(0) == 0)
    def _():
        st_ref[...] = jnp.zeros_like(st_ref)

    tb, c, s = GL_TB, GL_C, GL_S
    nchunk, nsub = tb // c, tb // s
    q = c_ref[:, 0:128] * (GLA_HEAD_K ** -0.5)
    k = c_ref[:, 128:256]
    v = c_ref[:, 256:512]
    pre = _dot(m_ref[...], gup_ref[...]) + gb_ref[...]
    la = -_softplus(-pre) * (1.0 / 16.0)
    tj, ti = _iota((tb, tb), 0), _iota((tb, tb), 1)
    b = _dot_sel_lhs(((tj // c == ti // c) & (ti <= tj)).astype(bf16), la)
    qi = q * jnp.exp(b)

    ind_e = (_iota((GLA_KEY, GROUP_W), 0) // GLA_HEAD_K == _iota((GLA_KEY, GROUP_W), 1) // HEAD_DIM).astype(bf16)
    bd_mask = (_iota((GROUP_W, GLA_KEY), 0) // HEAD_DIM == _iota((GROUP_W, GLA_KEY), 1) // GLA_HEAD_K).astype(f32)

    b3, q3, k3 = (t.reshape(nsub, s, GLA_KEY) for t in (b, q, k))
    ri = _iota((nsub, s, GLA_KEY), 1)
    terms = []
    for j in range(s):
        e = jnp.exp(jnp.where(ri >= j, b3 - b3[:, j:j + 1, :], NEG))
        terms.append((q3 * (k3[:, j:j + 1, :] * e)).reshape(tb, GLA_KEY).astype(bf16))
    sx_s[...] = jnp.dot(jnp.concatenate(terms, axis=0), ind_e, preferred_element_type=f32)
    v3 = v.reshape(nsub, s, GROUP_W)
    o3 = sx_s[0:tb, :].reshape(nsub, s, GROUP_W) * v3[:, 0:1, :]
    for j in range(1, s):
        o3 = o3 + sx_s[j * tb:(j + 1) * tb, :].reshape(nsub, s, GROUP_W) * v3[:, j:j + 1, :]

    b4, q4, k4 = (t.reshape(nchunk, 2, s, GLA_KEY) for t in (b, q, k))
    bref = b4[:, 0, s - 1:s, :]
    qd = (q4[:, 1] * jnp.exp(b4[:, 1] - bref)).reshape(nchunk * s, GLA_KEY)
    kd = (k4[:, 0] * jnp.exp(bref - b4[:, 0])).reshape(nchunk * s, GLA_KEY)
    v0 = v.reshape(nchunk, 2, s, GROUP_W)[:, 0].reshape(nchunk * s, GROUP_W)
    head_k = _iota((nchunk * s, GLA_KEY), 1) // GLA_HEAD_K
    head_v = _iota((nchunk * s, GROUP_W), 1) // HEAD_DIM
    ks = jnp.concatenate([jnp.where(head_k == h, kd, 0.0) for h in range(N_HEADS)], axis=0)
    vs = jnp.concatenate([jnp.where(head_v == h, v0, 0.0) for h in range(N_HEADS)], axis=0)
    sc = _dot(qd, ks, NT)
    sr, scol = _iota(sc.shape, 0), _iota(sc.shape, 1)
    sc = jnp.where(sr // s == (scol % (nchunk * s)) // s, sc, 0.0)
    o_off = _dot(sc, vs).reshape(nchunk, 1, s, GROUP_W)
    o4 = o3.reshape(nchunk, 2, s, GROUP_W)
    o_intra = jnp.concatenate([o4[:, 0:1], o4[:, 1:2] + o_off], axis=1).reshape(tb, GROUP_W)

    blasts = [b[(n + 1) * c - 1:(n + 1) * c, :] for n in range(nchunk)]
    upds = [_dot(v[n * c:(n + 1) * c], k[n * c:(n + 1) * c] * jnp.exp(blasts[n] - b[n * c:(n + 1) * c]), TN)
            * bd_mask for n in range(nchunk)]
    st = st_ref[...]
    for n in range(nchunk):
        rows = slice(n * c, (n + 1) * c)
        o_s[rows, :] = o_intra[rows] + _dot(qi[rows], st, NT)
        st = st * jnp.exp(blasts[n]) + upds[n]
    st_ref[...] = st

    o = o_s[...]
    ms = _segsum(o * o, _group_ones(GROUP_W, HEAD_DIM)) * (1.0 / HEAD_DIM)
    gate = c_ref[:, 512:768]
    y_ref[...] = (o * lax.rsqrt(ms + EPS) * ng_ref[...] * (gate * _sigmoid(gate))).astype(bf16)


def _gla(c_c, c_m, p, l):
    names = ("gk_up", "gk_bias", "norm_g")
    return pl.pallas_call(
        _gla_kernel,
        grid=(SEQ // GL_TB,),
        in_specs=[pl.BlockSpec((GL_TB, n), lambda i: (i, 0)) for n in (NC, NM)]
        + [_layer_spec(p[k], l) for k in names],
        out_specs=pl.BlockSpec((GL_TB, GROUP_W), lambda i: (i, 0)),
        out_shape=jax.ShapeDtypeStruct((SEQ, GROUP_W), bf16),
        scratch_shapes=[pltpu.VMEM((GROUP_W, GLA_KEY), f32),
                        pltpu.VMEM((GL_TB * GL_S, GROUP_W), f32), pltpu.VMEM((GL_TB, GROUP_W), f32)],
        compiler_params=pltpu.CompilerParams(dimension_semantics=("arbitrary",),
                                             vmem_limit_bytes=VMEM_LIMIT),
        name="gla",
    )(c_c, c_m, *[p[k] for k in names])


FF_TM = 512
FF_TF = 1024
FF_STAGE_ROWS = 512


def _load_as_bf16(src, dst, stage, sem):
    rows = stage.shape[1]
    nchunk = src.shape[0] // rows
    copies = [pltpu.make_async_copy(src.at[pl.ds(i * rows, rows)], stage.at[i % 2], sem.at[i % 2])
              for i in range(nchunk)]
    copies[0].start()
    for i in range(nchunk):
        copies[i].wait()
        if i + 1 < nchunk:
            copies[i + 1].start()
        dst[i * rows:(i + 1) * rows, :] = stage[i % 2].astype(bf16)


def _outffn_kernel(*refs, final, layer):
    if final:
        (x_ref, ya_ref, yb_ref, yc_ref, yd_ref, wo_hbm, g_ref, wu_hbm, wd_hbm, gf_ref, o_ref,
         wo_ref, wu_ref, wd_ref, stage_wide, stage_tall, sem) = refs
    else:
        (x_ref, ya_ref, yb_ref, yc_ref, yd_ref, wo_hbm, g_ref, wu_hbm, wd_hbm, o_ref,
         wo_ref, wu_ref, wd_ref, stage_wide, stage_tall, sem) = refs

    @pl.when(pl.program_id(0) == 0)
    def _():
        _load_as_bf16(wo_hbm.at[layer], wo_ref, stage_tall, sem)
        _load_as_bf16(wu_hbm.at[layer], wu_ref, stage_wide, sem)
        _load_as_bf16(wd_hbm.at[layer], wd_ref, stage_tall, sem)

    y = jnp.concatenate([ya_ref[...], yb_ref[...], yc_ref[...], yd_ref[...]], axis=1)
    x1 = x_ref[...] + jnp.dot(y, wo_ref[...], preferred_element_type=f32)
    ms = jnp.mean(x1 * x1, axis=-1, keepdims=True)
    h = (x1 * lax.rsqrt(ms + EPS) * g_ref[...]).astype(bf16)
    x2 = x1
    for kf in range(D_FF // FF_TF):
        cols = slice(kf * FF_TF, (kf + 1) * FF_TF)
        hid = jnp.maximum(jnp.dot(h, wu_ref[:, cols], preferred_element_type=f32), 0.0)
        x2 = x2 + jnp.dot((hid * hid).astype(bf16), wd_ref[cols, :], preferred_element_type=f32)
    if final:
        ms = jnp.mean(x2 * x2, axis=-1, keepdims=True)
        x2 = x2 * lax.rsqrt(ms + EPS) * gf_ref[...]
    o_ref[...] = x2


def _out_ffn(x, ys, p, l, final):
    hbm = pl.BlockSpec(memory_space=pl.ANY)
    yspec = pl.BlockSpec((FF_TM, GROUP_W), lambda i: (i, 0))
    in_specs = [pl.BlockSpec((FF_TM, D_MODEL), lambda i: (i, 0)), yspec, yspec, yspec, yspec,
                hbm, _layer_spec(p["g"], l), hbm, hbm]
    args = [x, *ys, p["w_out"], p["g"], p["w_up"], p["w_down"]]
    if final:
        in_specs.append(pl.BlockSpec((1, D_MODEL), lambda i: (0, 0)))
        args.append(p["g_final"])
    return pl.pallas_call(
        functools.partial(_outffn_kernel, final=final, layer=l),
        grid=(SEQ // FF_TM,),
        in_specs=in_specs,
        out_specs=pl.BlockSpec((FF_TM, D_MODEL), lambda i: (i, 0)),
        out_shape=jax.ShapeDtypeStruct((SEQ, D_MODEL), f32),
        scratch_shapes=[pltpu.VMEM((D_MODEL, D_MODEL), bf16), pltpu.VMEM((D_MODEL, D_FF), bf16),
                        pltpu.VMEM((D_FF, D_MODEL), bf16),
                        pltpu.VMEM((2, FF_STAGE_ROWS * D_MODEL // D_FF, D_FF), f32),
                        pltpu.VMEM((2, FF_STAGE_ROWS, D_MODEL), f32), pltpu.SemaphoreType.DMA((2,))],
        compiler_params=pltpu.CompilerParams(dimension_semantics=("arbitrary",),
                                             vmem_limit_bytes=VMEM_LIMIT),
        name="out_ffn",
    )(*args)


def kernel(x, w_in, w_out, norm_mix_g, norm_ffn_g, norm_final_g, rwkv_mu, rwkv_w0, rwkv_w_up, rwkv_a0, rwkv_a_up, rwkv_g_up, rwkv_k_k, rwkv_k_a, rwkv_r_k, rwkv_lnx_w, rwkv_lnx_b, rwkv_v0, rwkv_vres_down, rwkv_vres_up, gdn_conv_w, gdn_a_log, gdn_dt_bias, gdn_norm_g, gla_gk_up, gla_gk_bias, gla_norm_g, sgu_ln_g, sgu_ln_b, sgu_w_s, sgu_b_s, ffn_w_up, ffn_w_down):
    depth = w_in.shape[0]
    row = lambda a: a.reshape(depth, 1, -1)
    per_head = lambda a: jnp.repeat(a, HEAD_DIM, axis=-1).reshape(depth, 1, -1)
    pad_cols = lambda w, n: jnp.pad(w, ((0, 0), (0, 0), (0, n - w.shape[2])))
    pad_rows = lambda w, top, total: jnp.pad(w, ((0, 0), (top, total - top - w.shape[1]), (0, 0)))
    vres_down = jnp.pad(rwkv_vres_down, ((1, 0), (0, 0), (0, 0)))
    narrow = pad_cols(jnp.concatenate([vres_down, w_in[:, :, 2048:2056], w_in[:, :, 2824:2840]], axis=2), NM)
    w_comb = jnp.concatenate(
        [w_in[:, :, 0:1024], w_in[:, :, 1024:2048], w_in[:, :, 2056:2824], narrow,
         w_in[:, :, 2840:3352]], axis=2).astype(bf16)
    p_in = dict(g=row(norm_mix_g), w=w_comb, ln_g=row(sgu_ln_g), ln_b=row(sgu_ln_b),
                w_cat=sgu_w_s.transpose(0, 2, 1, 3).reshape(depth, SG_C, 4 * SG_C),
                bias_tile=jnp.repeat(sgu_b_s.transpose(0, 2, 1), HEAD_DIM, axis=2))
    p_rwkv = dict(mu=row(rwkv_mu), w0=row(rwkv_w0), w_up=pad_rows(rwkv_w_up, 0, 128),
                  a0=row(rwkv_a0), a_up=pad_rows(rwkv_a_up, 64, 128), g_up=rwkv_g_up,
                  k_k=row(rwkv_k_k), k_a=row(rwkv_k_a), r_k=row(rwkv_r_k),
                  lnx_w=row(rwkv_lnx_w), lnx_b=row(rwkv_lnx_b),
                  v0=rwkv_v0.reshape(depth - 1, 1, -1), vres_up=pad_rows(rwkv_vres_up, M_VRES, NM))
    p_gdn = dict(conv_w=gdn_conv_w, a_log=per_head(gdn_a_log), dt_bias=per_head(gdn_dt_bias),
                 a_log_col=jnp.pad(gdn_a_log, ((0, 0), (N_HEADS, 0))).reshape(depth, 8, 1),
                 dt_bias_col=jnp.pad(gdn_dt_bias, ((0, 0), (N_HEADS, 0))).reshape(depth, 8, 1),
                 norm_g=row(jnp.tile(gdn_norm_g, (1, N_HEADS))))
    p_gla = dict(gk_up=pad_rows(gla_gk_up, M_GLA, NM), gk_bias=row(gla_gk_bias),
                 norm_g=row(jnp.tile(gla_norm_g, (1, N_HEADS))))
    p_ffn = dict(w_out=w_out, g=row(norm_ffn_g), w_up=ffn_w_up, w_down=ffn_w_down,
                 g_final=norm_final_g.reshape(1, -1))

    xx = x[0]
    v_first = None
    for l in range(depth):
        c_a, c_b, c_c, c_m, y_d = _in_proj(xx, p_in, l)
        y_a, v_first = _rwkv(c_a, c_m, v_first, p_rwkv, l)
        y_b = _gdn(c_b, c_m, p_gdn, l)
        y_c = _gla(c_c, c_m, p_gla, l)
        xx = _out_ffn(xx, (y_a, y_b, y_c, y_d), p_ffn, l, final=(l == depth - 1))
    return xx[None]
```

```python
import functools

import jax
import jax.numpy as jnp
from jax import lax
from jax.experimental import pallas as pl
from jax.experimental.pallas import tpu as pltpu

f32 = jnp.float32
bf16 = jnp.bfloat16

SEQ = 16384
D_MODEL = 1024
GROUP_W = 256
HEAD_DIM = 64
N_HEADS = 4
GLA_KEY = 128
GLA_HEAD_K = 32
D_FF = 4096
EPS = 1e-6
RWKV_GN_EPS = 64e-5
NEG = -1e30

NA, NB, NC, NM, ND = 1024, 1024, 768, 128, 512
N_PAD = NA + NB + NC + NM + ND
M_VRES, M_GDN, M_GLA = 0, 32, 40

VMEM_LIMIT = 56 * 1024 * 1024

NN = (((1,), (0,)), ((), ()))
NT = (((1,), (1,)), ((), ()))
TN = (((0,), (0,)), ((), ()))


def _dot(a, b, dims=NN):
    return lax.dot_general(a.astype(bf16), b.astype(bf16), dims, preferred_element_type=f32)


def _iota(shape, axis):
    return lax.broadcasted_iota(jnp.int32, shape, axis)


def _layer_spec(a, l):
    return pl.BlockSpec((None,) + a.shape[1:], lambda *_: (l,) + (0,) * (a.ndim - 1))


def _segsum(x, ones_bd):
    return jnp.dot(x.astype(bf16), ones_bd, preferred_element_type=f32)


def _split2(x):
    hi = x.astype(bf16)
    lo = (x - hi.astype(f32)).astype(bf16)
    return hi, lo


def _dot_sel_lhs(sel, x, dims=NN):
    return sum(lax.dot_general(sel, t, dims, preferred_element_type=f32) for t in _split2(x))


def _dot_sel_rhs(x, sel, dims=NN):
    return sum(lax.dot_general(t, sel, dims, preferred_element_type=f32) for t in _split2(x))


def _group_ones(n, width):
    return (_iota((n, n), 0) // width == _iota((n, n), 1) // width).astype(bf16)


def _sigmoid(x):
    return 1.0 / (1.0 + jnp.exp(-x))


def _softplus(x):
    return jnp.maximum(x, 0.0) + jnp.log1p(jnp.exp(-jnp.abs(x)))


def _shift_rows(x, carry, s):
    xs = pltpu.roll(x, s, 0)
    fix = pltpu.roll(carry, s, 0)
    first = jnp.where(_iota(carry.shape, 0) < s, fix, xs[0:8])
    return jnp.concatenate([first, xs[8:]], axis=0)


def _bd(xp):
    xb = xp.astype(bf16)
    left = _iota(xb.shape, 1) < HEAD_DIM
    zero = jnp.zeros_like(xb)
    return jnp.concatenate([jnp.where(left, xb, zero), jnp.where(left, zero, xb)], axis=0)


def _pair_masks(c):
    ri, cj = _iota((c, 2 * c), 0), _iota((c, 2 * c), 1) & (c - 1)
    eye = (ri == cj).astype(f32)
    m16 = (ri // 16 == cj // 16).astype(f32)
    mo1 = ((ri // 32 == cj // 32) & (ri // 16 == cj // 16 + 1)).astype(f32)
    mo2 = ((ri // 32 == 1) & (cj // 32 == 0)).astype(f32)
    return dict(eye=eye, m16=m16, mo1=mo1, mo2=mo2, strict=ri > cj, incl=ri >= cj)


def _tri_inv(lms, masks, tick=lambda: None):
    c = RW_C

    def each(fn, *lists):
        out = []
        for args in zip(*lists):
            out.append(fn(*args))
            tick()
        return out

    ps = [-(lm * masks["m16"]) for lm in lms]
    ts = [masks["eye"] + p for p in ps]
    ps = each(lambda p: _dot(p, _bd(p)), ps)
    for _ in range(2):
        outs = each(lambda t, p: _dot(jnp.concatenate([t, p], axis=0), _bd(p)), ts, ps)
        ts = [t + o[0:c] for t, o in zip(ts, outs)]
        ps = [o[c:] for o in outs]
    ts = each(lambda t, p: t + _dot(t, _bd(p)), ts, ps)
    for mo in (masks["mo1"], masks["mo2"]):
        xs = each(lambda lm, t: _dot(lm * mo, _bd(t)), lms, ts)
        ts = each(lambda t, x: t - _dot(t, _bd(x)), ts, xs)
    return ts


IN_TM = 1024
IN_TN = 512
SG_C = 128


def _sgu_pieces(c_d, lg_ref, lb_ref, w_ref, bias_ref, y_ref):
    wr, wc = _iota((SG_C, 4 * SG_C), 0), _iota((SG_C, 4 * SG_C), 1)
    w = jnp.where(wc % SG_C <= wr, w_ref[...], 0.0).astype(bf16)
    lane_g = _iota((SG_C, GROUP_W), 1) // HEAD_DIM
    for n in range(c_d.shape[0] // SG_C):
        x = c_d[n * SG_C:(n + 1) * SG_C, :]
        gx = 0.5 * x * (1.0 + jnp.tanh(0.7978845608028654 * (x + 0.044715 * x * x * x)))
        u = gx[:, 0:256]
        v = gx[:, 256:512]
        mu = jnp.mean(v, axis=-1, keepdims=True)
        vc = v - mu
        var = jnp.mean(vc * vc, axis=-1, keepdims=True)
        v = vc * lax.rsqrt(var + 1e-5) * lg_ref[...] + lb_ref[...]
        vst = jnp.concatenate([jnp.where(lane_g == g, v, 0.0) for g in range(4)], axis=0)
        mixed = jnp.dot(w, vst.astype(bf16), preferred_element_type=f32) + bias_ref[...]
        y_ref[n * SG_C:(n + 1) * SG_C, :] = (u * mixed).astype(bf16)
        yield


def _inproj_kernel(x_ref, g_ref, w_hbm, lg_ref, lb_ref, ws_ref, bias_ref, oa_ref, ob_ref, oc_ref, om_ref,
                   yd_ref, w_ref, sem, *, layer):
    @pl.when(pl.program_id(0) == 0)
    def _():
        cp = pltpu.make_async_copy(w_hbm.at[layer], w_ref, sem.at[0])
        cp.start()
        cp.wait()

    x = x_ref[...]
    ms = jnp.mean(x * x, axis=-1, keepdims=True)
    h = (x * lax.rsqrt(ms + EPS) * g_ref[...]).astype(bf16)
    c_d = jnp.dot(h, w_ref[:, NA + NB + NC + NM:], preferred_element_type=f32)
    sgu = _sgu_pieces(c_d, lg_ref, lb_ref, ws_ref, bias_ref, yd_ref)
    off = 0
    for o_ref, n in ((oa_ref, NA), (ob_ref, NB), (oc_ref, NC), (om_ref, NM)):
        for j in range(0, n, IN_TN):
            w = min(IN_TN, n - j)
            o_ref[:, j:j + w] = jnp.dot(h, w_ref[:, off + j:off + j + w], preferred_element_type=f32)
            next(sgu, None)
        off += n
    for _ in sgu:
        pass


def _in_proj(x, p, l):
    tm = IN_TM
    names = ("g", "w", "ln_g", "ln_b", "w_cat", "bias_tile")
    specs = [pl.BlockSpec(memory_space=pl.ANY) if k == "w" else _layer_spec(p[k], l) for k in names]
    return pl.pallas_call(
        functools.partial(_inproj_kernel, layer=l),
        grid=(SEQ // tm,),
        in_specs=[pl.BlockSpec((tm, D_MODEL), lambda i: (i, 0))] + specs,
        out_specs=[pl.BlockSpec((tm, n), lambda i: (i, 0)) for n in (NA, NB, NC, NM, GROUP_W)],
        out_shape=[jax.ShapeDtypeStruct((SEQ, n), f32) for n in (NA, NB, NC, NM)]
        + [jax.ShapeDtypeStruct((SEQ, GROUP_W), bf16)],
        scratch_shapes=[pltpu.VMEM((D_MODEL, N_PAD), bf16), pltpu.SemaphoreType.DMA((1,))],
        compiler_params=pltpu.CompilerParams(dimension_semantics=("arbitrary",),
                                             vmem_limit_bytes=VMEM_LIMIT),
        name="in_proj",
    )(x, *[p[k] for k in names])


RW_TB = 512
RW_C = 64
RW_PRO = 2
RW_TICKS = 13


def _rwkv_kernel(*refs, has_vres):
    if has_vres:
        (c_ref, m_ref, vf_ref, mu_ref, w0_ref, wup_ref, a0_ref, aup_ref, gup_ref, kk_ref, ka_ref, rk_ref,
         lw_ref, lb_ref, v0_ref, vup_ref, y_ref,
         carry_ref, s_ref, r_s, k_s, v_s, lw_s, al_s, be_s, y_s,
         ta_p, rt_p, vb_p, bw_p, kw_p, arb_p, tv_p, av_p, bonus_p, g_p, dl_p) = refs
    else:
        (c_ref, mu_ref, w0_ref, wup_ref, a0_ref, aup_ref, gup_ref, kk_ref, ka_ref, rk_ref,
         lw_ref, lb_ref, y_ref, vf_out_ref,
         carry_ref, s_ref, r_s, k_s, v_s, lw_s, al_s, be_s, y_s,
         ta_p, rt_p, vb_p, bw_p, kw_p, arb_p, tv_p, av_p, bonus_p, g_p, dl_p) = refs
    step = pl.program_id(0)

    @pl.when(step == 0)
    def _():
        for ref in (carry_ref, s_ref, ta_p, rt_p, vb_p, bw_p, kw_p, arb_p, tv_p, av_p, bonus_p, g_p, dl_p):
            ref[...] = jnp.zeros_like(ref)

    wr = step % 2
    rd = 1 - wr
    c = RW_C
    nchunk = RW_TB // c
    npair = N_HEADS // 2
    hsl = lambda h: slice(2 * h * HEAD_DIM, 2 * (h + 1) * HEAD_DIM)
    bd_mask = (_iota((2 * c, 2 * c), 0) // c == _iota((2 * c, 2 * c), 1) // c).astype(f32)
    state = [s_ref[h] for h in range(npair)]

    def recurrence():
        for n in range(nchunk):
            rows = slice(n * c, (n + 1) * c)
            sas = [lax.dot_general(jnp.concatenate([ta_p[rd, rows, hsl(h)], rt_p[rd, rows, hsl(h)]], axis=0),
                                   state[h].astype(bf16), NT, preferred_element_type=f32)
                   for h in range(npair)]
            yield
            us = [sas[h][0:c] + tv_p[rd, rows, hsl(h)] for h in range(npair)]
            upd = [_dot(jnp.concatenate([us[h].astype(bf16), vb_p[rd, rows, hsl(h)]], axis=0),
                        jnp.concatenate([bw_p[rd, rows, hsl(h)], kw_p[rd, rows, hsl(h)]], axis=0), TN) * bd_mask
                   for h in range(npair)]
            yield
            dl = dl_p[rd, n:n + 1, :]
            for h in range(npair):
                state[h] = state[h] * dl[:, hsl(h)] + upd[h]
            ys = [sas[h][c:] + _dot(arb_p[rd, rows, hsl(h)], _bd(us[h])) + av_p[rd, rows, hsl(h)]
                  for h in range(npair)]
            y_s[rows, :] = jnp.concatenate(ys, axis=1)
            yield

    chain = recurrence()
    calls = [0]

    def advance(pieces=1):
        for _ in range(pieces):
            next(chain, None)

    def tick():
        calls[0] += 1
        if calls[0] % RW_TICKS == 0:
            advance()

    ones_bd = _group_ones(GROUP_W, HEAD_DIM)
    x = c_ref[:, 0:1024]
    x_prev = _shift_rows(x, carry_ref[...], 1)
    carry_ref[...] = x[RW_TB - 8:RW_TB]
    xs = x + (x_prev - x) * mu_ref[...]
    advance(RW_PRO)
    r = xs[:, 0:256]
    k = xs[:, 256:512]
    v = xs[:, 512:768]
    lora = xs[:, 768:896]
    w_pre = w0_ref[...] + _dot(jnp.tanh(lora), wup_ref[...])
    lw = -jnp.exp(-_softplus(-w_pre) - 0.5)
    advance(RW_PRO)
    a = _sigmoid(a0_ref[...] + _dot(lora, aup_ref[...]))
    g_p[wr] = _dot(_sigmoid(xs[:, 896:1024]), gup_ref[...])
    advance(RW_PRO)
    if has_vres:
        mix = _sigmoid(v0_ref[...] + _dot(m_ref[...], vup_ref[...]))
        v = v + (vf_ref[...] - v) * mix
    else:
        @pl.when(step < SEQ // RW_TB)
        def _():
            vf_out_ref[...] = v
    kk = k * kk_ref[...]
    kk = kk * lax.rsqrt(_segsum(kk * kk, ones_bd) + 1e-24)
    advance(RW_PRO)
    k = k * (1.0 + (a - 1.0) * ka_ref[...])
    bonus_p[wr] = _segsum(r * k * rk_ref[...], ones_bd) * v
    advance(RW_PRO)
    r_s[...] = r
    k_s[...] = k
    v_s[...] = v
    lw_s[...] = lw
    al_s[...] = -kk
    be_s[...] = kk * a

    tril_incl = (_iota((c, c), 0) >= _iota((c, c), 1)).astype(bf16)
    masks = _pair_masks(c)
    low_strict, low_incl = masks["strict"], masks["incl"]

    items = [(n, h) for n in range(nchunk) for h in range(npair)]
    pre = []
    for n in range(nchunk):
        rows = slice(n * c, (n + 1) * c)
        lwc = lw_s[rows, :]
        lc = _dot_sel_lhs(tril_incl, lwc)
        llast = lc[c - 1:c, :]
        e_out = jnp.exp(-lc)
        e_rest = jnp.exp(llast - lc)
        kc, bec = k_s[rows, :], be_s[rows, :]
        rt = r_s[rows, :] * jnp.exp(lc)
        pre.append(dict(rt=rt, at=al_s[rows, :] * jnp.exp(lc - lwc), bt=bec * e_out, kt=kc * e_out,
                        v=v_s[rows, :]))
        rt_p[wr, rows, :] = rt.astype(bf16)
        vb_p[wr, rows, :] = v_s[rows, :].astype(bf16)
        bw_p[wr, rows, :] = (bec * e_rest).astype(bf16)
        kw_p[wr, rows, :] = (kc * e_rest).astype(bf16)
        dl_p[wr, n:n + 1, :] = jnp.exp(llast)
        tick()
    ms = []
    for n, h in items:
        ms.append(_dot(jnp.concatenate([pre[n]["at"][:, hsl(h)], pre[n]["rt"][:, hsl(h)]], axis=0),
                       jnp.concatenate([_bd(pre[n]["bt"][:, hsl(h)]), _bd(pre[n]["kt"][:, hsl(h)])], axis=0),
                       NT))
        arb_p[wr, n * c:(n + 1) * c, hsl(h)] = jnp.where(low_incl, ms[-1][c:, 0:2 * c], 0.0).astype(bf16)
        tick()
    tms = _tri_inv([jnp.where(low_strict, -m[0:c, 0:2 * c], 0.0) for m in ms], masks, tick)
    avs = []
    for m, (n, h) in zip(ms, items):
        avs.append(_dot(jnp.concatenate([jnp.where(low_strict, m[0:c, 2 * c:], 0.0),
                                         jnp.where(low_incl, m[c:, 2 * c:], 0.0)], axis=0),
                        _bd(pre[n]["v"][:, hsl(h)])))
        av_p[wr, n * c:(n + 1) * c, hsl(h)] = avs[-1][c:]
        tick()
    for tm, av, (n, h) in zip(tms, avs, items):
        tt = _dot(tm, jnp.concatenate([_bd(pre[n]["at"][:, hsl(h)]), _bd(av[0:c])], axis=1))
        ta_p[wr, n * c:(n + 1) * c, hsl(h)] = tt[:, 0:2 * c].astype(bf16)
        tv_p[wr, n * c:(n + 1) * c, hsl(h)] = tt[:, 2 * c:]
        tick()
    for _ in chain:
        pass
    for h in range(npair):
        s_ref[h] = state[h]

    y = y_s[...]
    inv_d = 1.0 / HEAD_DIM
    mean = _segsum(y, ones_bd) * inv_d
    yc = y - mean
    var = _segsum(yc * yc, ones_bd) * inv_d
    y = yc * lax.rsqrt(var + RWKV_GN_EPS) * lw_ref[...] + lb_ref[...]
    y_ref[...] = ((y + bonus_p[rd]) * g_p[rd]).astype(bf16)


def _rwkv(c_a, c_m, v_first, p, l):
    has_vres = v_first is not None
    nblk = SEQ // RW_TB
    blk = lambda n: pl.BlockSpec((RW_TB, n), lambda i: (jnp.minimum(i, nblk - 1), 0))
    blk_prev = lambda n: pl.BlockSpec((RW_TB, n), lambda i: (jnp.maximum(i - 1, 0), 0))
    in_specs = [blk(NA)]
    args = [c_a]
    if has_vres:
        in_specs += [blk(NM), blk(GROUP_W)]
        args += [c_m, v_first]
    names = ("mu", "w0", "w_up", "a0", "a_up", "g_up", "k_k", "k_a", "r_k", "lnx_w", "lnx_b")
    in_specs += [_layer_spec(p[k], l) for k in names]
    args += [p[k] for k in names]
    if has_vres:
        in_specs += [_layer_spec(p["v0"], l - 1), _layer_spec(p["vres_up"], l - 1)]
        args += [p["v0"], p["vres_up"]]
        out_specs = blk_prev(GROUP_W)
        out_shape = jax.ShapeDtypeStruct((SEQ, GROUP_W), bf16)
    else:
        out_specs = [blk_prev(GROUP_W), blk(GROUP_W)]
        out_shape = [jax.ShapeDtypeStruct((SEQ, GROUP_W), bf16),
                     jax.ShapeDtypeStruct((SEQ, GROUP_W), f32)]
    scratch = [pltpu.VMEM((8, 1024), f32), pltpu.VMEM((N_HEADS // 2, 2 * HEAD_DIM, 2 * HEAD_DIM), f32)]
    scratch += [pltpu.VMEM((RW_TB, GROUP_W), f32) for _ in range(7)]
    scratch += [pltpu.VMEM((2, RW_TB, GROUP_W), bf16) for _ in range(6)]
    scratch += [pltpu.VMEM((2, RW_TB, GROUP_W), f32) for _ in range(4)]
    scratch += [pltpu.VMEM((2, RW_TB // RW_C, GROUP_W), f32)]
    out = pl.pallas_call(
        functools.partial(_rwkv_kernel, has_vres=has_vres),
        grid=(nblk + 1,),
        in_specs=in_specs,
        out_specs=out_specs,
        out_shape=out_shape,
        scratch_shapes=scratch,
        compiler_params=pltpu.CompilerParams(dimension_semantics=("arbitrary",),
                                             vmem_limit_bytes=VMEM_LIMIT),
        name="rwkv7",
    )(*args)
    if has_vres:
        return out, v_first
    return out[0], out[1]


GD_TB = 512
GD_C = 64
GD_PRO = 2
GD_TICKS = 9


def _gdn_kernel(c_ref, m_ref, cw_ref, alog_ref, dtb_ref, alogc_ref, dtbc_ref, ng_ref, y_ref,
                carry_ref, s_ref, q_s, k_s, v_s, be_s, g_s, o_s,
                u_p, w_p, qe_p, kd_p, qk_p, gz_p, egl_p):
    step = pl.program_id(0)

    @pl.when(step == 0)
    def _():
        for ref in (carry_ref, s_ref, u_p, w_p, qe_p, kd_p, qk_p, gz_p, egl_p):
            ref[...] = jnp.zeros_like(ref)

    wr = step % 2
    rd = 1 - wr
    c = GD_C
    nchunk = GD_TB // c
    npair = N_HEADS // 2
    hsl = lambda h: slice(2 * h * HEAD_DIM, 2 * (h + 1) * HEAD_DIM)
    bd_mask = (_iota((2 * c, 2 * c), 0) // c == _iota((2 * c, 2 * c), 1) // c).astype(f32)
    ones_bd = _group_ones(GROUP_W, HEAD_DIM)
    state = [s_ref[h] for h in range(npair)]

    def recurrence():
        for n in range(nchunk):
            rows = slice(n * c, (n + 1) * c)
            wss = [jnp.dot(jnp.concatenate([w_p[rd, rows, hsl(h)], qe_p[rd, rows, hsl(h)]], axis=0),
                           state[h].astype(bf16), preferred_element_type=f32)
                   for h in range(npair)]
            yield
            vns = [u_p[rd, rows, hsl(h)] - wss[h][0:c] for h in range(npair)]
            upd = [_dot(kd_p[rd, rows, hsl(h)], vns[h], TN) * bd_mask for h in range(npair)]
            yield
            egl = egl_p[rd, n:n + 1, :]
            for h in range(npair):
                state[h] = state[h] * egl[:, hsl(h)] + upd[h]
            os_ = [wss[h][c:] + _dot(qk_p[rd, rows, hsl(h)], _bd(vns[h])) for h in range(npair)]
            o_s[rows, :] = jnp.concatenate(os_, axis=1)
            yield

    chain = recurrence()
    calls = [0]

    def tick():
        calls[0] += 1
        if calls[0] % GD_TICKS == 0:
            next(chain, None)

    raw = c_ref[:, 0:768]
    carry = carry_ref[...]
    conv = raw * cw_ref[3:4, :]
    for s in range(1, 4):
        conv = conv + _shift_rows(raw, carry, s) * cw_ref[3 - s:4 - s, :]
    carry_ref[...] = raw[GD_TB - 8:GD_TB]
    qkv = conv * _sigmoid(conv)
    for _ in range(GD_PRO):
        next(chain, None)
    q = qkv[:, 0:256]
    k = qkv[:, 256:512]
    q_s[...] = q * lax.rsqrt(_segsum(q * q, ones_bd) + 1e-6) * (HEAD_DIM ** -0.5)
    for _ in range(GD_PRO):
        next(chain, None)
    k_s[...] = k * lax.rsqrt(_segsum(k * k, ones_bd) + 1e-6)
    v_s[...] = qkv[:, 512:768]
    z = c_ref[:, 768:1024]
    gz_p[wr] = z * _sigmoid(z)
    for _ in range(GD_PRO):
        next(chain, None)
    small = m_ref[...]
    er, ec = _iota((NM, GROUP_W), 0), _iota((NM, GROUP_W), 1)
    b_exp = _dot_sel_rhs(small, (er == ec // HEAD_DIM + M_GDN).astype(bf16))
    a_exp = _dot_sel_rhs(small, (er == ec // HEAD_DIM + M_GDN + N_HEADS).astype(bf16))
    be_s[...] = _sigmoid(b_exp)
    g_s[...] = -jnp.exp(alog_ref[...]) * _softplus(a_exp + dtb_ref[...])

    tril_incl = (_iota((c, c), 0) >= _iota((c, c), 1)).astype(bf16)
    masks = _pair_masks(c)
    low_strict, low_incl = masks["strict"], masks["incl"]
    sel8 = (_iota((8, NM), 0) + M_GDN == _iota((8, NM), 1)).astype(bf16)
    g_rows = -jnp.exp(alogc_ref[...]) * _softplus(_dot_sel_lhs(sel8, small, NT) + dtbc_ref[...])
    tj, ti = _iota((GD_TB, GD_TB), 0), _iota((GD_TB, GD_TB), 1)
    gc_rows = _dot_sel_rhs(g_rows, ((tj // c == ti // c) & (tj <= ti)).astype(bf16))

    items = [(n, h) for n in range(nchunk) for h in range(npair)]
    pre = []
    for n in range(nchunk):
        rows = slice(n * c, (n + 1) * c)
        gc = _dot_sel_lhs(tril_incl, g_s[rows, :])
        glast = gc[c - 1:c, :]
        egc = jnp.exp(gc)
        kc, bc = k_s[rows, :], be_s[rows, :]
        kb = kc * bc
        pre.append(dict(gc=gc, k=kc, kb=kb, q=q_s[rows, :], vb=v_s[rows, :] * bc, kbe=kb * egc))
        qe_p[wr, rows, :] = (q_s[rows, :] * egc).astype(bf16)
        kd_p[wr, rows, :] = (kc * jnp.exp(glast - gc)).astype(bf16)
        egl_p[wr, n:n + 1, :] = jnp.exp(glast)
        tick()

    def gc_row(n, h):
        return jnp.concatenate([gc_rows[N_HEADS + 2 * h + i:N_HEADS + 2 * h + i + 1, n * c:(n + 1) * c]
                                for i in range(2)], axis=1)

    dms, aqs = [], []
    for n, h in items:
        dms.append(jnp.exp(jnp.where(low_incl, pre[n]["gc"][:, hsl(h)] - gc_row(n, h), NEG)))
        aqs.append(_dot(jnp.concatenate([pre[n]["kb"][:, hsl(h)], pre[n]["q"][:, hsl(h)]], axis=0),
                        _bd(pre[n]["k"][:, hsl(h)]), NT))
        qk_p[wr, n * c:(n + 1) * c, hsl(h)] = (aqs[-1][c:] * dms[-1]).astype(bf16)
        tick()
    tms = _tri_inv([jnp.where(low_strict, aq[0:c] * dm, 0.0) for aq, dm in zip(aqs, dms)], masks, tick)
    for tm, (n, h) in zip(tms, items):
        uw = _dot(tm, jnp.concatenate([_bd(pre[n]["vb"][:, hsl(h)]), _bd(pre[n]["kbe"][:, hsl(h)])], axis=1))
        u_p[wr, n * c:(n + 1) * c, hsl(h)] = uw[:, 0:2 * c]
        w_p[wr, n * c:(n + 1) * c, hsl(h)] = uw[:, 2 * c:].astype(bf16)
        tick()
    for _ in chain:
        pass
    for h in range(npair):
        s_ref[h] = state[h]

    o = o_s[...]
    ms = _segsum(o * o, ones_bd) * (1.0 / HEAD_DIM)
    y_ref[...] = (o * lax.rsqrt(ms + EPS) * ng_ref[...] * gz_p[rd]).astype(bf16)


def _gdn(c_b, c_m, p, l):
    names = ("conv_w", "a_log", "dt_bias", "a_log_col", "dt_bias_col", "norm_g")
    nblk = SEQ // GD_TB
    scratch = [pltpu.VMEM((8, 768), f32), pltpu.VMEM((N_HEADS // 2, 2 * HEAD_DIM, 2 * HEAD_DIM), f32)]
    scratch += [pltpu.VMEM((GD_TB, GROUP_W), f32) for _ in range(6)]
    scratch += [pltpu.VMEM((2, GD_TB, GROUP_W), f32)]
    scratch += [pltpu.VMEM((2, GD_TB, GROUP_W), bf16) for _ in range(4)]
    scratch += [pltpu.VMEM((2, GD_TB, GROUP_W), f32), pltpu.VMEM((2, GD_TB // GD_C, GROUP_W), f32)]
    return pl.pallas_call(
        _gdn_kernel,
        grid=(nblk + 1,),
        in_specs=[pl.BlockSpec((GD_TB, n), lambda i: (jnp.minimum(i, nblk - 1), 0)) for n in (NB, NM)]
        + [_layer_spec(p[k], l) for k in names],
        out_specs=pl.BlockSpec((GD_TB, GROUP_W), lambda i: (jnp.maximum(i - 1, 0), 0)),
        out_shape=jax.ShapeDtypeStruct((SEQ, GROUP_W), bf16),
        scratch_shapes=scratch,
        compiler_params=pltpu.CompilerParams(dimension_semantics=("arbitrary",),
                                             vmem_limit_bytes=VMEM_LIMIT),
        name="gdn",
    )(c_b, c_m, *[p[k] for k in names])


GL_TB = 256
GL_C = 16
GL_S = 8


def _gla_kernel(c_ref, m_ref, gup_ref, gb_ref, ng_ref, y_ref, st_ref, sx_s, o_s):
    @pl.when(pl.program_id(0) == 0)
    def _():
        st_ref[...] = jnp.zeros_like(st_ref)

    tb, c, s = GL_TB, GL_C, GL_S
    nchunk, nsub = tb // c, tb // s
    q = c_ref[:, 0:128] * (GLA_HEAD_K ** -0.5)
    k = c_ref[:, 128:256]
    v = c_ref[:, 256:512]
    pre = _dot(m_ref[...], gup_ref[...]) + gb_ref[...]
    la = -_softplus(-pre) * (1.0 / 16.0)
    tj, ti = _iota((tb, tb), 0), _iota((tb, tb), 1)
    b = _dot_sel_lhs(((tj // c == ti // c) & (ti <= tj)).astype(bf16), la)
    qi = q * jnp.exp(b)

    ind_e = (_iota((GLA_KEY, GROUP_W), 0) // GLA_HEAD_K == _iota((GLA_KEY, GROUP_W), 1) // HEAD_DIM).astype(bf16)
    bd_mask = (_iota((GROUP_W, GLA_KEY), 0) // HEAD_DIM == _iota((GROUP_W, GLA_KEY), 1) // GLA_HEAD_K).astype(f32)

    b3, q3, k3 = (t.reshape(nsub, s, GLA_KEY) for t in (b, q, k))
    ri = _iota((nsub, s, GLA_KEY), 1)
    terms = []
    for j in range(s):
        e = jnp.exp(jnp.where(ri >= j, b3 - b3[:, j:j + 1, :], NEG))
        terms.append((q3 * (k3[:, j:j + 1, :] * e)).reshape(tb, GLA_KEY).astype(bf16))
    sx_s[...] = jnp.dot(jnp.concatenate(terms, axis=0), ind_e, preferred_element_type=f32)
    v3 = v.reshape(nsub, s, GROUP_W)
    o3 = sx_s[0:tb, :].reshape(nsub, s, GROUP_W) * v3[:, 0:1, :]
    for j in range(1, s):
        o3 = o3 + sx_s[j * tb:(j + 1) * tb, :].reshape(nsub, s, GROUP_W) * v3[:, j:j + 1, :]

    b4, q4, k4 = (t.reshape(nchunk, 2, s, GLA_KEY) for t in (b, q, k))
    bref = b4[:, 0, s - 1:s, :]
    qd = (q4[:, 1] * jnp.exp(b4[:, 1] - bref)).reshape(nchunk * s, GLA_KEY)
    kd = (k4[:, 0] * jnp.exp(bref - b4[:, 0])).reshape(nchunk * s, GLA_KEY)
    v0 = v.reshape(nchunk, 2, s, GROUP_W)[:, 0].reshape(nchunk * s, GROUP_W)
    head_k = _iota((nchunk * s, GLA_KEY), 1) // GLA_HEAD_K
    head_v = _iota((nchunk * s, GROUP_W), 1) // HEAD_DIM
    ks = jnp.concatenate([jnp.where(head_k == h, kd, 0.0) for h in range(N_HEADS)], axis=0)
    vs = jnp.concatenate([jnp.where(head_v == h, v0, 0.0) for h in range(N_HEADS)], axis=0)
    sc = _dot(qd, ks, NT)
    sr, scol = _iota(sc.shape, 0), _iota(sc.shape, 1)
    sc = jnp.where(sr // s == (scol % (nchunk * s)) // s, sc, 0.0)
    o_off = _dot(sc, vs).reshape(nchunk, 1, s, GROUP_W)
    o4 = o3.reshape(nchunk, 2, s, GROUP_W)
    o_intra = jnp.concatenate([o4[:, 0:1], o4[:, 1:2] + o_off], axis=1).reshape(tb, GROUP_W)

    blasts = [b[(n + 1) * c - 1:(n + 1) * c, :] for n in range(nchunk)]
    upds = [_dot(v[n * c:(n + 1) * c], k[n * c:(n + 1) * c] * jnp.exp(blasts[n] - b[n * c:(n + 1) * c]), TN)
            * bd_mask for n in range(nchunk)]
    st = st_ref[...]
    for n in range(nchunk):
        rows = slice(n * c, (n + 1) * c)
        o_s[rows, :] = o_intra[rows] + _dot(qi[rows], st, NT)
        st = st * jnp.exp(blasts[n]) + upds[n]
    st_ref[...] = st

    o = o_s[...]
    ms = _segsum(o * o, _group_ones(GROUP_W, HEAD_DIM)) * (1.0 / HEAD_DIM)
    gate = c_ref[:, 512:768]
    y_ref[...] = (o * lax.rsqrt(ms + EPS) * ng_ref[...] * (gate * _sigmoid(gate))).astype(bf16)


def _gla(c_c, c_m, p, l):
    names = ("gk_up", "gk_bias", "norm_g")
    return pl.pallas_call(
        _gla_kernel,
        grid=(SEQ // GL_TB,),
        in_specs=[pl.BlockSpec((GL_TB, n), lambda i: (i, 0)) for n in (NC, NM)]
        + [_layer_spec(p[k], l) for k in names],
        out_specs=pl.BlockSpec((GL_TB, GROUP_W), lambda i: (i, 0)),
        out_shape=jax.ShapeDtypeStruct((SEQ, GROUP_W), bf16),
        scratch_shapes=[pltpu.VMEM((GROUP_W, GLA_KEY), f32),
                        pltpu.VMEM((GL_TB * GL_S, GROUP_W), f32), pltpu.VMEM((GL_TB, GROUP_W), f32)],
        compiler_params=pltpu.CompilerParams(dimension_semantics=("arbitrary",),
                                             vmem_limit_bytes=VMEM_LIMIT),
        name="gla",
    )(c_c, c_m, *[p[k] for k in names])


FF_TM = 512
FF_TF = 1024


def _outffn_kernel(*refs, final, layer):
    if final:
        (x_ref, ya_ref, yb_ref, yc_ref, yd_ref, wo_hbm, g_ref, wu_hbm, wd_hbm, gf_ref, o_ref,
         wo_ref, wu_ref, wd_ref, sem) = refs
    else:
        (x_ref, ya_ref, yb_ref, yc_ref, yd_ref, wo_hbm, g_ref, wu_hbm, wd_hbm, o_ref,
         wo_ref, wu_ref, wd_ref, sem) = refs

    @pl.when(pl.program_id(0) == 0)
    def _():
        copies = [pltpu.make_async_copy(src.at[layer], dst, sem.at[i])
                  for i, (src, dst) in enumerate(((wo_hbm, wo_ref), (wu_hbm, wu_ref), (wd_hbm, wd_ref)))]
        for cp in copies:
            cp.start()
        for cp in copies:
            cp.wait()

    y = jnp.concatenate([ya_ref[...], yb_ref[...], yc_ref[...], yd_ref[...]], axis=1)
    x1 = x_ref[...] + jnp.dot(y, wo_ref[...], preferred_element_type=f32)
    ms = jnp.mean(x1 * x1, axis=-1, keepdims=True)
    h = (x1 * lax.rsqrt(ms + EPS) * g_ref[...]).astype(bf16)
    x2 = x1
    for kf in range(D_FF // FF_TF):
        cols = slice(kf * FF_TF, (kf + 1) * FF_TF)
        hid = jnp.maximum(jnp.dot(h, wu_ref[:, cols], preferred_element_type=f32), 0.0)
        x2 = x2 + jnp.dot((hid * hid).astype(bf16), wd_ref[cols, :], preferred_element_type=f32)
    if final:
        ms = jnp.mean(x2 * x2, axis=-1, keepdims=True)
        x2 = x2 * lax.rsqrt(ms + EPS) * gf_ref[...]
    o_ref[...] = x2


def _out_ffn(x, ys, p, l, final):
    hbm = pl.BlockSpec(memory_space=pl.ANY)
    yspec = pl.BlockSpec((FF_TM, GROUP_W), lambda i: (i, 0))
    in_specs = [pl.BlockSpec((FF_TM, D_MODEL), lambda i: (i, 0)), yspec, yspec, yspec, yspec,
                hbm, _layer_spec(p["g"], l), hbm, hbm]
    args = [x, *ys, p["w_out"], p["g"], p["w_up"], p["w_down"]]
    if final:
        in_specs.append(pl.BlockSpec((1, D_MODEL), lambda i: (0, 0)))
        args.append(p["g_final"])
    return pl.pallas_call(
        functools.partial(_outffn_kernel, final=final, layer=l),
        grid=(SEQ // FF_TM,),
        in_specs=in_specs,
        out_specs=pl.BlockSpec((FF_TM, D_MODEL), lambda i: (i, 0)),
        out_shape=jax.ShapeDtypeStruct((SEQ, D_MODEL), f32),
        scratch_shapes=[pltpu.VMEM((D_MODEL, D_MODEL), bf16), pltpu.VMEM((D_MODEL, D_FF), bf16),
                        pltpu.VMEM((D_FF, D_MODEL), bf16), pltpu.SemaphoreType.DMA((3,))],
        compiler_params=pltpu.CompilerParams(dimension_semantics=("arbitrary",),
                                             vmem_limit_bytes=VMEM_LIMIT),
        name="out_ffn",
    )(*args)


def kernel(x, w_in, w_out, norm_mix_g, norm_ffn_g, norm_final_g, rwkv_mu, rwkv_w0, rwkv_w_up, rwkv_a0, rwkv_a_up, rwkv_g_up, rwkv_k_k, rwkv_k_a, rwkv_r_k, rwkv_lnx_w, rwkv_lnx_b, rwkv_v0, rwkv_vres_down, rwkv_vres_up, gdn_conv_w, gdn_a_log, gdn_dt_bias, gdn_norm_g, gla_gk_up, gla_gk_bias, gla_norm_g, sgu_ln_g, sgu_ln_b, sgu_w_s, sgu_b_s, ffn_w_up, ffn_w_down):
    depth = w_in.shape[0]
    row = lambda a: a.reshape(depth, 1, -1)
    per_head = lambda a: jnp.repeat(a, HEAD_DIM, axis=-1).reshape(depth, 1, -1)
    pad_cols = lambda w, n: jnp.pad(w, ((0, 0), (0, 0), (0, n - w.shape[2])))
    pad_rows = lambda w, top, total: jnp.pad(w, ((0, 0), (top, total - top - w.shape[1]), (0, 0)))
    vres_down = jnp.pad(rwkv_vres_down, ((1, 0), (0, 0), (0, 0)))
    narrow = pad_cols(jnp.concatenate([vres_down, w_in[:, :, 2048:2056], w_in[:, :, 2824:2840]], axis=2), NM)
    w_comb = jnp.concatenate(
        [w_in[:, :, 0:1024], w_in[:, :, 1024:2048], w_in[:, :, 2056:2824], narrow,
         w_in[:, :, 2840:3352]], axis=2).astype(bf16)
    p_in = dict(g=row(norm_mix_g), w=w_comb, ln_g=row(sgu_ln_g), ln_b=row(sgu_ln_b),
                w_cat=sgu_w_s.transpose(0, 2, 1, 3).reshape(depth, SG_C, 4 * SG_C),
                bias_tile=jnp.repeat(sgu_b_s.transpose(0, 2, 1), HEAD_DIM, axis=2))
    p_rwkv = dict(mu=row(rwkv_mu), w0=row(rwkv_w0), w_up=pad_rows(rwkv_w_up, 0, 128),
                  a0=row(rwkv_a0), a_up=pad_rows(rwkv_a_up, 64, 128), g_up=rwkv_g_up,
                  k_k=row(rwkv_k_k), k_a=row(rwkv_k_a), r_k=row(rwkv_r_k),
                  lnx_w=row(rwkv_lnx_w), lnx_b=row(rwkv_lnx_b),
                  v0=rwkv_v0.reshape(depth - 1, 1, -1), vres_up=pad_rows(rwkv_vres_up, M_VRES, NM))
    p_gdn = dict(conv_w=gdn_conv_w, a_log=per_head(gdn_a_log), dt_bias=per_head(gdn_dt_bias),
                 a_log_col=jnp.pad(gdn_a_log, ((0, 0), (N_HEADS, 0))).reshape(depth, 8, 1),
                 dt_bias_col=jnp.pad(gdn_dt_bias, ((0, 0), (N_HEADS, 0))).reshape(depth, 8, 1),
                 norm_g=row(jnp.tile(gdn_norm_g, (1, N_HEADS))))
    p_gla = dict(gk_up=pad_rows(gla_gk_up, M_GLA, NM), gk_bias=row(gla_gk_bias),
                 norm_g=row(jnp.tile(gla_norm_g, (1, N_HEADS))))
    p_ffn = dict(w_out=w_out.astype(bf16), g=row(norm_ffn_g), w_up=ffn_w_up.astype(bf16),
                 w_down=ffn_w_down.astype(bf16), g_final=norm_final_g.reshape(1, -1))

    xx = x[0]
    v_first = None
    for l in range(depth):
        c_a, c_b, c_c, c_m, y_d = _in_proj(xx, p_in, l)
        y_a, v_first = _rwkv(c_a, c_m, v_first, p_rwkv, l)
        y_b = _gdn(c_b, c_m, p_gdn, l)
        y_c = _gla(c_c, c_m, p_gla, l)
        xx = _out_ffn(xx, (y_a, y_b, y_c, y_d), p_ffn, l, final=(l == depth - 1))
    return xx[None]
```

```python
import functools

import jax
import jax.numpy as jnp
from jax import lax
from jax.experimental import pallas as pl
from jax.experimental.pallas import tpu as pltpu

f32 = jnp.float32
bf16 = jnp.bfloat16

SEQ = 16384
D_MODEL = 1024
GROUP_W = 256
HEAD_DIM = 64
N_HEADS = 4
GLA_KEY = 128
GLA_HEAD_K = 32
D_FF = 4096
EPS = 1e-6
RWKV_GN_EPS = 64e-5
NEG = -1e30

NA, NB, NC, NM, ND = 1024, 1024, 768, 128, 512
N_PAD = NA + NB + NC + NM + ND
M_VRES, M_GDN, M_GLA = 0, 32, 40

VMEM_LIMIT = 56 * 1024 * 1024

NN = (((1,), (0,)), ((), ()))
NT = (((1,), (1,)), ((), ()))
TN = (((0,), (0,)), ((), ()))


def _dot(a, b, dims=NN):
    return lax.dot_general(a.astype(bf16), b.astype(bf16), dims, preferred_element_type=f32)


def _iota(shape, axis):
    return lax.broadcasted_iota(jnp.int32, shape, axis)


def _layer_spec(a, l):
    return pl.BlockSpec((None,) + a.shape[1:], lambda *_: (l,) + (0,) * (a.ndim - 1))


def _segsum(x, ones_bd):
    return jnp.dot(x.astype(bf16), ones_bd, preferred_element_type=f32)


def _split2(x):
    hi = x.astype(bf16)
    lo = (x - hi.astype(f32)).astype(bf16)
    return hi, lo


def _dot_sel_lhs(sel, x, dims=NN):
    return sum(lax.dot_general(sel, t, dims, preferred_element_type=f32) for t in _split2(x))


def _dot_sel_rhs(x, sel, dims=NN):
    return sum(lax.dot_general(t, sel, dims, preferred_element_type=f32) for t in _split2(x))


def _group_ones(n, width):
    return (_iota((n, n), 0) // width == _iota((n, n), 1) // width).astype(bf16)


def _sigmoid(x):
    return 1.0 / (1.0 + jnp.exp(-x))


def _softplus(x):
    return jnp.maximum(x, 0.0) + jnp.log1p(jnp.exp(-jnp.abs(x)))


def _shift_rows(x, carry, s):
    xs = pltpu.roll(x, s, 0)
    fix = pltpu.roll(carry, s, 0)
    first = jnp.where(_iota(carry.shape, 0) < s, fix, xs[0:8])
    return jnp.concatenate([first, xs[8:]], axis=0)


def _bd(xp):
    xb = xp.astype(bf16)
    left = _iota(xb.shape, 1) < HEAD_DIM
    zero = jnp.zeros_like(xb)
    return jnp.concatenate([jnp.where(left, xb, zero), jnp.where(left, zero, xb)], axis=0)


def _pair_masks(c):
    ri, cj = _iota((c, 2 * c), 0), _iota((c, 2 * c), 1) & (c - 1)
    eye = (ri == cj).astype(f32)
    m16 = (ri // 16 == cj // 16).astype(f32)
    mo1 = ((ri // 32 == cj // 32) & (ri // 16 == cj // 16 + 1)).astype(f32)
    mo2 = ((ri // 32 == 1) & (cj // 32 == 0)).astype(f32)
    return dict(eye=eye, m16=m16, mo1=mo1, mo2=mo2, strict=ri > cj, incl=ri >= cj)


def _tri_inv(lms, masks, tick=lambda: None):
    c = RW_C

    def each(fn, *lists):
        out = []
        for args in zip(*lists):
            out.append(fn(*args))
            tick()
        return out

    ps = [-(lm * masks["m16"]) for lm in lms]
    ts = [masks["eye"] + p for p in ps]
    ps = each(lambda p: _dot(p, _bd(p)), ps)
    for _ in range(2):
        outs = each(lambda t, p: _dot(jnp.concatenate([t, p], axis=0), _bd(p)), ts, ps)
        ts = [t + o[0:c] for t, o in zip(ts, outs)]
        ps = [o[c:] for o in outs]
    ts = each(lambda t, p: t + _dot(t, _bd(p)), ts, ps)
    for mo in (masks["mo1"], masks["mo2"]):
        xs = each(lambda lm, t: _dot(lm * mo, _bd(t)), lms, ts)
        ts = each(lambda t, x: t - _dot(t, _bd(x)), ts, xs)
    return ts


IN_TM = 1024
IN_TN = 512
SG_C = 128


def _sgu_pieces(c_d, lg_ref, lb_ref, w_ref, bias_ref, y_ref):
    wr, wc = _iota((SG_C, 4 * SG_C), 0), _iota((SG_C, 4 * SG_C), 1)
    w = jnp.where(wc % SG_C <= wr, w_ref[...], 0.0).astype(bf16)
    lane_g = _iota((SG_C, GROUP_W), 1) // HEAD_DIM
    for n in range(c_d.shape[0] // SG_C):
        x = c_d[n * SG_C:(n + 1) * SG_C, :]
        gx = 0.5 * x * (1.0 + jnp.tanh(0.7978845608028654 * (x + 0.044715 * x * x * x)))
        u = gx[:, 0:256]
        v = gx[:, 256:512]
        mu = jnp.mean(v, axis=-1, keepdims=True)
        vc = v - mu
        var = jnp.mean(vc * vc, axis=-1, keepdims=True)
        v = vc * lax.rsqrt(var + 1e-5) * lg_ref[...] + lb_ref[...]
        vst = jnp.concatenate([jnp.where(lane_g == g, v, 0.0) for g in range(4)], axis=0)
        mixed = jnp.dot(w, vst.astype(bf16), preferred_element_type=f32) + bias_ref[...]
        y_ref[n * SG_C:(n + 1) * SG_C, :] = (u * mixed).astype(bf16)
        yield


def _inproj_kernel(x_ref, g_ref, w_hbm, lg_ref, lb_ref, ws_ref, bias_ref, oa_ref, ob_ref, oc_ref, om_ref,
                   yd_ref, w_ref, sem, *, layer):
    @pl.when(pl.program_id(0) == 0)
    def _():
        cp = pltpu.make_async_copy(w_hbm.at[layer], w_ref, sem.at[0])
        cp.start()
        cp.wait()

    x = x_ref[...]
    ms = jnp.mean(x * x, axis=-1, keepdims=True)
    h = (x * lax.rsqrt(ms + EPS) * g_ref[...]).astype(bf16)
    c_d = jnp.dot(h, w_ref[:, NA + NB + NC + NM:], preferred_element_type=f32)
    sgu = _sgu_pieces(c_d, lg_ref, lb_ref, ws_ref, bias_ref, yd_ref)
    off = 0
    for o_ref, n in ((oa_ref, NA), (ob_ref, NB), (oc_ref, NC), (om_ref, NM)):
        for j in range(0, n, IN_TN):
            w = min(IN_TN, n - j)
            o_ref[:, j:j + w] = jnp.dot(h, w_ref[:, off + j:off + j + w], preferred_element_type=f32)
            next(sgu, None)
        off += n
    for _ in sgu:
        pass


def _in_proj(x, p, l):
    tm = IN_TM
    names = ("g", "w", "ln_g", "ln_b", "w_cat", "bias_tile")
    specs = [pl.BlockSpec(memory_space=pl.ANY) if k == "w" else _layer_spec(p[k], l) for k in names]
    return pl.pallas_call(
        functools.partial(_inproj_kernel, layer=l),
        grid=(SEQ // tm,),
        in_specs=[pl.BlockSpec((tm, D_MODEL), lambda i: (i, 0))] + specs,
        out_specs=[pl.BlockSpec((tm, n), lambda i: (i, 0)) for n in (NA, NB, NC, NM, GROUP_W)],
        out_shape=[jax.ShapeDtypeStruct((SEQ, n), f32) for n in (NA, NB, NC, NM)]
        + [jax.ShapeDtypeStruct((SEQ, GROUP_W), bf16)],
        scratch_shapes=[pltpu.VMEM((D_MODEL, N_PAD), bf16), pltpu.SemaphoreType.DMA((1,))],
        compiler_params=pltpu.CompilerParams(dimension_semantics=("arbitrary",),
                                             vmem_limit_bytes=VMEM_LIMIT),
        name="in_proj",
    )(x, *[p[k] for k in names])


RW_TB = 512
RW_C = 64
RW_PRO = 2
RW_TICKS = 13


def _rwkv_kernel(*refs, has_vres):
    if has_vres:
        (c_ref, m_ref, vf_ref, mu_ref, w0_ref, wup_ref, a0_ref, aup_ref, gup_ref, kk_ref, ka_ref, rk_ref,
         lw_ref, lb_ref, v0_ref, vup_ref, y_ref,
         carry_ref, s_ref, r_s, k_s, v_s, lw_s, al_s, be_s, y_s,
         ta_p, rt_p, vb_p, bw_p, kw_p, arb_p, tv_p, av_p, bonus_p, g_p, dl_p) = refs
    else:
        (c_ref, mu_ref, w0_ref, wup_ref, a0_ref, aup_ref, gup_ref, kk_ref, ka_ref, rk_ref,
         lw_ref, lb_ref, y_ref, vf_out_ref,
         carry_ref, s_ref, r_s, k_s, v_s, lw_s, al_s, be_s, y_s,
         ta_p, rt_p, vb_p, bw_p, kw_p, arb_p, tv_p, av_p, bonus_p, g_p, dl_p) = refs
    step = pl.program_id(0)

    @pl.when(step == 0)
    def _():
        for ref in (carry_ref, s_ref, ta_p, rt_p, vb_p, bw_p, kw_p, arb_p, tv_p, av_p, bonus_p, g_p, dl_p):
            ref[...] = jnp.zeros_like(ref)

    wr = step % 2
    rd = 1 - wr
    c = RW_C
    nchunk = RW_TB // c
    npair = N_HEADS // 2
    hsl = lambda h: slice(2 * h * HEAD_DIM, 2 * (h + 1) * HEAD_DIM)
    bd_mask = (_iota((2 * c, 2 * c), 0) // c == _iota((2 * c, 2 * c), 1) // c).astype(f32)
    state = [s_ref[h] for h in range(npair)]

    def recurrence():
        for n in range(nchunk):
            rows = slice(n * c, (n + 1) * c)
            sas = [lax.dot_general(jnp.concatenate([ta_p[rd, rows, hsl(h)], rt_p[rd, rows, hsl(h)]], axis=0),
                                   state[h].astype(bf16), NT, preferred_element_type=f32)
                   for h in range(npair)]
            yield
            us = [sas[h][0:c] + tv_p[rd, rows, hsl(h)] for h in range(npair)]
            upd = [_dot(jnp.concatenate([us[h].astype(bf16), vb_p[rd, rows, hsl(h)]], axis=0),
                        jnp.concatenate([bw_p[rd, rows, hsl(h)], kw_p[rd, rows, hsl(h)]], axis=0), TN) * bd_mask
                   for h in range(npair)]
            yield
            dl = dl_p[rd, n:n + 1, :]
            for h in range(npair):
                state[h] = state[h] * dl[:, hsl(h)] + upd[h]
            ys = [sas[h][c:] + _dot(arb_p[rd, rows, hsl(h)], _bd(us[h])) + av_p[rd, rows, hsl(h)]
                  for h in range(npair)]
            y_s[rows, :] = jnp.concatenate(ys, axis=1)
            yield

    chain = recurrence()
    calls = [0]

    def advance(pieces=1):
        for _ in range(pieces):
            next(chain, None)

    def tick():
        calls[0] += 1
        if calls[0] % RW_TICKS == 0:
            advance()

    ones_bd = _group_ones(GROUP_W, HEAD_DIM)
    x = c_ref[:, 0:1024]
    x_prev = _shift_rows(x, carry_ref[...], 1)
    carry_ref[...] = x[RW_TB - 8:RW_TB]
    xs = x + (x_prev - x) * mu_ref[...]
    advance(RW_PRO)
    r = xs[:, 0:256]
    k = xs[:, 256:512]
    v = xs[:, 512:768]
    lora = xs[:, 768:896]
    w_pre = w0_ref[...] + _dot(jnp.tanh(lora), wup_ref[...])
    lw = -jnp.exp(-_softplus(-w_pre) - 0.5)
    advance(RW_PRO)
    a = _sigmoid(a0_ref[...] + _dot(lora, aup_ref[...]))
    g_p[wr] = _dot(_sigmoid(xs[:, 896:1024]), gup_ref[...])
    advance(RW_PRO)
    if has_vres:
        mix = _sigmoid(v0_ref[...] + _dot(m_ref[...], vup_ref[...]))
        v = v + (vf_ref[...] - v) * mix
    else:
        @pl.when(step < SEQ // RW_TB)
        def _():
            vf_out_ref[...] = v
    kk = k * kk_ref[...]
    kk = kk * lax.rsqrt(_segsum(kk * kk, ones_bd) + 1e-24)
    advance(RW_PRO)
    k = k * (1.0 + (a - 1.0) * ka_ref[...])
    bonus_p[wr] = _segsum(r * k * rk_ref[...], ones_bd) * v
    advance(RW_PRO)
    r_s[...] = r
    k_s[...] = k
    v_s[...] = v
    lw_s[...] = lw
    al_s[...] = -kk
    be_s[...] = kk * a

    tril_incl = (_iota((c, c), 0) >= _iota((c, c), 1)).astype(bf16)
    masks = _pair_masks(c)
    low_strict, low_incl = masks["strict"], masks["incl"]

    items = [(n, h) for n in range(nchunk) for h in range(npair)]
    pre = []
    for n in range(nchunk):
        rows = slice(n * c, (n + 1) * c)
        lwc = lw_s[rows, :]
        lc = _dot_sel_lhs(tril_incl, lwc)
        llast = lc[c - 1:c, :]
        e_out = jnp.exp(-lc)
        e_rest = jnp.exp(llast - lc)
        kc, bec = k_s[rows, :], be_s[rows, :]
        rt = r_s[rows, :] * jnp.exp(lc)
        pre.append(dict(rt=rt, at=al_s[rows, :] * jnp.exp(lc - lwc), bt=bec * e_out, kt=kc * e_out,
                        v=v_s[rows, :]))
        rt_p[wr, rows, :] = rt.astype(bf16)
        vb_p[wr, rows, :] = v_s[rows, :].astype(bf16)
        bw_p[wr, rows, :] = (bec * e_rest).astype(bf16)
        kw_p[wr, rows, :] = (kc * e_rest).astype(bf16)
        dl_p[wr, n:n + 1, :] = jnp.exp(llast)
        tick()
    ms = []
    for n, h in items:
        ms.append(_dot(jnp.concatenate([pre[n]["at"][:, hsl(h)], pre[n]["rt"][:, hsl(h)]], axis=0),
                       jnp.concatenate([_bd(pre[n]["bt"][:, hsl(h)]), _bd(pre[n]["kt"][:, hsl(h)])], axis=0),
                       NT))
        arb_p[wr, n * c:(n + 1) * c, hsl(h)] = jnp.where(low_incl, ms[-1][c:, 0:2 * c], 0.0).astype(bf16)
        tick()
    tms = _tri_inv([jnp.where(low_strict, -m[0:c, 0:2 * c], 0.0) for m in ms], masks, tick)
    avs = []
    for m, (n, h) in zip(ms, items):
        avs.append(_dot(jnp.concatenate([jnp.where(low_strict, m[0:c, 2 * c:], 0.0),
                                         jnp.where(low_incl, m[c:, 2 * c:], 0.0)], axis=0),
                        _bd(pre[n]["v"][:, hsl(h)])))
        av_p[wr, n * c:(n + 1) * c, hsl(h)] = avs[-1][c:]
        tick()
    for tm, av, (n, h) in zip(tms, avs, items):
        tt = _dot(tm, jnp.concatenate([_bd(pre[n]["at"][:, hsl(h)]), _bd(av[0:c])], axis=1))
        ta_p[wr, n * c:(n + 1) * c, hsl(h)] = tt[:, 0:2 * c].astype(bf16)
        tv_p[wr, n * c:(n + 1) * c, hsl(h)] = tt[:, 2 * c:]
        tick()
    for _ in chain:
        pass
    for h in range(npair):
        s_ref[h] = state[h]

    y = y_s[...]
    inv_d = 1.0 / HEAD_DIM
    mean = _segsum(y, ones_bd) * inv_d
    yc = y - mean
    var = _segsum(yc * yc, ones_bd) * inv_d
    y = yc * lax.rsqrt(var + RWKV_GN_EPS) * lw_ref[...] + lb_ref[...]
    y_ref[...] = ((y + bonus_p[rd]) * g_p[rd]).astype(bf16)


def _rwkv(c_a, c_m, v_first, p, l):
    has_vres = v_first is not None
    nblk = SEQ // RW_TB
    blk = lambda n: pl.BlockSpec((RW_TB, n), lambda i: (jnp.minimum(i, nblk - 1), 0))
    blk_prev = lambda n: pl.BlockSpec((RW_TB, n), lambda i: (jnp.maximum(i - 1, 0), 0))
    in_specs = [blk(NA)]
    args = [c_a]
    if has_vres:
        in_specs += [blk(NM), blk(GROUP_W)]
        args += [c_m, v_first]
    names = ("mu", "w0", "w_up", "a0", "a_up", "g_up", "k_k", "k_a", "r_k", "lnx_w", "lnx_b")
    in_specs += [_layer_spec(p[k], l) for k in names]
    args += [p[k] for k in names]
    if has_vres:
        in_specs += [_layer_spec(p["v0"], l - 1), _layer_spec(p["vres_up"], l - 1)]
        args += [p["v0"], p["vres_up"]]
        out_specs = blk_prev(GROUP_W)
        out_shape = jax.ShapeDtypeStruct((SEQ, GROUP_W), bf16)
    else:
        out_specs = [blk_prev(GROUP_W), blk(GROUP_W)]
        out_shape = [jax.ShapeDtypeStruct((SEQ, GROUP_W), bf16),
                     jax.ShapeDtypeStruct((SEQ, GROUP_W), f32)]
    scratch = [pltpu.VMEM((8, 1024), f32), pltpu.VMEM((N_HEADS // 2, 2 * HEAD_DIM, 2 * HEAD_DIM), f32)]
    scratch += [pltpu.VMEM((RW_TB, GROUP_W), f32) for _ in range(7)]
    scratch += [pltpu.VMEM((2, RW_TB, GROUP_W), bf16) for _ in range(6)]
    scratch += [pltpu.VMEM((2, RW_TB, GROUP_W), f32) for _ in range(4)]
    scratch += [pltpu.VMEM((2, RW_TB // RW_C, GROUP_W), f32)]
    out = pl.pallas_call(
        functools.partial(_rwkv_kernel, has_vres=has_vres),
        grid=(nblk + 1,),
        in_specs=in_specs,
        out_specs=out_specs,
        out_shape=out_shape,
        scratch_shapes=scratch,
        compiler_params=pltpu.CompilerParams(dimension_semantics=("arbitrary",),
                                             vmem_limit_bytes=VMEM_LIMIT),
        name="rwkv7",
    )(*args)
    if has_vres:
        return out, v_first
    return out[0], out[1]


GD_TB = 512
GD_C = 64
GD_PRO = 2
GD_TICKS = 9


def _gdn_kernel(c_ref, m_ref, cw_ref, alog_ref, dtb_ref, alogc_ref, dtbc_ref, ng_ref, y_ref,
                carry_ref, s_ref, q_s, k_s, v_s, be_s, g_s, o_s,
                u_p, w_p, qe_p, kd_p, qk_p, gz_p, egl_p):
    step = pl.program_id(0)

    @pl.when(step == 0)
    def _():
        for ref in (carry_ref, s_ref, u_p, w_p, qe_p, kd_p, qk_p, gz_p, egl_p):
            ref[...] = jnp.zeros_like(ref)

    wr = step % 2
    rd = 1 - wr
    c = GD_C
    nchunk = GD_TB // c
    npair = N_HEADS // 2
    hsl = lambda h: slice(2 * h * HEAD_DIM, 2 * (h + 1) * HEAD_DIM)
    bd_mask = (_iota((2 * c, 2 * c), 0) // c == _iota((2 * c, 2 * c), 1) // c).astype(f32)
    ones_bd = _group_ones(GROUP_W, HEAD_DIM)
    state = [s_ref[h] for h in range(npair)]

    def recurrence():
        for n in range(nchunk):
            rows = slice(n * c, (n + 1) * c)
            wss = [jnp.dot(jnp.concatenate([w_p[rd, rows, hsl(h)], qe_p[rd, rows, hsl(h)]], axis=0),
                           state[h].astype(bf16), preferred_element_type=f32)
                   for h in range(npair)]
            yield
            vns = [u_p[rd, rows, hsl(h)] - wss[h][0:c] for h in range(npair)]
            upd = [_dot(kd_p[rd, rows, hsl(h)], vns[h], TN) * bd_mask for h in range(npair)]
            yield
            egl = egl_p[rd, n:n + 1, :]
            for h in range(npair):
                state[h] = state[h] * egl[:, hsl(h)] + upd[h]
            os_ = [wss[h][c:] + _dot(qk_p[rd, rows, hsl(h)], _bd(vns[h])) for h in range(npair)]
            o_s[rows, :] = jnp.concatenate(os_, axis=1)
            yield

    chain = recurrence()
    calls = [0]

    def tick():
        calls[0] += 1
        if calls[0] % GD_TICKS == 0:
            next(chain, None)

    raw = c_ref[:, 0:768]
    carry = carry_ref[...]
    conv = raw * cw_ref[3:4, :]
    for s in range(1, 4):
        conv = conv + _shift_rows(raw, carry, s) * cw_ref[3 - s:4 - s, :]
    carry_ref[...] = raw[GD_TB - 8:GD_TB]
    qkv = conv * _sigmoid(conv)
    for _ in range(GD_PRO):
        next(chain, None)
    q = qkv[:, 0:256]
    k = qkv[:, 256:512]
    q_s[...] = q * lax.rsqrt(_segsum(q * q, ones_bd) + 1e-6) * (HEAD_DIM ** -0.5)
    for _ in range(GD_PRO):
        next(chain, None)
    k_s[...] = k * lax.rsqrt(_segsum(k * k, ones_bd) + 1e-6)
    v_s[...] = qkv[:, 512:768]
    z = c_ref[:, 768:1024]
    gz_p[wr] = z * _sigmoid(z)
    for _ in range(GD_PRO):
        next(chain, None)
    small = m_ref[...]
    er, ec = _iota((NM, GROUP_W), 0), _iota((NM, GROUP_W), 1)
    b_exp = _dot_sel_rhs(small, (er == ec // HEAD_DIM + M_GDN).astype(bf16))
    a_exp = _dot_sel_rhs(small, (er == ec // HEAD_DIM + M_GDN + N_HEADS).astype(bf16))
    be_s[...] = _sigmoid(b_exp)
    g_s[...] = -jnp.exp(alog_ref[...]) * _softplus(a_exp + dtb_ref[...])

    tril_incl = (_iota((c, c), 0) >= _iota((c, c), 1)).astype(bf16)
    masks = _pair_masks(c)
    low_strict, low_incl = masks["strict"], masks["incl"]
    sel8 = (_iota((8, NM), 0) + M_GDN == _iota((8, NM), 1)).astype(bf16)
    g_rows = -jnp.exp(alogc_ref[...]) * _softplus(_dot_sel_lhs(sel8, small, NT) + dtbc_ref[...])
    tj, ti = _iota((GD_TB, GD_TB), 0), _iota((GD_TB, GD_TB), 1)
    gc_rows = _dot_sel_rhs(g_rows, ((tj // c == ti // c) & (tj <= ti)).astype(bf16))

    items = [(n, h) for n in range(nchunk) for h in range(npair)]
    pre = []
    for n in range(nchunk):
        rows = slice(n * c, (n + 1) * c)
        gc = _dot_sel_lhs(tril_incl, g_s[rows, :])
        glast = gc[c - 1:c, :]
        egc = jnp.exp(gc)
        kc, bc = k_s[rows, :], be_s[rows, :]
        kb = kc * bc
        pre.append(dict(gc=gc, k=kc, kb=kb, q=q_s[rows, :], vb=v_s[rows, :] * bc, kbe=kb * egc))
        qe_p[wr, rows, :] = (q_s[rows, :] * egc).astype(bf16)
        kd_p[wr, rows, :] = (kc * jnp.exp(glast - gc)).astype(bf16)
        egl_p[wr, n:n + 1, :] = jnp.exp(glast)
        tick()

    def gc_row(n, h):
        return jnp.concatenate([gc_rows[N_HEADS + 2 * h + i:N_HEADS + 2 * h + i + 1, n * c:(n + 1) * c]
                                for i in range(2)], axis=1)

    dms, aqs = [], []
    for n, h in items:
        dms.append(jnp.exp(jnp.where(low_incl, pre[n]["gc"][:, hsl(h)] - gc_row(n, h), NEG)))
        aqs.append(_dot(jnp.concatenate([pre[n]["kb"][:, hsl(h)], pre[n]["q"][:, hsl(h)]], axis=0),
                        _bd(pre[n]["k"][:, hsl(h)]), NT))
        qk_p[wr, n * c:(n + 1) * c, hsl(h)] = (aqs[-1][c:] * dms[-1]).astype(bf16)
        tick()
    tms = _tri_inv([jnp.where(low_strict, aq[0:c] * dm, 0.0) for aq, dm in zip(aqs, dms)], masks, tick)
    for tm, (n, h) in zip(tms, items):
        uw = _dot(tm, jnp.concatenate([_bd(pre[n]["vb"][:, hsl(h)]), _bd(pre[n]["kbe"][:, hsl(h)])], axis=1))
        u_p[wr, n * c:(n + 1) * c, hsl(h)] = uw[:, 0:2 * c]
        w_p[wr, n * c:(n + 1) * c, hsl(h)] = uw[:, 2 * c:].astype(bf16)
        tick()
    for _ in chain:
        pass
    for h in range(npair):
        s_ref[h] = state[h]

    o = o_s[...]
    ms = _segsum(o * o, ones_bd) * (1.0 / HEAD_DIM)
    y_ref[...] = (o * lax.rsqrt(ms + EPS) * ng_ref[...] * gz_p[rd]).astype(bf16)


def _gdn(c_b, c_m, p, l):
    names = ("conv_w", "a_log", "dt_bias", "a_log_col", "dt_bias_col", "norm_g")
    nblk = SEQ // GD_TB
    scratch = [pltpu.VMEM((8, 768), f32), pltpu.VMEM((N_HEADS // 2, 2 * HEAD_DIM, 2 * HEAD_DIM), f32)]
    scratch += [pltpu.VMEM((GD_TB, GROUP_W), f32) for _ in range(6)]
    scratch += [pltpu.VMEM((2, GD_TB, GROUP_W), f32)]
    scratch += [pltpu.VMEM((2, GD_TB, GROUP_W), bf16) for _ in range(4)]
    scratch += [pltpu.VMEM((2, GD_TB, GROUP_W), f32), pltpu.VMEM((2, GD_TB // GD_C, GROUP_W), f32)]
    return pl.pallas_call(
        _gdn_kernel,
        grid=(nblk + 1,),
        in_specs=[pl.BlockSpec((GD_TB, n), lambda i: (jnp.minimum(i, nblk - 1), 0)) for n in (NB, NM)]
        + [_layer_spec(p[k], l) for k in names],
        out_specs=pl.BlockSpec((GD_TB, GROUP_W), lambda i: (jnp.maximum(i - 1, 0), 0)),
        out_shape=jax.ShapeDtypeStruct((SEQ, GROUP_W), bf16),
        scratch_shapes=scratch,
        compiler_params=pltpu.CompilerParams(dimension_semantics=("arbitrary",),
                                             vmem_limit_bytes=VMEM_LIMIT),
        name="gdn",
    )(c_b, c_m, *[p[k] for k in names])


GL_TB = 256
GL_C = 16
GL_S = 8


def _gla_kernel(c_ref, m_ref, gup_ref, gb_ref, ng_ref, y_ref, st_ref, sx_s, o_s):
    @pl.when(pl.program_id(0) == 0)
    def _():
        st_ref[...] = jnp.zeros_like(st_ref)

    tb, c, s = GL_TB, GL_C, GL_S
    nchunk, nsub = tb // c, tb // s
    q = c_ref[:, 0:128] * (GLA_HEAD_K ** -0.5)
    k = c_ref[:, 128:256]
    v = c_ref[:, 256:512]
    pre = _dot(m_ref[...], gup_ref[...]) + gb_ref[...]
    la = -_softplus(-pre) * (1.0 / 16.0)
    tj, ti = _iota((tb, tb), 0), _iota((tb, tb), 1)
    b = _dot_sel_lhs(((tj // c == ti // c) & (ti <= tj)).astype(bf16), la)
    qi = q * jnp.exp(b)

    ind_e = (_iota((GLA_KEY, GROUP_W), 0) // GLA_HEAD_K == _iota((GLA_KEY, GROUP_W), 1) // HEAD_DIM).astype(bf16)
    bd_mask = (_iota((GROUP_W, GLA_KEY), 0) // HEAD_DIM == _iota((GROUP_W, GLA_KEY), 1) // GLA_HEAD_K).astype(f32)

    b3, q3, k3 = (t.reshape(nsub, s, GLA_KEY) for t in (b, q, k))
    ri = _iota((nsub, s, GLA_KEY), 1)
    terms = []
    for j in range(s):
        e = jnp.exp(jnp.where(ri >= j, b3 - b3[:, j:j + 1, :], NEG))
        terms.append((q3 * (k3[:, j:j + 1, :] * e)).reshape(tb, GLA_KEY).astype(bf16))
    sx_s[...] = jnp.dot(jnp.concatenate(terms, axis=0), ind_e, preferred_element_type=f32)
    v3 = v.reshape(nsub, s, GROUP_W)
    o3 = sx_s[0:tb, :].reshape(nsub, s, GROUP_W) * v3[:, 0:1, :]
    for j in range(1, s):
        o3 = o3 + sx_s[j * tb:(j + 1) * tb, :].reshape(nsub, s, GROUP_W) * v3[:, j:j + 1, :]

    b4, q4, k4 = (t.reshape(nchunk, 2, s, GLA_KEY) for t in (b, q, k))
    bref = b4[:, 0, s - 1:s, :]
    qd = (q4[:, 1] * jnp.exp(b4[:, 1] - bref)).reshape(nchunk * s, GLA_KEY)
    kd = (k4[:, 0] * jnp.exp(bref - b4[:, 0])).reshape(nchunk * s, GLA_KEY)
    v0 = v.reshape(nchunk, 2, s, GROUP_W)[:, 0].reshape(nchunk * s, GROUP_W)
    head_k = _iota((nchunk * s, GLA_KEY), 1) // GLA_HEAD_K
    head_v = _iota((nchunk * s, GROUP_W), 1) // HEAD_DIM
    ks = jnp.concatenate([jnp.where(head_k == h, kd, 0.0) for h in range(N_HEADS)], axis=0)
    vs = jnp.concatenate([jnp.where(head_v == h, v0, 0.0) for h in range(N_HEADS)], axis=0)
    sc = _dot(qd, ks, NT)
    sr, scol = _iota(sc.shape, 0), _iota(sc.shape, 1)
    sc = jnp.where(sr // s == (scol % (nchunk * s)) // s, sc, 0.0)
    o_off = _dot(sc, vs).reshape(nchunk, 1, s, GROUP_W)
    o4 = o3.reshape(nchunk, 2, s, GROUP_W)
    o_intra = jnp.concatenate([o4[:, 0:1], o4[:, 1:2] + o_off], axis=1).reshape(tb, GROUP_W)

    blasts = [b[(n + 1) * c - 1:(n + 1) * c, :] for n in range(nchunk)]
    upds = [_dot(v[n * c:(n + 1) * c], k[n * c:(n + 1) * c] * jnp.exp(blasts[n] - b[n * c:(n + 1) * c]), TN)
            * bd_mask for n in range(nchunk)]
    st = st_ref[...]
    for n in range(nchunk):
        rows = slice(n * c, (n + 1) * c)
        o_s[rows, :] = o_intra[rows] + _dot(qi[rows], st, NT)
        st = st * jnp.exp(blasts[n]) + upds[n]
    st_ref[...] = st

    o = o_s[...]
    ms = _segsum(o * o, _group_ones(GROUP_W, HEAD_DIM)) * (1.0 / HEAD_DIM)
    gate = c_ref[:, 512:768]
    y_ref[...] = (o * lax.rsqrt(ms + EPS) * ng_ref[...] * (gate * _sigmoid(gate))).astype(bf16)


def _gla(c_c, c_m, p, l):
    names = ("gk_up", "gk_bias", "norm_g")
    return pl.pallas_call(
        _gla_kernel,
        grid=(SEQ // GL_TB,),
        in_specs=[pl.BlockSpec((GL_TB, n), lambda i: (i, 0)) for n in (NC, NM)]
        + [_layer_spec(p[k], l) for k in names],
        out_specs=pl.BlockSpec((GL_TB, GROUP_W), lambda i: (i, 0)),
        out_shape=jax.ShapeDtypeStruct((SEQ, GROUP_W), bf16),
        scratch_shapes=[pltpu.VMEM((GROUP_W, GLA_KEY), f32),
                        pltpu.VMEM((GL_TB * GL_S, GROUP_W), f32), pltpu.VMEM((GL_TB, GROUP_W), f32)],
        compiler_params=pltpu.CompilerParams(dimension_semantics=("arbitrary",),
                                             vmem_limit_bytes=VMEM_LIMIT),
        name="gla",
    )(c_c, c_m, *[p[k] for k in names])


FF_TM = 512
FF_TF = 1024


def _outffn_kernel(*refs, final, layer):
    if final:
        (x_ref, ya_ref, yb_ref, yc_ref, yd_ref, wo_hbm, g_ref, wu_hbm, wd_hbm, gf_ref, o_ref,
         wo_ref, wu_ref, wd_ref, sem) = refs
    else:
        (x_ref, ya_ref, yb_ref, yc_ref, yd_ref, wo_hbm, g_ref, wu_hbm, wd_hbm, o_ref,
         wo_ref, wu_ref, wd_ref, sem) = refs

    @pl.when(pl.program_id(0) == 0)
    def _():
        copies = [pltpu.make_async_copy(src.at[layer], dst, sem.at[i])
                  for i, (src, dst) in enumerate(((wo_hbm, wo_ref), (wu_hbm, wu_ref), (wd_hbm, wd_ref)))]
        for cp in copies:
            cp.start()
        for cp in copies:
            cp.wait()

    y = jnp.concatenate([ya_ref[...], yb_ref[...], yc_ref[...], yd_ref[...]], axis=1)
    x1 = x_ref[...] + jnp.dot(y, wo_ref[...], preferred_element_type=f32)
    ms = jnp.mean(x1 * x1, axis=-1, keepdims=True)
    h = (x1 * lax.rsqrt(ms + EPS) * g_ref[...]).astype(bf16)
    x2 = x1
    for kf in range(D_FF // FF_TF):
        cols = slice(kf * FF_TF, (kf + 1) * FF_TF)
        hid = jnp.maximum(jnp.dot(h, wu_ref[:, cols], preferred_element_type=f32), 0.0)
        x2 = x2 + jnp.dot((hid * hid).astype(bf16), wd_ref[cols, :], preferred_element_type=f32)
    if final:
        ms = jnp.mean(x2 * x2, axis=-1, keepdims=True)
        x2 = x2 * lax.rsqrt(ms + EPS) * gf_ref[...]
    o_ref[...] = x2


def _out_ffn(x, ys, p, l, final):
    hbm = pl.BlockSpec(memory_space=pl.ANY)
    yspec = pl.BlockSpec((FF_TM, GROUP_W), lambda i: (i, 0))
    in_specs = [pl.BlockSpec((FF_TM, D_MODEL), lambda i: (i, 0)), yspec, yspec, yspec, yspec,
                hbm, _layer_spec(p["g"], l), hbm, hbm]
    args = [x, *ys, p["w_out"], p["g"], p["w_up"], p["w_down"]]
    if final:
        in_specs.append(pl.BlockSpec((1, D_MODEL), lambda i: (0, 0)))
        args.append(p["g_final"])
    return pl.pallas_call(
        functools.partial(_outffn_kernel, final=final, layer=l),
        grid=(SEQ // FF_TM,),
        in_specs=in_specs,
        out_specs=pl.BlockSpec((FF_TM, D_MODEL), lambda i: (i, 0)),
        out_shape=jax.ShapeDtypeStruct((SEQ, D_MODEL), f32),
        scratch_shapes=[pltpu.VMEM((D_MODEL, D_MODEL), bf16), pltpu.VMEM((D_MODEL, D_FF), bf16),
                        pltpu.VMEM((D_FF, D_MODEL), bf16), pltpu.SemaphoreType.DMA((3,))],
        compiler_params=pltpu.CompilerParams(dimension_semantics=("arbitrary",),
                                             vmem_limit_bytes=VMEM_LIMIT),
        name="out_ffn",
    )(*args)


def kernel(x, w_in, w_out, norm_mix_g, norm_ffn_g, norm_final_g, rwkv_mu, rwkv_w0, rwkv_w_up, rwkv_a0, rwkv_a_up, rwkv_g_up, rwkv_k_k, rwkv_k_a, rwkv_r_k, rwkv_lnx_w, rwkv_lnx_b, rwkv_v0, rwkv_vres_down, rwkv_vres_up, gdn_conv_w, gdn_a_log, gdn_dt_bias, gdn_norm_g, gla_gk_up, gla_gk_bias, gla_norm_g, sgu_ln_g, sgu_ln_b, sgu_w_s, sgu_b_s, ffn_w_up, ffn_w_down):
    depth = w_in.shape[0]
    assert x.shape == (1, SEQ, D_MODEL) and x.dtype == f32, (x.shape, x.dtype)
    assert w_in.shape == (depth, D_MODEL, 3352) and ffn_w_up.shape == (depth, D_MODEL, D_FF)
    assert rwkv_vres_down.shape == (depth - 1, D_MODEL, 32) and gdn_conv_w.shape == (depth, 4, 768)
    row = lambda a: a.reshape(depth, 1, -1)
    per_head = lambda a: jnp.repeat(a, HEAD_DIM, axis=-1).reshape(depth, 1, -1)
    pad_cols = lambda w, n: jnp.pad(w, ((0, 0), (0, 0), (0, n - w.shape[2])))
    pad_rows = lambda w, top, total: jnp.pad(w, ((0, 0), (top, total - top - w.shape[1]), (0, 0)))
    vres_down = jnp.pad(rwkv_vres_down, ((1, 0), (0, 0), (0, 0)))
    narrow = pad_cols(jnp.concatenate([vres_down, w_in[:, :, 2048:2056], w_in[:, :, 2824:2840]], axis=2), NM)
    w_comb = jnp.concatenate(
        [w_in[:, :, 0:1024], w_in[:, :, 1024:2048], w_in[:, :, 2056:2824], narrow,
         w_in[:, :, 2840:3352]], axis=2).astype(bf16)
    p_in = dict(g=row(norm_mix_g), w=w_comb, ln_g=row(sgu_ln_g), ln_b=row(sgu_ln_b),
                w_cat=sgu_w_s.transpose(0, 2, 1, 3).reshape(depth, SG_C, 4 * SG_C),
                bias_tile=jnp.repeat(sgu_b_s.transpose(0, 2, 1), HEAD_DIM, axis=2))
    p_rwkv = dict(mu=row(rwkv_mu), w0=row(rwkv_w0), w_up=pad_rows(rwkv_w_up, 0, 128),
                  a0=row(rwkv_a0), a_up=pad_rows(rwkv_a_up, 64, 128), g_up=rwkv_g_up,
                  k_k=row(rwkv_k_k), k_a=row(rwkv_k_a), r_k=row(rwkv_r_k),
                  lnx_w=row(rwkv_lnx_w), lnx_b=row(rwkv_lnx_b),
                  v0=rwkv_v0.reshape(depth - 1, 1, -1), vres_up=pad_rows(rwkv_vres_up, M_VRES, NM))
    p_gdn = dict(conv_w=gdn_conv_w, a_log=per_head(gdn_a_log), dt_bias=per_head(gdn_dt_bias),
                 a_log_col=jnp.pad(gdn_a_log, ((0, 0), (N_HEADS, 0))).reshape(depth, 8, 1),
                 dt_bias_col=jnp.pad(gdn_dt_bias, ((0, 0), (N_HEADS, 0))).reshape(depth, 8, 1),
                 norm_g=row(jnp.tile(gdn_norm_g, (1, N_HEADS))))
    p_gla = dict(gk_up=pad_rows(gla_gk_up, M_GLA, NM), gk_bias=row(gla_gk_bias),
                 norm_g=row(jnp.tile(gla_norm_g, (1, N_HEADS))))
    p_ffn = dict(w_out=w_out.astype(bf16), g=row(norm_ffn_g), w_up=ffn_w_up.astype(bf16),
                 w_down=ffn_w_down.astype(bf16), g_final=norm_final_g.reshape(1, -1))

    xx = x[0]
    v_first = None
    for l in range(depth):
        c_a, c_b, c_c, c_m, y_d = _in_proj(xx, p_in, l)
        y_a, v_first = _rwkv(c_a, c_m, v_first, p_rwkv, l)
        y_b = _gdn(c_b, c_m, p_gdn, l)
        y_c = _gla(c_c, c_m, p_gla, l)
        xx = _out_ffn(xx, (y_a, y_b, y_c, y_d), p_ffn, l, final=(l == depth - 1))
    return xx[None]
```

```python
import functools

import jax
import jax.numpy as jnp
from jax import lax
from jax.experimental import pallas as pl
from jax.experimental.pallas import tpu as pltpu

f32 = jnp.float32
bf16 = jnp.bfloat16

SEQ = 16384
D_MODEL = 1024
GROUP_W = 256
HEAD_DIM = 64
N_HEADS = 4
GLA_KEY = 128
GLA_HEAD_K = 32
D_FF = 4096
EPS = 1e-6
RWKV_GN_EPS = 64e-5
NEG = -1e30

NA, NB, NC, NM, ND = 1024, 1024, 768, 128, 512
N_PAD = NA + NB + NC + NM + ND
M_VRES, M_GDN, M_GLA = 0, 32, 40

VMEM_LIMIT = 56 * 1024 * 1024

NN = (((1,), (0,)), ((), ()))
NT = (((1,), (1,)), ((), ()))
TN = (((0,), (0,)), ((), ()))


def _dot(a, b, dims=NN):
    return lax.dot_general(a.astype(bf16), b.astype(bf16), dims, preferred_element_type=f32)


def _iota(shape, axis):
    return lax.broadcasted_iota(jnp.int32, shape, axis)


def _layer_spec(a, l):
    return pl.BlockSpec((None,) + a.shape[1:], lambda *_: (l,) + (0,) * (a.ndim - 1))


def _segsum(x, ones_bd):
    return jnp.dot(x.astype(bf16), ones_bd, preferred_element_type=f32)


def _split2(x):
    hi = x.astype(bf16)
    lo = (x - hi.astype(f32)).astype(bf16)
    return hi, lo


def _dot_sel_lhs(sel, x, dims=NN):
    return sum(lax.dot_general(sel, t, dims, preferred_element_type=f32) for t in _split2(x))


def _dot_sel_rhs(x, sel, dims=NN):
    return sum(lax.dot_general(t, sel, dims, preferred_element_type=f32) for t in _split2(x))


def _group_ones(n, width):
    return (_iota((n, n), 0) // width == _iota((n, n), 1) // width).astype(bf16)


def _sigmoid(x):
    return 1.0 / (1.0 + jnp.exp(-x))


def _softplus(x):
    return jnp.maximum(x, 0.0) + jnp.log1p(jnp.exp(-jnp.abs(x)))


def _shift_rows(x, carry, s):
    xs = pltpu.roll(x, s, 0)
    fix = pltpu.roll(carry, s, 0)
    first = jnp.where(_iota(carry.shape, 0) < s, fix, xs[0:8])
    return jnp.concatenate([first, xs[8:]], axis=0)


def _bd(xp):
    xb = xp.astype(bf16)
    left = _iota(xb.shape, 1) < HEAD_DIM
    zero = jnp.zeros_like(xb)
    return jnp.concatenate([jnp.where(left, xb, zero), jnp.where(left, zero, xb)], axis=0)


def _pair_masks(c):
    ri, cj = _iota((c, 2 * c), 0), _iota((c, 2 * c), 1) & (c - 1)
    eye = (ri == cj).astype(f32)
    m16 = (ri // 16 == cj // 16).astype(f32)
    mo1 = ((ri // 32 == cj // 32) & (ri // 16 == cj // 16 + 1)).astype(f32)
    mo2 = ((ri // 32 == 1) & (cj // 32 == 0)).astype(f32)
    return dict(eye=eye, m16=m16, mo1=mo1, mo2=mo2, strict=ri > cj, incl=ri >= cj)


def _tri_inv(lms, masks, tick=lambda: None):
    c = RW_C

    def each(fn, *lists):
        out = []
        for args in zip(*lists):
            out.append(fn(*args))
            tick()
        return out

    ps = [-(lm * masks["m16"]) for lm in lms]
    ts = [masks["eye"] + p for p in ps]
    ps = each(lambda p: _dot(p, _bd(p)), ps)
    for _ in range(2):
        outs = each(lambda t, p: _dot(jnp.concatenate([t, p], axis=0), _bd(p)), ts, ps)
        ts = [t + o[0:c] for t, o in zip(ts, outs)]
        ps = [o[c:] for o in outs]
    ts = each(lambda t, p: t + _dot(t, _bd(p)), ts, ps)
    for mo in (masks["mo1"], masks["mo2"]):
        xs = each(lambda lm, t: _dot(lm * mo, _bd(t)), lms, ts)
        ts = each(lambda t, x: t - _dot(t, _bd(x)), ts, xs)
    return ts


IN_TM = 1024
IN_TN = 512
SG_C = 128


def _sgu_pieces(c_d, lg_ref, lb_ref, w_ref, bias_ref, y_ref):
    wr, wc = _iota((SG_C, 4 * SG_C), 0), _iota((SG_C, 4 * SG_C), 1)
    w = jnp.where(wc % SG_C <= wr, w_ref[...], 0.0).astype(bf16)
    lane_g = _iota((SG_C, GROUP_W), 1) // HEAD_DIM
    for n in range(c_d.shape[0] // SG_C):
        x = c_d[n * SG_C:(n + 1) * SG_C, :]
        gx = 0.5 * x * (1.0 + jnp.tanh(0.7978845608028654 * (x + 0.044715 * x * x * x)))
        u = gx[:, 0:256]
        v = gx[:, 256:512]
        mu = jnp.mean(v, axis=-1, keepdims=True)
        vc = v - mu
        var = jnp.mean(vc * vc, axis=-1, keepdims=True)
        v = vc * lax.rsqrt(var + 1e-5) * lg_ref[...] + lb_ref[...]
        vst = jnp.concatenate([jnp.where(lane_g == g, v, 0.0) for g in range(4)], axis=0)
        mixed = jnp.dot(w, vst.astype(bf16), preferred_element_type=f32) + bias_ref[...]
        y_ref[n * SG_C:(n + 1) * SG_C, :] = (u * mixed).astype(bf16)
        yield


def _inproj_kernel(x_ref, g_ref, w_hbm, lg_ref, lb_ref, ws_ref, bias_ref, oa_ref, ob_ref, oc_ref, om_ref,
                   yd_ref, w_ref, sem, *, layer):
    @pl.when(pl.program_id(0) == 0)
    def _():
        cp = pltpu.make_async_copy(w_hbm.at[layer], w_ref, sem.at[0])
        cp.start()
        cp.wait()

    x = x_ref[...]
    ms = jnp.mean(x * x, axis=-1, keepdims=True)
    h = (x * lax.rsqrt(ms + EPS) * g_ref[...]).astype(bf16)
    c_d = jnp.dot(h, w_ref[:, NA + NB + NC + NM:], preferred_element_type=f32)
    sgu = _sgu_pieces(c_d, lg_ref, lb_ref, ws_ref, bias_ref, yd_ref)
    off = 0
    for o_ref, n in ((oa_ref, NA), (ob_ref, NB), (oc_ref, NC), (om_ref, NM)):
        for j in range(0, n, IN_TN):
            w = min(IN_TN, n - j)
            o_ref[:, j:j + w] = jnp.dot(h, w_ref[:, off + j:off + j + w], preferred_element_type=f32)
            next(sgu, None)
        off += n
    for _ in sgu:
        pass


def _in_proj(x, p, l):
    tm = IN_TM
    names = ("g", "w", "ln_g", "ln_b", "w_cat", "bias_tile")
    specs = [pl.BlockSpec(memory_space=pl.ANY) if k == "w" else _layer_spec(p[k], l) for k in names]
    return pl.pallas_call(
        functools.partial(_inproj_kernel, layer=l),
        grid=(SEQ // tm,),
        in_specs=[pl.BlockSpec((tm, D_MODEL), lambda i: (i, 0))] + specs,
        out_specs=[pl.BlockSpec((tm, n), lambda i: (i, 0)) for n in (NA, NB, NC, NM, GROUP_W)],
        out_shape=[jax.ShapeDtypeStruct((SEQ, n), f32) for n in (NA, NB, NC, NM)]
        + [jax.ShapeDtypeStruct((SEQ, GROUP_W), bf16)],
        scratch_shapes=[pltpu.VMEM((D_MODEL, N_PAD), bf16), pltpu.SemaphoreType.DMA((1,))],
        compiler_params=pltpu.CompilerParams(dimension_semantics=("arbitrary",),
                                             vmem_limit_bytes=VMEM_LIMIT),
        name="in_proj",
    )(x, *[p[k] for k in names])


RW_TB = 512
RW_C = 64
RW_PRO = 2
RW_TICKS = 13


def _rwkv_kernel(*refs, has_vres):
    if has_vres:
        (c_ref, m_ref, vf_ref, mu_ref, w0_ref, wup_ref, a0_ref, aup_ref, gup_ref, kk_ref, ka_ref, rk_ref,
         lw_ref, lb_ref, v0_ref, vup_ref, y_ref,
         carry_ref, s_ref, r_s, k_s, v_s, lw_s, al_s, be_s, y_s,
         ta_p, rt_p, vb_p, bw_p, kw_p, arb_p, tv_p, av_p, bonus_p, g_p, dl_p) = refs
    else:
        (c_ref, mu_ref, w0_ref, wup_ref, a0_ref, aup_ref, gup_ref, kk_ref, ka_ref, rk_ref,
         lw_ref, lb_ref, y_ref, vf_out_ref,
         carry_ref, s_ref, r_s, k_s, v_s, lw_s, al_s, be_s, y_s,
         ta_p, rt_p, vb_p, bw_p, kw_p, arb_p, tv_p, av_p, bonus_p, g_p, dl_p) = refs
    step = pl.program_id(0)

    @pl.when(step == 0)
    def _():
        for ref in (carry_ref, s_ref, ta_p, rt_p, vb_p, bw_p, kw_p, arb_p, tv_p, av_p, bonus_p, g_p, dl_p):
            ref[...] = jnp.zeros_like(ref)

    wr = step % 2
    rd = 1 - wr
    c = RW_C
    nchunk = RW_TB // c
    npair = N_HEADS // 2
    hsl = lambda h: slice(2 * h * HEAD_DIM, 2 * (h + 1) * HEAD_DIM)
    bd_mask = (_iota((2 * c, 2 * c), 0) // c == _iota((2 * c, 2 * c), 1) // c).astype(f32)
    state = [s_ref[h] for h in range(npair)]

    def recurrence():
        for n in range(nchunk):
            rows = slice(n * c, (n + 1) * c)
            sas = [lax.dot_general(jnp.concatenate([ta_p[rd, rows, hsl(h)], rt_p[rd, rows, hsl(h)]], axis=0),
                                   state[h].astype(bf16), NT, preferred_element_type=f32)
                   for h in range(npair)]
            yield
            us = [sas[h][0:c] + tv_p[rd, rows, hsl(h)] for h in range(npair)]
            upd = [_dot(jnp.concatenate([us[h].astype(bf16), vb_p[rd, rows, hsl(h)]], axis=0),
                        jnp.concatenate([bw_p[rd, rows, hsl(h)], kw_p[rd, rows, hsl(h)]], axis=0), TN) * bd_mask
                   for h in range(npair)]
            yield
            dl = dl_p[rd, n:n + 1, :]
            for h in range(npair):
                state[h] = state[h] * dl[:, hsl(h)] + upd[h]
            ys = [sas[h][c:] + _dot(arb_p[rd, rows, hsl(h)], _bd(us[h])) + av_p[rd, rows, hsl(h)]
                  for h in range(npair)]
            y_s[rows, :] = jnp.concatenate(ys, axis=1)
            yield

    chain = recurrence()
    calls = [0]

    def advance(pieces=1):
        for _ in range(pieces):
            next(chain, None)

    def tick():
        calls[0] += 1
        if calls[0] % RW_TICKS == 0:
            advance()

    ones_bd = _group_ones(GROUP_W, HEAD_DIM)
    x = c_ref[:, 0:1024]
    x_prev = _shift_rows(x, carry_ref[...], 1)
    carry_ref[...] = x[RW_TB - 8:RW_TB]
    xs = x + (x_prev - x) * mu_ref[...]
    advance(RW_PRO)
    r = xs[:, 0:256]
    k = xs[:, 256:512]
    v = xs[:, 512:768]
    lora = xs[:, 768:896]
    w_pre = w0_ref[...] + _dot(jnp.tanh(lora), wup_ref[...])
    lw = -jnp.exp(-_softplus(-w_pre) - 0.5)
    advance(RW_PRO)
    a = _sigmoid(a0_ref[...] + _dot(lora, aup_ref[...]))
    g_p[wr] = _dot(_sigmoid(xs[:, 896:1024]), gup_ref[...])
    advance(RW_PRO)
    if has_vres:
        mix = _sigmoid(v0_ref[...] + _dot(m_ref[...], vup_ref[...]))
        v = v + (vf_ref[...] - v) * mix
    else:
        @pl.when(step < SEQ // RW_TB)
        def _():
            vf_out_ref[...] = v
    kk = k * kk_ref[...]
    kk = kk * lax.rsqrt(_segsum(kk * kk, ones_bd) + 1e-24)
    advance(RW_PRO)
    k = k * (1.0 + (a - 1.0) * ka_ref[...])
    bonus_p[wr] = _segsum(r * k * rk_ref[...], ones_bd) * v
    advance(RW_PRO)
    r_s[...] = r
    k_s[...] = k
    v_s[...] = v
    lw_s[...] = lw
    al_s[...] = -kk
    be_s[...] = kk * a

    tril_incl = (_iota((c, c), 0) >= _iota((c, c), 1)).astype(bf16)
    masks = _pair_masks(c)
    low_strict, low_incl = masks["strict"], masks["incl"]

    items = [(n, h) for n in range(nchunk) for h in range(npair)]
    pre = []
    for n in range(nchunk):
        rows = slice(n * c, (n + 1) * c)
        lwc = lw_s[rows, :]
        lc = _dot_sel_lhs(tril_incl, lwc)
        llast = lc[c - 1:c, :]
        e_out = jnp.exp(-lc)
        e_rest = jnp.exp(llast - lc)
        kc, bec = k_s[rows, :], be_s[rows, :]
        rt = r_s[rows, :] * jnp.exp(lc)
        pre.append(dict(rt=rt, at=al_s[rows, :] * jnp.exp(lc - lwc), bt=bec * e_out, kt=kc * e_out,
                        v=v_s[rows, :]))
        rt_p[wr, rows, :] = rt.astype(bf16)
        vb_p[wr, rows, :] = v_s[rows, :].astype(bf16)
        bw_p[wr, rows, :] = (bec * e_rest).astype(bf16)
        kw_p[wr, rows, :] = (kc * e_rest).astype(bf16)
        dl_p[wr, n:n + 1, :] = jnp.exp(llast)
        tick()
    ms = []
    for n, h in items:
        ms.append(_dot(jnp.concatenate([pre[n]["at"][:, hsl(h)], pre[n]["rt"][:, hsl(h)]], axis=0),
                       jnp.concatenate([_bd(pre[n]["bt"][:, hsl(h)]), _bd(pre[n]["kt"][:, hsl(h)])], axis=0),
                       NT))
        arb_p[wr, n * c:(n + 1) * c, hsl(h)] = jnp.where(low_incl, ms[-1][c:, 0:2 * c], 0.0).astype(bf16)
        tick()
    tms = _tri_inv([jnp.where(low_strict, -m[0:c, 0:2 * c], 0.0) for m in ms], masks, tick)
    avs = []
    for m, (n, h) in zip(ms, items):
        avs.append(_dot(jnp.concatenate([jnp.where(low_strict, m[0:c, 2 * c:], 0.0),
                                         jnp.where(low_incl, m[c:, 2 * c:], 0.0)], axis=0),
                        _bd(pre[n]["v"][:, hsl(h)])))
        av_p[wr, n * c:(n + 1) * c, hsl(h)] = avs[-1][c:]
        tick()
    for tm, av, (n, h) in zip(tms, avs, items):
        tt = _dot(tm, jnp.concatenate([_bd(pre[n]["at"][:, hsl(h)]), _bd(av[0:c])], axis=1))
        ta_p[wr, n * c:(n + 1) * c, hsl(h)] = tt[:, 0:2 * c].astype(bf16)
        tv_p[wr, n * c:(n + 1) * c, hsl(h)] = tt[:, 2 * c:]
        tick()
    for _ in chain:
        pass
    for h in range(npair):
        s_ref[h] = state[h]

    y = y_s[...]
    inv_d = 1.0 / HEAD_DIM
    mean = _segsum(y, ones_bd) * inv_d
    yc = y - mean
    var = _segsum(yc * yc, ones_bd) * inv_d
    y = yc * lax.rsqrt(var + RWKV_GN_EPS) * lw_ref[...] + lb_ref[...]
    y_ref[...] = ((y + bonus_p[rd]) * g_p[rd]).astype(bf16)


def _rwkv(c_a, c_m, v_first, p, l):
    has_vres = v_first is not None
    nblk = SEQ // RW_TB
    blk = lambda n: pl.BlockSpec((RW_TB, n), lambda i: (jnp.minimum(i, nblk - 1), 0))
    blk_prev = lambda n: pl.BlockSpec((RW_TB, n), lambda i: (jnp.maximum(i - 1, 0), 0))
    in_specs = [blk(NA)]
    args = [c_a]
    if has_vres:
        in_specs += [blk(NM), blk(GROUP_W)]
        args += [c_m, v_first]
    names = ("mu", "w0", "w_up", "a0", "a_up", "g_up", "k_k", "k_a", "r_k", "lnx_w", "lnx_b")
    in_specs += [_layer_spec(p[k], l) for k in names]
    args += [p[k] for k in names]
    if has_vres:
        in_specs += [_layer_spec(p["v0"], l - 1), _layer_spec(p["vres_up"], l - 1)]
        args += [p["v0"], p["vres_up"]]
        out_specs = blk_prev(GROUP_W)
        out_shape = jax.ShapeDtypeStruct((SEQ, GROUP_W), bf16)
    else:
        out_specs = [blk_prev(GROUP_W), blk(GROUP_W)]
        out_shape = [jax.ShapeDtypeStruct((SEQ, GROUP_W), bf16),
                     jax.ShapeDtypeStruct((SEQ, GROUP_W), f32)]
    scratch = [pltpu.VMEM((8, 1024), f32), pltpu.VMEM((N_HEADS // 2, 2 * HEAD_DIM, 2 * HEAD_DIM), f32)]
    scratch += [pltpu.VMEM((RW_TB, GROUP_W), f32) for _ in range(7)]
    scratch += [pltpu.VMEM((2, RW_TB, GROUP_W), bf16) for _ in range(6)]
    scratch += [pltpu.VMEM((2, RW_TB, GROUP_W), f32) for _ in range(4)]
    scratch += [pltpu.VMEM((2, RW_TB // RW_C, GROUP_W), f32)]
    out = pl.pallas_call(
        functools.partial(_rwkv_kernel, has_vres=has_vres),
        grid=(nblk + 1,),
        in_specs=in_specs,
        out_specs=out_specs,
        out_shape=out_shape,
        scratch_shapes=scratch,
        compiler_params=pltpu.CompilerParams(dimension_semantics=("arbitrary",),
                                             vmem_limit_bytes=VMEM_LIMIT),
        name="rwkv7",
    )(*args)
    if has_vres:
        return out, v_first
    return out[0], out[1]


GD_TB = 512
GD_C = 64
GD_PRO = 2
GD_TICKS = 9


def _gdn_kernel(c_ref, m_ref, cw_ref, alog_ref, dtb_ref, alogc_ref, dtbc_ref, ng_ref, y_ref,
                carry_ref, s_ref, q_s, k_s, v_s, be_s, g_s, o_s,
                u_p, w_p, qe_p, kd_p, qk_p, gz_p, egl_p):
    step = pl.program_id(0)

    @pl.when(step == 0)
    def _():
        for ref in (carry_ref, s_ref, u_p, w_p, qe_p, kd_p, qk_p, gz_p, egl_p):
            ref[...] = jnp.zeros_like(ref)

    wr = step % 2
    rd = 1 - wr
    c = GD_C
    nchunk = GD_TB // c
    npair = N_HEADS // 2
    hsl = lambda h: slice(2 * h * HEAD_DIM, 2 * (h + 1) * HEAD_DIM)
    bd_mask = (_iota((2 * c, 2 * c), 0) // c == _iota((2 * c, 2 * c), 1) // c).astype(f32)
    ones_bd = _group_ones(GROUP_W, HEAD_DIM)
    state = [s_ref[h] for h in range(npair)]

    def recurrence():
        for n in range(nchunk):
            rows = slice(n * c, (n + 1) * c)
            wss = [jnp.dot(jnp.concatenate([w_p[rd, rows, hsl(h)], qe_p[rd, rows, hsl(h)]], axis=0),
                           state[h].astype(bf16), preferred_element_type=f32)
                   for h in range(npair)]
            yield
            vns = [u_p[rd, rows, hsl(h)] - wss[h][0:c] for h in range(npair)]
            upd = [_dot(kd_p[rd, rows, hsl(h)], vns[h], TN) * bd_mask for h in range(npair)]
            yield
            egl = egl_p[rd, n:n + 1, :]
            for h in range(npair):
                state[h] = state[h] * egl[:, hsl(h)] + upd[h]
            os_ = [wss[h][c:] + _dot(qk_p[rd, rows, hsl(h)], _bd(vns[h])) for h in range(npair)]
            o_s[rows, :] = jnp.concatenate(os_, axis=1)
            yield

    chain = recurrence()
    calls = [0]

    def tick():
        calls[0] += 1
        if calls[0] % GD_TICKS == 0:
            next(chain, None)

    raw = c_ref[:, 0:768]
    carry = carry_ref[...]
    conv = raw * cw_ref[3:4, :]
    for s in range(1, 4):
        conv = conv + _shift_rows(raw, carry, s) * cw_ref[3 - s:4 - s, :]
    carry_ref[...] = raw[GD_TB - 8:GD_TB]
    qkv = conv * _sigmoid(conv)
    for _ in range(GD_PRO):
        next(chain, None)
    q = qkv[:, 0:256]
    k = qkv[:, 256:512]
    q_s[...] = q * lax.rsqrt(_segsum(q * q, ones_bd) + 1e-6) * (HEAD_DIM ** -0.5)
    for _ in range(GD_PRO):
        next(chain, None)
    k_s[...] = k * lax.rsqrt(_segsum(k * k, ones_bd) + 1e-6)
    v_s[...] = qkv[:, 512:768]
    z = c_ref[:, 768:1024]
    gz_p[wr] = z * _sigmoid(z)
    for _ in range(GD_PRO):
        next(chain, None)
    small = m_ref[...]
    er, ec = _iota((NM, GROUP_W), 0), _iota((NM, GROUP_W), 1)
    b_exp = _dot_sel_rhs(small, (er == ec // HEAD_DIM + M_GDN).astype(bf16))
    a_exp = _dot_sel_rhs(small, (er == ec // HEAD_DIM + M_GDN + N_HEADS).astype(bf16))
    be_s[...] = _sigmoid(b_exp)
    g_s[...] = -jnp.exp(alog_ref[...]) * _softplus(a_exp + dtb_ref[...])

    tril_incl = (_iota((c, c), 0) >= _iota((c, c), 1)).astype(bf16)
    masks = _pair_masks(c)
    low_strict, low_incl = masks["strict"], masks["incl"]
    sel8 = (_iota((8, NM), 0) + M_GDN == _iota((8, NM), 1)).astype(bf16)
    g_rows = -jnp.exp(alogc_ref[...]) * _softplus(_dot_sel_lhs(sel8, small, NT) + dtbc_ref[...])
    tj, ti = _iota((GD_TB, GD_TB), 0), _iota((GD_TB, GD_TB), 1)
    gc_rows = _dot_sel_rhs(g_rows, ((tj // c == ti // c) & (tj <= ti)).astype(bf16))

    items = [(n, h) for n in range(nchunk) for h in range(npair)]
    pre = []
    for n in range(nchunk):
        rows = slice(n * c, (n + 1) * c)
        gc = _dot_sel_lhs(tril_incl, g_s[rows, :])
        glast = gc[c - 1:c, :]
        egc = jnp.exp(gc)
        kc, bc = k_s[rows, :], be_s[rows, :]
        kb = kc * bc
        pre.append(dict(gc=gc, k=kc, kb=kb, q=q_s[rows, :], vb=v_s[rows, :] * bc, kbe=kb * egc))
        qe_p[wr, rows, :] = (q_s[rows, :] * egc).astype(bf16)
        kd_p[wr, rows, :] = (kc * jnp.exp(glast - gc)).astype(bf16)
        egl_p[wr, n:n + 1, :] = jnp.exp(glast)
        tick()

    def gc_row(n, h):
        return jnp.concatenate([gc_rows[N_HEADS + 2 * h + i:N_HEADS + 2 * h + i + 1, n * c:(n + 1) * c]
                                for i in range(2)], axis=1)

    dms, aqs = [], []
    for n, h in items:
        dms.append(jnp.exp(jnp.where(low_incl, pre[n]["gc"][:, hsl(h)] - gc_row(n, h), NEG)))
        aqs.append(_dot(jnp.concatenate([pre[n]["kb"][:, hsl(h)], pre[n]["q"][:, hsl(h)]], axis=0),
                        _bd(pre[n]["k"][:, hsl(h)]), NT))
        qk_p[wr, n * c:(n + 1) * c, hsl(h)] = (aqs[-1][c:] * dms[-1]).astype(bf16)
        tick()
    tms = _tri_inv([jnp.where(low_strict, aq[0:c] * dm, 0.0) for aq, dm in zip(aqs, dms)], masks, tick)
    for tm, (n, h) in zip(tms, items):
        uw = _dot(tm, jnp.concatenate([_bd(pre[n]["vb"][:, hsl(h)]), _bd(pre[n]["kbe"][:, hsl(h)])], axis=1))
        u_p[wr, n * c:(n + 1) * c, hsl(h)] = uw[:, 0:2 * c]
        w_p[wr, n * c:(n + 1) * c, hsl(h)] = uw[:, 2 * c:].astype(bf16)
        tick()
    for _ in chain:
        pass
    for h in range(npair):
        s_ref[h] = state[h]

    o = o_s[...]
    ms = _segsum(o * o, ones_bd) * (1.0 / HEAD_DIM)
    y_ref[...] = (o * lax.rsqrt(ms + EPS) * ng_ref[...] * gz_p[rd]).astype(bf16)


def _gdn(c_b, c_m, p, l):
    names = ("conv_w", "a_log", "dt_bias", "a_log_col", "dt_bias_col", "norm_g")
    nblk = SEQ // GD_TB
    scratch = [pltpu.VMEM((8, 768), f32), pltpu.VMEM((N_HEADS // 2, 2 * HEAD_DIM, 2 * HEAD_DIM), f32)]
    scratch += [pltpu.VMEM((GD_TB, GROUP_W), f32) for _ in range(6)]
    scratch += [pltpu.VMEM((2, GD_TB, GROUP_W), f32)]
    scratch += [pltpu.VMEM((2, GD_TB, GROUP_W), bf16) for _ in range(4)]
    scratch += [pltpu.VMEM((2, GD_TB, GROUP_W), f32), pltpu.VMEM((2, GD_TB // GD_C, GROUP_W), f32)]
    return pl.pallas_call(
        _gdn_kernel,
        grid=(nblk + 1,),
        in_specs=[pl.BlockSpec((GD_TB, n), lambda i: (jnp.minimum(i, nblk - 1), 0)) for n in (NB, NM)]
        + [_layer_spec(p[k], l) for k in names],
        out_specs=pl.BlockSpec((GD_TB, GROUP_W), lambda i: (jnp.maximum(i - 1, 0), 0)),
        out_shape=jax.ShapeDtypeStruct((SEQ, GROUP_W), bf16),
        scratch_shapes=scratch,
        compiler_params=pltpu.CompilerParams(dimension_semantics=("arbitrary",),
                                             vmem_limit_bytes=VMEM_LIMIT),
        name="gdn",
    )(c_b, c_m, *[p[k] for k in names])


GL_TB = 256
GL_C = 16
GL_S = 8


def _gla_kernel(c_ref, m_ref, gup_ref, gb_ref, ng_ref, y_ref, st_ref, sx_s, o_s):
    @pl.when(pl.program_id(0) == 0)
    def _():
        st_ref[...] = jnp.zeros_like(st_ref)

    tb, c, s = GL_TB, GL_C, GL_S
    nchunk, nsub = tb // c, tb // s
    q = c_ref[:, 0:128] * (GLA_HEAD_K ** -0.5)
    k = c_ref[:, 128:256]
    v = c_ref[:, 256:512]
    pre = _dot(m_ref[...], gup_ref[...]) + gb_ref[...]
    la = -_softplus(-pre) * (1.0 / 16.0)
    tj, ti = _iota((tb, tb), 0), _iota((tb, tb), 1)
    b = _dot_sel_lhs(((tj // c == ti // c) & (ti <= tj)).astype(bf16), la)
    qi = q * jnp.exp(b)

    ind_e = (_iota((GLA_KEY, GROUP_W), 0) // GLA_HEAD_K == _iota((GLA_KEY, GROUP_W), 1) // HEAD_DIM).astype(bf16)
    bd_mask = (_iota((GROUP_W, GLA_KEY), 0) // HEAD_DIM == _iota((GROUP_W, GLA_KEY), 1) // GLA_HEAD_K).astype(f32)

    b3, q3, k3 = (t.reshape(nsub, s, GLA_KEY) for t in (b, q, k))
    ri = _iota((nsub, s, GLA_KEY), 1)
    terms = []
    for j in range(s):
        e = jnp.exp(jnp.where(ri >= j, b3 - b3[:, j:j + 1, :], NEG))
        terms.append((q3 * (k3[:, j:j + 1, :] * e)).reshape(tb, GLA_KEY).astype(bf16))
    sx_s[...] = jnp.dot(jnp.concatenate(terms, axis=0), ind_e, preferred_element_type=f32)
    v3 = v.reshape(nsub, s, GROUP_W)
    o3 = sx_s[0:tb, :].reshape(nsub, s, GROUP_W) * v3[:, 0:1, :]
    for j in range(1, s):
        o3 = o3 + sx_s[j * tb:(j + 1) * tb, :].reshape(nsub, s, GROUP_W) * v3[:, j:j + 1, :]

    b4, q4, k4 = (t.reshape(nchunk, 2, s, GLA_KEY) for t in (b, q, k))
    bref = b4[:, 0, s - 1:s, :]
    qd = (q4[:, 1] * jnp.exp(b4[:, 1] - bref)).reshape(nchunk * s, GLA_KEY)
    kd = (k4[:, 0] * jnp.exp(bref - b4[:, 0])).reshape(nchunk * s, GLA_KEY)
    v0 = v.reshape(nchunk, 2, s, GROUP_W)[:, 0].reshape(nchunk * s, GROUP_W)
    head_k = _iota((nchunk * s, GLA_KEY), 1) // GLA_HEAD_K
    head_v = _iota((nchunk * s, GROUP_W), 1) // HEAD_DIM
    ks = jnp.concatenate([jnp.where(head_k == h, kd, 0.0) for h in range(N_HEADS)], axis=0)
    vs = jnp.concatenate([jnp.where(head_v == h, v0, 0.0) for h in range(N_HEADS)], axis=0)
    sc = _dot(qd, ks, NT)
    sr, scol = _iota(sc.shape, 0), _iota(sc.shape, 1)
    sc = jnp.where(sr // s == (scol % (nchunk * s)) // s, sc, 0.0)
    o_off = _dot(sc, vs).reshape(nchunk, 1, s, GROUP_W)
    o4 = o3.reshape(nchunk, 2, s, GROUP_W)
    o_intra = jnp.concatenate([o4[:, 0:1], o4[:, 1:2] + o_off], axis=1).reshape(tb, GROUP_W)

    blasts = [b[(n + 1) * c - 1:(n + 1) * c, :] for n in range(nchunk)]
    upds = [_dot(v[n * c:(n + 1) * c], k[n * c:(n + 1) * c] * jnp.exp(blasts[n] - b[n * c:(n + 1) * c]), TN)
            * bd_mask for n in range(nchunk)]
    st = st_ref[...]
    for n in range(nchunk):
        rows = slice(n * c, (n + 1) * c)
        o_s[rows, :] = o_intra[rows] + _dot(qi[rows], st, NT)
        st = st * jnp.exp(blasts[n]) + upds[n]
    st_ref[...] = st

    o = o_s[...]
    ms = _segsum(o * o, _group_ones(GROUP_W, HEAD_DIM)) * (1.0 / HEAD_DIM)
    gate = c_ref[:, 512:768]
    y_ref[...] = (o * lax.rsqrt(ms + EPS) * ng_ref[...] * (gate * _sigmoid(gate))).astype(bf16)


def _gla(c_c, c_m, p, l):
    names = ("gk_up", "gk_bias", "norm_g")
    return pl.pallas_call(
        _gla_kernel,
        grid=(SEQ // GL_TB,),
        in_specs=[pl.BlockSpec((GL_TB, n), lambda i: (i, 0)) for n in (NC, NM)]
        + [_layer_spec(p[k], l) for k in names],
        out_specs=pl.BlockSpec((GL_TB, GROUP_W), lambda i: (i, 0)),
        out_shape=jax.ShapeDtypeStruct((SEQ, GROUP_W), bf16),
        scratch_shapes=[pltpu.VMEM((GROUP_W, GLA_KEY), f32),
                        pltpu.VMEM((GL_TB * GL_S, GROUP_W), f32), pltpu.VMEM((GL_TB, GROUP_W), f32)],
        compiler_params=pltpu.CompilerParams(dimension_semantics=("arbitrary",),
                                             vmem_limit_bytes=VMEM_LIMIT),
        name="gla",
    )(c_c, c_m, *[p[k] for k in names])


FF_TM = 1024
FF_TF = 512


def _outffn_kernel(*refs, final, layer):
    if final:
        (x_ref, ya_ref, yb_ref, yc_ref, yd_ref, wo_hbm, g_ref, wu_hbm, wd_hbm, gf_ref, o_ref,
         wo_ref, wu_ref, wd_ref, sem) = refs
    else:
        (x_ref, ya_ref, yb_ref, yc_ref, yd_ref, wo_hbm, g_ref, wu_hbm, wd_hbm, o_ref,
         wo_ref, wu_ref, wd_ref, sem) = refs

    @pl.when(pl.program_id(0) == 0)
    def _():
        copies = [pltpu.make_async_copy(src.at[layer], dst, sem.at[i])
                  for i, (src, dst) in enumerate(((wo_hbm, wo_ref), (wu_hbm, wu_ref), (wd_hbm, wd_ref)))]
        for cp in copies:
            cp.start()
        for cp in copies:
            cp.wait()

    y = jnp.concatenate([ya_ref[...], yb_ref[...], yc_ref[...], yd_ref[...]], axis=1)
    x1 = x_ref[...] + jnp.dot(y, wo_ref[...], preferred_element_type=f32)
    ms = jnp.mean(x1 * x1, axis=-1, keepdims=True)
    h = (x1 * lax.rsqrt(ms + EPS) * g_ref[...]).astype(bf16)
    x2 = x1
    for kf in range(D_FF // FF_TF):
        cols = slice(kf * FF_TF, (kf + 1) * FF_TF)
        hid = jnp.maximum(jnp.dot(h, wu_ref[:, cols], preferred_element_type=f32), 0.0)
        x2 = x2 + jnp.dot((hid * hid).astype(bf16), wd_ref[cols, :], preferred_element_type=f32)
    if final:
        ms = jnp.mean(x2 * x2, axis=-1, keepdims=True)
        x2 = x2 * lax.rsqrt(ms + EPS) * gf_ref[...]
    o_ref[...] = x2


def _out_ffn(x, ys, p, l, final):
    hbm = pl.BlockSpec(memory_space=pl.ANY)
    yspec = pl.BlockSpec((FF_TM, GROUP_W), lambda i: (i, 0))
    in_specs = [pl.BlockSpec((FF_TM, D_MODEL), lambda i: (i, 0)), yspec, yspec, yspec, yspec,
                hbm, _layer_spec(p["g"], l), hbm, hbm]
    args = [x, *ys, p["w_out"], p["g"], p["w_up"], p["w_down"]]
    if final:
        in_specs.append(pl.BlockSpec((1, D_MODEL), lambda i: (0, 0)))
        args.append(p["g_final"])
    return pl.pallas_call(
        functools.partial(_outffn_kernel, final=final, layer=l),
        grid=(SEQ // FF_TM,),
        in_specs=in_specs,
        out_specs=pl.BlockSpec((FF_TM, D_MODEL), lambda i: (i, 0)),
        out_shape=jax.ShapeDtypeStruct((SEQ, D_MODEL), f32),
        scratch_shapes=[pltpu.VMEM((D_MODEL, D_MODEL), bf16), pltpu.VMEM((D_MODEL, D_FF), bf16),
                        pltpu.VMEM((D_FF, D_MODEL), bf16), pltpu.SemaphoreType.DMA((3,))],
        compiler_params=pltpu.CompilerParams(dimension_semantics=("arbitrary",),
                                             vmem_limit_bytes=VMEM_LIMIT),
        name="out_ffn",
    )(*args)


def kernel(x, w_in, w_out, norm_mix_g, norm_ffn_g, norm_final_g, rwkv_mu, rwkv_w0, rwkv_w_up, rwkv_a0, rwkv_a_up, rwkv_g_up, rwkv_k_k, rwkv_k_a, rwkv_r_k, rwkv_lnx_w, rwkv_lnx_b, rwkv_v0, rwkv_vres_down, rwkv_vres_up, gdn_conv_w, gdn_a_log, gdn_dt_bias, gdn_norm_g, gla_gk_up, gla_gk_bias, gla_norm_g, sgu_ln_g, sgu_ln_b, sgu_w_s, sgu_b_s, ffn_w_up, ffn_w_down):
    depth = w_in.shape[0]
    assert x.shape == (1, SEQ, D_MODEL) and x.dtype == f32, (x.shape, x.dtype)
    assert w_in.shape == (depth, D_MODEL, 3352) and ffn_w_up.shape == (depth, D_MODEL, D_FF)
    assert rwkv_vres_down.shape == (depth - 1, D_MODEL, 32) and gdn_conv_w.shape == (depth, 4, 768)
    row = lambda a: a.reshape(depth, 1, -1)
    per_head = lambda a: jnp.repeat(a, HEAD_DIM, axis=-1).reshape(depth, 1, -1)
    pad_cols = lambda w, n: jnp.pad(w, ((0, 0), (0, 0), (0, n - w.shape[2])))
    pad_rows = lambda w, top, total: jnp.pad(w, ((0, 0), (top, total - top - w.shape[1]), (0, 0)))
    vres_down = jnp.pad(rwkv_vres_down, ((1, 0), (0, 0), (0, 0)))
    narrow = pad_cols(jnp.concatenate([vres_down, w_in[:, :, 2048:2056], w_in[:, :, 2824:2840]], axis=2), NM)
    w_comb = jnp.concatenate(
        [w_in[:, :, 0:1024], w_in[:, :, 1024:2048], w_in[:, :, 2056:2824], narrow,
         w_in[:, :, 2840:3352]], axis=2).astype(bf16)
    p_in = dict(g=row(norm_mix_g), w=w_comb, ln_g=row(sgu_ln_g), ln_b=row(sgu_ln_b),
                w_cat=sgu_w_s.transpose(0, 2, 1, 3).reshape(depth, SG_C, 4 * SG_C),
                bias_tile=jnp.repeat(sgu_b_s.transpose(0, 2, 1), HEAD_DIM, axis=2))
    p_rwkv = dict(mu=row(rwkv_mu), w0=row(rwkv_w0), w_up=pad_rows(rwkv_w_up, 0, 128),
                  a0=row(rwkv_a0), a_up=pad_rows(rwkv_a_up, 64, 128), g_up=rwkv_g_up,
                  k_k=row(rwkv_k_k), k_a=row(rwkv_k_a), r_k=row(rwkv_r_k),
                  lnx_w=row(rwkv_lnx_w), lnx_b=row(rwkv_lnx_b),
                  v0=rwkv_v0.reshape(depth - 1, 1, -1), vres_up=pad_rows(rwkv_vres_up, M_VRES, NM))
    p_gdn = dict(conv_w=gdn_conv_w, a_log=per_head(gdn_a_log), dt_bias=per_head(gdn_dt_bias),
                 a_log_col=jnp.pad(gdn_a_log, ((0, 0), (N_HEADS, 0))).reshape(depth, 8, 1),
                 dt_bias_col=jnp.pad(gdn_dt_bias, ((0, 0), (N_HEADS, 0))).reshape(depth, 8, 1),
                 norm_g=row(jnp.tile(gdn_norm_g, (1, N_HEADS))))
    p_gla = dict(gk_up=pad_rows(gla_gk_up, M_GLA, NM), gk_bias=row(gla_gk_bias),
                 norm_g=row(jnp.tile(gla_norm_g, (1, N_HEADS))))
    p_ffn = dict(w_out=w_out.astype(bf16), g=row(norm_ffn_g), w_up=ffn_w_up.astype(bf16),
                 w_down=ffn_w_down.astype(bf16), g_final=norm_final_g.reshape(1, -1))

    xx = x[0]
    v_first = None
    for l in range(depth):
        c_a, c_b, c_c, c_m, y_d = _in_proj(xx, p_in, l)
        y_a, v_first = _rwkv(c_a, c_m, v_first, p_rwkv, l)
        y_b = _gdn(c_b, c_m, p_gdn, l)
        y_c = _gla(c_c, c_m, p_gla, l)
        xx = _out_ffn(xx, (y_a, y_b, y_c, y_d), p_ffn, l, final=(l == depth - 1))
    return xx[None]
```
